```python
import math
import jax, jax.numpy as jnp
from jax import lax
import numpy as np

D_MODEL = 1024
BATCH = 8
SEQ = 4096
DEPTH = 2

N_A_LAYERS = DEPTH // 2
N_B_LAYERS = DEPTH - N_A_LAYERS
D_SSM = D_MODEL
SSM_GROUP = 16
N_GROUPS = D_SSM // SSM_GROUP
STATE = 64
N_HEADS = 16
HEAD_DIM = 64
D_ATT = N_HEADS * HEAD_DIM
D_FF = 2816
CONV_W = 3
Q_BLOCK = 128
EPS = 1e-6
DT_MIN = 1e-3
DT_MAX = 1e-1

kernel_name = "yoco_s5_stickbreaking_convffn"


def rms_norm(x, g):
    xf = x.astype(jnp.float32)
    y = xf * lax.rsqrt(jnp.mean(xf * xf, axis=-1, keepdims=True) + EPS)
    return (y * g.astype(jnp.float32)).astype(x.dtype)


def s5_mixer(h, w_in, a_re, a_im, log_dt, b_re, b_im, c_re, c_im, d_skip, w_glu):
    bsz, seq, _ = h.shape
    f32 = jnp.float32
    u = (h @ w_in).astype(f32).reshape(bsz, seq, N_GROUPS, SSM_GROUP)
    a_re = a_re.astype(f32)
    a_im = a_im.astype(f32)
    dt = jnp.exp(log_dt.astype(f32))[:, None]
    mag = jnp.exp(a_re * dt)
    ab_re = mag * jnp.cos(a_im * dt)
    ab_im = mag * jnp.sin(a_im * dt)
    den = a_re * a_re + a_im * a_im
    f_re = ((ab_re - 1.0) * a_re + ab_im * a_im) / den
    f_im = (ab_im * a_re - (ab_re - 1.0) * a_im) / den
    b_re = b_re.astype(f32)
    b_im = b_im.astype(f32)
    bb_re = f_re[..., None] * b_re - f_im[..., None] * b_im
    bb_im = f_re[..., None] * b_im + f_im[..., None] * b_re
    bu_re = jnp.einsum('blgh,gph->blgp', u, bb_re)
    bu_im = jnp.einsum('blgh,gph->blgp', u, bb_im)
    shape_a = (1, seq, N_GROUPS, STATE)
    a_seq_re = jnp.broadcast_to(ab_re[None, None], shape_a)
    a_seq_im = jnp.broadcast_to(ab_im[None, None], shape_a)

    def combine(e1, e2):
        a1r, a1i, b1r, b1i = e1
        a2r, a2i, b2r, b2i = e2
        return (a2r * a1r - a2i * a1i,
                a2r * a1i + a2i * a1r,
                a2r * b1r - a2i * b1i + b2r,
                a2r * b1i + a2i * b1r + b2i)

    _, _, s_re, s_im = lax.associative_scan(
        combine, (a_seq_re, a_seq_im, bu_re, bu_im), axis=1)
    y = (jnp.einsum('blgp,ghp->blgh', s_re, c_re.astype(f32))
         - jnp.einsum('blgp,ghp->blgh', s_im, c_im.astype(f32))
         + d_skip.astype(f32) * u)
    y = jax.nn.gelu(y.reshape(bsz, seq, D_SSM)).astype(h.dtype)
    z = y @ w_glu
    return z[..., :D_MODEL] * jax.nn.sigmoid(z[..., D_MODEL:])


def stick_breaking_attention(h, w_q, k, v, w_o):
    bsz, seq, _ = h.shape
    scale = HEAD_DIM ** -0.5
    q = (h @ w_q).reshape(bsz, seq, N_HEADS, HEAD_DIM).transpose(0, 2, 1, 3) * scale
    outs = []
    for blk in range(seq // Q_BLOCK):
        t0 = blk * Q_BLOCK
        nk = t0 + Q_BLOCK
        qb = q[:, :, t0:nk]
        kb = k[:, :, :nk]
        vb = v[:, :, :nk]
        z = jnp.einsum('bhqd,bhkd->bhqk', qb, kb).astype(jnp.float32)
        t_idx = t0 + jnp.arange(Q_BLOCK)[:, None]
        s_idx = jnp.arange(nk)[None, :]
        causal = s_idx < t_idx
        log_beta = jax.nn.log_sigmoid(z)
        log_one_minus = jnp.where(causal, log_beta - z, 0.0)
        rem = lax.cumsum(log_one_minus, axis=3, reverse=True) - log_one_minus
        w = jnp.where(causal, jnp.exp(log_beta + rem), 0.0)
        outs.append(jnp.einsum('bhqk,bhkd->bhqd', w.astype(vb.dtype), vb))
    o = jnp.concatenate(outs, axis=2).transpose(0, 2, 1, 3).reshape(bsz, seq, D_ATT)
    return o @ w_o


def conv_ffn(h, w_up, conv_w, conv_b, w_down):
    gu = h @ w_up
    g = gu[..., :D_FF]
    u = gu[..., D_FF:]
    g = lax.conv_general_dilated(
        g, conv_w, window_strides=(1,), padding=[(CONV_W - 1, 0)],
        dimension_numbers=('NWC', 'WIO', 'NWC'), feature_group_count=D_FF) + conv_b
    return (jax.nn.silu(g) * u) @ w_down


def _fwd_setup_inputs(seed: int = 0) -> dict:
    key = jax.random.key(seed)
    ks = jax.random.split(key, 24)
    f32 = jnp.float32
    nrm = lambda k, shape, s: jax.random.normal(k, shape, f32) * s
    n_idx = jnp.arange(STATE, dtype=f32)
    return {
        "x": jax.random.normal(ks[0], (BATCH, SEQ, D_MODEL), f32),
        "norm_mix": 1.0 + nrm(ks[1], (DEPTH, D_MODEL), 0.02),
        "norm_ffn": 1.0 + nrm(ks[2], (DEPTH, D_MODEL), 0.02),
        "norm_kv": 1.0 + nrm(ks[3], (D_MODEL,), 0.02),
        "norm_final": 1.0 + nrm(ks[4], (D_MODEL,), 0.02),
        "ssm_w_in": nrm(ks[5], (N_A_LAYERS, D_MODEL, D_SSM), D_MODEL ** -0.5),
        "ssm_a_re": -0.5 * jnp.exp(nrm(ks[6], (N_A_LAYERS, N_GROUPS, STATE), 0.05)),
        "ssm_a_im": math.pi * n_idx + nrm(ks[7], (N_A_LAYERS, N_GROUPS, STATE), 0.01),
        "ssm_log_dt": jax.random.uniform(ks[8], (N_A_LAYERS, N_GROUPS), f32,
                                         minval=math.log(DT_MIN), maxval=math.log(DT_MAX)),
        "ssm_b_re": nrm(ks[9], (N_A_LAYERS, N_GROUPS, STATE, SSM_GROUP), (2 * SSM_GROUP) ** -0.5),
        "ssm_b_im": nrm(ks[10], (N_A_LAYERS, N_GROUPS, STATE, SSM_GROUP), (2 * SSM_GROUP) ** -0.5),
        "ssm_c_re": nrm(ks[11], (N_A_LAYERS, N_GROUPS, SSM_GROUP, STATE), (2 * STATE) ** -0.5),
        "ssm_c_im": nrm(ks[12], (N_A_LAYERS, N_GROUPS, SSM_GROUP, STATE), (2 * STATE) ** -0.5),
        "ssm_d": nrm(ks[13], (N_A_LAYERS, N_GROUPS, SSM_GROUP), 1.0),
        "ssm_w_glu": nrm(ks[14], (N_A_LAYERS, D_SSM, 2 * D_MODEL), D_SSM ** -0.5),
        "kv_w": nrm(ks[15], (D_MODEL, 2 * D_ATT), D_MODEL ** -0.5),
        "attn_w_q": nrm(ks[16], (N_B_LAYERS, D_MODEL, D_ATT), D_MODEL ** -0.5),
        "attn_w_o": nrm(ks[17], (N_B_LAYERS, D_ATT, D_MODEL), D_ATT ** -0.5),
        "ffn_w_up": nrm(ks[18], (DEPTH, D_MODEL, 2 * D_FF), D_MODEL ** -0.5),
        "ffn_conv_w": nrm(ks[19], (DEPTH, CONV_W, 1, D_FF), CONV_W ** -0.5),
        "ffn_conv_b": nrm(ks[20], (DEPTH, D_FF), 0.01),
        "ffn_w_down": nrm(ks[21], (DEPTH, D_FF, D_MODEL), D_FF ** -0.5),
    }


def _fwd_reference(x, norm_mix, norm_ffn, norm_kv, norm_final,
              ssm_w_in, ssm_a_re, ssm_a_im, ssm_log_dt, ssm_b_re, ssm_b_im,
              ssm_c_re, ssm_c_im, ssm_d, ssm_w_glu,
              kv_w, attn_w_q, attn_w_o,
              ffn_w_up, ffn_conv_w, ffn_conv_b, ffn_w_down):
    bsz, seq, _ = x.shape
    k_shared = None
    v_shared = None
    for layer in range(DEPTH):
        h = rms_norm(x, norm_mix[layer])
        if layer < N_A_LAYERS:
            x = x + s5_mixer(h, ssm_w_in[layer], ssm_a_re[layer], ssm_a_im[layer],
                             ssm_log_dt[layer], ssm_b_re[layer], ssm_b_im[layer],
                             ssm_c_re[layer], ssm_c_im[layer], ssm_d[layer],
                             ssm_w_glu[layer])
        else:
            j = layer - N_A_LAYERS
            x = x + stick_breaking_attention(h, attn_w_q[j], k_shared, v_shared, attn_w_o[j])
        x = x + conv_ffn(rms_norm(x, norm_ffn[layer]), ffn_w_up[layer],
                         ffn_conv_w[layer], ffn_conv_b[layer], ffn_w_down[layer])
        if layer == N_A_LAYERS - 1:
            kv = rms_norm(x, norm_kv) @ kv_w
            k_shared = kv[..., :D_ATT].reshape(bsz, seq, N_HEADS, HEAD_DIM).transpose(0, 2, 1, 3)
            v_shared = kv[..., D_ATT:].reshape(bsz, seq, N_HEADS, HEAD_DIM).transpose(0, 2, 1, 3)
    return rms_norm(x, norm_final)


import jax as _jax
import jax.numpy as _jnp

TWIN_FORMAT = 'train_step'
FWD_PARAMS = ['x', 'norm_mix', 'norm_ffn', 'norm_kv', 'norm_final', 'ssm_w_in', 'ssm_a_re', 'ssm_a_im', 'ssm_log_dt', 'ssm_b_re', 'ssm_b_im', 'ssm_c_re', 'ssm_c_im', 'ssm_d', 'ssm_w_glu', 'kv_w', 'attn_w_q', 'attn_w_o', 'ffn_w_up', 'ffn_conv_w', 'ffn_conv_b', 'ffn_w_down']
TWIN_WEIGHTS = ['norm_mix', 'norm_ffn', 'norm_kv', 'norm_final', 'ssm_w_in', 'ssm_a_re', 'ssm_a_im', 'ssm_log_dt', 'ssm_b_re', 'ssm_b_im', 'ssm_c_re', 'ssm_c_im', 'ssm_d', 'ssm_w_glu', 'kv_w', 'attn_w_q', 'attn_w_o', 'ffn_w_up', 'ffn_conv_w', 'ffn_conv_b', 'ffn_w_down']
TWIN_DIFF_INPUT = 'x'
TWIN_INPUTS = ['x', 'norm_mix', 'norm_ffn', 'norm_kv', 'norm_final', 'ssm_w_in', 'ssm_a_re', 'ssm_a_im', 'ssm_log_dt', 'ssm_b_re', 'ssm_b_im', 'ssm_c_re', 'ssm_c_im', 'ssm_d', 'ssm_w_glu', 'kv_w', 'attn_w_q', 'attn_w_o', 'ffn_w_up', 'ffn_conv_w', 'ffn_conv_b', 'ffn_w_down', 'loss_target', 'm_norm_mix', 'm_norm_ffn', 'm_norm_kv', 'm_norm_final', 'm_ssm_w_in', 'm_ssm_a_re', 'm_ssm_a_im', 'm_ssm_log_dt', 'm_ssm_b_re', 'm_ssm_b_im', 'm_ssm_c_re', 'm_ssm_c_im', 'm_ssm_d', 'm_ssm_w_glu', 'm_kv_w', 'm_attn_w_q', 'm_attn_w_o', 'm_ffn_w_up', 'm_ffn_conv_w', 'm_ffn_conv_b', 'm_ffn_w_down', 'v_norm_mix', 'v_norm_ffn', 'v_norm_kv', 'v_norm_final', 'v_ssm_w_in', 'v_ssm_a_re', 'v_ssm_a_im', 'v_ssm_log_dt', 'v_ssm_b_re', 'v_ssm_b_im', 'v_ssm_c_re', 'v_ssm_c_im', 'v_ssm_d', 'v_ssm_w_glu', 'v_kv_w', 'v_attn_w_q', 'v_attn_w_o', 'v_ffn_w_up', 'v_ffn_conv_w', 'v_ffn_conv_b', 'v_ffn_w_down']
TWIN_OUTPUTS = ['loss', 'grad_x', 'grad_norm_mix', 'grad_norm_ffn', 'grad_norm_kv', 'grad_norm_final', 'grad_ssm_w_in', 'grad_ssm_a_re', 'grad_ssm_a_im', 'grad_ssm_log_dt', 'grad_ssm_b_re', 'grad_ssm_b_im', 'grad_ssm_c_re', 'grad_ssm_c_im', 'grad_ssm_d', 'grad_ssm_w_glu', 'grad_kv_w', 'grad_attn_w_q', 'grad_attn_w_o', 'grad_ffn_w_up', 'grad_ffn_conv_w', 'grad_ffn_conv_b', 'grad_ffn_w_down', 'delta_norm_mix', 'delta_norm_ffn', 'delta_norm_kv', 'delta_norm_final', 'delta_ssm_w_in', 'delta_ssm_a_re', 'delta_ssm_a_im', 'delta_ssm_log_dt', 'delta_ssm_b_re', 'delta_ssm_b_im', 'delta_ssm_c_re', 'delta_ssm_c_im', 'delta_ssm_d', 'delta_ssm_w_glu', 'delta_kv_w', 'delta_attn_w_q', 'delta_attn_w_o', 'delta_ffn_w_up', 'delta_ffn_conv_w', 'delta_ffn_conv_b', 'delta_ffn_w_down', 'new_m_norm_mix', 'new_m_norm_ffn', 'new_m_norm_kv', 'new_m_norm_final', 'new_m_ssm_w_in', 'new_m_ssm_a_re', 'new_m_ssm_a_im', 'new_m_ssm_log_dt', 'new_m_ssm_b_re', 'new_m_ssm_b_im', 'new_m_ssm_c_re', 'new_m_ssm_c_im', 'new_m_ssm_d', 'new_m_ssm_w_glu', 'new_m_kv_w', 'new_m_attn_w_q', 'new_m_attn_w_o', 'new_m_ffn_w_up', 'new_m_ffn_conv_w', 'new_m_ffn_conv_b', 'new_m_ffn_w_down', 'new_v_norm_mix', 'new_v_norm_ffn', 'new_v_norm_kv', 'new_v_norm_final', 'new_v_ssm_w_in', 'new_v_ssm_a_re', 'new_v_ssm_a_im', 'new_v_ssm_log_dt', 'new_v_ssm_b_re', 'new_v_ssm_b_im', 'new_v_ssm_c_re', 'new_v_ssm_c_im', 'new_v_ssm_d', 'new_v_ssm_w_glu', 'new_v_kv_w', 'new_v_attn_w_q', 'new_v_attn_w_o', 'new_v_ffn_w_up', 'new_v_ffn_conv_w', 'new_v_ffn_conv_b', 'new_v_ffn_w_down']
TWIN_LEAF_KINDS = {'loss': 'loss', 'grad_x': 'grad_x', 'grad_norm_mix': 'grad_w', 'grad_norm_ffn': 'grad_w', 'grad_norm_kv': 'grad_w', 'grad_norm_final': 'grad_w', 'grad_ssm_w_in': 'grad_w', 'grad_ssm_a_re': 'grad_w', 'grad_ssm_a_im': 'grad_w', 'grad_ssm_log_dt': 'grad_w', 'grad_ssm_b_re': 'grad_w', 'grad_ssm_b_im': 'grad_w', 'grad_ssm_c_re': 'grad_w', 'grad_ssm_c_im': 'grad_w', 'grad_ssm_d': 'grad_w', 'grad_ssm_w_glu': 'grad_w', 'grad_kv_w': 'grad_w', 'grad_attn_w_q': 'grad_w', 'grad_attn_w_o': 'grad_w', 'grad_ffn_w_up': 'grad_w', 'grad_ffn_conv_w': 'grad_w', 'grad_ffn_conv_b': 'grad_w', 'grad_ffn_w_down': 'grad_w', 'delta_norm_mix': 'delta_w', 'delta_norm_ffn': 'delta_w', 'delta_norm_kv': 'delta_w', 'delta_norm_final': 'delta_w', 'delta_ssm_w_in': 'delta_w', 'delta_ssm_a_re': 'delta_w', 'delta_ssm_a_im': 'delta_w', 'delta_ssm_log_dt': 'delta_w', 'delta_ssm_b_re': 'delta_w', 'delta_ssm_b_im': 'delta_w', 'delta_ssm_c_re': 'delta_w', 'delta_ssm_c_im': 'delta_w', 'delta_ssm_d': 'delta_w', 'delta_ssm_w_glu': 'delta_w', 'delta_kv_w': 'delta_w', 'delta_attn_w_q': 'delta_w', 'delta_attn_w_o': 'delta_w', 'delta_ffn_w_up': 'delta_w', 'delta_ffn_conv_w': 'delta_w', 'delta_ffn_conv_b': 'delta_w', 'delta_ffn_w_down': 'delta_w', 'new_m_norm_mix': 'new_m', 'new_m_norm_ffn': 'new_m', 'new_m_norm_kv': 'new_m', 'new_m_norm_final': 'new_m', 'new_m_ssm_w_in': 'new_m', 'new_m_ssm_a_re': 'new_m', 'new_m_ssm_a_im': 'new_m', 'new_m_ssm_log_dt': 'new_m', 'new_m_ssm_b_re': 'new_m', 'new_m_ssm_b_im': 'new_m', 'new_m_ssm_c_re': 'new_m', 'new_m_ssm_c_im': 'new_m', 'new_m_ssm_d': 'new_m', 'new_m_ssm_w_glu': 'new_m', 'new_m_kv_w': 'new_m', 'new_m_attn_w_q': 'new_m', 'new_m_attn_w_o': 'new_m', 'new_m_ffn_w_up': 'new_m', 'new_m_ffn_conv_w': 'new_m', 'new_m_ffn_conv_b': 'new_m', 'new_m_ffn_w_down': 'new_m', 'new_v_norm_mix': 'new_v', 'new_v_norm_ffn': 'new_v', 'new_v_norm_kv': 'new_v', 'new_v_norm_final': 'new_v', 'new_v_ssm_w_in': 'new_v', 'new_v_ssm_a_re': 'new_v', 'new_v_ssm_a_im': 'new_v', 'new_v_ssm_log_dt': 'new_v', 'new_v_ssm_b_re': 'new_v', 'new_v_ssm_b_im': 'new_v', 'new_v_ssm_c_re': 'new_v', 'new_v_ssm_c_im': 'new_v', 'new_v_ssm_d': 'new_v', 'new_v_ssm_w_glu': 'new_v', 'new_v_kv_w': 'new_v', 'new_v_attn_w_q': 'new_v', 'new_v_attn_w_o': 'new_v', 'new_v_ffn_w_up': 'new_v', 'new_v_ffn_conv_w': 'new_v', 'new_v_ffn_conv_b': 'new_v', 'new_v_ffn_w_down': 'new_v'}


def _forward(args):
    return _fwd_reference(*[args[k] for k in FWD_PARAMS])


def _output_shape():
    out = _jax.eval_shape(lambda: _forward(_fwd_setup_inputs(0)))
    return out.shape, out.dtype

N_MICROBATCH = 1
ADAM_LR = 0.001
ADAM_B1 = 0.9
ADAM_B2 = 0.999
ADAM_EPS = 1e-08
ADAM_WD = 0.01
ADAM_STEP = 10
PER_EXAMPLE_BATCH_AXIS = {'x': 0, 'loss_target': 0}
SHARED_INPUTS = []
_WEIGHT_DTYPES = {'norm_mix': _jnp.float32, 'norm_ffn': _jnp.float32, 'norm_kv': _jnp.float32, 'norm_final': _jnp.float32, 'ssm_w_in': _jnp.float32, 'ssm_a_re': _jnp.float32, 'ssm_a_im': _jnp.float32, 'ssm_log_dt': _jnp.float32, 'ssm_b_re': _jnp.float32, 'ssm_b_im': _jnp.float32, 'ssm_c_re': _jnp.float32, 'ssm_c_im': _jnp.float32, 'ssm_d': _jnp.float32, 'ssm_w_glu': _jnp.float32, 'kv_w': _jnp.float32, 'attn_w_q': _jnp.float32, 'attn_w_o': _jnp.float32, 'ffn_w_up': _jnp.float32, 'ffn_conv_w': _jnp.float32, 'ffn_conv_b': _jnp.float32, 'ffn_w_down': _jnp.float32}
MOMENT_SCALE = {'norm_mix': 6.503266e-02, 'norm_ffn': 1.241472e-01, 'norm_kv': 1.074441e-01, 'norm_final': 3.200332e+01, 'ssm_w_in': 8.108883e-02, 'ssm_a_re': 4.437427e-03, 'ssm_a_im': 3.601856e-03, 'ssm_log_dt': 3.933090e+00, 'ssm_b_re': 2.652648e-03, 'ssm_b_im': 2.593324e-03, 'ssm_c_re': 5.250564e-03, 'ssm_c_im': 5.263738e-03, 'ssm_d': 9.128512e-02, 'ssm_w_glu': 6.035885e-02, 'kv_w': 7.076048e-02, 'attn_w_q': 4.078800e-02, 'attn_w_o': 9.124501e-02, 'ffn_w_up': 5.283599e-02, 'ffn_conv_w': 5.453409e-02, 'ffn_conv_b': 5.190816e-02, 'ffn_w_down': 8.653008e-02}


def _to_microbatches(a, axis):
    t = _jnp.moveaxis(a, axis, 0)
    t = t.reshape((N_MICROBATCH, t.shape[0] // N_MICROBATCH) + t.shape[1:])
    return _jnp.moveaxis(t, 1, axis + 1)


def setup_inputs(seed: int = 0) -> dict:
    inp = _fwd_setup_inputs(seed)
    key = _jax.random.fold_in(_jax.random.key(seed), 7919)
    shape, _ = _output_shape()
    out = dict(inp)
    out["loss_target"] = _jax.random.normal(_jax.random.fold_in(key, 0), shape, _jnp.float32)
    for i, name in enumerate(TWIN_WEIGHTS):
        w = inp[name].astype(_jnp.float32)
        if MOMENT_SCALE is None:
            s = _jnp.sqrt(_jnp.mean(_jnp.square(w)) + 1e-30)
        else:
            s = MOMENT_SCALE[name]
        km, kv = _jax.random.split(_jax.random.fold_in(key, i + 1))
        out[name] = w
        out["m_" + name] = s * _jax.random.normal(km, w.shape, _jnp.float32)
        out["v_" + name] = (s * s) * _jax.random.uniform(kv, w.shape, _jnp.float32, 0.5, 1.5)
    if N_MICROBATCH > 1:
        for name, axis in PER_EXAMPLE_BATCH_AXIS.items():
            out[name] = _to_microbatches(out[name], axis)
    return {'x': out['x'], 'norm_mix': out['norm_mix'], 'norm_ffn': out['norm_ffn'], 'norm_kv': out['norm_kv'], 'norm_final': out['norm_final'], 'ssm_w_in': out['ssm_w_in'], 'ssm_a_re': out['ssm_a_re'], 'ssm_a_im': out['ssm_a_im'], 'ssm_log_dt': out['ssm_log_dt'], 'ssm_b_re': out['ssm_b_re'], 'ssm_b_im': out['ssm_b_im'], 'ssm_c_re': out['ssm_c_re'], 'ssm_c_im': out['ssm_c_im'], 'ssm_d': out['ssm_d'], 'ssm_w_glu': out['ssm_w_glu'], 'kv_w': out['kv_w'], 'attn_w_q': out['attn_w_q'], 'attn_w_o': out['attn_w_o'], 'ffn_w_up': out['ffn_w_up'], 'ffn_conv_w': out['ffn_conv_w'], 'ffn_conv_b': out['ffn_conv_b'], 'ffn_w_down': out['ffn_w_down'], 'loss_target': out['loss_target'], 'm_norm_mix': out['m_norm_mix'], 'm_norm_ffn': out['m_norm_ffn'], 'm_norm_kv': out['m_norm_kv'], 'm_norm_final': out['m_norm_final'], 'm_ssm_w_in': out['m_ssm_w_in'], 'm_ssm_a_re': out['m_ssm_a_re'], 'm_ssm_a_im': out['m_ssm_a_im'], 'm_ssm_log_dt': out['m_ssm_log_dt'], 'm_ssm_b_re': out['m_ssm_b_re'], 'm_ssm_b_im': out['m_ssm_b_im'], 'm_ssm_c_re': out['m_ssm_c_re'], 'm_ssm_c_im': out['m_ssm_c_im'], 'm_ssm_d': out['m_ssm_d'], 'm_ssm_w_glu': out['m_ssm_w_glu'], 'm_kv_w': out['m_kv_w'], 'm_attn_w_q': out['m_attn_w_q'], 'm_attn_w_o': out['m_attn_w_o'], 'm_ffn_w_up': out['m_ffn_w_up'], 'm_ffn_conv_w': out['m_ffn_conv_w'], 'm_ffn_conv_b': out['m_ffn_conv_b'], 'm_ffn_w_down': out['m_ffn_w_down'], 'v_norm_mix': out['v_norm_mix'], 'v_norm_ffn': out['v_norm_ffn'], 'v_norm_kv': out['v_norm_kv'], 'v_norm_final': out['v_norm_final'], 'v_ssm_w_in': out['v_ssm_w_in'], 'v_ssm_a_re': out['v_ssm_a_re'], 'v_ssm_a_im': out['v_ssm_a_im'], 'v_ssm_log_dt': out['v_ssm_log_dt'], 'v_ssm_b_re': out['v_ssm_b_re'], 'v_ssm_b_im': out['v_ssm_b_im'], 'v_ssm_c_re': out['v_ssm_c_re'], 'v_ssm_c_im': out['v_ssm_c_im'], 'v_ssm_d': out['v_ssm_d'], 'v_ssm_w_glu': out['v_ssm_w_glu'], 'v_kv_w': out['v_kv_w'], 'v_attn_w_q': out['v_attn_w_q'], 'v_attn_w_o': out['v_attn_w_o'], 'v_ffn_w_up': out['v_ffn_w_up'], 'v_ffn_conv_w': out['v_ffn_conv_w'], 'v_ffn_conv_b': out['v_ffn_conv_b'], 'v_ffn_w_down': out['v_ffn_w_down']}


def _loss(weights, diff, rest, loss_target):
    with _jax.named_scope("forward"):
        args = {**rest, TWIN_DIFF_INPUT: diff, **{k: w.astype(_WEIGHT_DTYPES[k]) for k, w in weights.items()}}
        y = _forward(args)
    with _jax.named_scope("loss_head"):
        err = _jnp.square(y.astype(_jnp.float32) - loss_target)
        return 0.5 * _jnp.sum(_jnp.mean(err, axis=-1)) if err.ndim else 0.5 * err


def _adamw(w, g, m, v):
    m = ADAM_B1 * m + (1.0 - ADAM_B1) * g
    v = ADAM_B2 * v + (1.0 - ADAM_B2) * _jnp.square(g)
    m_hat = m / (1.0 - ADAM_B1 ** ADAM_STEP)
    v_hat = v / (1.0 - ADAM_B2 ** ADAM_STEP)
    delta = -ADAM_LR * (m_hat / (_jnp.sqrt(v_hat) + ADAM_EPS) + ADAM_WD * w)
    return delta, m, v


def reference(x, norm_mix, norm_ffn, norm_kv, norm_final, ssm_w_in, ssm_a_re, ssm_a_im, ssm_log_dt, ssm_b_re, ssm_b_im, ssm_c_re, ssm_c_im, ssm_d, ssm_w_glu, kv_w, attn_w_q, attn_w_o, ffn_w_up, ffn_conv_w, ffn_conv_b, ffn_w_down, loss_target, m_norm_mix, m_norm_ffn, m_norm_kv, m_norm_final, m_ssm_w_in, m_ssm_a_re, m_ssm_a_im, m_ssm_log_dt, m_ssm_b_re, m_ssm_b_im, m_ssm_c_re, m_ssm_c_im, m_ssm_d, m_ssm_w_glu, m_kv_w, m_attn_w_q, m_attn_w_o, m_ffn_w_up, m_ffn_conv_w, m_ffn_conv_b, m_ffn_w_down, v_norm_mix, v_norm_ffn, v_norm_kv, v_norm_final, v_ssm_w_in, v_ssm_a_re, v_ssm_a_im, v_ssm_log_dt, v_ssm_b_re, v_ssm_b_im, v_ssm_c_re, v_ssm_c_im, v_ssm_d, v_ssm_w_glu, v_kv_w, v_attn_w_q, v_attn_w_o, v_ffn_w_up, v_ffn_conv_w, v_ffn_conv_b, v_ffn_w_down):
    given = dict(x=x, norm_mix=norm_mix, norm_ffn=norm_ffn, norm_kv=norm_kv, norm_final=norm_final, ssm_w_in=ssm_w_in, ssm_a_re=ssm_a_re, ssm_a_im=ssm_a_im, ssm_log_dt=ssm_log_dt, ssm_b_re=ssm_b_re, ssm_b_im=ssm_b_im, ssm_c_re=ssm_c_re, ssm_c_im=ssm_c_im, ssm_d=ssm_d, ssm_w_glu=ssm_w_glu, kv_w=kv_w, attn_w_q=attn_w_q, attn_w_o=attn_w_o, ffn_w_up=ffn_w_up, ffn_conv_w=ffn_conv_w, ffn_conv_b=ffn_conv_b, ffn_w_down=ffn_w_down, loss_target=loss_target, m_norm_mix=m_norm_mix, m_norm_ffn=m_norm_ffn, m_norm_kv=m_norm_kv, m_norm_final=m_norm_final, m_ssm_w_in=m_ssm_w_in, m_ssm_a_re=m_ssm_a_re, m_ssm_a_im=m_ssm_a_im, m_ssm_log_dt=m_ssm_log_dt, m_ssm_b_re=m_ssm_b_re, m_ssm_b_im=m_ssm_b_im, m_ssm_c_re=m_ssm_c_re, m_ssm_c_im=m_ssm_c_im, m_ssm_d=m_ssm_d, m_ssm_w_glu=m_ssm_w_glu, m_kv_w=m_kv_w, m_attn_w_q=m_attn_w_q, m_attn_w_o=m_attn_w_o, m_ffn_w_up=m_ffn_w_up, m_ffn_conv_w=m_ffn_conv_w, m_ffn_conv_b=m_ffn_conv_b, m_ffn_w_down=m_ffn_w_down, v_norm_mix=v_norm_mix, v_norm_ffn=v_norm_ffn, v_norm_kv=v_norm_kv, v_norm_final=v_norm_final, v_ssm_w_in=v_ssm_w_in, v_ssm_a_re=v_ssm_a_re, v_ssm_a_im=v_ssm_a_im, v_ssm_log_dt=v_ssm_log_dt, v_ssm_b_re=v_ssm_b_re, v_ssm_b_im=v_ssm_b_im, v_ssm_c_re=v_ssm_c_re, v_ssm_c_im=v_ssm_c_im, v_ssm_d=v_ssm_d, v_ssm_w_glu=v_ssm_w_glu, v_kv_w=v_kv_w, v_attn_w_q=v_attn_w_q, v_attn_w_o=v_attn_w_o, v_ffn_w_up=v_ffn_w_up, v_ffn_conv_w=v_ffn_conv_w, v_ffn_conv_b=v_ffn_conv_b, v_ffn_w_down=v_ffn_w_down)
    weights = {n: given[n] for n in TWIN_WEIGHTS}
    shared = {n: given[n] for n in SHARED_INPUTS}
    per_example = {n: given[n] for n in ['x']}
    grad_fn = _jax.value_and_grad(_loss, argnums=(0, 1))

    def one_microbatch(ex, loss_target):
        ex = dict(ex)
        diff = ex.pop(TWIN_DIFF_INPUT)
        return grad_fn(weights, diff, {**shared, **ex}, loss_target)

    if N_MICROBATCH == 1:
        loss, (grad_w, grad_x) = one_microbatch(per_example, given["loss_target"])
    else:
        def body(carry, xs):
            loss_sum, grad_sum = carry
            l_k, (gw_k, gx_k) = one_microbatch(xs[0], xs[1])
            with _jax.named_scope("update"):
                return (loss_sum + l_k, _jax.tree.map(_jnp.add, grad_sum, gw_k)), gx_k

        init = (_jnp.zeros((), _jnp.float32), _jax.tree.map(_jnp.zeros_like, weights))
        (loss, grad_w), grad_x = _jax.lax.scan(body, init, (per_example, given["loss_target"]))
    with _jax.named_scope("update"):
        delta_w, new_m, new_v = {}, {}, {}
        for n in TWIN_WEIGHTS:
            delta_w[n], new_m[n], new_v[n] = _adamw(weights[n], grad_w[n], given["m_" + n], given["v_" + n])
    return (loss, grad_x, *[grad_w[n] for n in TWIN_WEIGHTS], *[delta_w[n] for n in TWIN_WEIGHTS],
            *[new_m[n] for n in TWIN_WEIGHTS], *[new_v[n] for n in TWIN_WEIGHTS])
```

```python
import functools
import math

import jax
import jax.numpy as jnp
from jax import lax
from jax.experimental import pallas as pl
from jax.experimental.pallas import tpu as pltpu

F32 = jnp.float32
BF16 = jnp.bfloat16

EPS = 1e-6
SSM_GROUP = 16
STATE = 64
HEAD_DIM = 64
N_STREAMS = 8
MXU_DIM = 256
VMEM_LIMIT = 56 * 1024 * 1024

ADAM_LR = 0.001
ADAM_B1 = 0.9
ADAM_B2 = 0.999
ADAM_EPS = 1e-08
ADAM_WD = 0.01
ADAM_STEP = 10


def _cparams(sem=None):
    return pltpu.CompilerParams(dimension_semantics=sem, vmem_limit_bytes=VMEM_LIMIT)


def _blk(n, want):
    b = min(n, want)
    while n % b:
        b -= 1
    return b


NN = (((1,), (0,)), ((), ()))
NT = (((1,), (1,)), ((), ()))
TN = (((0,), (0,)), ((), ()))


def _matmul(name, a, b, *, a_spec, b_spec, o_spec, out_shape, grid, dn, nk, out_dtype):
    nax = len(grid)

    def body(a_ref, b_ref, o_ref, *scr):
        part = lax.dot_general(a_ref[...], b_ref[...], dn, preferred_element_type=F32)
        if nk == 1:
            o_ref[...] = part.astype(o_ref.dtype)
            return
        acc = scr[0] if scr else o_ref
        k = pl.program_id(nax - 1)

        @pl.when(k == 0)
        def _():
            acc[...] = part

        @pl.when(k > 0)
        def _():
            acc[...] += part

        if scr:
            @pl.when(k == nk - 1)
            def _():
                o_ref[...] = acc[...].astype(o_ref.dtype)

    scratch = []
    if nk > 1 and out_dtype != F32:
        blk = tuple(d for d in o_spec.block_shape if d is not None)
        scratch = [pltpu.VMEM(blk, F32)]
    sem = ("parallel",) * (nax - 1) + ("arbitrary",)
    return pl.pallas_call(
        body, name=name, grid=grid, in_specs=[a_spec, b_spec], out_specs=o_spec,
        out_shape=jax.ShapeDtypeStruct(out_shape, out_dtype), scratch_shapes=scratch,
        compiler_params=_cparams(sem),
    )(a, b)


def mm_nn(name, a, w, out_dtype, tm=512):
    m, k = a.shape
    n = w.shape[1]
    tm = _blk(m, tm)
    return _matmul(name, a, w, a_spec=pl.BlockSpec((tm, k), lambda i, kk: (i, 0)),
                   b_spec=pl.BlockSpec((k, n), lambda i, kk: (0, 0)),
                   o_spec=pl.BlockSpec((tm, n), lambda i, kk: (i, 0)),
                   out_shape=(m, n), grid=(m // tm, 1), dn=NN, nk=1, out_dtype=out_dtype)


def mm_nt(name, a, w, out_dtype, tm=512, tn=None):
    m, n = a.shape
    k = w.shape[0]
    tm = _blk(m, tm)
    tn = k if tn is None else tn
    return _matmul(name, a, w, a_spec=pl.BlockSpec((tm, n), lambda i, j, kk: (i, 0)),
                   b_spec=pl.BlockSpec((tn, n), lambda i, j, kk: (j, 0)),
                   o_spec=pl.BlockSpec((tm, tn), lambda i, j, kk: (i, j)),
                   out_shape=(m, k), grid=(m // tm, k // tn, 1), dn=NT, nk=1, out_dtype=out_dtype)


def mm_tn(name, a, b, tmo=None, tk=512):
    l, m = a.shape
    n = b.shape[1]
    tk = _blk(l, tk)
    tmo = m if tmo is None else tmo
    nk = l // tk
    return _matmul(name, a, b, a_spec=pl.BlockSpec((tk, tmo), lambda i, kk: (kk, i)),
                   b_spec=pl.BlockSpec((tk, n), lambda i, kk: (kk, 0)),
                   o_spec=pl.BlockSpec((tmo, n), lambda i, kk: (i, 0)),
                   out_shape=(m, n), grid=(m // tmo, nk), dn=TN, nk=nk, out_dtype=F32)


def mm_nn_colshard(name, a, w4, out_dtype, tm=512):
    m, k = a.shape
    s, _, ns = w4.shape
    tm = _blk(m, tm)
    return _matmul(name, a, w4, a_spec=pl.BlockSpec((tm, k), lambda i, j, kk: (i, 0)),
                   b_spec=pl.BlockSpec((None, k, ns), lambda i, j, kk: (j, 0, 0)),
                   o_spec=pl.BlockSpec((tm, ns), lambda i, j, kk: (i, j)),
                   out_shape=(m, s * ns), grid=(m // tm, s, 1), dn=NN, nk=1, out_dtype=out_dtype)


def mm_nt_colshard(name, a, w4, out_dtype, tm=512):
    m, _ = a.shape
    s, k, ns = w4.shape
    tm = _blk(m, tm)
    return _matmul(name, a, w4, a_spec=pl.BlockSpec((tm, ns), lambda i, kk: (i, kk)),
                   b_spec=pl.BlockSpec((None, k, ns), lambda i, kk: (kk, 0, 0)),
                   o_spec=pl.BlockSpec((tm, k), lambda i, kk: (i, 0)),
                   out_shape=(m, k), grid=(m // tm, s), dn=NT, nk=s, out_dtype=out_dtype)


def mm_tn_colshard(name, a, b, s, tk=512):
    l, k = a.shape
    ns = b.shape[1] // s
    tk = _blk(l, tk)
    nk = l // tk
    return _matmul(name, a, b, a_spec=pl.BlockSpec((tk, k), lambda j, kk: (kk, 0)),
                   b_spec=pl.BlockSpec((tk, ns), lambda j, kk: (kk, j)),
                   o_spec=pl.BlockSpec((None, k, ns), lambda j, kk: (j, 0, 0)),
                   out_shape=(s, k, ns), grid=(s, nk), dn=TN, nk=nk, out_dtype=F32)


def mm_up(name, h, wup4, layer, tm=512):
    m, k = h.shape
    s, _, _, ns = wup4.shape
    half = s // 2
    tm = _blk(m, tm)
    return _matmul(name, h, wup4, a_spec=pl.BlockSpec((tm, k), lambda i, j, kk: (i, 0)),
                   b_spec=pl.BlockSpec((None, None, k, ns), lambda i, j, kk: (j, layer, 0, 0)),
                   o_spec=pl.BlockSpec((None, tm, ns), lambda i, j, kk: (j // half, i, j % half)),
                   out_shape=(2, m, half * ns), grid=(m // tm, s, 1), dn=NN, nk=1, out_dtype=BF16)


def mm_up_nt(name, dgu, wup4, layer, tm=512):
    _, m, _ = dgu.shape
    s, _, k, ns = wup4.shape
    half = s // 2
    tm = _blk(m, tm)
    return _matmul(name, dgu, wup4,
                   a_spec=pl.BlockSpec((None, tm, ns), lambda i, kk: (kk // half, i, kk % half)),
                   b_spec=pl.BlockSpec((None, None, k, ns), lambda i, kk: (kk, layer, 0, 0)),
                   o_spec=pl.BlockSpec((tm, k), lambda i, kk: (i, 0)),
                   out_shape=(m, k), grid=(m // tm, s), dn=NT, nk=s, out_dtype=F32)


def mm_up_tn(name, h, dgu, s, tk=512):
    l, k = h.shape
    half = s // 2
    ns = dgu.shape[2] // half
    tk = _blk(l, tk)
    nk = l // tk
    return _matmul(name, h, dgu, a_spec=pl.BlockSpec((tk, k), lambda j, kk: (kk, 0)),
                   b_spec=pl.BlockSpec((None, tk, ns), lambda j, kk: (j // half, kk, j % half)),
                   o_spec=pl.BlockSpec((None, k, ns), lambda j, kk: (j, 0, 0)),
                   out_shape=(s, k, ns), grid=(s, nk), dn=TN, nk=nk, out_dtype=F32)


def add_rmsnorm(name, x, r, gains, tm=512):
    l, d = x.shape
    tm = _blk(l, tm)
    ng = len(gains)
    has_r = r is not None

    def body(*refs):
        x_ref = refs[0]
        pos = 1
        xs = x_ref[...]
        if has_r:
            xs = xs + refs[pos][...]
            pos += 1
        g_refs = refs[pos:pos + ng]
        outs = refs[pos + ng:]
        o = 0
        if has_r:
            outs[0][...] = xs
            o = 1
        xh = xs * lax.rsqrt(jnp.mean(xs * xs, axis=-1, keepdims=True) + EPS)
        for gi in range(ng):
            outs[o + gi][...] = (xh * g_refs[gi][...]).astype(BF16)

    row = pl.BlockSpec((tm, d), lambda i: (i, 0))
    gsp = pl.BlockSpec((1, d), lambda i: (0, 0))
    ins = [x] + ([r] if has_r else []) + [g.reshape(1, d) for g in gains]
    in_specs = [row] * (1 + has_r) + [gsp] * ng
    out_shape = ([jax.ShapeDtypeStruct((l, d), F32)] if has_r else []) + [jax.ShapeDtypeStruct((l, d), BF16)] * ng
    return pl.pallas_call(body, name=name, grid=(l // tm,), in_specs=in_specs,
                          out_specs=[row] * len(out_shape), out_shape=out_shape,
                          compiler_params=_cparams(("parallel",)))(*ins)


def norm_bwd(name, x, dres, pairs, tm=512):
    l, d = x.shape
    tm = _blk(l, tm)
    npair = len(pairs)
    has_r = dres is not None

    def body(*refs):
        x_ref = refs[0]
        pos = 1
        xs = x_ref[...]
        dx = jnp.zeros_like(xs)
        if has_r:
            dx = refs[pos][...]
            pos += 1
        ins = refs[pos:pos + 2 * npair]
        outs = refs[pos + 2 * npair:]
        rs = lax.rsqrt(jnp.mean(xs * xs, axis=-1, keepdims=True) + EPS)
        xh = xs * rs
        first = pl.program_id(0) == 0
        for pi in range(npair):
            g = ins[2 * pi][...]
            dh = ins[2 * pi + 1][...].astype(F32)
            dgp = jnp.sum(dh * xh, axis=0, keepdims=True)
            dg_ref = outs[1 + pi]

            @pl.when(first)
            def _():
                dg_ref[...] = dgp

            @pl.when(jnp.logical_not(first))
            def _():
                dg_ref[...] += dgp

            dxh = dh * g
            dx = dx + rs * (dxh - xh * jnp.mean(dxh * xh, axis=-1, keepdims=True))
        outs[0][...] = dx

    row = pl.BlockSpec((tm, d), lambda i: (i, 0))
    gsp = pl.BlockSpec((1, d), lambda i: (0, 0))
    ins = [x] + ([dres] if has_r else [])
    in_specs = [row] * (1 + has_r)
    for g, dh in pairs:
        ins += [g.reshape(1, d), dh]
        in_specs += [gsp, row]
    out_shape = [jax.ShapeDtypeStruct((l, d), F32)] + [jax.ShapeDtypeStruct((1, d), F32)] * npair
    return pl.pallas_call(body, name=name, grid=(l // tm,), in_specs=in_specs,
                          out_specs=[row] + [gsp] * npair, out_shape=out_shape,
                          compiler_params=_cparams(("arbitrary",)))(*ins)


def final_loss(name, x, r, tgt, g, tm=512):
    l, d = x.shape
    tm = _blk(l, tm)

    def body(x_ref, r_ref, t_ref, g_ref, loss_ref, dx_ref, dg_ref):
        xs = x_ref[...] + r_ref[...]
        gg = g_ref[...]
        rs = lax.rsqrt(jnp.mean(xs * xs, axis=-1, keepdims=True) + EPS)
        xh = xs * rs
        e = xh * gg - t_ref[...]
        part = 0.5 * jnp.sum(jnp.mean(e * e, axis=-1, keepdims=True), axis=0, keepdims=True)
        dy = e * (1.0 / d)
        dgp = jnp.sum(dy * xh, axis=0, keepdims=True)
        dxh = dy * gg
        dx_ref[...] = rs * (dxh - xh * jnp.mean(dxh * xh, axis=-1, keepdims=True))
        first = pl.program_id(0) == 0

        @pl.when(first)
        def _():
            loss_ref[...] = jnp.broadcast_to(part, loss_ref.shape)
            dg_ref[...] = dgp

        @pl.when(jnp.logical_not(first))
        def _():
            loss_ref[...] += jnp.broadcast_to(part, loss_ref.shape)
            dg_ref[...] += dgp

    row = pl.BlockSpec((tm, d), lambda i: (i, 0))
    gsp = pl.BlockSpec((1, d), lambda i: (0, 0))
    lsp = pl.BlockSpec((8, 128), lambda i: (0, 0))
    return pl.pallas_call(
        body, name=name, grid=(l // tm,), in_specs=[row, row, row, gsp], out_specs=[lsp, row, gsp],
        out_shape=[jax.ShapeDtypeStruct((8, 128), F32), jax.ShapeDtypeStruct((l, d), F32),
                   jax.ShapeDtypeStruct((1, d), F32)],
        compiler_params=_cparams(("arbitrary",)))(x, r, tgt, g.reshape(1, d))


def _sigmoid(x):
    return 1.0 / (1.0 + jnp.exp(-x))


def glu_fwd(name, z, tm=512):
    l, d2 = z.shape
    d = d2 // 2
    tm = _blk(l, tm)

    def body(z_ref, o_ref):
        o_ref[...] = z_ref[:, :d] * _sigmoid(z_ref[:, d:])

    return pl.pallas_call(body, name=name, grid=(l // tm,),
                          in_specs=[pl.BlockSpec((tm, d2), lambda i: (i, 0))],
                          out_specs=pl.BlockSpec((tm, d), lambda i: (i, 0)),
                          out_shape=jax.ShapeDtypeStruct((l, d), F32),
                          compiler_params=_cparams(("parallel",)))(z)


def glu_bwd(name, z, dm, tm=512):
    l, d2 = z.shape
    d = d2 // 2
    tm = _blk(l, tm)

    def body(z_ref, dm_ref, o_ref):
        sg = _sigmoid(z_ref[:, d:])
        g = dm_ref[...]
        o_ref[:, :d] = (g * sg).astype(BF16)
        o_ref[:, d:] = (g * z_ref[:, :d] * sg * (1.0 - sg)).astype(BF16)

    return pl.pallas_call(body, name=name, grid=(l // tm,),
                          in_specs=[pl.BlockSpec((tm, d2), lambda i: (i, 0)), pl.BlockSpec((tm, d), lambda i: (i, 0))],
                          out_specs=pl.BlockSpec((tm, d2), lambda i: (i, 0)),
                          out_shape=jax.ShapeDtypeStruct((l, d2), BF16),
                          compiler_params=_cparams(("parallel",)))(z, dm)


_GELU_C = math.sqrt(2.0 / math.pi)


def _gelu(y):
    return 0.5 * y * (1.0 + jnp.tanh(_GELU_C * (y + 0.044715 * y * y * y)))


def _gelu_grad(y):
    t = jnp.tanh(_GELU_C * (y + 0.044715 * y * y * y))
    return 0.5 * (1.0 + t) + 0.5 * y * (1.0 - t * t) * _GELU_C * (1.0 + 3.0 * 0.044715 * y * y)


def gelu_bwd(name, ypre, dyg, tm=512):
    l, d = ypre.shape
    tm = _blk(l, tm)

    def body(y_ref, d_ref, o_ref):
        o_ref[...] = (d_ref[...] * _gelu_grad(y_ref[...])).astype(BF16)

    row = pl.BlockSpec((tm, d), lambda i: (i, 0))
    return pl.pallas_call(body, name=name, grid=(l // tm,), in_specs=[row, row], out_specs=row,
                          out_shape=jax.ShapeDtypeStruct((l, d), BF16),
                          compiler_params=_cparams(("parallel",)))(ypre, dyg)


def _shift_down(x, n):
    rolled = pltpu.roll(x, n, axis=0)
    rows = lax.broadcasted_iota(jnp.int32, x.shape, 0)
    return jnp.where(rows >= n, rolled, 0.0)


def _shift_up(x, n):
    l = x.shape[0]
    rolled = pltpu.roll(x, l - n, axis=0)
    rows = lax.broadcasted_iota(jnp.int32, x.shape, 0)
    return jnp.where(rows < l - n, rolled, 0.0)


def ffn_mid_fwd(name, gu, cw, cb, tc=128):
    _, l, f = gu.shape
    tc = _blk(f, tc)

    def body(gu_ref, w_ref, b_ref, a_ref):
        g = gu_ref[0].astype(F32)
        u = gu_ref[1].astype(F32)
        gc = w_ref[0:1, :] * _shift_down(g, 2) + w_ref[1:2, :] * _shift_down(g, 1) + w_ref[2:3, :] * g + b_ref[...]
        a_ref[...] = (gc * _sigmoid(gc) * u).astype(BF16)

    return pl.pallas_call(
        body, name=name, grid=(f // tc,),
        in_specs=[pl.BlockSpec((2, l, tc), lambda c: (0, 0, c)), pl.BlockSpec((3, tc), lambda c: (0, c)),
                  pl.BlockSpec((1, tc), lambda c: (0, c))],
        out_specs=pl.BlockSpec((l, tc), lambda c: (0, c)),
        out_shape=jax.ShapeDtypeStruct((l, f), BF16),
        compiler_params=_cparams(("parallel",)))(gu, cw, cb)


def ffn_mid_bwd(name, gu, da, cw, cb, tc=128):
    _, l, f = gu.shape
    tc = _blk(f, tc)

    def body(gu_ref, da_ref, w_ref, b_ref, dgu_ref, dw_ref, db_ref):
        g = gu_ref[0].astype(F32)
        u = gu_ref[1].astype(F32)
        g1 = _shift_down(g, 1)
        g2 = _shift_down(g, 2)
        w0, w1, w2 = w_ref[0:1, :], w_ref[1:2, :], w_ref[2:3, :]
        gc = w0 * g2 + w1 * g1 + w2 * g + b_ref[...]
        sg = _sigmoid(gc)
        da = da_ref[...].astype(F32)
        dgu_ref[1] = (da * gc * sg).astype(BF16)
        dgc = da * u * (sg * (1.0 + gc * (1.0 - sg)))
        dgu_ref[0] = (w2 * dgc + w1 * _shift_up(dgc, 1) + w0 * _shift_up(dgc, 2)).astype(BF16)
        dw_ref[0:1, :] = jnp.sum(dgc * g2, axis=0, keepdims=True)
        dw_ref[1:2, :] = jnp.sum(dgc * g1, axis=0, keepdims=True)
        dw_ref[2:3, :] = jnp.sum(dgc * g, axis=0, keepdims=True)
        db_ref[...] = jnp.sum(dgc, axis=0, keepdims=True)

    return pl.pallas_call(
        body, name=name, grid=(f // tc,),
        in_specs=[pl.BlockSpec((2, l, tc), lambda c: (0, 0, c)), pl.BlockSpec((l, tc), lambda c: (0, c)),
                  pl.BlockSpec((3, tc), lambda c: (0, c)), pl.BlockSpec((1, tc), lambda c: (0, c))],
        out_specs=[pl.BlockSpec((2, l, tc), lambda c: (0, 0, c)), pl.BlockSpec((3, tc), lambda c: (0, c)),
                   pl.BlockSpec((1, tc), lambda c: (0, c))],
        out_shape=[jax.ShapeDtypeStruct((2, l, f), BF16), jax.ShapeDtypeStruct((3, f), F32),
                   jax.ShapeDtypeStruct((1, f), F32)],
        compiler_params=_cparams(("parallel",)))(gu, da, cw, cb)


SCAN_LANES = 512


def ssm_discretize(name, a_re, a_im, ldt, b_re, b_im):
    dt = jnp.exp(ldt)
    mag = jnp.exp(a_re * dt)
    ab_re = mag * jnp.cos(a_im * dt)
    ab_im = mag * jnp.sin(a_im * dt)
    den = a_re * a_re + a_im * a_im
    f_re = ((ab_re - 1.0) * a_re + ab_im * a_im) / den
    f_im = (ab_im * a_re - (ab_re - 1.0) * a_im) / den
    return ab_re, ab_im, f_re * b_re - f_im * b_im, f_re * b_im + f_im * b_re


def ssm_disc_fwd(name, a_re, a_im, ldt, b_re, b_im):
    def body(ar, ai, ld, br, bi, o_ar, o_ai, o_br, o_bi):
        r = ssm_discretize(None, ar[...], ai[...], ld[...], br[...], bi[...])
        o_ar[...], o_ai[...], o_br[...], o_bi[...] = r

    sd = jax.ShapeDtypeStruct
    return pl.pallas_call(body, name=name,
                          out_shape=[sd(a_re.shape, F32), sd(a_re.shape, F32), sd(b_re.shape, F32), sd(b_re.shape, F32)],
                          compiler_params=_cparams())(a_re, a_im, ldt, b_re, b_im)


def ssm_disc_bwd(name, a_re, a_im, ldt, b_re, b_im, d_ar, d_ai, d_br, d_bi):
    def body(ar, ai, ld, br, bi, g_ar, g_ai, g_br, g_bi, o_ar, o_ai, o_ld, o_br, o_bi):
        fn = functools.partial(ssm_discretize, None)
        _, vjp = jax.vjp(fn, ar[...], ai[...], ld[...], br[...], bi[...])
        r = vjp((g_ar[...], g_ai[...], g_br[...], g_bi[...]))
        o_ar[...], o_ai[...], o_ld[...], o_br[...], o_bi[...] = r

    sd = jax.ShapeDtypeStruct
    return pl.pallas_call(body, name=name,
                          out_shape=[sd(a_re.shape, F32)] * 3 + [sd(b_re.shape, F32)] * 2,
                          compiler_params=_cparams())(a_re, a_im, ldt, b_re, b_im, d_ar, d_ai, d_br, d_bi)


def _drive(x_ref, wre_ref, wim_ref, dre_ref, dim_ref):
    xb = x_ref[...]
    dre_ref[...] = jnp.dot(xb, wre_ref[...], preferred_element_type=F32)
    dim_ref[...] = jnp.dot(xb, wim_ref[...], preferred_element_type=F32)


def _ssm_dims(l, wre, tj):
    nb, kb, nsb = wre.shape
    jn = l // N_STREAMS
    tj = _blk(jn, tj)
    return nb, kb, nsb, jn, tj, tj * N_STREAMS, jn // tj


def _scan(dre_ref, dim_ref, st_re, st_im, a_ref, tj, reverse, write):
    ns = dre_ref.shape[1]
    cw = min(SCAN_LANES, ns)
    for cb in range(ns // cw):
        sl = slice(cb * cw, (cb + 1) * cw)
        ar = jnp.broadcast_to(a_ref[0:1, sl], (N_STREAMS, cw))
        ai = jnp.broadcast_to(a_ref[1:2, sl], (N_STREAMS, cw))

        def step(jj, carry, sl=sl, ar=ar, ai=ai):
            sr, si = carry
            j = (tj - 1 - jj) if reverse else jj
            off = pl.multiple_of(j * N_STREAMS, N_STREAMS)
            nr = ar * sr - ai * si + dre_ref[pl.ds(off, N_STREAMS), sl]
            ni = ar * si + ai * sr + dim_ref[pl.ds(off, N_STREAMS), sl]
            if write:
                dre_ref[pl.ds(off, N_STREAMS), sl] = nr
                dim_ref[pl.ds(off, N_STREAMS), sl] = ni
            return nr, ni

        sr, si = lax.fori_loop(0, tj, step, (st_re[:, sl], st_im[:, sl]))
        st_re[:, sl] = sr
        st_im[:, sl] = si


def ssm_pass1(name, x, wre, wim, a2, *, reverse, tj=64):
    l, d = x.shape
    nb, kb, nsb, jn, tj, r, nblk = _ssm_dims(l, wre, tj)
    ns = nb * nsb

    def body(x_ref, wre_ref, wim_ref, a_ref, cre_ref, cim_ref, dre, dim, st_re, st_im):
        i = pl.program_id(1)

        @pl.when(i == 0)
        def _():
            st_re[...] = jnp.zeros_like(st_re)
            st_im[...] = jnp.zeros_like(st_im)

        _drive(x_ref, wre_ref, wim_ref, dre, dim)
        _scan(dre, dim, st_re, st_im, a_ref, tj, reverse, False)

        @pl.when(i == nblk - 1)
        def _():
            pr, pi = a_ref[0:1, :], a_ref[1:2, :]
            rr, ri = jnp.ones_like(pr), jnp.zeros_like(pr)
            e = jn
            while e:
                if e & 1:
                    rr, ri = rr * pr - ri * pi, rr * pi + ri * pr
                pr, pi = pr * pr - pi * pi, 2.0 * pr * pi
                e >>= 1
            order = range(N_STREAMS - 1, -1, -1) if reverse else range(N_STREAMS)
            cr = jnp.zeros_like(rr)
            ci = jnp.zeros_like(rr)
            for s in order:
                cre_ref[s:s + 1, :] = cr
                cim_ref[s:s + 1, :] = ci
                fr, fi = st_re[s:s + 1, :], st_im[s:s + 1, :]
                cr, ci = fr + rr * cr - ri * ci, fi + rr * ci + ri * cr

    blk = (lambda b, i: (nblk - 1 - i, b)) if reverse else (lambda b, i: (i, b))
    w3 = pl.BlockSpec((None, kb, nsb), lambda b, i: (b, 0, 0))
    st = pl.BlockSpec((N_STREAMS, nsb), lambda b, i: (0, b))
    return pl.pallas_call(
        body, name=name, grid=(nb, nblk),
        in_specs=[pl.BlockSpec((r, kb), blk), w3, w3, pl.BlockSpec((2, nsb), lambda b, i: (0, b))],
        out_specs=[st, st], out_shape=[jax.ShapeDtypeStruct((N_STREAMS, ns), F32)] * 2,
        scratch_shapes=[pltpu.VMEM((r, nsb), F32), pltpu.VMEM((r, nsb), F32),
                        pltpu.VMEM((N_STREAMS, nsb), F32), pltpu.VMEM((N_STREAMS, nsb), F32)],
        compiler_params=_cparams(("parallel", "arbitrary")))(x, wre, wim, a2)


def ssm_fwd2(name, u, bre, bim, a2, init_re, init_im, cre, cim, dskip, *, tj=64):
    l, d = u.shape
    nb, kb, nsb, jn, tj, r, nblk = _ssm_dims(l, bre, tj)
    ns = nb * nsb

    def body(u_ref, bre_ref, bim_ref, a_ref, ire_ref, iim_ref, cre_ref, cim_ref, d_ref,
             sre_ref, sim_ref, y_ref, yg_ref, st_re, st_im):
        @pl.when(pl.program_id(1) == 0)
        def _():
            st_re[...] = ire_ref[...]
            st_im[...] = iim_ref[...]

        _drive(u_ref, bre_ref, bim_ref, sre_ref, sim_ref)
        _scan(sre_ref, sim_ref, st_re, st_im, a_ref, tj, False, True)
        y = (jnp.dot(sre_ref[...].astype(BF16), cre_ref[...], preferred_element_type=F32)
             - jnp.dot(sim_ref[...].astype(BF16), cim_ref[...], preferred_element_type=F32)
             + d_ref[...] * u_ref[...].astype(F32))
        y_ref[...] = y
        yg_ref[...] = _gelu(y).astype(BF16)

    rows = lambda w: pl.BlockSpec((r, w), lambda b, i: (i, b))
    w3 = pl.BlockSpec((None, kb, nsb), lambda b, i: (b, 0, 0))
    c3 = pl.BlockSpec((None, nsb, kb), lambda b, i: (b, 0, 0))
    st = pl.BlockSpec((N_STREAMS, nsb), lambda b, i: (0, b))
    return pl.pallas_call(
        body, name=name, grid=(nb, nblk),
        in_specs=[rows(kb), w3, w3, pl.BlockSpec((2, nsb), lambda b, i: (0, b)), st, st, c3, c3,
                  pl.BlockSpec((1, kb), lambda b, i: (0, b))],
        out_specs=[rows(nsb), rows(nsb), rows(kb), rows(kb)],
        out_shape=[jax.ShapeDtypeStruct((l, ns), F32), jax.ShapeDtypeStruct((l, ns), F32),
                   jax.ShapeDtypeStruct((l, d), F32), jax.ShapeDtypeStruct((l, d), BF16)],
        scratch_shapes=[pltpu.VMEM((N_STREAMS, nsb), F32), pltpu.VMEM((N_STREAMS, nsb), F32)],
        compiler_params=_cparams(("parallel", "arbitrary")))(u, bre, bim, a2, init_re, init_im, cre, cim, dskip)


def ssm_bwd2(name, dy, u, sre, sim, ctre, ctim, a2c, init_re, init_im, fre, fim, bre, bim, dskip, *, tj=64):
    l, d = u.shape
    nb, kb, nsb, jn, tj, r, nblk = _ssm_dims(l, bre, tj)
    ns = nb * nsb

    def body(dy_ref, u_ref, sre_ref, sim_ref, pre_ref, pim_ref, ctre_ref, ctim_ref, a_ref, ire_ref, iim_ref,
             fre_ref, fim_ref, bre_ref, bim_ref, d_ref,
             du_ref, dbre_ref, dbim_ref, dcre_ref, dcim_ref, dare_ref, daim_ref, dd_ref,
             lre, lim, st_re, st_im):
        i = pl.program_id(1)
        first = i == 0

        @pl.when(first)
        def _():
            st_re[...] = ire_ref[...]
            st_im[...] = iim_ref[...]
            dbre_ref[...] = jnp.zeros_like(dbre_ref)
            dbim_ref[...] = jnp.zeros_like(dbim_ref)
            dcre_ref[...] = jnp.zeros_like(dcre_ref)
            dcim_ref[...] = jnp.zeros_like(dcim_ref)
            dare_ref[...] = jnp.zeros_like(dare_ref)
            daim_ref[...] = jnp.zeros_like(daim_ref)
            dd_ref[...] = jnp.zeros_like(dd_ref)

        _drive(dy_ref, ctre_ref, ctim_ref, lre, lim)
        _scan(lre, lim, st_re, st_im, a_ref, tj, True, True)

        is_t0 = i == nblk - 1
        cw = min(SCAN_LANES, nsb)
        for cb in range(nsb // cw):
            sl = slice(cb * cw, (cb + 1) * cw)
            p_r = jnp.where(is_t0, fre_ref[:, sl], pre_ref[:, sl])
            p_i = jnp.where(is_t0, fim_ref[:, sl], pim_ref[:, sl])
            l_r, l_i = lre[0:N_STREAMS, sl], lim[0:N_STREAMS, sl]
            acc = (l_r * p_r + l_i * p_i, l_i * p_r - l_r * p_i)

            def step(jj, carry, sl=sl):
                a_r, a_i = carry
                off = pl.multiple_of(jj * N_STREAMS, N_STREAMS)
                prev = pl.multiple_of((jj - 1) * N_STREAMS, N_STREAMS)
                l_r, l_i = lre[pl.ds(off, N_STREAMS), sl], lim[pl.ds(off, N_STREAMS), sl]
                p_r, p_i = sre_ref[pl.ds(prev, N_STREAMS), sl], sim_ref[pl.ds(prev, N_STREAMS), sl]
                return a_r + l_r * p_r + l_i * p_i, a_i + l_i * p_r - l_r * p_i

            a_r, a_i = lax.fori_loop(1, tj, step, acc)
            dare_ref[:, sl] += a_r
            daim_ref[:, sl] += a_i

        dyf = dy_ref[...].astype(F32)
        uf = u_ref[...].astype(F32)
        dd_ref[...] += jnp.sum(dyf * uf, axis=0, keepdims=True)
        lrb = lre[...].astype(BF16)
        lib = lim[...].astype(BF16)
        ub = u_ref[...]
        dyb = dy_ref[...]
        dbre_ref[...] += lax.dot_general(ub, lrb, TN, preferred_element_type=F32)
        dbim_ref[...] += lax.dot_general(ub, lib, TN, preferred_element_type=F32)
        dcre_ref[...] += lax.dot_general(sre_ref[...].astype(BF16), dyb, TN, preferred_element_type=F32)
        dcim_ref[...] -= lax.dot_general(sim_ref[...].astype(BF16), dyb, TN, preferred_element_type=F32)
        du = (lax.dot_general(lrb, bre_ref[...], NT, preferred_element_type=F32)
              + lax.dot_general(lib, bim_ref[...], NT, preferred_element_type=F32)
              + d_ref[...] * dyf)
        du_ref[...] = du.astype(BF16)

    rev = lambda w: pl.BlockSpec((r, w), lambda b, i: (nblk - 1 - i, b))
    prev_tile = pl.BlockSpec((N_STREAMS, nsb), lambda b, i: (jnp.maximum((nblk - 1 - i) * tj - 1, 0), b))
    st = pl.BlockSpec((N_STREAMS, nsb), lambda b, i: (0, b))
    w3 = pl.BlockSpec((None, kb, nsb), lambda b, i: (b, 0, 0))
    c3 = pl.BlockSpec((None, nsb, kb), lambda b, i: (b, 0, 0))
    dsp = pl.BlockSpec((1, kb), lambda b, i: (0, b))
    return pl.pallas_call(
        body, name=name, grid=(nb, nblk),
        in_specs=[rev(kb), rev(kb), rev(nsb), rev(nsb), prev_tile, prev_tile, w3, w3,
                  pl.BlockSpec((2, nsb), lambda b, i: (0, b)), st, st, st, st, w3, w3, dsp],
        out_specs=[rev(kb), w3, w3, c3, c3, st, st, dsp],
        out_shape=[jax.ShapeDtypeStruct((l, d), BF16), jax.ShapeDtypeStruct((nb, kb, nsb), F32),
                   jax.ShapeDtypeStruct((nb, kb, nsb), F32), jax.ShapeDtypeStruct((nb, nsb, kb), F32),
                   jax.ShapeDtypeStruct((nb, nsb, kb), F32), jax.ShapeDtypeStruct((N_STREAMS, ns), F32),
                   jax.ShapeDtypeStruct((N_STREAMS, ns), F32), jax.ShapeDtypeStruct((1, d), F32)],
        scratch_shapes=[pltpu.VMEM((r, nsb), F32), pltpu.VMEM((r, nsb), F32),
                        pltpu.VMEM((N_STREAMS, nsb), F32), pltpu.VMEM((N_STREAMS, nsb), F32)],
        compiler_params=_cparams(("parallel", "arbitrary")))(
            dy, u, sre, sim, sre, sim, ctre, ctim, a2c, init_re, init_im, fre, fim, bre, bim, dskip)


ATT_TILE = 256


def _split_dot(x, tri):
    hi = x.astype(BF16)
    lo = (x - hi.astype(F32)).astype(BF16)
    return jnp.dot(hi, tri, preferred_element_type=F32) + jnp.dot(lo, tri, preferred_element_type=F32)


def _sb_logs(qs, kj, causal):
    z = lax.dot_general(qs, kj, NT, preferred_element_type=F32)
    sp = jnp.maximum(z, 0.0) + jnp.log(1.0 + jnp.exp(-jnp.abs(z)))
    l1m = -sp
    if causal is not None:
        l1m = jnp.where(causal, l1m, 0.0)
    return z - sp, l1m


def attn_fwd(name, q, k, v, t=ATT_TILE):
    h, l, dh = q.shape
    t = _blk(l, t)
    scale = dh ** -0.5

    def body(q_ref, k_ref, v_ref, o_ref):
        i = pl.program_id(1)
        qs = (q_ref[...].astype(F32) * scale).astype(BF16)
        row = lax.broadcasted_iota(jnp.int32, (t, t), 0)
        col = lax.broadcasted_iota(jnp.int32, (t, t), 1)
        tri_gt = jnp.where(row > col, 1.0, 0.0).astype(BF16)
        causal = col < row

        def tile(jb, acc, run, mask):
            off = pl.multiple_of(jb * t, t)
            lb, l1m = _sb_logs(qs, k_ref[pl.ds(off, t), :], mask)
            rem = _split_dot(l1m, tri_gt)
            w = jnp.exp(lb + rem + run)
            if mask is not None:
                w = jnp.where(mask, w, 0.0)
            acc = acc + jnp.dot(w.astype(BF16), v_ref[pl.ds(off, t), :], preferred_element_type=F32)
            return acc, run + rem[:, 0:1] + l1m[:, 0:1]

        acc, run = tile(i, jnp.zeros((t, dh), F32), jnp.zeros((t, 1), F32), causal)
        acc, run = lax.fori_loop(0, i, lambda it, c: tile(i - 1 - it, c[0], c[1], None), (acc, run))
        o_ref[:, :dh] = acc
        o_ref[:, dh:] = jnp.broadcast_to(run, (t, dh))

    qsp = pl.BlockSpec((None, t, dh), lambda hh, i: (hh, i, 0))
    ksp = pl.BlockSpec((None, l, dh), lambda hh, i: (hh, 0, 0))
    osp = pl.BlockSpec((None, t, 2 * dh), lambda hh, i: (hh, i, 0))
    return pl.pallas_call(body, name=name, grid=(h, l // t), in_specs=[qsp, ksp, ksp], out_specs=osp,
                          out_shape=jax.ShapeDtypeStruct((h, l, 2 * dh), F32),
                          compiler_params=_cparams(("parallel", "parallel")))(q, k, v)


def attn_bwd(name, q, k, v, ot, do, t=ATT_TILE):
    h, l, dh = q.shape
    t = _blk(l, t)
    scale = dh ** -0.5

    def body(q_ref, k_ref, v_ref, ot_ref, do_ref, dq_ref, dk_ref, dv_ref):
        i = pl.program_id(1)

        @pl.when(i == 0)
        def _():
            dk_ref[...] = jnp.zeros_like(dk_ref)
            dv_ref[...] = jnp.zeros_like(dv_ref)

        qs = (q_ref[...].astype(F32) * scale).astype(BF16)
        dob = do_ref[...]
        total = ot_ref[:, dh:dh + 1]
        row = lax.broadcasted_iota(jnp.int32, (t, t), 0)
        col = lax.broadcasted_iota(jnp.int32, (t, t), 1)
        tri_le = jnp.where(row <= col, 1.0, 0.0).astype(BF16)
        tri_lt = jnp.where(row < col, 1.0, 0.0).astype(BF16)
        causal = col < row

        def tile(jb, dq, lrun, prun, mask):
            off = pl.multiple_of(jb * t, t)
            kj = k_ref[pl.ds(off, t), :]
            vj = v_ref[pl.ds(off, t), :]
            lb, l1m = _sb_logs(qs, kj, mask)
            lpre = _split_dot(l1m, tri_le)
            w = jnp.exp(lb + (total - lrun - lpre))
            if mask is not None:
                w = jnp.where(mask, w, 0.0)
            p = w * lax.dot_general(dob, vj, NT, preferred_element_type=F32)
            qpre = prun + _split_dot(p, tri_lt)
            dz = p - jnp.exp(lb) * (p + qpre)
            if mask is not None:
                dz = jnp.where(mask, dz, 0.0)
            dzb = dz.astype(BF16)
            dk_ref[pl.ds(off, t), :] += lax.dot_general(dzb, qs, TN, preferred_element_type=F32)
            dv_ref[pl.ds(off, t), :] += lax.dot_general(w.astype(BF16), dob, TN, preferred_element_type=F32)
            dq = dq + jnp.dot(dzb, kj, preferred_element_type=F32)
            return dq, lrun + lpre[:, t - 1:t], qpre[:, t - 1:t] + p[:, t - 1:t]

        zero = jnp.zeros((t, 1), F32)
        carry = lax.fori_loop(0, i, lambda jb, c: tile(jb, c[0], c[1], c[2], None), (jnp.zeros((t, dh), F32), zero, zero))
        dq, _, _ = tile(i, carry[0], carry[1], carry[2], causal)
        dq_ref[...] = dq * scale

    qsp = pl.BlockSpec((None, t, dh), lambda hh, i: (hh, i, 0))
    ksp = pl.BlockSpec((None, l, dh), lambda hh, i: (hh, 0, 0))
    osp = pl.BlockSpec((None, t, 2 * dh), lambda hh, i: (hh, i, 0))
    sd = jax.ShapeDtypeStruct((h, l, dh), F32)
    return pl.pallas_call(body, name=name, grid=(h, l // t), in_specs=[qsp, ksp, ksp, osp, qsp],
                          out_specs=[qsp, ksp, ksp], out_shape=[sd, sd, sd],
                          compiler_params=_cparams(("parallel", "arbitrary")))(q, k, v, ot, do)


def adamw(name, w, g, m, v):
    shape = w.shape
    cols = shape[-1]
    rows = w.size // cols
    tr = rows
    if rows % 8 == 0:
        tr = 8 * _blk(rows // 8, 64)
    c1 = 1.0 - ADAM_B1 ** ADAM_STEP
    c2 = 1.0 - ADAM_B2 ** ADAM_STEP

    def body(w_ref, g_ref, m_ref, v_ref, d_ref, nm_ref, nv_ref):
        gg = g_ref[...]
        mm = ADAM_B1 * m_ref[...] + (1.0 - ADAM_B1) * gg
        vv = ADAM_B2 * v_ref[...] + (1.0 - ADAM_B2) * (gg * gg)
        nm_ref[...] = mm
        nv_ref[...] = vv
        d_ref[...] = -ADAM_LR * ((mm / c1) / (jnp.sqrt(vv / c2) + ADAM_EPS) + ADAM_WD * w_ref[...])

    sp = pl.BlockSpec((tr, cols), lambda i: (i, 0))
    sd = jax.ShapeDtypeStruct((rows, cols), F32)
    outs = pl.pallas_call(body, name=name, grid=(rows // tr,), in_specs=[sp] * 4, out_specs=[sp] * 3,
                          out_shape=[sd] * 3, compiler_params=_cparams(("parallel",)))(
        w.reshape(rows, cols), g.reshape(rows, cols), m.reshape(rows, cols), v.reshape(rows, cols))
    return tuple(o.reshape(shape) for o in outs)


def _perm(a):
    l, d = a.shape
    return a.reshape(N_STREAMS, l // N_STREAMS, d).transpose(1, 0, 2).reshape(l, d)


def _unperm(a):
    l, d = a.shape
    return a.reshape(l // N_STREAMS, N_STREAMS, d).transpose(1, 0, 2).reshape(l, d)


def _heads(a):
    l, d = a.shape
    return a.reshape(l, d // HEAD_DIM, HEAD_DIM).transpose(1, 0, 2)


def _unheads(a):
    h, l, dh = a.shape
    return a.transpose(1, 0, 2).reshape(l, h * dh)


def _ssm_layouts(d):
    g = d // SSM_GROUP
    gb = MXU_DIM // SSM_GROUP if d >= MXU_DIM else g
    return g, gb, g // gb


def _b_blocks(bb, d):
    g, gb, nb = _ssm_layouts(d)
    b4 = bb.reshape(SSM_GROUP, nb, gb, STATE)
    eye = jnp.eye(gb, dtype=bb.dtype)
    return jnp.einsum('hbqp,gq->bghqp', b4, eye).reshape(nb, gb * SSM_GROUP, gb * STATE)


def _b_unblocks(db, d):
    g, gb, nb = _ssm_layouts(d)
    eye = jnp.eye(gb, dtype=db.dtype)
    return jnp.einsum('bghqp,gq->hbqp', db.reshape(nb, gb, SSM_GROUP, gb, STATE), eye).reshape(SSM_GROUP, g * STATE)


def _c_blocks(c, d):
    g, gb, nb = _ssm_layouts(d)
    eye = jnp.eye(gb, dtype=c.dtype)
    return jnp.einsum('bghp,gq->bqpgh', c.reshape(nb, gb, SSM_GROUP, STATE), eye).reshape(nb, gb * STATE, gb * SSM_GROUP)


def _c_unblocks(dc, d):
    g, gb, nb = _ssm_layouts(d)
    eye = jnp.eye(gb, dtype=dc.dtype)
    return jnp.einsum('bqpgh,gq->bghp', dc.reshape(nb, gb, STATE, gb, SSM_GROUP), eye).reshape(g, SSM_GROUP, STATE)


def _ffn_fwd(tag, x, r, gain, wup4, layer, wdown, cw, cb):
    xs, h = add_rmsnorm(f"norm_ffn{tag}", x, r, [gain])
    gu = mm_up(f"ffn_up{tag}", h, wup4, layer)
    a = ffn_mid_fwd(f"ffn_mid{tag}", gu, cw, cb)
    f = mm_nn(f"ffn_down{tag}", a, wdown, F32)
    return xs, h, gu, a, f


def _ffn_bwd(tag, dxo, xs, h, gu, a, gain, wup4, layer, wdown, cw, cb, s):
    dfb = dxo.astype(BF16)
    f = wdown.shape[0]
    da = mm_nt(f"ffn_down_dx{tag}", dfb, wdown, BF16, tn=f // 2)
    dwdown = mm_tn(f"ffn_down_dw{tag}", a, dfb, tmo=f // 2)
    dgu, dcw, dcb = ffn_mid_bwd(f"ffn_mid_bwd{tag}", gu, da, cw, cb)
    dh = mm_up_nt(f"ffn_up_dx{tag}", dgu, wup4, layer)
    dwup = mm_up_tn(f"ffn_up_dw{tag}", h, dgu, s)
    dxs, dg = norm_bwd(f"norm_ffn_bwd{tag}", xs, dxo, [(gain, dh)])
    return dxs, dg, dwup, dwdown, dcw, dcb


def _local_step(x, tgt, p):
    l, d = x.shape
    s = p["wglu4"].shape[0]
    gr = {}

    a_re, a_im = p["a_re"].reshape(1, -1), p["a_im"].reshape(1, -1)
    ldt = jnp.repeat(p["log_dt"].reshape(-1), STATE).reshape(1, -1)
    bk_re = p["b_re"].transpose(2, 0, 1).reshape(SSM_GROUP, -1)
    bk_im = p["b_im"].transpose(2, 0, 1).reshape(SSM_GROUP, -1)
    ab_re, ab_im, bb_re, bb_im = ssm_disc_fwd("ssm_disc", a_re, a_im, ldt, bk_re, bk_im)
    a2 = jnp.concatenate([ab_re, ab_im], axis=0)
    a2c = jnp.concatenate([ab_re, -ab_im], axis=0)
    bre, bim = _b_blocks(bb_re, d).astype(BF16), _b_blocks(bb_im, d).astype(BF16)
    cre, cim = _c_blocks(p["c_re"], d).astype(BF16), _c_blocks(p["c_im"], d).astype(BF16)
    ctre, ctim = cre.transpose(0, 2, 1), -cim.transpose(0, 2, 1)
    dskip = p["d"].reshape(1, d)

    xp = _perm(x)
    (h0p,) = add_rmsnorm("norm_mix0", xp, None, [p["norm_mix"][0]])
    u = mm_nn("ssm_in", h0p, p["win"], BF16)
    f_re, f_im = ssm_pass1("ssm_fwd1", u, bre, bim, a2, reverse=False)
    s_re, s_im, ypre, yg = ssm_fwd2("ssm_fwd2", u, bre, bim, a2, f_re, f_im, cre, cim, dskip)
    z = mm_nn_colshard("ssm_glu", yg, p["wglu4"], F32)
    mix = _unperm(glu_fwd("glu", z))

    x1, h1, gu0, a0, f0 = _ffn_fwd("0", x, mix, p["norm_ffn"][0], p["wup4"], 0, p["wdown"][0], p["conv_w"][0], p["conv_b"][0:1])
    x2, hk, h2 = add_rmsnorm("norm_kv_mix1", x1, f0, [p["norm_kv"], p["norm_mix"][1]])
    kv = mm_nn_colshard("kv_proj", hk, p["kvw4"], BF16)
    qf = mm_nn("q_proj", h2, p["wq"], BF16)
    dat = qf.shape[1]
    q3, k3, v3 = _heads(qf), _heads(kv[:, :dat]), _heads(kv[:, dat:])
    o3 = attn_fwd("attn", q3, k3, v3)
    ob = _unheads(o3[:, :, :HEAD_DIM]).astype(BF16)
    ao = mm_nn("o_proj", ob, p["wo"], F32)
    x3, h3, gu1, a1, f1 = _ffn_fwd("1", x2, ao, p["norm_ffn"][1], p["wup4"], 1, p["wdown"][1], p["conv_w"][1], p["conv_b"][1:2])
    loss, dx4, dg_final = final_loss("final_loss", x3, f1, tgt, p["norm_final"])
    gr["norm_final"] = dg_final.reshape(d)

    dx3, dg_ffn1, dwup1, dwdown1, dcw1, dcb1 = _ffn_bwd(
        "1", dx4, x3, h3, gu1, a1, p["norm_ffn"][1], p["wup4"], 1, p["wdown"][1], p["conv_w"][1], p["conv_b"][1:2], s)
    dx3b = dx3.astype(BF16)
    do2 = mm_nt("o_proj_dx", dx3b, p["wo"], BF16)
    gr["wo"] = mm_tn("o_proj_dw", ob, dx3b)
    dq3, dk3, dv3 = attn_bwd("attn_bwd", q3, k3, v3, o3, _heads(do2))
    dqf = _unheads(dq3).astype(BF16)
    dkv = jnp.concatenate([_unheads(dk3), _unheads(dv3)], axis=1).astype(BF16)
    dh2 = mm_nt("q_proj_dx", dqf, p["wq"], F32)
    gr["wq"] = mm_tn("q_proj_dw", h2, dqf)
    dhk = mm_nt_colshard("kv_proj_dx", dkv, p["kvw4"], F32)
    gr["kvw4"] = mm_tn_colshard("kv_proj_dw", hk, dkv, s)
    dx2, dg_mix1, dg_kv = norm_bwd("norm_kv_mix1_bwd", x2, dx3, [(p["norm_mix"][1], dh2), (p["norm_kv"], dhk)])
    gr["norm_kv"] = dg_kv.reshape(d)

    dx1, dg_ffn0, dwup0, dwdown0, dcw0, dcb0 = _ffn_bwd(
        "0", dx2, x1, h1, gu0, a0, p["norm_ffn"][0], p["wup4"], 0, p["wdown"][0], p["conv_w"][0], p["conv_b"][0:1], s)
    dx1p = _perm(dx1)
    dz = glu_bwd("glu_bwd", z, dx1p)
    dyg = mm_nt_colshard("ssm_glu_dx", dz, p["wglu4"], F32)
    gr["wglu4"] = mm_tn_colshard("ssm_glu_dw", yg, dz, s)
    dy = gelu_bwd("gelu_bwd", ypre, dyg)
    i_re, i_im = ssm_pass1("ssm_bwd1", dy, ctre, ctim, a2c, reverse=True)
    du, dbre, dbim, dcre, dcim, da_re, da_im, dd = ssm_bwd2(
        "ssm_bwd2", dy, u, s_re, s_im, ctre, ctim, a2c, i_re, i_im, f_re, f_im, bre, bim, dskip)
    dh0p = mm_nt("ssm_in_dx", du, p["win"], F32)
    gr["win"] = mm_tn("ssm_in_dw", h0p, du)
    dxp, dg_mix0 = norm_bwd("norm_mix0_bwd", xp, dx1p, [(p["norm_mix"][0], dh0p)])
    dx = _unperm(dxp)

    g_are, g_aim, g_ldt, g_bre, g_bim = ssm_disc_bwd(
        "ssm_disc_bwd", a_re, a_im, ldt, bk_re, bk_im,
        jnp.sum(da_re, axis=0, keepdims=True), jnp.sum(da_im, axis=0, keepdims=True),
        _b_unblocks(dbre, d), _b_unblocks(dbim, d))
    g = d // SSM_GROUP
    gr["a_re"] = g_are.reshape(g, STATE)
    gr["a_im"] = g_aim.reshape(g, STATE)
    gr["log_dt"] = jnp.sum(g_ldt.reshape(g, STATE), axis=1)
    gr["b_re"] = g_bre.reshape(SSM_GROUP, g, STATE).transpose(1, 2, 0)
    gr["b_im"] = g_bim.reshape(SSM_GROUP, g, STATE).transpose(1, 2, 0)
    gr["c_re"] = _c_unblocks(dcre, d)
    gr["c_im"] = _c_unblocks(dcim, d)
    gr["d"] = dd.reshape(g, SSM_GROUP)
    gr["norm_mix"] = jnp.concatenate([dg_mix0, dg_mix1], axis=0)
    gr["norm_ffn"] = jnp.concatenate([dg_ffn0, dg_ffn1], axis=0)
    gr["conv_w"] = jnp.stack([dcw0, dcw1])
    gr["conv_b"] = jnp.concatenate([dcb0, dcb1], axis=0)
    gr["wup4"] = (dwup0, dwup1)
    gr["wdown"] = (dwdown0, dwdown1)
    return loss, dx, gr


MESH = pl.DeviceIdType.MESH
N_CHIPS = 4
N_DEV = 8
ANY = pl.BlockSpec(memory_space=pl.ANY)


def _pos():
    x, y, c = lax.axis_index("x"), lax.axis_index("y"), lax.axis_index("c")
    return x, y, c, 2 * x + y


def _other_chips(x, y):
    return [(1 - x, y), (x, 1 - y), (1 - x, 1 - y)]


def _remote(src, dst, send_sem, recv_sem, dev):
    return pltpu.make_async_remote_copy(src_ref=src, dst_ref=dst, send_sem=send_sem, recv_sem=recv_sem,
                                        device_id=dev, device_id_type=MESH)


def gather_weights(name, shards):
    n = len(shards)

    def body(*refs):
        ins, outs = refs[:n], refs[n:2 * n]
        send_a, recv_a, send_b, recv_b, loc = refs[2 * n:]
        x, y, c, m = _pos()
        chips = _other_chips(x, y)

        def half(ref, chip, core):
            hr = ref.shape[1] // 2
            return ref.at[chip, pl.ds(core * hr, hr), :]

        pending, local = [], []
        for w in range(n):
            lc = pltpu.make_async_copy(ins[w], outs[w].at[m], loc.at[w])
            lc.start()
            local.append(lc)
            hr = ins[w].shape[0] // 2
            for j, (px, py) in enumerate(chips):
                cp = _remote(ins[w].at[pl.ds(c * hr, hr), :], half(outs[w], m, c), send_a.at[w, j], recv_a.at[w, j], (px, py, c))
                cp.start()
                pending.append(cp)
        for j, (px, py) in enumerate(chips):
            pm = 2 * px + py
            for w in range(n):
                blk = half(outs[w], pm, c)
                _remote(blk, blk, send_a.at[w, j], recv_a.at[w, j], (px, py, c)).wait_recv()
                cp = _remote(blk, blk, send_b.at[w, j], recv_b.at[w, j], (x, y, 1 - c))
                cp.start()
                pending.append(cp)
        for j, (px, py) in enumerate(chips):
            pm = 2 * px + py
            for w in range(n):
                blk = half(outs[w], pm, 1 - c)
                _remote(blk, blk, send_b.at[w, j], recv_b.at[w, j], (x, y, 1 - c)).wait_recv()
        for cp in pending:
            cp.wait_send()
        for lc in local:
            lc.wait()

    sem = pltpu.SemaphoreType.DMA
    return pl.pallas_call(
        body, name=name, in_specs=[ANY] * n, out_specs=[ANY] * n,
        out_shape=[jax.ShapeDtypeStruct((N_CHIPS,) + s.shape, s.dtype) for s in shards],
        scratch_shapes=[sem((n, 3)), sem((n, 3)), sem((n, 3)), sem((n, 3)), sem((n,))],
        compiler_params=pltpu.CompilerParams(has_side_effects=True),
    )(*shards)


def exchange_halves(name, arrs, mine):
    n = len(arrs)

    def body(*refs):
        ins, outs = refs[:n], refs[n:2 * n]
        send, recv = refs[2 * n:]
        x, y, c, _ = _pos()
        sel = c if mine else 1 - c
        cps = []
        for w in range(n):
            hr = ins[w].shape[1] // 2
            cp = _remote(ins[w].at[:, pl.ds(sel * hr, hr), :], outs[w], send.at[w], recv.at[w], (x, y, 1 - c))
            cp.start()
            cps.append(cp)
        for cp in cps:
            cp.wait()

    sem = pltpu.SemaphoreType.DMA
    return pl.pallas_call(
        body, name=name, in_specs=[ANY] * n, out_specs=[ANY] * n,
        out_shape=[jax.ShapeDtypeStruct((a.shape[0], a.shape[1] // 2, a.shape[2]), a.dtype) for a in arrs],
        scratch_shapes=[sem((n,)), sem((n,))],
        compiler_params=pltpu.CompilerParams(has_side_effects=True),
    )(*arrs)


def scatter_to_chips(name, arrs):
    n = len(arrs)

    def body(*refs):
        ins, outs = refs[:n], refs[n:2 * n]
        send, recv, loc = refs[2 * n:]
        x, y, c, m = _pos()
        chips = _other_chips(x, y)
        cps = []
        for w in range(n):
            lc = pltpu.make_async_copy(ins[w].at[m], outs[w].at[m], loc.at[w])
            lc.start()
            cps.append((lc, None))
            for j, (px, py) in enumerate(chips):
                cp = _remote(ins[w].at[2 * px + py], outs[w].at[m], send.at[w, j], recv.at[w, j], (px, py, c))
                cp.start()
                cps.append((cp, outs[w].at[2 * px + py]))
        for cp, landing in cps:
            if landing is None:
                cp.wait()
            else:
                cp.wait_send()
        for w in range(n):
            for j, (px, py) in enumerate(chips):
                blk = outs[w].at[2 * px + py]
                _remote(blk, blk, send.at[w, j], recv.at[w, j], (px, py, c)).wait_recv()

    sem = pltpu.SemaphoreType.DMA
    return pl.pallas_call(
        body, name=name, in_specs=[ANY] * n, out_specs=[ANY] * n,
        out_shape=[jax.ShapeDtypeStruct(a.shape, a.dtype) for a in arrs],
        scratch_shapes=[sem((n, 3)), sem((n, 3)), sem((n,))],
        compiler_params=pltpu.CompilerParams(has_side_effects=True),
    )(*arrs)


def share_halves(name, halves):
    n = len(halves)

    def body(*refs):
        ins, outs = refs[:n], refs[n:2 * n]
        send, recv, loc = refs[2 * n:]
        x, y, c, _ = _pos()
        cps = []
        for w in range(n):
            hr = ins[w].shape[0]
            lc = pltpu.make_async_copy(ins[w], outs[w].at[pl.ds(c * hr, hr), :], loc.at[w])
            lc.start()
            cp = _remote(ins[w], outs[w].at[pl.ds(c * hr, hr), :], send.at[w], recv.at[w], (x, y, 1 - c))
            cp.start()
            cps.append((lc, cp))
        for w, (lc, cp) in enumerate(cps):
            hr = ins[w].shape[0]
            lc.wait()
            cp.wait_send()
            blk = outs[w].at[pl.ds((1 - c) * hr, hr), :]
            _remote(blk, blk, send.at[w], recv.at[w], (x, y, 1 - c)).wait_recv()

    sem = pltpu.SemaphoreType.DMA
    return pl.pallas_call(
        body, name=name, in_specs=[ANY] * n, out_specs=[ANY] * n,
        out_shape=[jax.ShapeDtypeStruct((2 * a.shape[0], a.shape[1]), a.dtype) for a in halves],
        scratch_shapes=[sem((n,)), sem((n,)), sem((n,))],
        compiler_params=pltpu.CompilerParams(has_side_effects=True),
    )(*halves)


def gather_all(name, a):
    def body(a_ref, o_ref, send, recv, loc):
        x, y, c, m = _pos()
        me = 2 * m + c
        lc = pltpu.make_async_copy(a_ref, o_ref.at[me], loc)
        lc.start()
        peers = [(x, y, 1 - c)] + [(px, py, pc) for (px, py) in _other_chips(x, y) for pc in (c, 1 - c)]
        cps = []
        for k, dev in enumerate(peers):
            cp = _remote(a_ref, o_ref.at[me], send.at[k], recv.at[k], dev)
            cp.start()
            cps.append(cp)
        for k, (px, py, pc) in enumerate(peers):
            blk = o_ref.at[4 * px + 2 * py + pc]
            _remote(blk, blk, send.at[k], recv.at[k], (px, py, pc)).wait_recv()
        for cp in cps:
            cp.wait_send()
        lc.wait()

    sem = pltpu.SemaphoreType.DMA
    return pl.pallas_call(
        body, name=name, in_specs=[ANY], out_specs=ANY,
        out_shape=jax.ShapeDtypeStruct((N_DEV,) + a.shape, a.dtype),
        scratch_shapes=[sem((N_DEV - 1,)), sem((N_DEV - 1,)), sem],
        compiler_params=pltpu.CompilerParams(has_side_effects=True),
    )(a)


def add_own_half(name, full, got, core, out_dtype):
    n, r, cdim = full.shape
    hr = r // 2
    tr = 8 * _blk(hr // 8, 32) if out_dtype == F32 else 16 * _blk(hr // 16, 16)
    nbh = hr // tr

    def body(c_ref, f_ref, g_ref, o_ref):
        o_ref[...] = (f_ref[...] + g_ref[...]).astype(o_ref.dtype)

    gs = pltpu.PrefetchScalarGridSpec(
        num_scalar_prefetch=1, grid=(n, nbh),
        in_specs=[pl.BlockSpec((None, tr, cdim), lambda s, i, c_ref: (s, c_ref[0] * nbh + i, 0)),
                  pl.BlockSpec((None, tr, cdim), lambda s, i, c_ref: (s, i, 0))],
        out_specs=pl.BlockSpec((None, tr, cdim), lambda s, i, c_ref: (s, i, 0)))
    return pl.pallas_call(body, name=name, grid_spec=gs, out_shape=jax.ShapeDtypeStruct((n, hr, cdim), out_dtype),
                          compiler_params=_cparams(("parallel", "parallel")))(core, full, got)


def sum_slots(name, a, out_dtype=F32):
    n, r, cdim = a.shape
    unit = 8 if a.dtype == F32 else 16
    tr = unit * _blk(r // unit, 256 // unit)

    def body(a_ref, o_ref):
        acc = a_ref[0].astype(F32)
        for k in range(1, n):
            acc = acc + a_ref[k].astype(F32)
        o_ref[...] = acc.astype(o_ref.dtype)

    return pl.pallas_call(body, name=name, grid=(r // tr,),
                          in_specs=[pl.BlockSpec((n, tr, cdim), lambda i: (0, i, 0))],
                          out_specs=pl.BlockSpec((tr, cdim), lambda i: (i, 0)),
                          out_shape=jax.ShapeDtypeStruct((r, cdim), out_dtype),
                          compiler_params=_cparams(("parallel",)))(a)


WEIGHTS = ('norm_mix', 'norm_ffn', 'norm_kv', 'norm_final', 'ssm_w_in', 'ssm_a_re', 'ssm_a_im', 'ssm_log_dt',
           'ssm_b_re', 'ssm_b_im', 'ssm_c_re', 'ssm_c_im', 'ssm_d', 'ssm_w_glu', 'kv_w', 'attn_w_q', 'attn_w_o',
           'ffn_w_up', 'ffn_conv_w', 'ffn_conv_b', 'ffn_w_down')
SMALL = ('norm_mix', 'norm_ffn', 'norm_kv', 'norm_final', 'ssm_a_re', 'ssm_a_im', 'ssm_log_dt', 'ssm_b_re', 'ssm_b_im',
         'ssm_c_re', 'ssm_c_im', 'ssm_d', 'ffn_conv_w', 'ffn_conv_b')
LANES = 128


def _pad_rows(flat, unit):
    n = flat.shape[0]
    total = -(-n // unit) * unit
    return jnp.pad(flat, (0, total - n)).reshape(total // LANES, LANES)


def kernel(x, norm_mix, norm_ffn, norm_kv, norm_final, ssm_w_in, ssm_a_re, ssm_a_im, ssm_log_dt, ssm_b_re, ssm_b_im, ssm_c_re, ssm_c_im, ssm_d, ssm_w_glu, kv_w, attn_w_q, attn_w_o, ffn_w_up, ffn_conv_w, ffn_conv_b, ffn_w_down, loss_target, m_norm_mix, m_norm_ffn, m_norm_kv, m_norm_final, m_ssm_w_in, m_ssm_a_re, m_ssm_a_im, m_ssm_log_dt, m_ssm_b_re, m_ssm_b_im, m_ssm_c_re, m_ssm_c_im, m_ssm_d, m_ssm_w_glu, m_kv_w, m_attn_w_q, m_attn_w_o, m_ffn_w_up, m_ffn_conv_w, m_ffn_conv_b, m_ffn_w_down, v_norm_mix, v_norm_ffn, v_norm_kv, v_norm_final, v_ssm_w_in, v_ssm_a_re, v_ssm_a_im, v_ssm_log_dt, v_ssm_b_re, v_ssm_b_im, v_ssm_c_re, v_ssm_c_im, v_ssm_d, v_ssm_w_glu, v_kv_w, v_attn_w_q, v_attn_w_o, v_ffn_w_up, v_ffn_conv_w, v_ffn_conv_b, v_ffn_w_down):
    a = dict(locals())
    l, d = x.shape[1], x.shape[2]
    f = ffn_conv_b.shape[1]
    fs = f // N_CHIPS
    m = 2 * lax.axis_index("x") + lax.axis_index("y")
    core = lax.axis_index("c").astype(jnp.int32).reshape(1)

    shards = [ssm_w_in[0].astype(BF16), ssm_w_glu[0].astype(BF16), kv_w.astype(BF16), attn_w_q[0].astype(BF16),
              attn_w_o[0].astype(BF16), ffn_w_up.reshape(-1, ffn_w_up.shape[-1]).astype(BF16),
              ffn_w_down.reshape(-1, d).astype(BF16), _pad_rows(ffn_conv_w.reshape(-1), 16 * LANES)]
    g_in, g_glu, g_kv, g_q, g_o, g_up, g_down, g_cw = gather_weights("gather_weights", shards)
    conv_w = g_cw.reshape(N_CHIPS, -1)[:, :2 * 3 * fs].reshape(N_CHIPS, 2, 3, fs).transpose(1, 2, 0, 3).reshape(2, 3, f)
    p = dict(
        norm_mix=norm_mix, norm_ffn=norm_ffn, norm_kv=norm_kv, norm_final=norm_final,
        a_re=ssm_a_re[0], a_im=ssm_a_im[0], log_dt=ssm_log_dt[0], b_re=ssm_b_re[0], b_im=ssm_b_im[0],
        c_re=ssm_c_re[0], c_im=ssm_c_im[0], d=ssm_d[0],
        win=g_in.reshape(-1, g_in.shape[-1]), wglu4=g_glu, kvw4=g_kv, wq=g_q.reshape(-1, g_q.shape[-1]),
        wo=g_o.reshape(-1, g_o.shape[-1]), wup4=g_up.reshape(N_CHIPS, 2, d, -1),
        wdown=g_down.reshape(N_CHIPS, 2, fs, d).transpose(1, 0, 2, 3).reshape(2, f, d),
        conv_w=conv_w, conv_b=ffn_conv_b)

    loss_slab, dx, gr = _local_step(x[0], loss_target[0], p)

    def chipwise(g):
        return g.reshape(N_CHIPS, g.shape[0] // N_CHIPS, g.shape[1])

    big = [chipwise(gr["win"]), gr["wglu4"], gr["kvw4"], chipwise(gr["wq"]), chipwise(gr["wo"]),
           gr["wup4"][0], gr["wup4"][1], chipwise(gr["wdown"][0]), chipwise(gr["wdown"][1])]
    got = exchange_halves("rs_siblings", big, mine=False)
    part = [add_own_half(f"rs_add_{i}", g, r, core, BF16) for i, (g, r) in enumerate(zip(big, got))]
    slots = scatter_to_chips("rs_chips", part)
    halves = [sum_slots(f"rs_sum_{i}", s) for i, s in enumerate(slots)]
    r_in, r_glu, r_kv, r_q, r_o, r_up0, r_up1, r_down0, r_down1 = share_halves("rs_share", halves)
    grads = {"ssm_w_in": r_in[None], "ssm_w_glu": r_glu[None], "kv_w": r_kv, "attn_w_q": r_q[None], "attn_w_o": r_o[None],
             "ffn_w_up": jnp.stack([r_up0, r_up1]), "ffn_w_down": jnp.stack([r_down0, r_down1])}

    small = {"norm_mix": gr["norm_mix"], "norm_ffn": gr["norm_ffn"], "norm_kv": gr["norm_kv"], "norm_final": gr["norm_final"],
             "ssm_a_re": gr["a_re"], "ssm_a_im": gr["a_im"], "ssm_log_dt": gr["log_dt"], "ssm_b_re": gr["b_re"],
             "ssm_b_im": gr["b_im"], "ssm_c_re": gr["c_re"], "ssm_c_im": gr["c_im"], "ssm_d": gr["d"],
             "ffn_conv_w": gr["conv_w"], "ffn_conv_b": gr["conv_b"]}
    flat = jnp.concatenate([small[k].reshape(-1) for k in SMALL] + [loss_slab[0, 0:1]])
    total = sum_slots("small_sum", gather_all("small_gather", _pad_rows(flat, 8 * LANES))).reshape(-1)
    off = 0
    for k in SMALL:
        n = small[k].size
        full = total[off:off + n].reshape(small[k].shape)
        off += n
        if k == "ffn_conv_w":
            full = lax.dynamic_slice_in_dim(full, m * fs, fs, axis=2)
        grads[k] = full.reshape(a[k].shape)
    loss = total[off]

    outs = {}
    for k in WEIGHTS:
        outs[k] = adamw(f"adamw_{k}", a[k], grads[k], a["m_" + k], a["v_" + k])
    return (loss, dx[None], *[grads[k] for k in WEIGHTS], *[outs[k][0] for k in WEIGHTS],
            *[outs[k][1] for k in WEIGHTS], *[outs[k][2] for k in WEIGHTS])
```

```python
import functools
import math

import jax
import jax.numpy as jnp
from jax import lax
from jax.experimental import pallas as pl
from jax.experimental.pallas import tpu as pltpu

F32 = jnp.float32
BF16 = jnp.bfloat16

EPS = 1e-6
SSM_GROUP = 16
STATE = 64
HEAD_DIM = 64
N_STREAMS = 8
MXU_DIM = 256
VMEM_LIMIT = 56 * 1024 * 1024

ADAM_LR = 0.001
ADAM_B1 = 0.9
ADAM_B2 = 0.999
ADAM_EPS = 1e-08
ADAM_WD = 0.01
ADAM_STEP = 10


def _cparams(sem=None):
    return pltpu.CompilerParams(dimension_semantics=sem, vmem_limit_bytes=VMEM_LIMIT)


def _blk(n, want):
    b = min(n, want)
    while n % b:
        b -= 1
    return b


NN = (((1,), (0,)), ((), ()))
NT = (((1,), (1,)), ((), ()))
TN = (((0,), (0,)), ((), ()))


def _matmul(name, a, b, *, a_spec, b_spec, o_spec, out_shape, grid, dn, nk, out_dtype):
    nax = len(grid)

    def body(a_ref, b_ref, o_ref, *scr):
        part = lax.dot_general(a_ref[...], b_ref[...], dn, preferred_element_type=F32)
        if nk == 1:
            o_ref[...] = part.astype(o_ref.dtype)
            return
        acc = scr[0] if scr else o_ref
        k = pl.program_id(nax - 1)

        @pl.when(k == 0)
        def _():
            acc[...] = part

        @pl.when(k > 0)
        def _():
            acc[...] += part

        if scr:
            @pl.when(k == nk - 1)
            def _():
                o_ref[...] = acc[...].astype(o_ref.dtype)

    scratch = []
    if nk > 1 and out_dtype != F32:
        blk = tuple(d for d in o_spec.block_shape if d is not None)
        scratch = [pltpu.VMEM(blk, F32)]
    sem = ("parallel",) * (nax - 1) + ("arbitrary",)
    return pl.pallas_call(
        body, name=name, grid=grid, in_specs=[a_spec, b_spec], out_specs=o_spec,
        out_shape=jax.ShapeDtypeStruct(out_shape, out_dtype), scratch_shapes=scratch,
        compiler_params=_cparams(sem),
    )(a, b)


def mm_nn(name, a, w, out_dtype, tm=512):
    m, k = a.shape
    n = w.shape[1]
    tm = _blk(m, tm)
    return _matmul(name, a, w, a_spec=pl.BlockSpec((tm, k), lambda i, kk: (i, 0)),
                   b_spec=pl.BlockSpec((k, n), lambda i, kk: (0, 0)),
                   o_spec=pl.BlockSpec((tm, n), lambda i, kk: (i, 0)),
                   out_shape=(m, n), grid=(m // tm, 1), dn=NN, nk=1, out_dtype=out_dtype)


def mm_nt(name, a, w, out_dtype, tm=512, tn=None):
    m, n = a.shape
    k = w.shape[0]
    tm = _blk(m, tm)
    tn = k if tn is None else tn
    return _matmul(name, a, w, a_spec=pl.BlockSpec((tm, n), lambda i, j, kk: (i, 0)),
                   b_spec=pl.BlockSpec((tn, n), lambda i, j, kk: (j, 0)),
                   o_spec=pl.BlockSpec((tm, tn), lambda i, j, kk: (i, j)),
                   out_shape=(m, k), grid=(m // tm, k // tn, 1), dn=NT, nk=1, out_dtype=out_dtype)


def mm_tn(name, a, b, tmo=None, tk=512):
    l, m = a.shape
    n = b.shape[1]
    tk = _blk(l, tk)
    tmo = m if tmo is None else tmo
    nk = l // tk
    return _matmul(name, a, b, a_spec=pl.BlockSpec((tk, tmo), lambda i, kk: (kk, i)),
                   b_spec=pl.BlockSpec((tk, n), lambda i, kk: (kk, 0)),
                   o_spec=pl.BlockSpec((tmo, n), lambda i, kk: (i, 0)),
                   out_shape=(m, n), grid=(m // tmo, nk), dn=TN, nk=nk, out_dtype=F32)


def mm_nn_colshard(name, a, w4, out_dtype, tm=512):
    m, k = a.shape
    s, _, ns = w4.shape
    tm = _blk(m, tm)
    return _matmul(name, a, w4, a_spec=pl.BlockSpec((tm, k), lambda i, j, kk: (i, 0)),
                   b_spec=pl.BlockSpec((None, k, ns), lambda i, j, kk: (j, 0, 0)),
                   o_spec=pl.BlockSpec((tm, ns), lambda i, j, kk: (i, j)),
                   out_shape=(m, s * ns), grid=(m // tm, s, 1), dn=NN, nk=1, out_dtype=out_dtype)


def mm_nt_colshard(name, a, w4, out_dtype, tm=512):
    m, _ = a.shape
    s, k, ns = w4.shape
    tm = _blk(m, tm)
    return _matmul(name, a, w4, a_spec=pl.BlockSpec((tm, ns), lambda i, kk: (i, kk)),
                   b_spec=pl.BlockSpec((None, k, ns), lambda i, kk: (kk, 0, 0)),
                   o_spec=pl.BlockSpec((tm, k), lambda i, kk: (i, 0)),
                   out_shape=(m, k), grid=(m // tm, s), dn=NT, nk=s, out_dtype=out_dtype)


def mm_tn_colshard(name, a, b, s, tk=512):
    l, k = a.shape
    ns = b.shape[1] // s
    tk = _blk(l, tk)
    nk = l // tk
    return _matmul(name, a, b, a_spec=pl.BlockSpec((tk, k), lambda j, kk: (kk, 0)),
                   b_spec=pl.BlockSpec((tk, ns), lambda j, kk: (kk, j)),
                   o_spec=pl.BlockSpec((None, k, ns), lambda j, kk: (j, 0, 0)),
                   out_shape=(s, k, ns), grid=(s, nk), dn=TN, nk=nk, out_dtype=F32)


def mm_up(name, h, wup4, layer, tm=512):
    m, k = h.shape
    s, _, _, ns = wup4.shape
    half = s // 2
    tm = _blk(m, tm)
    return _matmul(name, h, wup4, a_spec=pl.BlockSpec((tm, k), lambda i, j, kk: (i, 0)),
                   b_spec=pl.BlockSpec((None, None, k, ns), lambda i, j, kk: (j, layer, 0, 0)),
                   o_spec=pl.BlockSpec((None, tm, ns), lambda i, j, kk: (j // half, i, j % half)),
                   out_shape=(2, m, half * ns), grid=(m // tm, s, 1), dn=NN, nk=1, out_dtype=BF16)


def mm_up_nt(name, dgu, wup4, layer, tm=512):
    _, m, _ = dgu.shape
    s, _, k, ns = wup4.shape
    half = s // 2
    tm = _blk(m, tm)
    return _matmul(name, dgu, wup4,
                   a_spec=pl.BlockSpec((None, tm, ns), lambda i, kk: (kk // half, i, kk % half)),
                   b_spec=pl.BlockSpec((None, None, k, ns), lambda i, kk: (kk, layer, 0, 0)),
                   o_spec=pl.BlockSpec((tm, k), lambda i, kk: (i, 0)),
                   out_shape=(m, k), grid=(m // tm, s), dn=NT, nk=s, out_dtype=F32)


def mm_up_tn(name, h, dgu, s, tk=512):
    l, k = h.shape
    half = s // 2
    ns = dgu.shape[2] // half
    tk = _blk(l, tk)
    nk = l // tk
    return _matmul(name, h, dgu, a_spec=pl.BlockSpec((tk, k), lambda j, kk: (kk, 0)),
                   b_spec=pl.BlockSpec((None, tk, ns), lambda j, kk: (j // half, kk, j % half)),
                   o_spec=pl.BlockSpec((None, k, ns), lambda j, kk: (j, 0, 0)),
                   out_shape=(s, k, ns), grid=(s, nk), dn=TN, nk=nk, out_dtype=F32)


def add_rmsnorm(name, x, r, gains, tm=512):
    l, d = x.shape
    tm = _blk(l, tm)
    ng = len(gains)
    has_r = r is not None

    def body(*refs):
        x_ref = refs[0]
        pos = 1
        xs = x_ref[...]
        if has_r:
            xs = xs + refs[pos][...]
            pos += 1
        g_refs = refs[pos:pos + ng]
        outs = refs[pos + ng:]
        o = 0
        if has_r:
            outs[0][...] = xs
            o = 1
        xh = xs * lax.rsqrt(jnp.mean(xs * xs, axis=-1, keepdims=True) + EPS)
        for gi in range(ng):
            outs[o + gi][...] = (xh * g_refs[gi][...]).astype(BF16)

    row = pl.BlockSpec((tm, d), lambda i: (i, 0))
    gsp = pl.BlockSpec((1, d), lambda i: (0, 0))
    ins = [x] + ([r] if has_r else []) + [g.reshape(1, d) for g in gains]
    in_specs = [row] * (1 + has_r) + [gsp] * ng
    out_shape = ([jax.ShapeDtypeStruct((l, d), F32)] if has_r else []) + [jax.ShapeDtypeStruct((l, d), BF16)] * ng
    return pl.pallas_call(body, name=name, grid=(l // tm,), in_specs=in_specs,
                          out_specs=[row] * len(out_shape), out_shape=out_shape,
                          compiler_params=_cparams(("parallel",)))(*ins)


def norm_bwd(name, x, dres, pairs, tm=512):
    l, d = x.shape
    tm = _blk(l, tm)
    npair = len(pairs)
    has_r = dres is not None

    def body(*refs):
        x_ref = refs[0]
        pos = 1
        xs = x_ref[...]
        dx = jnp.zeros_like(xs)
        if has_r:
            dx = refs[pos][...]
            pos += 1
        ins = refs[pos:pos + 2 * npair]
        outs = refs[pos + 2 * npair:]
        rs = lax.rsqrt(jnp.mean(xs * xs, axis=-1, keepdims=True) + EPS)
        xh = xs * rs
        first = pl.program_id(0) == 0
        for pi in range(npair):
            g = ins[2 * pi][...]
            dh = ins[2 * pi + 1][...].astype(F32)
            dgp = jnp.sum(dh * xh, axis=0, keepdims=True)
            dg_ref = outs[1 + pi]

            @pl.when(first)
            def _():
                dg_ref[...] = dgp

            @pl.when(jnp.logical_not(first))
            def _():
                dg_ref[...] += dgp

            dxh = dh * g
            dx = dx + rs * (dxh - xh * jnp.mean(dxh * xh, axis=-1, keepdims=True))
        outs[0][...] = dx

    row = pl.BlockSpec((tm, d), lambda i: (i, 0))
    gsp = pl.BlockSpec((1, d), lambda i: (0, 0))
    ins = [x] + ([dres] if has_r else [])
    in_specs = [row] * (1 + has_r)
    for g, dh in pairs:
        ins += [g.reshape(1, d), dh]
        in_specs += [gsp, row]
    out_shape = [jax.ShapeDtypeStruct((l, d), F32)] + [jax.ShapeDtypeStruct((1, d), F32)] * npair
    return pl.pallas_call(body, name=name, grid=(l // tm,), in_specs=in_specs,
                          out_specs=[row] + [gsp] * npair, out_shape=out_shape,
                          compiler_params=_cparams(("arbitrary",)))(*ins)


def final_loss(name, x, r, tgt, g, tm=512):
    l, d = x.shape
    tm = _blk(l, tm)

    def body(x_ref, r_ref, t_ref, g_ref, loss_ref, dx_ref, dg_ref):
        xs = x_ref[...] + r_ref[...]
        gg = g_ref[...]
        rs = lax.rsqrt(jnp.mean(xs * xs, axis=-1, keepdims=True) + EPS)
        xh = xs * rs
        e = xh * gg - t_ref[...]
        part = 0.5 * jnp.sum(jnp.mean(e * e, axis=-1, keepdims=True), axis=0, keepdims=True)
        dy = e * (1.0 / d)
        dgp = jnp.sum(dy * xh, axis=0, keepdims=True)
        dxh = dy * gg
        dx_ref[...] = rs * (dxh - xh * jnp.mean(dxh * xh, axis=-1, keepdims=True))
        first = pl.program_id(0) == 0

        @pl.when(first)
        def _():
            loss_ref[...] = jnp.broadcast_to(part, loss_ref.shape)
            dg_ref[...] = dgp

        @pl.when(jnp.logical_not(first))
        def _():
            loss_ref[...] += jnp.broadcast_to(part, loss_ref.shape)
            dg_ref[...] += dgp

    row = pl.BlockSpec((tm, d), lambda i: (i, 0))
    gsp = pl.BlockSpec((1, d), lambda i: (0, 0))
    lsp = pl.BlockSpec((8, 128), lambda i: (0, 0))
    return pl.pallas_call(
        body, name=name, grid=(l // tm,), in_specs=[row, row, row, gsp], out_specs=[lsp, row, gsp],
        out_shape=[jax.ShapeDtypeStruct((8, 128), F32), jax.ShapeDtypeStruct((l, d), F32),
                   jax.ShapeDtypeStruct((1, d), F32)],
        compiler_params=_cparams(("arbitrary",)))(x, r, tgt, g.reshape(1, d))


def _sigmoid(x):
    return 1.0 / (1.0 + jnp.exp(-x))


def glu_fwd(name, z, tm=512):
    l, d2 = z.shape
    d = d2 // 2
    tm = _blk(l, tm)

    def body(z_ref, o_ref):
        o_ref[...] = z_ref[:, :d] * _sigmoid(z_ref[:, d:])

    return pl.pallas_call(body, name=name, grid=(l // tm,),
                          in_specs=[pl.BlockSpec((tm, d2), lambda i: (i, 0))],
                          out_specs=pl.BlockSpec((tm, d), lambda i: (i, 0)),
                          out_shape=jax.ShapeDtypeStruct((l, d), F32),
                          compiler_params=_cparams(("parallel",)))(z)


def glu_bwd(name, z, dm, tm=512):
    l, d2 = z.shape
    d = d2 // 2
    tm = _blk(l, tm)

    def body(z_ref, dm_ref, o_ref):
        sg = _sigmoid(z_ref[:, d:])
        g = dm_ref[...]
        o_ref[:, :d] = (g * sg).astype(BF16)
        o_ref[:, d:] = (g * z_ref[:, :d] * sg * (1.0 - sg)).astype(BF16)

    return pl.pallas_call(body, name=name, grid=(l // tm,),
                          in_specs=[pl.BlockSpec((tm, d2), lambda i: (i, 0)), pl.BlockSpec((tm, d), lambda i: (i, 0))],
                          out_specs=pl.BlockSpec((tm, d2), lambda i: (i, 0)),
                          out_shape=jax.ShapeDtypeStruct((l, d2), BF16),
                          compiler_params=_cparams(("parallel",)))(z, dm)


_GELU_C = math.sqrt(2.0 / math.pi)


def _gelu(y):
    return 0.5 * y * (1.0 + jnp.tanh(_GELU_C * (y + 0.044715 * y * y * y)))


def _gelu_grad(y):
    t = jnp.tanh(_GELU_C * (y + 0.044715 * y * y * y))
    return 0.5 * (1.0 + t) + 0.5 * y * (1.0 - t * t) * _GELU_C * (1.0 + 3.0 * 0.044715 * y * y)


def gelu_bwd(name, ypre, dyg, tm=512):
    l, d = ypre.shape
    tm = _blk(l, tm)

    def body(y_ref, d_ref, o_ref):
        o_ref[...] = (d_ref[...] * _gelu_grad(y_ref[...])).astype(BF16)

    row = pl.BlockSpec((tm, d), lambda i: (i, 0))
    return pl.pallas_call(body, name=name, grid=(l // tm,), in_specs=[row, row], out_specs=row,
                          out_shape=jax.ShapeDtypeStruct((l, d), BF16),
                          compiler_params=_cparams(("parallel",)))(ypre, dyg)


def _shift_down(x, n):
    rolled = pltpu.roll(x, n, axis=0)
    rows = lax.broadcasted_iota(jnp.int32, x.shape, 0)
    return jnp.where(rows >= n, rolled, 0.0)


def _shift_up(x, n):
    l = x.shape[0]
    rolled = pltpu.roll(x, l - n, axis=0)
    rows = lax.broadcasted_iota(jnp.int32, x.shape, 0)
    return jnp.where(rows < l - n, rolled, 0.0)


def ffn_mid_fwd(name, gu, cw, cb, tc=128):
    _, l, f = gu.shape
    tc = _blk(f, tc)

    def body(gu_ref, w_ref, b_ref, a_ref):
        g = gu_ref[0].astype(F32)
        u = gu_ref[1].astype(F32)
        gc = w_ref[0:1, :] * _shift_down(g, 2) + w_ref[1:2, :] * _shift_down(g, 1) + w_ref[2:3, :] * g + b_ref[...]
        a_ref[...] = (gc * _sigmoid(gc) * u).astype(BF16)

    return pl.pallas_call(
        body, name=name, grid=(f // tc,),
        in_specs=[pl.BlockSpec((2, l, tc), lambda c: (0, 0, c)), pl.BlockSpec((3, tc), lambda c: (0, c)),
                  pl.BlockSpec((1, tc), lambda c: (0, c))],
        out_specs=pl.BlockSpec((l, tc), lambda c: (0, c)),
        out_shape=jax.ShapeDtypeStruct((l, f), BF16),
        compiler_params=_cparams(("parallel",)))(gu, cw, cb)


def ffn_mid_bwd(name, gu, da, cw, cb, tc=128):
    _, l, f = gu.shape
    tc = _blk(f, tc)

    def body(gu_ref, da_ref, w_ref, b_ref, dgu_ref, dw_ref, db_ref):
        g = gu_ref[0].astype(F32)
        u = gu_ref[1].astype(F32)
        g1 = _shift_down(g, 1)
        g2 = _shift_down(g, 2)
        w0, w1, w2 = w_ref[0:1, :], w_ref[1:2, :], w_ref[2:3, :]
        gc = w0 * g2 + w1 * g1 + w2 * g + b_ref[...]
        sg = _sigmoid(gc)
        da = da_ref[...].astype(F32)
        dgu_ref[1] = (da * gc * sg).astype(BF16)
        dgc = da * u * (sg * (1.0 + gc * (1.0 - sg)))
        dgu_ref[0] = (w2 * dgc + w1 * _shift_up(dgc, 1) + w0 * _shift_up(dgc, 2)).astype(BF16)
        dw_ref[0:1, :] = jnp.sum(dgc * g2, axis=0, keepdims=True)
        dw_ref[1:2, :] = jnp.sum(dgc * g1, axis=0, keepdims=True)
        dw_ref[2:3, :] = jnp.sum(dgc * g, axis=0, keepdims=True)
        db_ref[...] = jnp.sum(dgc, axis=0, keepdims=True)

    return pl.pallas_call(
        body, name=name, grid=(f // tc,),
        in_specs=[pl.BlockSpec((2, l, tc), lambda c: (0, 0, c)), pl.BlockSpec((l, tc), lambda c: (0, c)),
                  pl.BlockSpec((3, tc), lambda c: (0, c)), pl.BlockSpec((1, tc), lambda c: (0, c))],
        out_specs=[pl.BlockSpec((2, l, tc), lambda c: (0, 0, c)), pl.BlockSpec((3, tc), lambda c: (0, c)),
                   pl.BlockSpec((1, tc), lambda c: (0, c))],
        out_shape=[jax.ShapeDtypeStruct((2, l, f), BF16), jax.ShapeDtypeStruct((3, f), F32),
                   jax.ShapeDtypeStruct((1, f), F32)],
        compiler_params=_cparams(("parallel",)))(gu, da, cw, cb)


SCAN_LANES = 512


def ssm_discretize(name, a_re, a_im, ldt, b_re, b_im):
    dt = jnp.exp(ldt)
    mag = jnp.exp(a_re * dt)
    ab_re = mag * jnp.cos(a_im * dt)
    ab_im = mag * jnp.sin(a_im * dt)
    den = a_re * a_re + a_im * a_im
    f_re = ((ab_re - 1.0) * a_re + ab_im * a_im) / den
    f_im = (ab_im * a_re - (ab_re - 1.0) * a_im) / den
    return ab_re, ab_im, f_re * b_re - f_im * b_im, f_re * b_im + f_im * b_re


def ssm_disc_fwd(name, a_re, a_im, ldt, b_re, b_im):
    def body(ar, ai, ld, br, bi, o_ar, o_ai, o_br, o_bi):
        r = ssm_discretize(None, ar[...], ai[...], ld[...], br[...], bi[...])
        o_ar[...], o_ai[...], o_br[...], o_bi[...] = r

    sd = jax.ShapeDtypeStruct
    return pl.pallas_call(body, name=name,
                          out_shape=[sd(a_re.shape, F32), sd(a_re.shape, F32), sd(b_re.shape, F32), sd(b_re.shape, F32)],
                          compiler_params=_cparams())(a_re, a_im, ldt, b_re, b_im)


def ssm_disc_bwd(name, a_re, a_im, ldt, b_re, b_im, d_ar, d_ai, d_br, d_bi):
    def body(ar, ai, ld, br, bi, g_ar, g_ai, g_br, g_bi, o_ar, o_ai, o_ld, o_br, o_bi):
        fn = functools.partial(ssm_discretize, None)
        _, vjp = jax.vjp(fn, ar[...], ai[...], ld[...], br[...], bi[...])
        r = vjp((g_ar[...], g_ai[...], g_br[...], g_bi[...]))
        o_ar[...], o_ai[...], o_ld[...], o_br[...], o_bi[...] = r

    sd = jax.ShapeDtypeStruct
    return pl.pallas_call(body, name=name,
                          out_shape=[sd(a_re.shape, F32)] * 3 + [sd(b_re.shape, F32)] * 2,
                          compiler_params=_cparams())(a_re, a_im, ldt, b_re, b_im, d_ar, d_ai, d_br, d_bi)


def _drive(x_ref, wre_ref, wim_ref, dre_ref, dim_ref):
    xb = x_ref[...]
    dre_ref[...] = jnp.dot(xb, wre_ref[...], preferred_element_type=F32)
    dim_ref[...] = jnp.dot(xb, wim_ref[...], preferred_element_type=F32)


def _ssm_dims(l, wre, tj):
    nb, kb, nsb = wre.shape
    jn = l // N_STREAMS
    tj = _blk(jn, tj)
    return nb, kb, nsb, jn, tj, tj * N_STREAMS, jn // tj


def _scan(dre_ref, dim_ref, st_re, st_im, a_ref, tj, reverse, write):
    ns = dre_ref.shape[1]
    cw = min(SCAN_LANES, ns)
    for cb in range(ns // cw):
        sl = slice(cb * cw, (cb + 1) * cw)
        ar = jnp.broadcast_to(a_ref[0:1, sl], (N_STREAMS, cw))
        ai = jnp.broadcast_to(a_ref[1:2, sl], (N_STREAMS, cw))

        def step(jj, carry, sl=sl, ar=ar, ai=ai):
            sr, si = carry
            j = (tj - 1 - jj) if reverse else jj
            off = pl.multiple_of(j * N_STREAMS, N_STREAMS)
            nr = ar * sr - ai * si + dre_ref[pl.ds(off, N_STREAMS), sl]
            ni = ar * si + ai * sr + dim_ref[pl.ds(off, N_STREAMS), sl]
            if write:
                dre_ref[pl.ds(off, N_STREAMS), sl] = nr
                dim_ref[pl.ds(off, N_STREAMS), sl] = ni
            return nr, ni

        sr, si = lax.fori_loop(0, tj, step, (st_re[:, sl], st_im[:, sl]))
        st_re[:, sl] = sr
        st_im[:, sl] = si


def ssm_pass1(name, x, wre, wim, a2, *, reverse, tj=64):
    l, d = x.shape
    nb, kb, nsb, jn, tj, r, nblk = _ssm_dims(l, wre, tj)
    ns = nb * nsb

    def body(x_ref, wre_ref, wim_ref, a_ref, cre_ref, cim_ref, dre, dim, st_re, st_im):
        i = pl.program_id(1)

        @pl.when(i == 0)
        def _():
            st_re[...] = jnp.zeros_like(st_re)
            st_im[...] = jnp.zeros_like(st_im)

        _drive(x_ref, wre_ref, wim_ref, dre, dim)
        _scan(dre, dim, st_re, st_im, a_ref, tj, reverse, False)

        @pl.when(i == nblk - 1)
        def _():
            pr, pi = a_ref[0:1, :], a_ref[1:2, :]
            rr, ri = jnp.ones_like(pr), jnp.zeros_like(pr)
            e = jn
            while e:
                if e & 1:
                    rr, ri = rr * pr - ri * pi, rr * pi + ri * pr
                pr, pi = pr * pr - pi * pi, 2.0 * pr * pi
                e >>= 1
            order = range(N_STREAMS - 1, -1, -1) if reverse else range(N_STREAMS)
            cr = jnp.zeros_like(rr)
            ci = jnp.zeros_like(rr)
            for s in order:
                cre_ref[s:s + 1, :] = cr
                cim_ref[s:s + 1, :] = ci
                fr, fi = st_re[s:s + 1, :], st_im[s:s + 1, :]
                cr, ci = fr + rr * cr - ri * ci, fi + rr * ci + ri * cr

    blk = (lambda b, i: (nblk - 1 - i, b)) if reverse else (lambda b, i: (i, b))
    w3 = pl.BlockSpec((None, kb, nsb), lambda b, i: (b, 0, 0))
    st = pl.BlockSpec((N_STREAMS, nsb), lambda b, i: (0, b))
    return pl.pallas_call(
        body, name=name, grid=(nb, nblk),
        in_specs=[pl.BlockSpec((r, kb), blk), w3, w3, pl.BlockSpec((2, nsb), lambda b, i: (0, b))],
        out_specs=[st, st], out_shape=[jax.ShapeDtypeStruct((N_STREAMS, ns), F32)] * 2,
        scratch_shapes=[pltpu.VMEM((r, nsb), F32), pltpu.VMEM((r, nsb), F32),
                        pltpu.VMEM((N_STREAMS, nsb), F32), pltpu.VMEM((N_STREAMS, nsb), F32)],
        compiler_params=_cparams(("parallel", "arbitrary")))(x, wre, wim, a2)


def ssm_fwd2(name, u, bre, bim, a2, init_re, init_im, cre, cim, dskip, *, tj=64):
    l, d = u.shape
    nb, kb, nsb, jn, tj, r, nblk = _ssm_dims(l, bre, tj)
    ns = nb * nsb

    def body(u_ref, bre_ref, bim_ref, a_ref, ire_ref, iim_ref, cre_ref, cim_ref, d_ref,
             sre_ref, sim_ref, y_ref, yg_ref, st_re, st_im):
        @pl.when(pl.program_id(1) == 0)
        def _():
            st_re[...] = ire_ref[...]
            st_im[...] = iim_ref[...]

        _drive(u_ref, bre_ref, bim_ref, sre_ref, sim_ref)
        _scan(sre_ref, sim_ref, st_re, st_im, a_ref, tj, False, True)
        y = (jnp.dot(sre_ref[...].astype(BF16), cre_ref[...], preferred_element_type=F32)
             - jnp.dot(sim_ref[...].astype(BF16), cim_ref[...], preferred_element_type=F32)
             + d_ref[...] * u_ref[...].astype(F32))
        y_ref[...] = y
        yg_ref[...] = _gelu(y).astype(BF16)

    rows = lambda w: pl.BlockSpec((r, w), lambda b, i: (i, b))
    w3 = pl.BlockSpec((None, kb, nsb), lambda b, i: (b, 0, 0))
    c3 = pl.BlockSpec((None, nsb, kb), lambda b, i: (b, 0, 0))
    st = pl.BlockSpec((N_STREAMS, nsb), lambda b, i: (0, b))
    return pl.pallas_call(
        body, name=name, grid=(nb, nblk),
        in_specs=[rows(kb), w3, w3, pl.BlockSpec((2, nsb), lambda b, i: (0, b)), st, st, c3, c3,
                  pl.BlockSpec((1, kb), lambda b, i: (0, b))],
        out_specs=[rows(nsb), rows(nsb), rows(kb), rows(kb)],
        out_shape=[jax.ShapeDtypeStruct((l, ns), F32), jax.ShapeDtypeStruct((l, ns), F32),
                   jax.ShapeDtypeStruct((l, d), F32), jax.ShapeDtypeStruct((l, d), BF16)],
        scratch_shapes=[pltpu.VMEM((N_STREAMS, nsb), F32), pltpu.VMEM((N_STREAMS, nsb), F32)],
        compiler_params=_cparams(("parallel", "arbitrary")))(u, bre, bim, a2, init_re, init_im, cre, cim, dskip)


def ssm_bwd2(name, dy, u, sre, sim, ctre, ctim, a2c, init_re, init_im, fre, fim, bre, bim, dskip, *, tj=64):
    l, d = u.shape
    nb, kb, nsb, jn, tj, r, nblk = _ssm_dims(l, bre, tj)
    ns = nb * nsb

    def body(dy_ref, u_ref, sre_ref, sim_ref, pre_ref, pim_ref, ctre_ref, ctim_ref, a_ref, ire_ref, iim_ref,
             fre_ref, fim_ref, bre_ref, bim_ref, d_ref,
             du_ref, dbre_ref, dbim_ref, dcre_ref, dcim_ref, dare_ref, daim_ref, dd_ref,
             lre, lim, st_re, st_im):
        i = pl.program_id(1)
        first = i == 0

        @pl.when(first)
        def _():
            st_re[...] = ire_ref[...]
            st_im[...] = iim_ref[...]
            dbre_ref[...] = jnp.zeros_like(dbre_ref)
            dbim_ref[...] = jnp.zeros_like(dbim_ref)
            dcre_ref[...] = jnp.zeros_like(dcre_ref)
            dcim_ref[...] = jnp.zeros_like(dcim_ref)
            dare_ref[...] = jnp.zeros_like(dare_ref)
            daim_ref[...] = jnp.zeros_like(daim_ref)
            dd_ref[...] = jnp.zeros_like(dd_ref)

        _drive(dy_ref, ctre_ref, ctim_ref, lre, lim)
        _scan(lre, lim, st_re, st_im, a_ref, tj, True, True)

        is_t0 = i == nblk - 1
        cw = min(SCAN_LANES, nsb)
        for cb in range(nsb // cw):
            sl = slice(cb * cw, (cb + 1) * cw)
            p_r = jnp.where(is_t0, fre_ref[:, sl], pre_ref[:, sl])
            p_i = jnp.where(is_t0, fim_ref[:, sl], pim_ref[:, sl])
            l_r, l_i = lre[0:N_STREAMS, sl], lim[0:N_STREAMS, sl]
            acc = (l_r * p_r + l_i * p_i, l_i * p_r - l_r * p_i)

            def step(jj, carry, sl=sl):
                a_r, a_i = carry
                off = pl.multiple_of(jj * N_STREAMS, N_STREAMS)
                prev = pl.multiple_of((jj - 1) * N_STREAMS, N_STREAMS)
                l_r, l_i = lre[pl.ds(off, N_STREAMS), sl], lim[pl.ds(off, N_STREAMS), sl]
                p_r, p_i = sre_ref[pl.ds(prev, N_STREAMS), sl], sim_ref[pl.ds(prev, N_STREAMS), sl]
                return a_r + l_r * p_r + l_i * p_i, a_i + l_i * p_r - l_r * p_i

            a_r, a_i = lax.fori_loop(1, tj, step, acc)
            dare_ref[:, sl] += a_r
            daim_ref[:, sl] += a_i

        dyf = dy_ref[...].astype(F32)
        uf = u_ref[...].astype(F32)
        dd_ref[...] += jnp.sum(dyf * uf, axis=0, keepdims=True)
        lrb = lre[...].astype(BF16)
        lib = lim[...].astype(BF16)
        ub = u_ref[...]
        dyb = dy_ref[...]
        dbre_ref[...] += lax.dot_general(ub, lrb, TN, preferred_element_type=F32)
        dbim_ref[...] += lax.dot_general(ub, lib, TN, preferred_element_type=F32)
        dcre_ref[...] += lax.dot_general(sre_ref[...].astype(BF16), dyb, TN, preferred_element_type=F32)
        dcim_ref[...] -= lax.dot_general(sim_ref[...].astype(BF16), dyb, TN, preferred_element_type=F32)
        du = (lax.dot_general(lrb, bre_ref[...], NT, preferred_element_type=F32)
              + lax.dot_general(lib, bim_ref[...], NT, preferred_element_type=F32)
              + d_ref[...] * dyf)
        du_ref[...] = du.astype(BF16)

    rev = lambda w: pl.BlockSpec((r, w), lambda b, i: (nblk - 1 - i, b))
    prev_tile = pl.BlockSpec((N_STREAMS, nsb), lambda b, i: (jnp.maximum((nblk - 1 - i) * tj - 1, 0), b))
    st = pl.BlockSpec((N_STREAMS, nsb), lambda b, i: (0, b))
    w3 = pl.BlockSpec((None, kb, nsb), lambda b, i: (b, 0, 0))
    c3 = pl.BlockSpec((None, nsb, kb), lambda b, i: (b, 0, 0))
    dsp = pl.BlockSpec((1, kb), lambda b, i: (0, b))
    return pl.pallas_call(
        body, name=name, grid=(nb, nblk),
        in_specs=[rev(kb), rev(kb), rev(nsb), rev(nsb), prev_tile, prev_tile, w3, w3,
                  pl.BlockSpec((2, nsb), lambda b, i: (0, b)), st, st, st, st, w3, w3, dsp],
        out_specs=[rev(kb), w3, w3, c3, c3, st, st, dsp],
        out_shape=[jax.ShapeDtypeStruct((l, d), BF16), jax.ShapeDtypeStruct((nb, kb, nsb), F32),
                   jax.ShapeDtypeStruct((nb, kb, nsb), F32), jax.ShapeDtypeStruct((nb, nsb, kb), F32),
                   jax.ShapeDtypeStruct((nb, nsb, kb), F32), jax.ShapeDtypeStruct((N_STREAMS, ns), F32),
                   jax.ShapeDtypeStruct((N_STREAMS, ns), F32), jax.ShapeDtypeStruct((1, d), F32)],
        scratch_shapes=[pltpu.VMEM((r, nsb), F32), pltpu.VMEM((r, nsb), F32),
                        pltpu.VMEM((N_STREAMS, nsb), F32), pltpu.VMEM((N_STREAMS, nsb), F32)],
        compiler_params=_cparams(("parallel", "arbitrary")))(
            dy, u, sre, sim, sre, sim, ctre, ctim, a2c, init_re, init_im, fre, fim, bre, bim, dskip)


ATT_TILE = 256
ATT_HEADS = 4
LANES = 128


def _split_dot(x, tri):
    hi = x.astype(BF16)
    lo = (x - hi.astype(F32)).astype(BF16)
    return jnp.dot(hi, tri, preferred_element_type=F32) + jnp.dot(lo, tri, preferred_element_type=F32)


def _sb_logs(z, causal):
    sp = jnp.maximum(z, 0.0) + jnp.log(1.0 + jnp.exp(-jnp.abs(z)))
    l1m = -sp
    if causal is not None:
        l1m = jnp.where(causal, l1m, 0.0)
    return z - sp, l1m


def attn_fwd(name, q, kv, t=ATT_TILE):
    l, dm = q.shape
    dh = HEAD_DIM
    h = dm // dh
    t = _blk(l, t)
    hb = _blk(h, ATT_HEADS)
    wb = hb * dh
    scale = dh ** -0.5

    def body(q_ref, k_ref, v_ref, o_ref, tot_ref):
        i = pl.program_id(1)
        hd = lambda g: slice(g * dh, (g + 1) * dh)
        qs = [(q_ref[:, hd(g)].astype(F32) * scale).astype(BF16) for g in range(hb)]
        row = lax.broadcasted_iota(jnp.int32, (t, t), 0)
        col = lax.broadcasted_iota(jnp.int32, (t, t), 1)
        tri_gt = jnp.where(row > col, 1.0, 0.0).astype(BF16)
        causal = col < row

        def tile(jb, carry, mask):
            off = pl.multiple_of(jb * t, t)
            heads = range(hb)
            z = [lax.dot_general(qs[g], k_ref[pl.ds(off, t), hd(g)], NT, preferred_element_type=F32) for g in heads]
            logs = [_sb_logs(z[g], mask) for g in heads]
            rem = [_split_dot(logs[g][1], tri_gt) for g in heads]
            w = [jnp.exp(logs[g][0] + rem[g] + carry[g][1]) for g in heads]
            if mask is not None:
                w = [jnp.where(mask, w[g], 0.0) for g in heads]
            pv = [jnp.dot(w[g].astype(BF16), v_ref[pl.ds(off, t), hd(g)], preferred_element_type=F32) for g in heads]
            return tuple((carry[g][0] + pv[g], carry[g][1] + rem[g][:, 0:1] + logs[g][1][:, 0:1]) for g in heads)

        carry = tile(i, ((jnp.zeros((t, dh), F32), jnp.zeros((t, 1), F32)),) * hb, causal)
        carry = lax.fori_loop(0, i, lambda it, c: tile(i - 1 - it, c, None), carry)
        tot_ref[...] = jnp.zeros_like(tot_ref)
        for g, (acc, run) in enumerate(carry):
            o_ref[:, hd(g)] = acc.astype(BF16)
            tot_ref[:, g:g + 1] = run

    qsp = pl.BlockSpec((t, wb), lambda hh, i: (i, hh))
    ksp = pl.BlockSpec((None, l, wb), lambda hh, i: (0, 0, hh))
    vsp = pl.BlockSpec((None, l, wb), lambda hh, i: (1, 0, hh))
    tsp = pl.BlockSpec((None, t, LANES), lambda hh, i: (hh, i, 0))
    return pl.pallas_call(body, name=name, grid=(h // hb, l // t), in_specs=[qsp, ksp, vsp], out_specs=[qsp, tsp],
                          out_shape=[jax.ShapeDtypeStruct((l, dm), BF16), jax.ShapeDtypeStruct((h // hb, l, LANES), F32)],
                          compiler_params=_cparams(("parallel", "parallel")))(q, kv, kv)


def attn_bwd(name, q, kv, tot, do, t=ATT_TILE):
    l, dm = q.shape
    dh = HEAD_DIM
    h = dm // dh
    t = _blk(l, t)
    hb = _blk(h, ATT_HEADS)
    wb = hb * dh
    nq = l // t
    scale = dh ** -0.5

    def body(q_ref, k_ref, v_ref, tot_ref, do_ref, dq_ref, dkv_ref, dk_ref, dv_ref):
        i = pl.program_id(1)
        hd = lambda g: slice(g * dh, (g + 1) * dh)

        @pl.when(i == 0)
        def _():
            dk_ref[...] = jnp.zeros_like(dk_ref)
            dv_ref[...] = jnp.zeros_like(dv_ref)

        qs = [(q_ref[:, hd(g)].astype(F32) * scale).astype(BF16) for g in range(hb)]
        dob = [do_ref[:, hd(g)] for g in range(hb)]
        total = [tot_ref[:, g:g + 1] for g in range(hb)]
        row = lax.broadcasted_iota(jnp.int32, (t, t), 0)
        col = lax.broadcasted_iota(jnp.int32, (t, t), 1)
        tri_le = jnp.where(row <= col, 1.0, 0.0).astype(BF16)
        tri_lt = jnp.where(row < col, 1.0, 0.0).astype(BF16)
        causal = col < row

        def tile(jb, carry, mask):
            off = pl.multiple_of(jb * t, t)
            heads = range(hb)
            kj = [k_ref[pl.ds(off, t), hd(g)] for g in heads]
            z = [lax.dot_general(qs[g], kj[g], NT, preferred_element_type=F32) for g in heads]
            dp = [lax.dot_general(dob[g], v_ref[pl.ds(off, t), hd(g)], NT, preferred_element_type=F32) for g in heads]
            logs = [_sb_logs(z[g], mask) for g in heads]
            lpre = [_split_dot(logs[g][1], tri_le) for g in heads]
            w = [jnp.exp(logs[g][0] + (total[g] - carry[g][1] - lpre[g])) for g in heads]
            if mask is not None:
                w = [jnp.where(mask, w[g], 0.0) for g in heads]
            p = [w[g] * dp[g] for g in heads]
            for g in heads:
                dv_ref[pl.ds(off, t), hd(g)] += lax.dot_general(w[g].astype(BF16), dob[g], TN, preferred_element_type=F32)
            qpre = [carry[g][2] + jnp.dot(p[g].astype(BF16), tri_lt, preferred_element_type=F32) for g in heads]
            dz = [p[g] - jnp.exp(logs[g][0]) * (p[g] + qpre[g]) for g in heads]
            if mask is not None:
                dz = [jnp.where(mask, dz[g], 0.0) for g in heads]
            dzb = [dz[g].astype(BF16) for g in heads]
            for g in heads:
                dk_ref[pl.ds(off, t), hd(g)] += lax.dot_general(dzb[g], qs[g], TN, preferred_element_type=F32)
            dq = [carry[g][0] + jnp.dot(dzb[g], kj[g], preferred_element_type=F32) for g in heads]
            return tuple((dq[g], carry[g][1] + lpre[g][:, t - 1:t], qpre[g][:, t - 1:t] + p[g][:, t - 1:t]) for g in heads)

        zero = jnp.zeros((t, 1), F32)
        carry = lax.fori_loop(0, i, lambda jb, c: tile(jb, c, None), ((jnp.zeros((t, dh), F32), zero, zero),) * hb)
        carry = tile(i, carry, causal)
        for g in range(hb):
            dq_ref[:, hd(g)] = (carry[g][0] * scale).astype(BF16)

        @pl.when(i == nq - 1)
        def _():
            dkv_ref[0] = dk_ref[...].astype(BF16)
            dkv_ref[1] = dv_ref[...].astype(BF16)

    qsp = pl.BlockSpec((t, wb), lambda hh, i: (i, hh))
    ksp = pl.BlockSpec((None, l, wb), lambda hh, i: (0, 0, hh))
    vsp = pl.BlockSpec((None, l, wb), lambda hh, i: (1, 0, hh))
    tsp = pl.BlockSpec((None, t, LANES), lambda hh, i: (hh, i, 0))
    return pl.pallas_call(body, name=name, grid=(h // hb, nq), in_specs=[qsp, ksp, vsp, tsp, qsp],
                          out_specs=[qsp, pl.BlockSpec((2, l, wb), lambda hh, i: (0, 0, hh))],
                          out_shape=[jax.ShapeDtypeStruct((l, dm), BF16), jax.ShapeDtypeStruct((2, l, dm), BF16)],
                          scratch_shapes=[pltpu.VMEM((l, wb), F32), pltpu.VMEM((l, wb), F32)],
                          compiler_params=_cparams(("parallel", "arbitrary")))(q, kv, kv, tot, do)


def adamw(name, w, g, m, v):
    shape = w.shape
    cols = shape[-1]
    rows = w.size // cols
    tr = rows
    if rows % 8 == 0:
        tr = 8 * _blk(rows // 8, 64)
    c1 = 1.0 - ADAM_B1 ** ADAM_STEP
    c2 = 1.0 - ADAM_B2 ** ADAM_STEP

    def body(w_ref, g_ref, m_ref, v_ref, d_ref, nm_ref, nv_ref):
        gg = g_ref[...]
        mm = ADAM_B1 * m_ref[...] + (1.0 - ADAM_B1) * gg
        vv = ADAM_B2 * v_ref[...] + (1.0 - ADAM_B2) * (gg * gg)
        nm_ref[...] = mm
        nv_ref[...] = vv
        d_ref[...] = -ADAM_LR * ((mm / c1) / (jnp.sqrt(vv / c2) + ADAM_EPS) + ADAM_WD * w_ref[...])

    sp = pl.BlockSpec((tr, cols), lambda i: (i, 0))
    sd = jax.ShapeDtypeStruct((rows, cols), F32)
    outs = pl.pallas_call(body, name=name, grid=(rows // tr,), in_specs=[sp] * 4, out_specs=[sp] * 3,
                          out_shape=[sd] * 3, compiler_params=_cparams(("parallel",)))(
        w.reshape(rows, cols), g.reshape(rows, cols), m.reshape(rows, cols), v.reshape(rows, cols))
    return tuple(o.reshape(shape) for o in outs)


def _perm(a):
    l, d = a.shape
    return a.reshape(N_STREAMS, l // N_STREAMS, d).transpose(1, 0, 2).reshape(l, d)


def _unperm(a):
    l, d = a.shape
    return a.reshape(l // N_STREAMS, N_STREAMS, d).transpose(1, 0, 2).reshape(l, d)


def _heads(a):
    l, d = a.shape
    return a.reshape(l, d // HEAD_DIM, HEAD_DIM).transpose(1, 0, 2)


def _unheads(a):
    h, l, dh = a.shape
    return a.transpose(1, 0, 2).reshape(l, h * dh)


def _ssm_layouts(d):
    g = d // SSM_GROUP
    gb = MXU_DIM // SSM_GROUP if d >= MXU_DIM else g
    return g, gb, g // gb


def _b_blocks(bb, d):
    g, gb, nb = _ssm_layouts(d)
    b4 = bb.reshape(SSM_GROUP, nb, gb, STATE)
    eye = jnp.eye(gb, dtype=bb.dtype)
    return jnp.einsum('hbqp,gq->bghqp', b4, eye).reshape(nb, gb * SSM_GROUP, gb * STATE)


def _b_unblocks(db, d):
    g, gb, nb = _ssm_layouts(d)
    eye = jnp.eye(gb, dtype=db.dtype)
    return jnp.einsum('bghqp,gq->hbqp', db.reshape(nb, gb, SSM_GROUP, gb, STATE), eye).reshape(SSM_GROUP, g * STATE)


def _c_blocks(c, d):
    g, gb, nb = _ssm_layouts(d)
    eye = jnp.eye(gb, dtype=c.dtype)
    return jnp.einsum('bghp,gq->bqpgh', c.reshape(nb, gb, SSM_GROUP, STATE), eye).reshape(nb, gb * STATE, gb * SSM_GROUP)


def _c_unblocks(dc, d):
    g, gb, nb = _ssm_layouts(d)
    eye = jnp.eye(gb, dtype=dc.dtype)
    return jnp.einsum('bqpgh,gq->bghp', dc.reshape(nb, gb, STATE, gb, SSM_GROUP), eye).reshape(g, SSM_GROUP, STATE)


def _ffn_fwd(tag, x, r, gain, wup4, layer, wdown, cw, cb):
    xs, h = add_rmsnorm(f"norm_ffn{tag}", x, r, [gain])
    gu = mm_up(f"ffn_up{tag}", h, wup4, layer)
    a = ffn_mid_fwd(f"ffn_mid{tag}", gu, cw, cb)
    f = mm_nn(f"ffn_down{tag}", a, wdown, F32)
    return xs, h, gu, a, f


def _ffn_bwd(tag, dxo, xs, h, gu, a, gain, wup4, layer, wdown, cw, cb, s):
    dfb = dxo.astype(BF16)
    f = wdown.shape[0]
    da = mm_nt(f"ffn_down_dx{tag}", dfb, wdown, BF16, tn=f // 2)
    dwdown = mm_tn(f"ffn_down_dw{tag}", a, dfb, tmo=f // 2)
    dgu, dcw, dcb = ffn_mid_bwd(f"ffn_mid_bwd{tag}", gu, da, cw, cb)
    dh = mm_up_nt(f"ffn_up_dx{tag}", dgu, wup4, layer)
    dwup = mm_up_tn(f"ffn_up_dw{tag}", h, dgu, s)
    dxs, dg = norm_bwd(f"norm_ffn_bwd{tag}", xs, dxo, [(gain, dh)])
    return dxs, dg, dwup, dwdown, dcw, dcb


def _local_step(x, tgt, p):
    l, d = x.shape
    s = p["wglu4"].shape[0]
    gr = {}

    a_re, a_im = p["a_re"].reshape(1, -1), p["a_im"].reshape(1, -1)
    ldt = jnp.repeat(p["log_dt"].reshape(-1), STATE).reshape(1, -1)
    bk_re = p["b_re"].transpose(2, 0, 1).reshape(SSM_GROUP, -1)
    bk_im = p["b_im"].transpose(2, 0, 1).reshape(SSM_GROUP, -1)
    ab_re, ab_im, bb_re, bb_im = ssm_disc_fwd("ssm_disc", a_re, a_im, ldt, bk_re, bk_im)
    a2 = jnp.concatenate([ab_re, ab_im], axis=0)
    a2c = jnp.concatenate([ab_re, -ab_im], axis=0)
    bre, bim = _b_blocks(bb_re, d).astype(BF16), _b_blocks(bb_im, d).astype(BF16)
    cre, cim = _c_blocks(p["c_re"], d).astype(BF16), _c_blocks(p["c_im"], d).astype(BF16)
    ctre, ctim = cre.transpose(0, 2, 1), -cim.transpose(0, 2, 1)
    dskip = p["d"].reshape(1, d)

    xp = _perm(x)
    (h0p,) = add_rmsnorm("norm_mix0", xp, None, [p["norm_mix"][0]])
    u = mm_nn("ssm_in", h0p, p["win"], BF16)
    f_re, f_im = ssm_pass1("ssm_fwd1", u, bre, bim, a2, reverse=False)
    s_re, s_im, ypre, yg = ssm_fwd2("ssm_fwd2", u, bre, bim, a2, f_re, f_im, cre, cim, dskip)
    z = mm_nn_colshard("ssm_glu", yg, p["wglu4"], F32)
    mix = _unperm(glu_fwd("glu", z))

    x1, h1, gu0, a0, f0 = _ffn_fwd("0", x, mix, p["norm_ffn"][0], p["wup4"], 0, p["wdown"][0], p["conv_w"][0], p["conv_b"][0:1])
    x2, hk, h2 = add_rmsnorm("norm_kv_mix1", x1, f0, [p["norm_kv"], p["norm_mix"][1]])
    kvw = p["kvw4"][:, None]
    kv = mm_up("kv_proj", hk, kvw, 0)
    qf = mm_nn("q_proj", h2, p["wq"], BF16)
    ob, tot = attn_fwd("attn", qf, kv)
    ao = mm_nn("o_proj", ob, p["wo"], F32)
    x3, h3, gu1, a1, f1 = _ffn_fwd("1", x2, ao, p["norm_ffn"][1], p["wup4"], 1, p["wdown"][1], p["conv_w"][1], p["conv_b"][1:2])
    loss, dx4, dg_final = final_loss("final_loss", x3, f1, tgt, p["norm_final"])
    gr["norm_final"] = dg_final.reshape(d)

    dx3, dg_ffn1, dwup1, dwdown1, dcw1, dcb1 = _ffn_bwd(
        "1", dx4, x3, h3, gu1, a1, p["norm_ffn"][1], p["wup4"], 1, p["wdown"][1], p["conv_w"][1], p["conv_b"][1:2], s)
    dx3b = dx3.astype(BF16)
    do2 = mm_nt("o_proj_dx", dx3b, p["wo"], BF16)
    gr["wo"] = mm_tn("o_proj_dw", ob, dx3b)
    dqf, dkv = attn_bwd("attn_bwd", qf, kv, tot, do2)
    dh2 = mm_nt("q_proj_dx", dqf, p["wq"], F32)
    gr["wq"] = mm_tn("q_proj_dw", h2, dqf)
    dhk = mm_up_nt("kv_proj_dx", dkv, kvw, 0)
    gr["kvw4"] = mm_up_tn("kv_proj_dw", hk, dkv, s)
    dx2, dg_mix1, dg_kv = norm_bwd("norm_kv_mix1_bwd", x2, dx3, [(p["norm_mix"][1], dh2), (p["norm_kv"], dhk)])
    gr["norm_kv"] = dg_kv.reshape(d)

    dx1, dg_ffn0, dwup0, dwdown0, dcw0, dcb0 = _ffn_bwd(
        "0", dx2, x1, h1, gu0, a0, p["norm_ffn"][0], p["wup4"], 0, p["wdown"][0], p["conv_w"][0], p["conv_b"][0:1], s)
    dx1p = _perm(dx1)
    dz = glu_bwd("glu_bwd", z, dx1p)
    dyg = mm_nt_colshard("ssm_glu_dx", dz, p["wglu4"], F32)
    gr["wglu4"] = mm_tn_colshard("ssm_glu_dw", yg, dz, s)
    dy = gelu_bwd("gelu_bwd", ypre, dyg)
    i_re, i_im = ssm_pass1("ssm_bwd1", dy, ctre, ctim, a2c, reverse=True)
    du, dbre, dbim, dcre, dcim, da_re, da_im, dd = ssm_bwd2(
        "ssm_bwd2", dy, u, s_re, s_im, ctre, ctim, a2c, i_re, i_im, f_re, f_im, bre, bim, dskip)
    dh0p = mm_nt("ssm_in_dx", du, p["win"], F32)
    gr["win"] = mm_tn("ssm_in_dw", h0p, du)
    dxp, dg_mix0 = norm_bwd("norm_mix0_bwd", xp, dx1p, [(p["norm_mix"][0], dh0p)])
    dx = _unperm(dxp)

    g_are, g_aim, g_ldt, g_bre, g_bim = ssm_disc_bwd(
        "ssm_disc_bwd", a_re, a_im, ldt, bk_re, bk_im,
        jnp.sum(da_re, axis=0, keepdims=True), jnp.sum(da_im, axis=0, keepdims=True),
        _b_unblocks(dbre, d), _b_unblocks(dbim, d))
    g = d // SSM_GROUP
    gr["a_re"] = g_are.reshape(g, STATE)
    gr["a_im"] = g_aim.reshape(g, STATE)
    gr["log_dt"] = jnp.sum(g_ldt.reshape(g, STATE), axis=1)
    gr["b_re"] = g_bre.reshape(SSM_GROUP, g, STATE).transpose(1, 2, 0)
    gr["b_im"] = g_bim.reshape(SSM_GROUP, g, STATE).transpose(1, 2, 0)
    gr["c_re"] = _c_unblocks(dcre, d)
    gr["c_im"] = _c_unblocks(dcim, d)
    gr["d"] = dd.reshape(g, SSM_GROUP)
    gr["norm_mix"] = jnp.concatenate([dg_mix0, dg_mix1], axis=0)
    gr["norm_ffn"] = jnp.concatenate([dg_ffn0, dg_ffn1], axis=0)
    gr["conv_w"] = jnp.stack([dcw0, dcw1])
    gr["conv_b"] = jnp.concatenate([dcb0, dcb1], axis=0)
    gr["wup4"] = (dwup0, dwup1)
    gr["wdown"] = (dwdown0, dwdown1)
    return loss, dx, gr


MESH = pl.DeviceIdType.MESH
N_CHIPS = 4
N_DEV = 8
ANY = pl.BlockSpec(memory_space=pl.ANY)


def _pos():
    x, y, c = lax.axis_index("x"), lax.axis_index("y"), lax.axis_index("c")
    return x, y, c, 2 * x + y


def _other_chips(x, y):
    return [(1 - x, y), (x, 1 - y), (1 - x, 1 - y)]


def _remote(src, dst, send_sem, recv_sem, dev):
    return pltpu.make_async_remote_copy(src_ref=src, dst_ref=dst, send_sem=send_sem, recv_sem=recv_sem,
                                        device_id=dev, device_id_type=MESH)


def cast_into_slot(name, a, chip, dtype):
    r, cdim = a.shape
    tr = 16 * _blk(r // 16, 32) if r % 16 == 0 else r

    def body(m_ref, a_ref, o_ref):
        o_ref[...] = a_ref[...].astype(o_ref.dtype)

    gs = pltpu.PrefetchScalarGridSpec(
        num_scalar_prefetch=1, grid=(r // tr,),
        in_specs=[pl.BlockSpec((tr, cdim), lambda i, m_ref: (i, 0))],
        out_specs=pl.BlockSpec((None, tr, cdim), lambda i, m_ref: (m_ref[0], i, 0)))
    return pl.pallas_call(body, name=name, grid_spec=gs, out_shape=jax.ShapeDtypeStruct((N_CHIPS, r, cdim), dtype),
                          compiler_params=_cparams(("parallel",)))(chip, a)


def gather_weights(name, bufs):
    n = len(bufs)

    def body(*refs):
        outs = refs[n:2 * n]
        send_a, recv_a, send_b, recv_b = refs[2 * n:]
        x, y, c, m = _pos()
        chips = _other_chips(x, y)

        def half(ref, chip, core):
            hr = ref.shape[1] // 2
            return ref.at[chip, pl.ds(core * hr, hr), :]

        pending = []
        for w in range(n):
            for j, (px, py) in enumerate(chips):
                blk = half(outs[w], m, c)
                cp = _remote(blk, blk, send_a.at[w, j], recv_a.at[w, j], (px, py, c))
                cp.start()
                pending.append(cp)
        for j, (px, py) in enumerate(chips):
            pm = 2 * px + py
            for w in range(n):
                blk = half(outs[w], pm, c)
                _remote(blk, blk, send_a.at[w, j], recv_a.at[w, j], (px, py, c)).wait_recv()
                cp = _remote(blk, blk, send_b.at[w, j], recv_b.at[w, j], (x, y, 1 - c))
                cp.start()
                pending.append(cp)
        for j, (px, py) in enumerate(chips):
            pm = 2 * px + py
            for w in range(n):
                blk = half(outs[w], pm, 1 - c)
                _remote(blk, blk, send_b.at[w, j], recv_b.at[w, j], (x, y, 1 - c)).wait_recv()
        for cp in pending:
            cp.wait_send()

    sem = pltpu.SemaphoreType.DMA
    return pl.pallas_call(
        body, name=name, in_specs=[ANY] * n, out_specs=[ANY] * n,
        out_shape=[jax.ShapeDtypeStruct(b.shape, b.dtype) for b in bufs],
        input_output_aliases={w: w for w in range(n)},
        scratch_shapes=[sem((n, 3)), sem((n, 3)), sem((n, 3)), sem((n, 3))],
        compiler_params=pltpu.CompilerParams(has_side_effects=True),
    )(*bufs)


def exchange_halves(name, arrs, mine):
    n = len(arrs)

    def body(*refs):
        ins, outs = refs[:n], refs[n:2 * n]
        send, recv = refs[2 * n:]
        x, y, c, _ = _pos()
        sel = c if mine else 1 - c
        cps = []
        for w in range(n):
            hr = ins[w].shape[1] // 2
            cp = _remote(ins[w].at[:, pl.ds(sel * hr, hr), :], outs[w], send.at[w], recv.at[w], (x, y, 1 - c))
            cp.start()
            cps.append(cp)
        for cp in cps:
            cp.wait()

    sem = pltpu.SemaphoreType.DMA
    return pl.pallas_call(
        body, name=name, in_specs=[ANY] * n, out_specs=[ANY] * n,
        out_shape=[jax.ShapeDtypeStruct((a.shape[0], a.shape[1] // 2, a.shape[2]), a.dtype) for a in arrs],
        scratch_shapes=[sem((n,)), sem((n,))],
        compiler_params=pltpu.CompilerParams(has_side_effects=True),
    )(*arrs)


def scatter_to_chips(name, arrs):
    n = len(arrs)

    def body(*refs):
        ins, outs = refs[:n], refs[n:2 * n]
        send, recv = refs[2 * n:]
        x, y, c, m = _pos()
        chips = _other_chips(x, y)
        cps = []
        for w in range(n):
            for j, (px, py) in enumerate(chips):
                cp = _remote(ins[w].at[2 * px + py], outs[w].at[j], send.at[w, j], recv.at[w, j], (px, py, c))
                cp.start()
                cps.append(cp)
        for cp in cps:
            cp.wait()

    sem = pltpu.SemaphoreType.DMA
    return pl.pallas_call(
        body, name=name, in_specs=[ANY] * n, out_specs=[ANY] * n,
        out_shape=[jax.ShapeDtypeStruct((3,) + a.shape[1:], a.dtype) for a in arrs],
        scratch_shapes=[sem((n, 3)), sem((n, 3))],
        compiler_params=pltpu.CompilerParams(has_side_effects=True),
    )(*arrs)


def share_halves(name, fulls):
    n = len(fulls)

    def body(*refs):
        outs = refs[n:2 * n]
        send, recv = refs[2 * n:]
        x, y, c, _ = _pos()
        cps = []
        for w in range(n):
            hr = outs[w].shape[0] // 2
            blk = outs[w].at[pl.ds(c * hr, hr), :]
            cp = _remote(blk, blk, send.at[w], recv.at[w], (x, y, 1 - c))
            cp.start()
            cps.append(cp)
        for w, cp in enumerate(cps):
            hr = outs[w].shape[0] // 2
            cp.wait_send()
            blk = outs[w].at[pl.ds((1 - c) * hr, hr), :]
            _remote(blk, blk, send.at[w], recv.at[w], (x, y, 1 - c)).wait_recv()

    sem = pltpu.SemaphoreType.DMA
    return pl.pallas_call(
        body, name=name, in_specs=[ANY] * n, out_specs=[ANY] * n,
        out_shape=[jax.ShapeDtypeStruct(a.shape, a.dtype) for a in fulls],
        input_output_aliases={w: w for w in range(n)},
        scratch_shapes=[sem((n,)), sem((n,))],
        compiler_params=pltpu.CompilerParams(has_side_effects=True),
    )(*fulls)


def add_own_half(name, full, got, core, out_dtype):
    n, r, cdim = full.shape
    hr = r // 2
    tr = 8 * _blk(hr // 8, 32) if out_dtype == F32 else 16 * _blk(hr // 16, 16)
    nbh = hr // tr

    def body(c_ref, f_ref, g_ref, o_ref):
        o_ref[...] = (f_ref[...] + g_ref[...]).astype(o_ref.dtype)

    gs = pltpu.PrefetchScalarGridSpec(
        num_scalar_prefetch=1, grid=(n, nbh),
        in_specs=[pl.BlockSpec((None, tr, cdim), lambda s, i, c_ref: (s, c_ref[0] * nbh + i, 0)),
                  pl.BlockSpec((None, tr, cdim), lambda s, i, c_ref: (s, i, 0))],
        out_specs=pl.BlockSpec((None, tr, cdim), lambda s, i, c_ref: (s, i, 0)))
    return pl.pallas_call(body, name=name, grid_spec=gs, out_shape=jax.ShapeDtypeStruct((n, hr, cdim), out_dtype),
                          compiler_params=_cparams(("parallel", "parallel")))(core, full, got)


def sum_into_half(name, part, slots, chip_core, chip_order):
    _, hr, cdim = part.shape
    unit = 8 if part.dtype == F32 else 16
    tr = unit * _blk(hr // unit, 256 // unit)
    nbh = hr // tr

    def body(mc_ref, p_ref, s_ref, o_ref):
        terms = [p_ref[...].astype(F32)] + [s_ref[k].astype(F32) for k in range(3)]
        if chip_order:
            m = mc_ref[0]
            own, fx, fy, fxy = terms
            terms = [jnp.where((k ^ m) == 0, own, jnp.where((k ^ m) == 2, fx, jnp.where((k ^ m) == 1, fy, fxy)))
                     for k in range(N_CHIPS)]
        o_ref[...] = ((terms[0] + terms[1]) + terms[2]) + terms[3]

    gs = pltpu.PrefetchScalarGridSpec(
        num_scalar_prefetch=1, grid=(nbh,),
        in_specs=[pl.BlockSpec((None, tr, cdim), lambda i, mc: (mc[0], i, 0)),
                  pl.BlockSpec((3, tr, cdim), lambda i, mc: (0, i, 0))],
        out_specs=pl.BlockSpec((tr, cdim), lambda i, mc: (mc[1] * nbh + i, 0)))
    return pl.pallas_call(body, name=name, grid_spec=gs, out_shape=jax.ShapeDtypeStruct((2 * hr, cdim), F32),
                          compiler_params=_cparams(("parallel",)))(chip_core, part, slots)


WEIGHTS = ('norm_mix', 'norm_ffn', 'norm_kv', 'norm_final', 'ssm_w_in', 'ssm_a_re', 'ssm_a_im', 'ssm_log_dt',
           'ssm_b_re', 'ssm_b_im', 'ssm_c_re', 'ssm_c_im', 'ssm_d', 'ssm_w_glu', 'kv_w', 'attn_w_q', 'attn_w_o',
           'ffn_w_up', 'ffn_conv_w', 'ffn_conv_b', 'ffn_w_down')
SMALL = ('norm_mix', 'norm_ffn', 'norm_kv', 'norm_final', 'ssm_a_re', 'ssm_a_im', 'ssm_log_dt', 'ssm_b_re', 'ssm_b_im',
         'ssm_c_re', 'ssm_c_im', 'ssm_d', 'ffn_conv_w', 'ffn_conv_b')


def _pad_rows(flat, unit):
    n = flat.shape[0]
    total = -(-n // unit) * unit
    return jnp.pad(flat, (0, total - n)).reshape(total // LANES, LANES)


def kernel(x, norm_mix, norm_ffn, norm_kv, norm_final, ssm_w_in, ssm_a_re, ssm_a_im, ssm_log_dt, ssm_b_re, ssm_b_im, ssm_c_re, ssm_c_im, ssm_d, ssm_w_glu, kv_w, attn_w_q, attn_w_o, ffn_w_up, ffn_conv_w, ffn_conv_b, ffn_w_down, loss_target, m_norm_mix, m_norm_ffn, m_norm_kv, m_norm_final, m_ssm_w_in, m_ssm_a_re, m_ssm_a_im, m_ssm_log_dt, m_ssm_b_re, m_ssm_b_im, m_ssm_c_re, m_ssm_c_im, m_ssm_d, m_ssm_w_glu, m_kv_w, m_attn_w_q, m_attn_w_o, m_ffn_w_up, m_ffn_conv_w, m_ffn_conv_b, m_ffn_w_down, v_norm_mix, v_norm_ffn, v_norm_kv, v_norm_final, v_ssm_w_in, v_ssm_a_re, v_ssm_a_im, v_ssm_log_dt, v_ssm_b_re, v_ssm_b_im, v_ssm_c_re, v_ssm_c_im, v_ssm_d, v_ssm_w_glu, v_kv_w, v_attn_w_q, v_attn_w_o, v_ffn_w_up, v_ffn_conv_w, v_ffn_conv_b, v_ffn_w_down):
    a = dict(locals())
    l, d = x.shape[1], x.shape[2]
    f = ffn_conv_b.shape[1]
    fs = f // N_CHIPS
    m = 2 * lax.axis_index("x") + lax.axis_index("y")
    core = lax.axis_index("c").astype(jnp.int32).reshape(1)
    chip = m.astype(jnp.int32).reshape(1)
    chip_core = jnp.concatenate([chip, core])

    shards = [ssm_w_in[0], ssm_w_glu[0], kv_w, attn_w_q[0], attn_w_o[0], ffn_w_up.reshape(-1, ffn_w_up.shape[-1]),
              ffn_w_down.reshape(-1, d), _pad_rows(ffn_conv_w.reshape(-1), 16 * LANES)]
    bufs = [cast_into_slot(f"cast_{i}", s, chip, BF16 if i < 7 else F32) for i, s in enumerate(shards)]
    g_in, g_glu, g_kv, g_q, g_o, g_up, g_down, g_cw = gather_weights("gather_weights", bufs)
    conv_w = g_cw.reshape(N_CHIPS, -1)[:, :2 * 3 * fs].reshape(N_CHIPS, 2, 3, fs).transpose(1, 2, 0, 3).reshape(2, 3, f)
    p = dict(
        norm_mix=norm_mix, norm_ffn=norm_ffn, norm_kv=norm_kv, norm_final=norm_final,
        a_re=ssm_a_re[0], a_im=ssm_a_im[0], log_dt=ssm_log_dt[0], b_re=ssm_b_re[0], b_im=ssm_b_im[0],
        c_re=ssm_c_re[0], c_im=ssm_c_im[0], d=ssm_d[0],
        win=g_in.reshape(-1, g_in.shape[-1]), wglu4=g_glu, kvw4=g_kv, wq=g_q.reshape(-1, g_q.shape[-1]),
        wo=g_o.reshape(-1, g_o.shape[-1]), wup4=g_up.reshape(N_CHIPS, 2, d, -1),
        wdown=g_down.reshape(N_CHIPS, 2, fs, d).transpose(1, 0, 2, 3).reshape(2, f, d),
        conv_w=conv_w, conv_b=ffn_conv_b)

    loss_slab, dx, gr = _local_step(x[0], loss_target[0], p)

    def chipwise(g):
        return g.reshape(N_CHIPS, g.shape[0] // N_CHIPS, g.shape[1])

    small = {"norm_mix": gr["norm_mix"], "norm_ffn": gr["norm_ffn"], "norm_kv": gr["norm_kv"], "norm_final": gr["norm_final"],
             "ssm_a_re": gr["a_re"], "ssm_a_im": gr["a_im"], "ssm_log_dt": gr["log_dt"], "ssm_b_re": gr["b_re"],
             "ssm_b_im": gr["b_im"], "ssm_c_re": gr["c_re"], "ssm_c_im": gr["c_im"], "ssm_d": gr["d"],
             "ffn_conv_w": gr["conv_w"], "ffn_conv_b": gr["conv_b"]}
    packed = _pad_rows(jnp.concatenate([small[k].reshape(-1) for k in SMALL] + [loss_slab[0, 0:1]]), 16 * LANES)
    big = [chipwise(gr["win"]), gr["wglu4"], gr["kvw4"], chipwise(gr["wq"]), chipwise(gr["wo"]),
           gr["wup4"][0], gr["wup4"][1], chipwise(gr["wdown"][0]), chipwise(gr["wdown"][1]),
           jnp.broadcast_to(packed, (N_CHIPS,) + packed.shape)]
    payload = [BF16] * 9 + [F32]
    got = exchange_halves("rs_siblings", big, mine=False)
    part = [add_own_half(f"rs_add_{i}", g, r, core, dt) for i, (g, r, dt) in enumerate(zip(big, got, payload))]
    slots = scatter_to_chips("rs_chips", part)
    fulls = [sum_into_half(f"rs_sum_{i}", pt, s, chip_core, i == 9) for i, (pt, s) in enumerate(zip(part, slots))]
    r_in, r_glu, r_kv, r_q, r_o, r_up0, r_up1, r_down0, r_down1, total = share_halves("rs_share", fulls)
    grads = {"ssm_w_in": r_in[None], "ssm_w_glu": r_glu[None], "kv_w": r_kv, "attn_w_q": r_q[None], "attn_w_o": r_o[None],
             "ffn_w_up": jnp.stack([r_up0, r_up1]), "ffn_w_down": jnp.stack([r_down0, r_down1])}
    total = total.reshape(-1)
    off = 0
    for k in SMALL:
        n = small[k].size
        full = total[off:off + n].reshape(small[k].shape)
        off += n
        if k == "ffn_conv_w":
            full = lax.dynamic_slice_in_dim(full, m * fs, fs, axis=2)
        grads[k] = full.reshape(a[k].shape)
    loss = total[off]

    outs = {}
    for k in WEIGHTS:
        outs[k] = adamw(f"adamw_{k}", a[k], grads[k], a["m_" + k], a["v_" + k])
    return (loss, dx[None], *[grads[k] for k in WEIGHTS], *[outs[k][0] for k in WEIGHTS],
            *[outs[k][1] for k in WEIGHTS], *[outs[k][2] for k in WEIGHTS])
```

```python
import functools
import math

import jax
import jax.numpy as jnp
from jax import lax
from jax.experimental import pallas as pl
from jax.experimental.pallas import tpu as pltpu

F32 = jnp.float32
BF16 = jnp.bfloat16

EPS = 1e-6
SSM_GROUP = 16
STATE = 64
HEAD_DIM = 64
N_STREAMS = 8
MXU_DIM = 256
VMEM_LIMIT = 56 * 1024 * 1024

ADAM_LR = 0.001
ADAM_B1 = 0.9
ADAM_B2 = 0.999
ADAM_EPS = 1e-08
ADAM_WD = 0.01
ADAM_STEP = 10


def _cparams(sem=None):
    return pltpu.CompilerParams(dimension_semantics=sem, vmem_limit_bytes=VMEM_LIMIT)


def _blk(n, want):
    b = min(n, want)
    while n % b:
        b -= 1
    return b


NN = (((1,), (0,)), ((), ()))
NT = (((1,), (1,)), ((), ()))
TN = (((0,), (0,)), ((), ()))


def _matmul(name, a, b, *, a_spec, b_spec, o_spec, out_shape, grid, dn, nk, out_dtype):
    nax = len(grid)

    def body(a_ref, b_ref, o_ref, *scr):
        part = lax.dot_general(a_ref[...], b_ref[...], dn, preferred_element_type=F32)
        if nk == 1:
            o_ref[...] = part.astype(o_ref.dtype)
            return
        acc = scr[0] if scr else o_ref
        k = pl.program_id(nax - 1)

        @pl.when(k == 0)
        def _():
            acc[...] = part

        @pl.when(k > 0)
        def _():
            acc[...] += part

        if scr:
            @pl.when(k == nk - 1)
            def _():
                o_ref[...] = acc[...].astype(o_ref.dtype)

    scratch = []
    if nk > 1 and out_dtype != F32:
        blk = tuple(d for d in o_spec.block_shape if d is not None)
        scratch = [pltpu.VMEM(blk, F32)]
    sem = ("parallel",) * (nax - 1) + ("arbitrary",)
    return pl.pallas_call(
        body, name=name, grid=grid, in_specs=[a_spec, b_spec], out_specs=o_spec,
        out_shape=jax.ShapeDtypeStruct(out_shape, out_dtype), scratch_shapes=scratch,
        compiler_params=_cparams(sem),
    )(a, b)


def mm_nn(name, a, w, out_dtype, tm=512):
    m, k = a.shape
    n = w.shape[1]
    tm = _blk(m, tm)
    return _matmul(name, a, w, a_spec=pl.BlockSpec((tm, k), lambda i, kk: (i, 0)),
                   b_spec=pl.BlockSpec((k, n), lambda i, kk: (0, 0)),
                   o_spec=pl.BlockSpec((tm, n), lambda i, kk: (i, 0)),
                   out_shape=(m, n), grid=(m // tm, 1), dn=NN, nk=1, out_dtype=out_dtype)


def mm_nt(name, a, w, out_dtype, tm=512, tn=None):
    m, n = a.shape
    k = w.shape[0]
    tm = _blk(m, tm)
    tn = k if tn is None else tn
    return _matmul(name, a, w, a_spec=pl.BlockSpec((tm, n), lambda i, j, kk: (i, 0)),
                   b_spec=pl.BlockSpec((tn, n), lambda i, j, kk: (j, 0)),
                   o_spec=pl.BlockSpec((tm, tn), lambda i, j, kk: (i, j)),
                   out_shape=(m, k), grid=(m // tm, k // tn, 1), dn=NT, nk=1, out_dtype=out_dtype)


def mm_tn(name, a, b, tmo=None, tk=512):
    l, m = a.shape
    n = b.shape[1]
    tk = _blk(l, tk)
    tmo = m if tmo is None else tmo
    nk = l // tk
    return _matmul(name, a, b, a_spec=pl.BlockSpec((tk, tmo), lambda i, kk: (kk, i)),
                   b_spec=pl.BlockSpec((tk, n), lambda i, kk: (kk, 0)),
                   o_spec=pl.BlockSpec((tmo, n), lambda i, kk: (i, 0)),
                   out_shape=(m, n), grid=(m // tmo, nk), dn=TN, nk=nk, out_dtype=F32)


def mm_nn_colshard(name, a, w4, out_dtype, tm=512):
    m, k = a.shape
    s, _, ns = w4.shape
    tm = _blk(m, tm)
    return _matmul(name, a, w4, a_spec=pl.BlockSpec((tm, k), lambda i, j, kk: (i, 0)),
                   b_spec=pl.BlockSpec((None, k, ns), lambda i, j, kk: (j, 0, 0)),
                   o_spec=pl.BlockSpec((tm, ns), lambda i, j, kk: (i, j)),
                   out_shape=(m, s * ns), grid=(m // tm, s, 1), dn=NN, nk=1, out_dtype=out_dtype)


def mm_nt_colshard(name, a, w4, out_dtype, tm=512):
    m, _ = a.shape
    s, k, ns = w4.shape
    tm = _blk(m, tm)
    return _matmul(name, a, w4, a_spec=pl.BlockSpec((tm, ns), lambda i, kk: (i, kk)),
                   b_spec=pl.BlockSpec((None, k, ns), lambda i, kk: (kk, 0, 0)),
                   o_spec=pl.BlockSpec((tm, k), lambda i, kk: (i, 0)),
                   out_shape=(m, k), grid=(m // tm, s), dn=NT, nk=s, out_dtype=out_dtype)


def mm_tn_colshard(name, a, b, s, tk=512):
    l, k = a.shape
    ns = b.shape[1] // s
    tk = _blk(l, tk)
    nk = l // tk
    return _matmul(name, a, b, a_spec=pl.BlockSpec((tk, k), lambda j, kk: (kk, 0)),
                   b_spec=pl.BlockSpec((tk, ns), lambda j, kk: (kk, j)),
                   o_spec=pl.BlockSpec((None, k, ns), lambda j, kk: (j, 0, 0)),
                   out_shape=(s, k, ns), grid=(s, nk), dn=TN, nk=nk, out_dtype=F32)


def mm_up(name, h, wup4, layer, tm=512):
    m, k = h.shape
    s, _, _, ns = wup4.shape
    half = s // 2
    tm = _blk(m, tm)
    return _matmul(name, h, wup4, a_spec=pl.BlockSpec((tm, k), lambda i, j, kk: (i, 0)),
                   b_spec=pl.BlockSpec((None, None, k, ns), lambda i, j, kk: (j, layer, 0, 0)),
                   o_spec=pl.BlockSpec((None, tm, ns), lambda i, j, kk: (j // half, i, j % half)),
                   out_shape=(2, m, half * ns), grid=(m // tm, s, 1), dn=NN, nk=1, out_dtype=BF16)


def mm_up_nt(name, dgu, wup4, layer, tm=512):
    _, m, _ = dgu.shape
    s, _, k, ns = wup4.shape
    half = s // 2
    tm = _blk(m, tm)
    return _matmul(name, dgu, wup4,
                   a_spec=pl.BlockSpec((None, tm, ns), lambda i, kk: (kk // half, i, kk % half)),
                   b_spec=pl.BlockSpec((None, None, k, ns), lambda i, kk: (kk, layer, 0, 0)),
                   o_spec=pl.BlockSpec((tm, k), lambda i, kk: (i, 0)),
                   out_shape=(m, k), grid=(m // tm, s), dn=NT, nk=s, out_dtype=F32)


def mm_up_tn(name, h, dgu, s, tk=512):
    l, k = h.shape
    half = s // 2
    ns = dgu.shape[2] // half
    tk = _blk(l, tk)
    nk = l // tk
    return _matmul(name, h, dgu, a_spec=pl.BlockSpec((tk, k), lambda j, kk: (kk, 0)),
                   b_spec=pl.BlockSpec((None, tk, ns), lambda j, kk: (j // half, kk, j % half)),
                   o_spec=pl.BlockSpec((None, k, ns), lambda j, kk: (j, 0, 0)),
                   out_shape=(s, k, ns), grid=(s, nk), dn=TN, nk=nk, out_dtype=F32)


def add_rmsnorm(name, x, r, gains, tm=512):
    l, d = x.shape
    tm = _blk(l, tm)
    ng = len(gains)
    has_r = r is not None

    def body(*refs):
        x_ref = refs[0]
        pos = 1
        xs = x_ref[...]
        if has_r:
            xs = xs + refs[pos][...]
            pos += 1
        g_refs = refs[pos:pos + ng]
        outs = refs[pos + ng:]
        o = 0
        if has_r:
            outs[0][...] = xs
            o = 1
        xh = xs * lax.rsqrt(jnp.mean(xs * xs, axis=-1, keepdims=True) + EPS)
        for gi in range(ng):
            outs[o + gi][...] = (xh * g_refs[gi][...]).astype(BF16)

    row = pl.BlockSpec((tm, d), lambda i: (i, 0))
    gsp = pl.BlockSpec((1, d), lambda i: (0, 0))
    ins = [x] + ([r] if has_r else []) + [g.reshape(1, d) for g in gains]
    in_specs = [row] * (1 + has_r) + [gsp] * ng
    out_shape = ([jax.ShapeDtypeStruct((l, d), F32)] if has_r else []) + [jax.ShapeDtypeStruct((l, d), BF16)] * ng
    return pl.pallas_call(body, name=name, grid=(l // tm,), in_specs=in_specs,
                          out_specs=[row] * len(out_shape), out_shape=out_shape,
                          compiler_params=_cparams(("parallel",)))(*ins)


def norm_bwd(name, x, dres, pairs, tm=512):
    l, d = x.shape
    tm = _blk(l, tm)
    npair = len(pairs)
    has_r = dres is not None

    def body(*refs):
        x_ref = refs[0]
        pos = 1
        xs = x_ref[...]
        dx = jnp.zeros_like(xs)
        if has_r:
            dx = refs[pos][...]
            pos += 1
        ins = refs[pos:pos + 2 * npair]
        outs = refs[pos + 2 * npair:]
        rs = lax.rsqrt(jnp.mean(xs * xs, axis=-1, keepdims=True) + EPS)
        xh = xs * rs
        first = pl.program_id(0) == 0
        for pi in range(npair):
            g = ins[2 * pi][...]
            dh = ins[2 * pi + 1][...].astype(F32)
            dgp = jnp.sum(dh * xh, axis=0, keepdims=True)
            dg_ref = outs[1 + pi]

            @pl.when(first)
            def _():
                dg_ref[...] = dgp

            @pl.when(jnp.logical_not(first))
            def _():
                dg_ref[...] += dgp

            dxh = dh * g
            dx = dx + rs * (dxh - xh * jnp.mean(dxh * xh, axis=-1, keepdims=True))
        outs[0][...] = dx

    row = pl.BlockSpec((tm, d), lambda i: (i, 0))
    gsp = pl.BlockSpec((1, d), lambda i: (0, 0))
    ins = [x] + ([dres] if has_r else [])
    in_specs = [row] * (1 + has_r)
    for g, dh in pairs:
        ins += [g.reshape(1, d), dh]
        in_specs += [gsp, row]
    out_shape = [jax.ShapeDtypeStruct((l, d), F32)] + [jax.ShapeDtypeStruct((1, d), F32)] * npair
    return pl.pallas_call(body, name=name, grid=(l // tm,), in_specs=in_specs,
                          out_specs=[row] + [gsp] * npair, out_shape=out_shape,
                          compiler_params=_cparams(("arbitrary",)))(*ins)


def final_loss(name, x, r, tgt, g, tm=512):
    l, d = x.shape
    tm = _blk(l, tm)

    def body(x_ref, r_ref, t_ref, g_ref, loss_ref, dx_ref, dg_ref):
        xs = x_ref[...] + r_ref[...]
        gg = g_ref[...]
        rs = lax.rsqrt(jnp.mean(xs * xs, axis=-1, keepdims=True) + EPS)
        xh = xs * rs
        e = xh * gg - t_ref[...]
        part = 0.5 * jnp.sum(jnp.mean(e * e, axis=-1, keepdims=True), axis=0, keepdims=True)
        dy = e * (1.0 / d)
        dgp = jnp.sum(dy * xh, axis=0, keepdims=True)
        dxh = dy * gg
        dx_ref[...] = rs * (dxh - xh * jnp.mean(dxh * xh, axis=-1, keepdims=True))
        first = pl.program_id(0) == 0

        @pl.when(first)
        def _():
            loss_ref[...] = jnp.broadcast_to(part, loss_ref.shape)
            dg_ref[...] = dgp

        @pl.when(jnp.logical_not(first))
        def _():
            loss_ref[...] += jnp.broadcast_to(part, loss_ref.shape)
            dg_ref[...] += dgp

    row = pl.BlockSpec((tm, d), lambda i: (i, 0))
    gsp = pl.BlockSpec((1, d), lambda i: (0, 0))
    lsp = pl.BlockSpec((8, 128), lambda i: (0, 0))
    return pl.pallas_call(
        body, name=name, grid=(l // tm,), in_specs=[row, row, row, gsp], out_specs=[lsp, row, gsp],
        out_shape=[jax.ShapeDtypeStruct((8, 128), F32), jax.ShapeDtypeStruct((l, d), F32),
                   jax.ShapeDtypeStruct((1, d), F32)],
        compiler_params=_cparams(("arbitrary",)))(x, r, tgt, g.reshape(1, d))


def _sigmoid(x):
    return 0.5 * jnp.tanh(0.5 * x) + 0.5


def glu_fwd(name, z, tm=512):
    l, d2 = z.shape
    d = d2 // 2
    tm = _blk(l, tm)

    def body(z_ref, o_ref):
        o_ref[...] = z_ref[:, :d] * _sigmoid(z_ref[:, d:])

    return pl.pallas_call(body, name=name, grid=(l // tm,),
                          in_specs=[pl.BlockSpec((tm, d2), lambda i: (i, 0))],
                          out_specs=pl.BlockSpec((tm, d), lambda i: (i, 0)),
                          out_shape=jax.ShapeDtypeStruct((l, d), F32),
                          compiler_params=_cparams(("parallel",)))(z)


def glu_bwd(name, z, dm, tm=512):
    l, d2 = z.shape
    d = d2 // 2
    tm = _blk(l, tm)

    def body(z_ref, dm_ref, o_ref):
        sg = _sigmoid(z_ref[:, d:])
        g = dm_ref[...]
        o_ref[:, :d] = (g * sg).astype(BF16)
        o_ref[:, d:] = (g * z_ref[:, :d] * sg * (1.0 - sg)).astype(BF16)

    return pl.pallas_call(body, name=name, grid=(l // tm,),
                          in_specs=[pl.BlockSpec((tm, d2), lambda i: (i, 0)), pl.BlockSpec((tm, d), lambda i: (i, 0))],
                          out_specs=pl.BlockSpec((tm, d2), lambda i: (i, 0)),
                          out_shape=jax.ShapeDtypeStruct((l, d2), BF16),
                          compiler_params=_cparams(("parallel",)))(z, dm)


_GELU_C = math.sqrt(2.0 / math.pi)


def _gelu(y):
    return 0.5 * y * (1.0 + jnp.tanh(_GELU_C * (y + 0.044715 * y * y * y)))


def _gelu_grad(y):
    t = jnp.tanh(_GELU_C * (y + 0.044715 * y * y * y))
    return 0.5 * (1.0 + t) + 0.5 * y * (1.0 - t * t) * _GELU_C * (1.0 + 3.0 * 0.044715 * y * y)


def gelu_bwd(name, ypre, dyg, tm=512):
    l, d = ypre.shape
    tm = _blk(l, tm)

    def body(y_ref, d_ref, o_ref):
        o_ref[...] = (d_ref[...] * _gelu_grad(y_ref[...])).astype(BF16)

    row = pl.BlockSpec((tm, d), lambda i: (i, 0))
    return pl.pallas_call(body, name=name, grid=(l // tm,), in_specs=[row, row], out_specs=row,
                          out_shape=jax.ShapeDtypeStruct((l, d), BF16),
                          compiler_params=_cparams(("parallel",)))(ypre, dyg)


def _shift_down(x, n):
    rolled = pltpu.roll(x, n, axis=0)
    rows = lax.broadcasted_iota(jnp.int32, (8, x.shape[1]), 0)
    return jnp.concatenate([jnp.where(rows >= n, rolled[:8], 0.0), rolled[8:]], axis=0)


def _shift_up(x, n):
    l = x.shape[0]
    rolled = pltpu.roll(x, l - n, axis=0)
    rows = lax.broadcasted_iota(jnp.int32, (8, x.shape[1]), 0)
    return jnp.concatenate([rolled[:l - 8], jnp.where(rows < 8 - n, rolled[l - 8:], 0.0)], axis=0)


def ffn_mid_fwd(name, gu, cw, cb, tc=128):
    _, l, f = gu.shape
    tc = _blk(f, tc)

    def body(gu_ref, w_ref, b_ref, a_ref):
        g = gu_ref[0].astype(F32)
        u = gu_ref[1].astype(F32)
        gc = w_ref[0:1, :] * _shift_down(g, 2) + w_ref[1:2, :] * _shift_down(g, 1) + w_ref[2:3, :] * g + b_ref[...]
        a_ref[...] = (gc * _sigmoid(gc) * u).astype(BF16)

    return pl.pallas_call(
        body, name=name, grid=(f // tc,),
        in_specs=[pl.BlockSpec((2, l, tc), lambda c: (0, 0, c)), pl.BlockSpec((3, tc), lambda c: (0, c)),
                  pl.BlockSpec((1, tc), lambda c: (0, c))],
        out_specs=pl.BlockSpec((l, tc), lambda c: (0, c)),
        out_shape=jax.ShapeDtypeStruct((l, f), BF16),
        compiler_params=_cparams(("parallel",)))(gu, cw, cb)


def ffn_mid_bwd(name, gu, da, cw, cb, tc=128):
    _, l, f = gu.shape
    tc = _blk(f, tc)

    def body(gu_ref, da_ref, w_ref, b_ref, dgu_ref, dw_ref, db_ref):
        g = gu_ref[0].astype(F32)
        u = gu_ref[1].astype(F32)
        g1 = _shift_down(g, 1)
        g2 = _shift_down(g, 2)
        w0, w1, w2 = w_ref[0:1, :], w_ref[1:2, :], w_ref[2:3, :]
        gc = w0 * g2 + w1 * g1 + w2 * g + b_ref[...]
        sg = _sigmoid(gc)
        da = da_ref[...].astype(F32)
        dgu_ref[1] = (da * gc * sg).astype(BF16)
        dgc = da * u * (sg * (1.0 + gc * (1.0 - sg)))
        dgu_ref[0] = (w2 * dgc + w1 * _shift_up(dgc, 1) + w0 * _shift_up(dgc, 2)).astype(BF16)
        dw_ref[0:1, :] = jnp.sum(dgc * g2, axis=0, keepdims=True)
        dw_ref[1:2, :] = jnp.sum(dgc * g1, axis=0, keepdims=True)
        dw_ref[2:3, :] = jnp.sum(dgc * g, axis=0, keepdims=True)
        db_ref[...] = jnp.sum(dgc, axis=0, keepdims=True)

    return pl.pallas_call(
        body, name=name, grid=(f // tc,),
        in_specs=[pl.BlockSpec((2, l, tc), lambda c: (0, 0, c)), pl.BlockSpec((l, tc), lambda c: (0, c)),
                  pl.BlockSpec((3, tc), lambda c: (0, c)), pl.BlockSpec((1, tc), lambda c: (0, c))],
        out_specs=[pl.BlockSpec((2, l, tc), lambda c: (0, 0, c)), pl.BlockSpec((3, tc), lambda c: (0, c)),
                   pl.BlockSpec((1, tc), lambda c: (0, c))],
        out_shape=[jax.ShapeDtypeStruct((2, l, f), BF16), jax.ShapeDtypeStruct((3, f), F32),
                   jax.ShapeDtypeStruct((1, f), F32)],
        compiler_params=_cparams(("parallel",)))(gu, da, cw, cb)


SCAN_LANES = 512


def ssm_discretize(name, a_re, a_im, ldt, b_re, b_im):
    dt = jnp.exp(ldt)
    mag = jnp.exp(a_re * dt)
    ab_re = mag * jnp.cos(a_im * dt)
    ab_im = mag * jnp.sin(a_im * dt)
    den = a_re * a_re + a_im * a_im
    f_re = ((ab_re - 1.0) * a_re + ab_im * a_im) / den
    f_im = (ab_im * a_re - (ab_re - 1.0) * a_im) / den
    return ab_re, ab_im, f_re * b_re - f_im * b_im, f_re * b_im + f_im * b_re


def ssm_disc_fwd(name, a_re, a_im, ldt, b_re, b_im):
    def body(ar, ai, ld, br, bi, o_ar, o_ai, o_br, o_bi):
        r = ssm_discretize(None, ar[...], ai[...], ld[...], br[...], bi[...])
        o_ar[...], o_ai[...], o_br[...], o_bi[...] = r

    sd = jax.ShapeDtypeStruct
    return pl.pallas_call(body, name=name,
                          out_shape=[sd(a_re.shape, F32), sd(a_re.shape, F32), sd(b_re.shape, F32), sd(b_re.shape, F32)],
                          compiler_params=_cparams())(a_re, a_im, ldt, b_re, b_im)


def ssm_disc_bwd(name, a_re, a_im, ldt, b_re, b_im, d_ar, d_ai, d_br, d_bi):
    def body(ar, ai, ld, br, bi, g_ar, g_ai, g_br, g_bi, o_ar, o_ai, o_ld, o_br, o_bi):
        fn = functools.partial(ssm_discretize, None)
        _, vjp = jax.vjp(fn, ar[...], ai[...], ld[...], br[...], bi[...])
        r = vjp((g_ar[...], g_ai[...], g_br[...], g_bi[...]))
        o_ar[...], o_ai[...], o_ld[...], o_br[...], o_bi[...] = r

    sd = jax.ShapeDtypeStruct
    return pl.pallas_call(body, name=name,
                          out_shape=[sd(a_re.shape, F32)] * 3 + [sd(b_re.shape, F32)] * 2,
                          compiler_params=_cparams())(a_re, a_im, ldt, b_re, b_im, d_ar, d_ai, d_br, d_bi)


def _drive(x_ref, wre_ref, wim_ref, dre_ref, dim_ref):
    xb = x_ref[...]
    dre_ref[...] = jnp.dot(xb, wre_ref[...], preferred_element_type=F32)
    dim_ref[...] = jnp.dot(xb, wim_ref[...], preferred_element_type=F32)


def _ssm_dims(l, wre, tj):
    nb, kb, nsb = wre.shape
    jn = l // N_STREAMS
    tj = _blk(jn, tj)
    return nb, kb, nsb, jn, tj, tj * N_STREAMS, jn // tj


def _scan(dre_ref, dim_ref, st_re, st_im, a_ref, tj, reverse, write):
    ns = dre_ref.shape[1]
    cw = min(SCAN_LANES, ns)
    for cb in range(ns // cw):
        sl = slice(cb * cw, (cb + 1) * cw)
        ar = jnp.broadcast_to(a_ref[0:1, sl], (N_STREAMS, cw))
        ai = jnp.broadcast_to(a_ref[1:2, sl], (N_STREAMS, cw))

        def step(jj, carry, sl=sl, ar=ar, ai=ai):
            sr, si = carry
            j = (tj - 1 - jj) if reverse else jj
            off = pl.multiple_of(j * N_STREAMS, N_STREAMS)
            nr = ar * sr - ai * si + dre_ref[pl.ds(off, N_STREAMS), sl]
            ni = ar * si + ai * sr + dim_ref[pl.ds(off, N_STREAMS), sl]
            if write:
                dre_ref[pl.ds(off, N_STREAMS), sl] = nr
                dim_ref[pl.ds(off, N_STREAMS), sl] = ni
            return nr, ni

        sr, si = lax.fori_loop(0, tj, step, (st_re[:, sl], st_im[:, sl]))
        st_re[:, sl] = sr
        st_im[:, sl] = si


def ssm_pass1(name, x, wre, wim, a2, *, reverse, tj=64):
    l, d = x.shape
    nb, kb, nsb, jn, tj, r, nblk = _ssm_dims(l, wre, tj)
    ns = nb * nsb

    def body(x_ref, wre_ref, wim_ref, a_ref, cre_ref, cim_ref, dre, dim, st_re, st_im):
        i = pl.program_id(1)

        @pl.when(i == 0)
        def _():
            st_re[...] = jnp.zeros_like(st_re)
            st_im[...] = jnp.zeros_like(st_im)

        _drive(x_ref, wre_ref, wim_ref, dre, dim)
        _scan(dre, dim, st_re, st_im, a_ref, tj, reverse, False)

        @pl.when(i == nblk - 1)
        def _():
            pr, pi = a_ref[0:1, :], a_ref[1:2, :]
            rr, ri = jnp.ones_like(pr), jnp.zeros_like(pr)
            e = jn
            while e:
                if e & 1:
                    rr, ri = rr * pr - ri * pi, rr * pi + ri * pr
                pr, pi = pr * pr - pi * pi, 2.0 * pr * pi
                e >>= 1
            order = range(N_STREAMS - 1, -1, -1) if reverse else range(N_STREAMS)
            cr = jnp.zeros_like(rr)
            ci = jnp.zeros_like(rr)
            for s in order:
                cre_ref[s:s + 1, :] = cr
                cim_ref[s:s + 1, :] = ci
                fr, fi = st_re[s:s + 1, :], st_im[s:s + 1, :]
                cr, ci = fr + rr * cr - ri * ci, fi + rr * ci + ri * cr

    blk = (lambda b, i: (nblk - 1 - i, b)) if reverse else (lambda b, i: (i, b))
    w3 = pl.BlockSpec((None, kb, nsb), lambda b, i: (b, 0, 0))
    st = pl.BlockSpec((N_STREAMS, nsb), lambda b, i: (0, b))
    return pl.pallas_call(
        body, name=name, grid=(nb, nblk),
        in_specs=[pl.BlockSpec((r, kb), blk), w3, w3, pl.BlockSpec((2, nsb), lambda b, i: (0, b))],
        out_specs=[st, st], out_shape=[jax.ShapeDtypeStruct((N_STREAMS, ns), F32)] * 2,
        scratch_shapes=[pltpu.VMEM((r, nsb), F32), pltpu.VMEM((r, nsb), F32),
                        pltpu.VMEM((N_STREAMS, nsb), F32), pltpu.VMEM((N_STREAMS, nsb), F32)],
        compiler_params=_cparams(("parallel", "arbitrary")))(x, wre, wim, a2)


def ssm_fwd2(name, u, bre, bim, a2, init_re, init_im, cre, cim, dskip, *, tj=64):
    l, d = u.shape
    nb, kb, nsb, jn, tj, r, nblk = _ssm_dims(l, bre, tj)
    ns = nb * nsb

    def body(u_ref, bre_ref, bim_ref, a_ref, ire_ref, iim_ref, cre_ref, cim_ref, d_ref,
             sre_ref, sim_ref, y_ref, yg_ref, st_re, st_im):
        @pl.when(pl.program_id(1) == 0)
        def _():
            st_re[...] = ire_ref[...]
            st_im[...] = iim_ref[...]

        _drive(u_ref, bre_ref, bim_ref, sre_ref, sim_ref)
        _scan(sre_ref, sim_ref, st_re, st_im, a_ref, tj, False, True)
        y = (jnp.dot(sre_ref[...].astype(BF16), cre_ref[...], preferred_element_type=F32)
             - jnp.dot(sim_ref[...].astype(BF16), cim_ref[...], preferred_element_type=F32)
             + d_ref[...] * u_ref[...].astype(F32))
        y_ref[...] = y
        yg_ref[...] = _gelu(y).astype(BF16)

    rows = lambda w: pl.BlockSpec((r, w), lambda b, i: (i, b))
    w3 = pl.BlockSpec((None, kb, nsb), lambda b, i: (b, 0, 0))
    c3 = pl.BlockSpec((None, nsb, kb), lambda b, i: (b, 0, 0))
    st = pl.BlockSpec((N_STREAMS, nsb), lambda b, i: (0, b))
    return pl.pallas_call(
        body, name=name, grid=(nb, nblk),
        in_specs=[rows(kb), w3, w3, pl.BlockSpec((2, nsb), lambda b, i: (0, b)), st, st, c3, c3,
                  pl.BlockSpec((1, kb), lambda b, i: (0, b))],
        out_specs=[rows(nsb), rows(nsb), rows(kb), rows(kb)],
        out_shape=[jax.ShapeDtypeStruct((l, ns), F32), jax.ShapeDtypeStruct((l, ns), F32),
                   jax.ShapeDtypeStruct((l, d), F32), jax.ShapeDtypeStruct((l, d), BF16)],
        scratch_shapes=[pltpu.VMEM((N_STREAMS, nsb), F32), pltpu.VMEM((N_STREAMS, nsb), F32)],
        compiler_params=_cparams(("parallel", "arbitrary")))(u, bre, bim, a2, init_re, init_im, cre, cim, dskip)


def ssm_bwd2(name, dy, u, sre, sim, ctre, ctim, a2c, init_re, init_im, fre, fim, bre, bim, dskip, *, tj=64):
    l, d = u.shape
    nb, kb, nsb, jn, tj, r, nblk = _ssm_dims(l, bre, tj)
    ns = nb * nsb

    def body(dy_ref, u_ref, sre_ref, sim_ref, pre_ref, pim_ref, ctre_ref, ctim_ref, a_ref, ire_ref, iim_ref,
             fre_ref, fim_ref, bre_ref, bim_ref, d_ref,
             du_ref, dbre_ref, dbim_ref, dcre_ref, dcim_ref, dare_ref, daim_ref, dd_ref,
             lre, lim, st_re, st_im):
        i = pl.program_id(1)
        first = i == 0

        @pl.when(first)
        def _():
            st_re[...] = ire_ref[...]
            st_im[...] = iim_ref[...]
            dbre_ref[...] = jnp.zeros_like(dbre_ref)
            dbim_ref[...] = jnp.zeros_like(dbim_ref)
            dcre_ref[...] = jnp.zeros_like(dcre_ref)
            dcim_ref[...] = jnp.zeros_like(dcim_ref)
            dare_ref[...] = jnp.zeros_like(dare_ref)
            daim_ref[...] = jnp.zeros_like(daim_ref)
            dd_ref[...] = jnp.zeros_like(dd_ref)

        _drive(dy_ref, ctre_ref, ctim_ref, lre, lim)
        _scan(lre, lim, st_re, st_im, a_ref, tj, True, True)

        is_t0 = i == nblk - 1
        cw = min(SCAN_LANES, nsb)
        for cb in range(nsb // cw):
            sl = slice(cb * cw, (cb + 1) * cw)
            p_r = jnp.where(is_t0, fre_ref[:, sl], pre_ref[:, sl])
            p_i = jnp.where(is_t0, fim_ref[:, sl], pim_ref[:, sl])
            l_r, l_i = lre[0:N_STREAMS, sl], lim[0:N_STREAMS, sl]
            acc = (l_r * p_r + l_i * p_i, l_i * p_r - l_r * p_i)

            def step(jj, carry, sl=sl):
                a_r, a_i = carry
                off = pl.multiple_of(jj * N_STREAMS, N_STREAMS)
                prev = pl.multiple_of((jj - 1) * N_STREAMS, N_STREAMS)
                l_r, l_i = lre[pl.ds(off, N_STREAMS), sl], lim[pl.ds(off, N_STREAMS), sl]
                p_r, p_i = sre_ref[pl.ds(prev, N_STREAMS), sl], sim_ref[pl.ds(prev, N_STREAMS), sl]
                return a_r + l_r * p_r + l_i * p_i, a_i + l_i * p_r - l_r * p_i

            a_r, a_i = lax.fori_loop(1, tj, step, acc)
            dare_ref[:, sl] += a_r
            daim_ref[:, sl] += a_i

        dyf = dy_ref[...].astype(F32)
        uf = u_ref[...].astype(F32)
        dd_ref[...] += jnp.sum(dyf * uf, axis=0, keepdims=True)
        lrb = lre[...].astype(BF16)
        lib = lim[...].astype(BF16)
        ub = u_ref[...]
        dyb = dy_ref[...]
        dbre_ref[...] += lax.dot_general(ub, lrb, TN, preferred_element_type=F32)
        dbim_ref[...] += lax.dot_general(ub, lib, TN, preferred_element_type=F32)
        dcre_ref[...] += lax.dot_general(sre_ref[...].astype(BF16), dyb, TN, preferred_element_type=F32)
        dcim_ref[...] -= lax.dot_general(sim_ref[...].astype(BF16), dyb, TN, preferred_element_type=F32)
        du = (lax.dot_general(lrb, bre_ref[...], NT, preferred_element_type=F32)
              + lax.dot_general(lib, bim_ref[...], NT, preferred_element_type=F32)
              + d_ref[...] * dyf)
        du_ref[...] = du.astype(BF16)

    rev = lambda w: pl.BlockSpec((r, w), lambda b, i: (nblk - 1 - i, b))
    prev_tile = pl.BlockSpec((N_STREAMS, nsb), lambda b, i: (jnp.maximum((nblk - 1 - i) * tj - 1, 0), b))
    st = pl.BlockSpec((N_STREAMS, nsb), lambda b, i: (0, b))
    w3 = pl.BlockSpec((None, kb, nsb), lambda b, i: (b, 0, 0))
    c3 = pl.BlockSpec((None, nsb, kb), lambda b, i: (b, 0, 0))
    dsp = pl.BlockSpec((1, kb), lambda b, i: (0, b))
    return pl.pallas_call(
        body, name=name, grid=(nb, nblk),
        in_specs=[rev(kb), rev(kb), rev(nsb), rev(nsb), prev_tile, prev_tile, w3, w3,
                  pl.BlockSpec((2, nsb), lambda b, i: (0, b)), st, st, st, st, w3, w3, dsp],
        out_specs=[rev(kb), w3, w3, c3, c3, st, st, dsp],
        out_shape=[jax.ShapeDtypeStruct((l, d), BF16), jax.ShapeDtypeStruct((nb, kb, nsb), F32),
                   jax.ShapeDtypeStruct((nb, kb, nsb), F32), jax.ShapeDtypeStruct((nb, nsb, kb), F32),
                   jax.ShapeDtypeStruct((nb, nsb, kb), F32), jax.ShapeDtypeStruct((N_STREAMS, ns), F32),
                   jax.ShapeDtypeStruct((N_STREAMS, ns), F32), jax.ShapeDtypeStruct((1, d), F32)],
        scratch_shapes=[pltpu.VMEM((r, nsb), F32), pltpu.VMEM((r, nsb), F32),
                        pltpu.VMEM((N_STREAMS, nsb), F32), pltpu.VMEM((N_STREAMS, nsb), F32)],
        compiler_params=_cparams(("parallel", "arbitrary")))(
            dy, u, sre, sim, sre, sim, ctre, ctim, a2c, init_re, init_im, fre, fim, bre, bim, dskip)


ATT_TILE = 256
ATT_HEADS = 4
EXP_ZERO_BELOW = -104.0
LANES = 128


def _split_dot(x, tri):
    hi = x.astype(BF16)
    lo = (x - hi.astype(F32)).astype(BF16)
    return jnp.dot(hi, tri, preferred_element_type=F32) + jnp.dot(lo, tri, preferred_element_type=F32)


def _sb_logs(z, causal):
    sp = jnp.maximum(z, 0.0) + jnp.log(1.0 + jnp.exp(-jnp.abs(z)))
    l1m = -sp
    if causal is not None:
        l1m = jnp.where(causal, l1m, 0.0)
    return z - sp, l1m


def attn_fwd(name, q, kv, t=ATT_TILE):
    l, dm = q.shape
    dh = HEAD_DIM
    h = dm // dh
    t = _blk(l, t)
    hb = _blk(h, ATT_HEADS)
    wb = hb * dh
    scale = dh ** -0.5

    def body(q_ref, k_ref, v_ref, o_ref, tot_ref):
        i = pl.program_id(1)
        hd = lambda g: slice(g * dh, (g + 1) * dh)
        qs = [(q_ref[:, hd(g)].astype(F32) * scale).astype(BF16) for g in range(hb)]
        row = lax.broadcasted_iota(jnp.int32, (t, t), 0)
        col = lax.broadcasted_iota(jnp.int32, (t, t), 1)
        tri_gt = jnp.where(row > col, 1.0, 0.0).astype(BF16)
        causal = col < row

        def tile(jb, carry, mask):
            off = pl.multiple_of(jb * t, t)
            heads = range(hb)
            z = [lax.dot_general(qs[g], k_ref[pl.ds(off, t), hd(g)], NT, preferred_element_type=F32) for g in heads]
            logs = [_sb_logs(z[g], mask) for g in heads]
            rem = [_split_dot(logs[g][1], tri_gt) for g in heads]
            w = [jnp.exp(logs[g][0] + rem[g] + carry[g][1]) for g in heads]
            if mask is not None:
                w = [jnp.where(mask, w[g], 0.0) for g in heads]
            pv = [jnp.dot(w[g].astype(BF16), v_ref[pl.ds(off, t), hd(g)], preferred_element_type=F32) for g in heads]
            return tuple((carry[g][0] + pv[g], carry[g][1] + rem[g][:, 0:1] + logs[g][1][:, 0:1]) for g in heads)

        def live(carry):
            top = carry[0][1]
            for g in range(1, hb):
                top = jnp.maximum(top, carry[g][1])
            return (jnp.max(top) > EXP_ZERO_BELOW).astype(jnp.int32)

        def more(c):
            it, alive, _ = c
            return jnp.logical_and(it < i, alive > 0)

        def step(c):
            it, _, carry = c
            carry = tile(i - 1 - it, carry, None)
            return it + 1, live(carry), carry

        carry = tile(i, ((jnp.zeros((t, dh), F32), jnp.zeros((t, 1), F32)),) * hb, causal)
        done, _, carry = lax.while_loop(more, step, (jnp.int32(0), live(carry), carry))
        tot_ref[...] = jnp.zeros_like(tot_ref)
        for g, (acc, run) in enumerate(carry):
            o_ref[:, hd(g)] = acc.astype(BF16)
            tot_ref[:, g:g + 1] = run
        tot_ref[:, hb:hb + 1] = jnp.full((t, 1), done, jnp.int32).astype(F32)

    qsp = pl.BlockSpec((t, wb), lambda hh, i: (i, hh))
    ksp = pl.BlockSpec((None, l, wb), lambda hh, i: (0, 0, hh))
    vsp = pl.BlockSpec((None, l, wb), lambda hh, i: (1, 0, hh))
    tsp = pl.BlockSpec((None, t, LANES), lambda hh, i: (hh, i, 0))
    return pl.pallas_call(body, name=name, grid=(h // hb, l // t), in_specs=[qsp, ksp, vsp], out_specs=[qsp, tsp],
                          out_shape=[jax.ShapeDtypeStruct((l, dm), BF16), jax.ShapeDtypeStruct((h // hb, l, LANES), F32)],
                          compiler_params=_cparams(("parallel", "parallel")))(q, kv, kv)


def attn_bwd(name, q, kv, tot, do, t=ATT_TILE):
    l, dm = q.shape
    dh = HEAD_DIM
    h = dm // dh
    t = _blk(l, t)
    hb = _blk(h, ATT_HEADS)
    wb = hb * dh
    nq = l // t
    scale = dh ** -0.5

    def body(q_ref, k_ref, v_ref, tot_ref, do_ref, dq_ref, dkv_ref, dk_ref, dv_ref):
        i = pl.program_id(1)
        hd = lambda g: slice(g * dh, (g + 1) * dh)

        @pl.when(i == 0)
        def _():
            dk_ref[...] = jnp.zeros_like(dk_ref)
            dv_ref[...] = jnp.zeros_like(dv_ref)

        qs = [(q_ref[:, hd(g)].astype(F32) * scale).astype(BF16) for g in range(hb)]
        dob = [do_ref[:, hd(g)] for g in range(hb)]
        total = [tot_ref[:, g:g + 1] for g in range(hb)]
        row = lax.broadcasted_iota(jnp.int32, (t, t), 0)
        col = lax.broadcasted_iota(jnp.int32, (t, t), 1)
        tri_le = jnp.where(row <= col, 1.0, 0.0).astype(BF16)
        tri_lt = jnp.where(row < col, 1.0, 0.0).astype(BF16)
        causal = col < row

        def tile(jb, carry, mask):
            off = pl.multiple_of(jb * t, t)
            heads = range(hb)
            kj = [k_ref[pl.ds(off, t), hd(g)] for g in heads]
            z = [lax.dot_general(qs[g], kj[g], NT, preferred_element_type=F32) for g in heads]
            dp = [lax.dot_general(dob[g], v_ref[pl.ds(off, t), hd(g)], NT, preferred_element_type=F32) for g in heads]
            logs = [_sb_logs(z[g], mask) for g in heads]
            lpre = [_split_dot(logs[g][1], tri_le) for g in heads]
            w = [jnp.exp(logs[g][0] + (total[g] - carry[g][1] - lpre[g])) for g in heads]
            if mask is not None:
                w = [jnp.where(mask, w[g], 0.0) for g in heads]
            p = [w[g] * dp[g] for g in heads]
            for g in heads:
                dv_ref[pl.ds(off, t), hd(g)] += lax.dot_general(w[g].astype(BF16), dob[g], TN, preferred_element_type=F32)
            qpre = [carry[g][2] + jnp.dot(p[g].astype(BF16), tri_lt, preferred_element_type=F32) for g in heads]
            dz = [p[g] - jnp.exp(logs[g][0]) * (p[g] + qpre[g]) for g in heads]
            if mask is not None:
                dz = [jnp.where(mask, dz[g], 0.0) for g in heads]
            dzb = [dz[g].astype(BF16) for g in heads]
            for g in heads:
                dk_ref[pl.ds(off, t), hd(g)] += lax.dot_general(dzb[g], qs[g], TN, preferred_element_type=F32)
            dq = [carry[g][0] + jnp.dot(dzb[g], kj[g], preferred_element_type=F32) for g in heads]
            return tuple((dq[g], carry[g][1] + lpre[g][:, t - 1:t], qpre[g][:, t - 1:t] + p[g][:, t - 1:t]) for g in heads)

        zero = jnp.zeros((t, 1), F32)
        done = jnp.max(tot_ref[:, hb:hb + 1]).astype(jnp.int32)
        carry = lax.fori_loop(i - done, i, lambda jb, c: tile(jb, c, None), ((jnp.zeros((t, dh), F32), zero, zero),) * hb)
        carry = tile(i, carry, causal)
        for g in range(hb):
            dq_ref[:, hd(g)] = (carry[g][0] * scale).astype(BF16)

        @pl.when(i == nq - 1)
        def _():
            dkv_ref[0] = dk_ref[...].astype(BF16)
            dkv_ref[1] = dv_ref[...].astype(BF16)

    qsp = pl.BlockSpec((t, wb), lambda hh, i: (i, hh))
    ksp = pl.BlockSpec((None, l, wb), lambda hh, i: (0, 0, hh))
    vsp = pl.BlockSpec((None, l, wb), lambda hh, i: (1, 0, hh))
    tsp = pl.BlockSpec((None, t, LANES), lambda hh, i: (hh, i, 0))
    return pl.pallas_call(body, name=name, grid=(h // hb, nq), in_specs=[qsp, ksp, vsp, tsp, qsp],
                          out_specs=[qsp, pl.BlockSpec((2, l, wb), lambda hh, i: (0, 0, hh))],
                          out_shape=[jax.ShapeDtypeStruct((l, dm), BF16), jax.ShapeDtypeStruct((2, l, dm), BF16)],
                          scratch_shapes=[pltpu.VMEM((l, wb), F32), pltpu.VMEM((l, wb), F32)],
                          compiler_params=_cparams(("parallel", "arbitrary")))(q, kv, kv, tot, do)


def adamw(name, w, g, m, v):
    shape = w.shape
    cols = shape[-1]
    rows = w.size // cols
    tr = rows
    if rows % 8 == 0:
        tr = 8 * _blk(rows // 8, 64)
    c1 = 1.0 - ADAM_B1 ** ADAM_STEP
    c2 = 1.0 - ADAM_B2 ** ADAM_STEP

    def body(w_ref, g_ref, m_ref, v_ref, d_ref, nm_ref, nv_ref):
        gg = g_ref[...]
        mm = ADAM_B1 * m_ref[...] + (1.0 - ADAM_B1) * gg
        vv = ADAM_B2 * v_ref[...] + (1.0 - ADAM_B2) * (gg * gg)
        nm_ref[...] = mm
        nv_ref[...] = vv
        d_ref[...] = -ADAM_LR * ((mm / c1) / (jnp.sqrt(vv / c2) + ADAM_EPS) + ADAM_WD * w_ref[...])

    sp = pl.BlockSpec((tr, cols), lambda i: (i, 0))
    sd = jax.ShapeDtypeStruct((rows, cols), F32)
    outs = pl.pallas_call(body, name=name, grid=(rows // tr,), in_specs=[sp] * 4, out_specs=[sp] * 3,
                          out_shape=[sd] * 3, compiler_params=_cparams(("parallel",)))(
        w.reshape(rows, cols), g.reshape(rows, cols), m.reshape(rows, cols), v.reshape(rows, cols))
    return tuple(o.reshape(shape) for o in outs)


def _perm(a):
    l, d = a.shape
    return a.reshape(N_STREAMS, l // N_STREAMS, d).transpose(1, 0, 2).reshape(l, d)


def _unperm(a):
    l, d = a.shape
    return a.reshape(l // N_STREAMS, N_STREAMS, d).transpose(1, 0, 2).reshape(l, d)


def _heads(a):
    l, d = a.shape
    return a.reshape(l, d // HEAD_DIM, HEAD_DIM).transpose(1, 0, 2)


def _unheads(a):
    h, l, dh = a.shape
    return a.transpose(1, 0, 2).reshape(l, h * dh)


def _ssm_layouts(d):
    g = d // SSM_GROUP
    gb = MXU_DIM // SSM_GROUP if d >= MXU_DIM else g
    return g, gb, g // gb


def _b_blocks(bb, d):
    g, gb, nb = _ssm_layouts(d)
    b4 = bb.reshape(SSM_GROUP, nb, gb, STATE)
    eye = jnp.eye(gb, dtype=bb.dtype)
    return jnp.einsum('hbqp,gq->bghqp', b4, eye).reshape(nb, gb * SSM_GROUP, gb * STATE)


def _b_unblocks(db, d):
    g, gb, nb = _ssm_layouts(d)
    eye = jnp.eye(gb, dtype=db.dtype)
    return jnp.einsum('bghqp,gq->hbqp', db.reshape(nb, gb, SSM_GROUP, gb, STATE), eye).reshape(SSM_GROUP, g * STATE)


def _c_blocks(c, d):
    g, gb, nb = _ssm_layouts(d)
    eye = jnp.eye(gb, dtype=c.dtype)
    return jnp.einsum('bghp,gq->bqpgh', c.reshape(nb, gb, SSM_GROUP, STATE), eye).reshape(nb, gb * STATE, gb * SSM_GROUP)


def _c_unblocks(dc, d):
    g, gb, nb = _ssm_layouts(d)
    eye = jnp.eye(gb, dtype=dc.dtype)
    return jnp.einsum('bqpgh,gq->bghp', dc.reshape(nb, gb, STATE, gb, SSM_GROUP), eye).reshape(g, SSM_GROUP, STATE)


def _ffn_fwd(tag, x, r, gain, wup4, layer, wdown, cw, cb):
    xs, h = add_rmsnorm(f"norm_ffn{tag}", x, r, [gain])
    gu = mm_up(f"ffn_up{tag}", h, wup4, layer)
    a = ffn_mid_fwd(f"ffn_mid{tag}", gu, cw, cb)
    f = mm_nn(f"ffn_down{tag}", a, wdown, F32)
    return xs, h, gu, a, f


def _ffn_bwd(tag, dxo, xs, h, gu, a, gain, wup4, layer, wdown, cw, cb, s):
    dfb = dxo.astype(BF16)
    f = wdown.shape[0]
    da = mm_nt(f"ffn_down_dx{tag}", dfb, wdown, BF16, tn=f // 2)
    dwdown = mm_tn(f"ffn_down_dw{tag}", a, dfb, tmo=f // 2)
    dgu, dcw, dcb = ffn_mid_bwd(f"ffn_mid_bwd{tag}", gu, da, cw, cb)
    dh = mm_up_nt(f"ffn_up_dx{tag}", dgu, wup4, layer)
    dwup = mm_up_tn(f"ffn_up_dw{tag}", h, dgu, s)
    dxs, dg = norm_bwd(f"norm_ffn_bwd{tag}", xs, dxo, [(gain, dh)])
    return dxs, dg, dwup, dwdown, dcw, dcb


def _local_step(x, tgt, p):
    l, d = x.shape
    s = p["wglu4"].shape[0]
    gr = {}

    a_re, a_im = p["a_re"].reshape(1, -1), p["a_im"].reshape(1, -1)
    ldt = jnp.repeat(p["log_dt"].reshape(-1), STATE).reshape(1, -1)
    bk_re = p["b_re"].transpose(2, 0, 1).reshape(SSM_GROUP, -1)
    bk_im = p["b_im"].transpose(2, 0, 1).reshape(SSM_GROUP, -1)
    ab_re, ab_im, bb_re, bb_im = ssm_disc_fwd("ssm_disc", a_re, a_im, ldt, bk_re, bk_im)
    a2 = jnp.concatenate([ab_re, ab_im], axis=0)
    a2c = jnp.concatenate([ab_re, -ab_im], axis=0)
    bre, bim = _b_blocks(bb_re, d).astype(BF16), _b_blocks(bb_im, d).astype(BF16)
    cre, cim = _c_blocks(p["c_re"], d).astype(BF16), _c_blocks(p["c_im"], d).astype(BF16)
    ctre, ctim = cre.transpose(0, 2, 1), -cim.transpose(0, 2, 1)
    dskip = p["d"].reshape(1, d)

    xp = _perm(x)
    (h0p,) = add_rmsnorm("norm_mix0", xp, None, [p["norm_mix"][0]])
    u = mm_nn("ssm_in", h0p, p["win"], BF16)
    f_re, f_im = ssm_pass1("ssm_fwd1", u, bre, bim, a2, reverse=False)
    s_re, s_im, ypre, yg = ssm_fwd2("ssm_fwd2", u, bre, bim, a2, f_re, f_im, cre, cim, dskip)
    z = mm_nn_colshard("ssm_glu", yg, p["wglu4"], F32)
    mix = _unperm(glu_fwd("glu", z))

    x1, h1, gu0, a0, f0 = _ffn_fwd("0", x, mix, p["norm_ffn"][0], p["wup4"], 0, p["wdown"][0], p["conv_w"][0], p["conv_b"][0:1])
    x2, hk, h2 = add_rmsnorm("norm_kv_mix1", x1, f0, [p["norm_kv"], p["norm_mix"][1]])
    kvw = p["kvw4"][:, None]
    kv = mm_up("kv_proj", hk, kvw, 0)
    qf = mm_nn("q_proj", h2, p["wq"], BF16)
    ob, tot = attn_fwd("attn", qf, kv)
    ao = mm_nn("o_proj", ob, p["wo"], F32)
    x3, h3, gu1, a1, f1 = _ffn_fwd("1", x2, ao, p["norm_ffn"][1], p["wup4"], 1, p["wdown"][1], p["conv_w"][1], p["conv_b"][1:2])
    loss, dx4, dg_final = final_loss("final_loss", x3, f1, tgt, p["norm_final"])
    gr["norm_final"] = dg_final.reshape(d)

    dx3, dg_ffn1, dwup1, dwdown1, dcw1, dcb1 = _ffn_bwd(
        "1", dx4, x3, h3, gu1, a1, p["norm_ffn"][1], p["wup4"], 1, p["wdown"][1], p["conv_w"][1], p["conv_b"][1:2], s)
    dx3b = dx3.astype(BF16)
    do2 = mm_nt("o_proj_dx", dx3b, p["wo"], BF16)
    gr["wo"] = mm_tn("o_proj_dw", ob, dx3b)
    dqf, dkv = attn_bwd("attn_bwd", qf, kv, tot, do2)
    dh2 = mm_nt("q_proj_dx", dqf, p["wq"], F32)
    gr["wq"] = mm_tn("q_proj_dw", h2, dqf)
    dhk = mm_up_nt("kv_proj_dx", dkv, kvw, 0)
    gr["kvw4"] = mm_up_tn("kv_proj_dw", hk, dkv, s)
    dx2, dg_mix1, dg_kv = norm_bwd("norm_kv_mix1_bwd", x2, dx3, [(p["norm_mix"][1], dh2), (p["norm_kv"], dhk)])
    gr["norm_kv"] = dg_kv.reshape(d)

    dx1, dg_ffn0, dwup0, dwdown0, dcw0, dcb0 = _ffn_bwd(
        "0", dx2, x1, h1, gu0, a0, p["norm_ffn"][0], p["wup4"], 0, p["wdown"][0], p["conv_w"][0], p["conv_b"][0:1], s)
    dx1p = _perm(dx1)
    dz = glu_bwd("glu_bwd", z, dx1p)
    dyg = mm_nt_colshard("ssm_glu_dx", dz, p["wglu4"], F32)
    gr["wglu4"] = mm_tn_colshard("ssm_glu_dw", yg, dz, s)
    dy = gelu_bwd("gelu_bwd", ypre, dyg)
    i_re, i_im = ssm_pass1("ssm_bwd1", dy, ctre, ctim, a2c, reverse=True)
    du, dbre, dbim, dcre, dcim, da_re, da_im, dd = ssm_bwd2(
        "ssm_bwd2", dy, u, s_re, s_im, ctre, ctim, a2c, i_re, i_im, f_re, f_im, bre, bim, dskip)
    dh0p = mm_nt("ssm_in_dx", du, p["win"], F32)
    gr["win"] = mm_tn("ssm_in_dw", h0p, du)
    dxp, dg_mix0 = norm_bwd("norm_mix0_bwd", xp, dx1p, [(p["norm_mix"][0], dh0p)])
    dx = _unperm(dxp)

    g_are, g_aim, g_ldt, g_bre, g_bim = ssm_disc_bwd(
        "ssm_disc_bwd", a_re, a_im, ldt, bk_re, bk_im,
        jnp.sum(da_re, axis=0, keepdims=True), jnp.sum(da_im, axis=0, keepdims=True),
        _b_unblocks(dbre, d), _b_unblocks(dbim, d))
    g = d // SSM_GROUP
    gr["a_re"] = g_are.reshape(g, STATE)
    gr["a_im"] = g_aim.reshape(g, STATE)
    gr["log_dt"] = jnp.sum(g_ldt.reshape(g, STATE), axis=1)
    gr["b_re"] = g_bre.reshape(SSM_GROUP, g, STATE).transpose(1, 2, 0)
    gr["b_im"] = g_bim.reshape(SSM_GROUP, g, STATE).transpose(1, 2, 0)
    gr["c_re"] = _c_unblocks(dcre, d)
    gr["c_im"] = _c_unblocks(dcim, d)
    gr["d"] = dd.reshape(g, SSM_GROUP)
    gr["norm_mix"] = jnp.concatenate([dg_mix0, dg_mix1], axis=0)
    gr["norm_ffn"] = jnp.concatenate([dg_ffn0, dg_ffn1], axis=0)
    gr["conv_w"] = jnp.stack([dcw0, dcw1])
    gr["conv_b"] = jnp.concatenate([dcb0, dcb1], axis=0)
    gr["wup4"] = (dwup0, dwup1)
    gr["wdown"] = (dwdown0, dwdown1)
    return loss, dx, gr


MESH = pl.DeviceIdType.MESH
N_CHIPS = 4
N_DEV = 8
ANY = pl.BlockSpec(memory_space=pl.ANY)


def _pos():
    x, y, c = lax.axis_index("x"), lax.axis_index("y"), lax.axis_index("c")
    return x, y, c, 2 * x + y


def _other_chips(x, y):
    return [(1 - x, y), (x, 1 - y), (1 - x, 1 - y)]


def _remote(src, dst, send_sem, recv_sem, dev):
    return pltpu.make_async_remote_copy(src_ref=src, dst_ref=dst, send_sem=send_sem, recv_sem=recv_sem,
                                        device_id=dev, device_id_type=MESH)


def cast_into_slot(name, a, chip, dtype):
    r, cdim = a.shape
    tr = 16 * _blk(r // 16, 32) if r % 16 == 0 else r

    def body(m_ref, a_ref, o_ref):
        o_ref[...] = a_ref[...].astype(o_ref.dtype)

    gs = pltpu.PrefetchScalarGridSpec(
        num_scalar_prefetch=1, grid=(r // tr,),
        in_specs=[pl.BlockSpec((tr, cdim), lambda i, m_ref: (i, 0))],
        out_specs=pl.BlockSpec((None, tr, cdim), lambda i, m_ref: (m_ref[0], i, 0)))
    return pl.pallas_call(body, name=name, grid_spec=gs, out_shape=jax.ShapeDtypeStruct((N_CHIPS, r, cdim), dtype),
                          compiler_params=_cparams(("parallel",)))(chip, a)


def gather_weights(name, bufs):
    n = len(bufs)

    def body(*refs):
        outs = refs[n:2 * n]
        send_a, recv_a, send_b, recv_b = refs[2 * n:]
        x, y, c, m = _pos()
        chips = _other_chips(x, y)

        def half(ref, chip, core):
            hr = ref.shape[1] // 2
            return ref.at[chip, pl.ds(core * hr, hr), :]

        pending = []
        for w in range(n):
            for j, (px, py) in enumerate(chips):
                blk = half(outs[w], m, c)
                cp = _remote(blk, blk, send_a.at[w, j], recv_a.at[w, j], (px, py, c))
                cp.start()
                pending.append(cp)
        for j, (px, py) in enumerate(chips):
            pm = 2 * px + py
            for w in range(n):
                blk = half(outs[w], pm, c)
                _remote(blk, blk, send_a.at[w, j], recv_a.at[w, j], (px, py, c)).wait_recv()
                cp = _remote(blk, blk, send_b.at[w, j], recv_b.at[w, j], (x, y, 1 - c))
                cp.start()
                pending.append(cp)
        for j, (px, py) in enumerate(chips):
            pm = 2 * px + py
            for w in range(n):
                blk = half(outs[w], pm, 1 - c)
                _remote(blk, blk, send_b.at[w, j], recv_b.at[w, j], (x, y, 1 - c)).wait_recv()
        for cp in pending:
            cp.wait_send()

    sem = pltpu.SemaphoreType.DMA
    return pl.pallas_call(
        body, name=name, in_specs=[ANY] * n, out_specs=[ANY] * n,
        out_shape=[jax.ShapeDtypeStruct(b.shape, b.dtype) for b in bufs],
        input_output_aliases={w: w for w in range(n)},
        scratch_shapes=[sem((n, 3)), sem((n, 3)), sem((n, 3)), sem((n, 3))],
        compiler_params=pltpu.CompilerParams(has_side_effects=True),
    )(*bufs)


def exchange_halves(name, arrs, mine):
    n = len(arrs)

    def body(*refs):
        ins, outs = refs[:n], refs[n:2 * n]
        send, recv = refs[2 * n:]
        x, y, c, _ = _pos()
        sel = c if mine else 1 - c
        cps = []
        for w in range(n):
            hr = ins[w].shape[1] // 2
            cp = _remote(ins[w].at[:, pl.ds(sel * hr, hr), :], outs[w], send.at[w], recv.at[w], (x, y, 1 - c))
            cp.start()
            cps.append(cp)
        for cp in cps:
            cp.wait()

    sem = pltpu.SemaphoreType.DMA
    return pl.pallas_call(
        body, name=name, in_specs=[ANY] * n, out_specs=[ANY] * n,
        out_shape=[jax.ShapeDtypeStruct((a.shape[0], a.shape[1] // 2, a.shape[2]), a.dtype) for a in arrs],
        scratch_shapes=[sem((n,)), sem((n,))],
        compiler_params=pltpu.CompilerParams(has_side_effects=True),
    )(*arrs)


def scatter_to_chips(name, arrs):
    n = len(arrs)

    def body(*refs):
        ins, outs = refs[:n], refs[n:2 * n]
        send, recv = refs[2 * n:]
        x, y, c, m = _pos()
        chips = _other_chips(x, y)
        cps = []
        for w in range(n):
            for j, (px, py) in enumerate(chips):
                cp = _remote(ins[w].at[2 * px + py], outs[w].at[j], send.at[w, j], recv.at[w, j], (px, py, c))
                cp.start()
                cps.append(cp)
        for cp in cps:
            cp.wait()

    sem = pltpu.SemaphoreType.DMA
    return pl.pallas_call(
        body, name=name, in_specs=[ANY] * n, out_specs=[ANY] * n,
        out_shape=[jax.ShapeDtypeStruct((3,) + a.shape[1:], a.dtype) for a in arrs],
        scratch_shapes=[sem((n, 3)), sem((n, 3))],
        compiler_params=pltpu.CompilerParams(has_side_effects=True),
    )(*arrs)


def share_halves(name, fulls):
    n = len(fulls)

    def body(*refs):
        outs = refs[n:2 * n]
        send, recv = refs[2 * n:]
        x, y, c, _ = _pos()
        cps = []
        for w in range(n):
            hr = outs[w].shape[0] // 2
            blk = outs[w].at[pl.ds(c * hr, hr), :]
            cp = _remote(blk, blk, send.at[w], recv.at[w], (x, y, 1 - c))
            cp.start()
            cps.append(cp)
        for w, cp in enumerate(cps):
            hr = outs[w].shape[0] // 2
            cp.wait_send()
            blk = outs[w].at[pl.ds((1 - c) * hr, hr), :]
            _remote(blk, blk, send.at[w], recv.at[w], (x, y, 1 - c)).wait_recv()

    sem = pltpu.SemaphoreType.DMA
    return pl.pallas_call(
        body, name=name, in_specs=[ANY] * n, out_specs=[ANY] * n,
        out_shape=[jax.ShapeDtypeStruct(a.shape, a.dtype) for a in fulls],
        input_output_aliases={w: w for w in range(n)},
        scratch_shapes=[sem((n,)), sem((n,))],
        compiler_params=pltpu.CompilerParams(has_side_effects=True),
    )(*fulls)


def add_own_half(name, full, got, core, out_dtype):
    n, r, cdim = full.shape
    hr = r // 2
    tr = 8 * _blk(hr // 8, 32) if out_dtype == F32 else 16 * _blk(hr // 16, 16)
    nbh = hr // tr

    def body(c_ref, f_ref, g_ref, o_ref):
        o_ref[...] = (f_ref[...] + g_ref[...]).astype(o_ref.dtype)

    gs = pltpu.PrefetchScalarGridSpec(
        num_scalar_prefetch=1, grid=(n, nbh),
        in_specs=[pl.BlockSpec((None, tr, cdim), lambda s, i, c_ref: (s, c_ref[0] * nbh + i, 0)),
                  pl.BlockSpec((None, tr, cdim), lambda s, i, c_ref: (s, i, 0))],
        out_specs=pl.BlockSpec((None, tr, cdim), lambda s, i, c_ref: (s, i, 0)))
    return pl.pallas_call(body, name=name, grid_spec=gs, out_shape=jax.ShapeDtypeStruct((n, hr, cdim), out_dtype),
                          compiler_params=_cparams(("parallel", "parallel")))(core, full, got)


def sum_into_half(name, part, slots, chip_core, chip_order):
    _, hr, cdim = part.shape
    unit = 8 if part.dtype == F32 else 16
    tr = unit * _blk(hr // unit, 256 // unit)
    nbh = hr // tr

    def body(mc_ref, p_ref, s_ref, o_ref):
        terms = [p_ref[...].astype(F32)] + [s_ref[k].astype(F32) for k in range(3)]
        if chip_order:
            m = mc_ref[0]
            own, fx, fy, fxy = terms
            terms = [jnp.where((k ^ m) == 0, own, jnp.where((k ^ m) == 2, fx, jnp.where((k ^ m) == 1, fy, fxy)))
                     for k in range(N_CHIPS)]
        o_ref[...] = ((terms[0] + terms[1]) + terms[2]) + terms[3]

    gs = pltpu.PrefetchScalarGridSpec(
        num_scalar_prefetch=1, grid=(nbh,),
        in_specs=[pl.BlockSpec((None, tr, cdim), lambda i, mc: (mc[0], i, 0)),
                  pl.BlockSpec((3, tr, cdim), lambda i, mc: (0, i, 0))],
        out_specs=pl.BlockSpec((tr, cdim), lambda i, mc: (mc[1] * nbh + i, 0)))
    return pl.pallas_call(body, name=name, grid_spec=gs, out_shape=jax.ShapeDtypeStruct((2 * hr, cdim), F32),
                          compiler_params=_cparams(("parallel",)))(chip_core, part, slots)


WEIGHTS = ('norm_mix', 'norm_ffn', 'norm_kv', 'norm_final', 'ssm_w_in', 'ssm_a_re', 'ssm_a_im', 'ssm_log_dt',
           'ssm_b_re', 'ssm_b_im', 'ssm_c_re', 'ssm_c_im', 'ssm_d', 'ssm_w_glu', 'kv_w', 'attn_w_q', 'attn_w_o',
           'ffn_w_up', 'ffn_conv_w', 'ffn_conv_b', 'ffn_w_down')
SMALL = ('norm_mix', 'norm_ffn', 'norm_kv', 'norm_final', 'ssm_a_re', 'ssm_a_im', 'ssm_log_dt', 'ssm_b_re', 'ssm_b_im',
         'ssm_c_re', 'ssm_c_im', 'ssm_d', 'ffn_conv_w', 'ffn_conv_b')


def _pad_rows(flat, unit):
    n = flat.shape[0]
    total = -(-n // unit) * unit
    return jnp.pad(flat, (0, total - n)).reshape(total // LANES, LANES)


def kernel(x, norm_mix, norm_ffn, norm_kv, norm_final, ssm_w_in, ssm_a_re, ssm_a_im, ssm_log_dt, ssm_b_re, ssm_b_im, ssm_c_re, ssm_c_im, ssm_d, ssm_w_glu, kv_w, attn_w_q, attn_w_o, ffn_w_up, ffn_conv_w, ffn_conv_b, ffn_w_down, loss_target, m_norm_mix, m_norm_ffn, m_norm_kv, m_norm_final, m_ssm_w_in, m_ssm_a_re, m_ssm_a_im, m_ssm_log_dt, m_ssm_b_re, m_ssm_b_im, m_ssm_c_re, m_ssm_c_im, m_ssm_d, m_ssm_w_glu, m_kv_w, m_attn_w_q, m_attn_w_o, m_ffn_w_up, m_ffn_conv_w, m_ffn_conv_b, m_ffn_w_down, v_norm_mix, v_norm_ffn, v_norm_kv, v_norm_final, v_ssm_w_in, v_ssm_a_re, v_ssm_a_im, v_ssm_log_dt, v_ssm_b_re, v_ssm_b_im, v_ssm_c_re, v_ssm_c_im, v_ssm_d, v_ssm_w_glu, v_kv_w, v_attn_w_q, v_attn_w_o, v_ffn_w_up, v_ffn_conv_w, v_ffn_conv_b, v_ffn_w_down):
    a = dict(locals())
    l, d = x.shape[1], x.shape[2]
    f = ffn_conv_b.shape[1]
    fs = f // N_CHIPS
    m = 2 * lax.axis_index("x") + lax.axis_index("y")
    core = lax.axis_index("c").astype(jnp.int32).reshape(1)
    chip = m.astype(jnp.int32).reshape(1)
    chip_core = jnp.concatenate([chip, core])

    shards = [ssm_w_in[0], ssm_w_glu[0], kv_w, attn_w_q[0], attn_w_o[0], ffn_w_up.reshape(-1, ffn_w_up.shape[-1]),
              ffn_w_down.reshape(-1, d), _pad_rows(ffn_conv_w.reshape(-1), 16 * LANES)]
    bufs = [cast_into_slot(f"cast_{i}", s, chip, BF16 if i < 7 else F32) for i, s in enumerate(shards)]
    g_in, g_glu, g_kv, g_q, g_o, g_up, g_down, g_cw = gather_weights("gather_weights", bufs)
    conv_w = g_cw.reshape(N_CHIPS, -1)[:, :2 * 3 * fs].reshape(N_CHIPS, 2, 3, fs).transpose(1, 2, 0, 3).reshape(2, 3, f)
    p = dict(
        norm_mix=norm_mix, norm_ffn=norm_ffn, norm_kv=norm_kv, norm_final=norm_final,
        a_re=ssm_a_re[0], a_im=ssm_a_im[0], log_dt=ssm_log_dt[0], b_re=ssm_b_re[0], b_im=ssm_b_im[0],
        c_re=ssm_c_re[0], c_im=ssm_c_im[0], d=ssm_d[0],
        win=g_in.reshape(-1, g_in.shape[-1]), wglu4=g_glu, kvw4=g_kv, wq=g_q.reshape(-1, g_q.shape[-1]),
        wo=g_o.reshape(-1, g_o.shape[-1]), wup4=g_up.reshape(N_CHIPS, 2, d, -1),
        wdown=g_down.reshape(N_CHIPS, 2, fs, d).transpose(1, 0, 2, 3).reshape(2, f, d),
        conv_w=conv_w, conv_b=ffn_conv_b)

    loss_slab, dx, gr = _local_step(x[0], loss_target[0], p)

    def chipwise(g):
        return g.reshape(N_CHIPS, g.shape[0] // N_CHIPS, g.shape[1])

    small = {"norm_mix": gr["norm_mix"], "norm_ffn": gr["norm_ffn"], "norm_kv": gr["norm_kv"], "norm_final": gr["norm_final"],
             "ssm_a_re": gr["a_re"], "ssm_a_im": gr["a_im"], "ssm_log_dt": gr["log_dt"], "ssm_b_re": gr["b_re"],
             "ssm_b_im": gr["b_im"], "ssm_c_re": gr["c_re"], "ssm_c_im": gr["c_im"], "ssm_d": gr["d"],
             "ffn_conv_w": gr["conv_w"], "ffn_conv_b": gr["conv_b"]}
    packed = _pad_rows(jnp.concatenate([small[k].reshape(-1) for k in SMALL] + [loss_slab[0, 0:1]]), 16 * LANES)
    big = [chipwise(gr["win"]), gr["wglu4"], gr["kvw4"], chipwise(gr["wq"]), chipwise(gr["wo"]),
           gr["wup4"][0], gr["wup4"][1], chipwise(gr["wdown"][0]), chipwise(gr["wdown"][1]),
           jnp.broadcast_to(packed, (N_CHIPS,) + packed.shape)]
    payload = [BF16] * 9 + [F32]
    got = exchange_halves("rs_siblings", big, mine=False)
    part = [add_own_half(f"rs_add_{i}", g, r, core, dt) for i, (g, r, dt) in enumerate(zip(big, got, payload))]
    slots = scatter_to_chips("rs_chips", part)
    fulls = [sum_into_half(f"rs_sum_{i}", pt, s, chip_core, i == 9) for i, (pt, s) in enumerate(zip(part, slots))]
    r_in, r_glu, r_kv, r_q, r_o, r_up0, r_up1, r_down0, r_down1, total = share_halves("rs_share", fulls)
    grads = {"ssm_w_in": r_in[None], "ssm_w_glu": r_glu[None], "kv_w": r_kv, "attn_w_q": r_q[None], "attn_w_o": r_o[None],
             "ffn_w_up": jnp.stack([r_up0, r_up1]), "ffn_w_down": jnp.stack([r_down0, r_down1])}
    total = total.reshape(-1)
    off = 0
    for k in SMALL:
        n = small[k].size
        full = total[off:off + n].reshape(small[k].shape)
        off += n
        if k == "ffn_conv_w":
            full = lax.dynamic_slice_in_dim(full, m * fs, fs, axis=2)
        grads[k] = full.reshape(a[k].shape)
    loss = total[off]

    outs = {}
    for k in WEIGHTS:
        outs[k] = adamw(f"adamw_{k}", a[k], grads[k], a["m_" + k], a["v_" + k])
    return (loss, dx[None], *[grads[k] for k in WEIGHTS], *[outs[k][0] for k in WEIGHTS],
            *[outs[k][1] for k in WEIGHTS], *[outs[k][2] for k in WEIGHTS])
```

```python
import functools
import math

import jax
import jax.numpy as jnp
from jax import lax
from jax.experimental import pallas as pl
from jax.experimental.pallas import tpu as pltpu

F32 = jnp.float32
BF16 = jnp.bfloat16

EPS = 1e-6
SSM_GROUP = 16
STATE = 64
HEAD_DIM = 64
N_STREAMS = 8
MXU_DIM = 256
VMEM_LIMIT = 56 * 1024 * 1024

ADAM_LR = 0.001
ADAM_B1 = 0.9
ADAM_B2 = 0.999
ADAM_EPS = 1e-08
ADAM_WD = 0.01
ADAM_STEP = 10


def _cparams(sem=None):
    return pltpu.CompilerParams(dimension_semantics=sem, vmem_limit_bytes=VMEM_LIMIT)


class Rider:
    def __init__(self, ins, outs, alias, sems, start, finish):
        self.ins, self.outs, self.alias, self.sems, self.start, self.finish = ins, outs, alias, sems, start, finish


def _call(name, body, *, grid, in_specs, out_specs, out_shape, args, scratch=(), sem=None, rider=None):
    if rider is None:
        return pl.pallas_call(body, name=name, grid=grid, in_specs=in_specs, out_specs=out_specs, out_shape=out_shape,
                              scratch_shapes=list(scratch), compiler_params=_cparams(sem))(*args)
    nin, nout, nscr = len(in_specs), len(out_specs), len(scratch)
    nri, nro = len(rider.ins), len(rider.outs)
    steps = math.prod(grid)

    def hosted(*refs):
        ins, refs = refs[:nin], refs[nin:]
        r_in, refs = refs[:nri], refs[nri:]
        outs, refs = refs[:nout], refs[nout:]
        r_out, refs = refs[:nro], refs[nro:]
        scr, sems = refs[:nscr], refs[nscr:]
        lin = 0
        for ax, size in enumerate(grid):
            lin = lin * size + pl.program_id(ax)

        @pl.when(lin == 0)
        def _():
            rider.start(r_in, r_out, sems)

        body(*ins, *outs, *scr)

        @pl.when(lin == steps - 1)
        def _():
            rider.finish(r_in, r_out, sems)

    any_spec = pl.BlockSpec(memory_space=pl.ANY)
    dma = pltpu.SemaphoreType.DMA
    return pl.pallas_call(
        hosted, name=name, grid=grid, in_specs=list(in_specs) + [any_spec] * nri,
        out_specs=list(out_specs) + [any_spec] * nro, out_shape=list(out_shape) + list(rider.outs),
        input_output_aliases={nin + i: nout + o for i, o in rider.alias.items()},
        scratch_shapes=list(scratch) + [dma(s) for s in rider.sems],
        compiler_params=pltpu.CompilerParams(dimension_semantics=("arbitrary",) * len(grid),
                                             vmem_limit_bytes=VMEM_LIMIT, has_side_effects=True),
    )(*args, *rider.ins)


def _blk(n, want):
    b = min(n, want)
    while n % b:
        b -= 1
    return b


NN = (((1,), (0,)), ((), ()))
NT = (((1,), (1,)), ((), ()))
TN = (((0,), (0,)), ((), ()))


def _matmul(name, a, b, *, a_spec, b_spec, o_spec, out_shape, grid, dn, nk, out_dtype, rider=None):
    nax = len(grid)

    def body(a_ref, b_ref, o_ref, *scr):
        part = lax.dot_general(a_ref[...], b_ref[...], dn, preferred_element_type=F32)
        if nk == 1:
            o_ref[...] = part.astype(o_ref.dtype)
            return
        acc = scr[0] if scr else o_ref
        k = pl.program_id(nax - 1)

        @pl.when(k == 0)
        def _():
            acc[...] = part

        @pl.when(k > 0)
        def _():
            acc[...] += part

        if scr:
            @pl.when(k == nk - 1)
            def _():
                o_ref[...] = acc[...].astype(o_ref.dtype)

    scratch = []
    if nk > 1 and out_dtype != F32:
        blk = tuple(d for d in o_spec.block_shape if d is not None)
        scratch = [pltpu.VMEM(blk, F32)]
    sem = ("parallel",) * (nax - 1) + ("arbitrary",)
    res = _call(name, body, grid=grid, in_specs=[a_spec, b_spec], out_specs=[o_spec],
                out_shape=[jax.ShapeDtypeStruct(out_shape, out_dtype)], scratch=scratch, sem=sem, args=(a, b), rider=rider)
    return res[0] if rider is None else (res[0], res[1:])


def mm_nn(name, a, w, out_dtype, tm=512):
    m, k = a.shape
    n = w.shape[1]
    tm = _blk(m, tm)
    return _matmul(name, a, w, a_spec=pl.BlockSpec((tm, k), lambda i, kk: (i, 0)),
                   b_spec=pl.BlockSpec((k, n), lambda i, kk: (0, 0)),
                   o_spec=pl.BlockSpec((tm, n), lambda i, kk: (i, 0)),
                   out_shape=(m, n), grid=(m // tm, 1), dn=NN, nk=1, out_dtype=out_dtype)


def mm_nt(name, a, w, out_dtype, tm=512, tn=None):
    m, n = a.shape
    k = w.shape[0]
    tm = _blk(m, tm)
    tn = k if tn is None else tn
    return _matmul(name, a, w, a_spec=pl.BlockSpec((tm, n), lambda i, j, kk: (i, 0)),
                   b_spec=pl.BlockSpec((tn, n), lambda i, j, kk: (j, 0)),
                   o_spec=pl.BlockSpec((tm, tn), lambda i, j, kk: (i, j)),
                   out_shape=(m, k), grid=(m // tm, k // tn, 1), dn=NT, nk=1, out_dtype=out_dtype)


def mm_tn(name, a, b, tmo=None, tk=512):
    l, m = a.shape
    n = b.shape[1]
    tk = _blk(l, tk)
    tmo = m if tmo is None else tmo
    nk = l // tk
    return _matmul(name, a, b, a_spec=pl.BlockSpec((tk, tmo), lambda i, kk: (kk, i)),
                   b_spec=pl.BlockSpec((tk, n), lambda i, kk: (kk, 0)),
                   o_spec=pl.BlockSpec((tmo, n), lambda i, kk: (i, 0)),
                   out_shape=(m, n), grid=(m // tmo, nk), dn=TN, nk=nk, out_dtype=F32)


def mm_nn_colshard(name, a, w4, out_dtype, tm=512):
    m, k = a.shape
    s, _, ns = w4.shape
    tm = _blk(m, tm)
    return _matmul(name, a, w4, a_spec=pl.BlockSpec((tm, k), lambda i, j, kk: (i, 0)),
                   b_spec=pl.BlockSpec((None, k, ns), lambda i, j, kk: (j, 0, 0)),
                   o_spec=pl.BlockSpec((tm, ns), lambda i, j, kk: (i, j)),
                   out_shape=(m, s * ns), grid=(m // tm, s, 1), dn=NN, nk=1, out_dtype=out_dtype)


def mm_nt_colshard(name, a, w4, out_dtype, tm=512):
    m, _ = a.shape
    s, k, ns = w4.shape
    tm = _blk(m, tm)
    return _matmul(name, a, w4, a_spec=pl.BlockSpec((tm, ns), lambda i, kk: (i, kk)),
                   b_spec=pl.BlockSpec((None, k, ns), lambda i, kk: (kk, 0, 0)),
                   o_spec=pl.BlockSpec((tm, k), lambda i, kk: (i, 0)),
                   out_shape=(m, k), grid=(m // tm, s), dn=NT, nk=s, out_dtype=out_dtype)


def mm_tn_colshard(name, a, b, s, tk=512):
    l, k = a.shape
    ns = b.shape[1] // s
    tk = _blk(l, tk)
    nk = l // tk
    return _matmul(name, a, b, a_spec=pl.BlockSpec((tk, k), lambda j, kk: (kk, 0)),
                   b_spec=pl.BlockSpec((tk, ns), lambda j, kk: (kk, j)),
                   o_spec=pl.BlockSpec((None, k, ns), lambda j, kk: (j, 0, 0)),
                   out_shape=(s, k, ns), grid=(s, nk), dn=TN, nk=nk, out_dtype=F32)


def mm_up(name, h, wup4, layer, tm=512, rider=None):
    m, k = h.shape
    s, _, _, ns = wup4.shape
    half = s // 2
    tm = _blk(m, tm)
    return _matmul(name, h, wup4, a_spec=pl.BlockSpec((tm, k), lambda i, j, kk: (i, 0)),
                   b_spec=pl.BlockSpec((None, None, k, ns), lambda i, j, kk: (j, layer, 0, 0)),
                   o_spec=pl.BlockSpec((None, tm, ns), lambda i, j, kk: (j // half, i, j % half)),
                   out_shape=(2, m, half * ns), grid=(m // tm, s, 1), dn=NN, nk=1, out_dtype=BF16, rider=rider)


def mm_up_nt(name, dgu, wup4, layer, tm=512, rider=None):
    _, m, _ = dgu.shape
    s, _, k, ns = wup4.shape
    half = s // 2
    tm = _blk(m, tm)
    return _matmul(name, dgu, wup4,
                   a_spec=pl.BlockSpec((None, tm, ns), lambda i, kk: (kk // half, i, kk % half)),
                   b_spec=pl.BlockSpec((None, None, k, ns), lambda i, kk: (kk, layer, 0, 0)),
                   o_spec=pl.BlockSpec((tm, k), lambda i, kk: (i, 0)),
                   out_shape=(m, k), grid=(m // tm, s), dn=NT, nk=s, out_dtype=F32, rider=rider)


def mm_up_tn(name, h, dgu, s, tk=512):
    l, k = h.shape
    half = s // 2
    ns = dgu.shape[2] // half
    tk = _blk(l, tk)
    nk = l // tk
    return _matmul(name, h, dgu, a_spec=pl.BlockSpec((tk, k), lambda j, kk: (kk, 0)),
                   b_spec=pl.BlockSpec((None, tk, ns), lambda j, kk: (j // half, kk, j % half)),
                   o_spec=pl.BlockSpec((None, k, ns), lambda j, kk: (j, 0, 0)),
                   out_shape=(s, k, ns), grid=(s, nk), dn=TN, nk=nk, out_dtype=F32)


def add_rmsnorm(name, x, r, gains, tm=512):
    l, d = x.shape
    tm = _blk(l, tm)
    ng = len(gains)
    has_r = r is not None

    def body(*refs):
        x_ref = refs[0]
        pos = 1
        xs = x_ref[...]
        if has_r:
            xs = xs + refs[pos][...]
            pos += 1
        g_refs = refs[pos:pos + ng]
        outs = refs[pos + ng:]
        o = 0
        if has_r:
            outs[0][...] = xs
            o = 1
        xh = xs * lax.rsqrt(jnp.mean(xs * xs, axis=-1, keepdims=True) + EPS)
        for gi in range(ng):
            outs[o + gi][...] = (xh * g_refs[gi][...]).astype(BF16)

    row = pl.BlockSpec((tm, d), lambda i: (i, 0))
    gsp = pl.BlockSpec((1, d), lambda i: (0, 0))
    ins = [x] + ([r] if has_r else []) + [g.reshape(1, d) for g in gains]
    in_specs = [row] * (1 + has_r) + [gsp] * ng
    out_shape = ([jax.ShapeDtypeStruct((l, d), F32)] if has_r else []) + [jax.ShapeDtypeStruct((l, d), BF16)] * ng
    return pl.pallas_call(body, name=name, grid=(l // tm,), in_specs=in_specs,
                          out_specs=[row] * len(out_shape), out_shape=out_shape,
                          compiler_params=_cparams(("parallel",)))(*ins)


def norm_bwd(name, x, dres, pairs, tm=512):
    l, d = x.shape
    tm = _blk(l, tm)
    npair = len(pairs)
    has_r = dres is not None

    def body(*refs):
        x_ref = refs[0]
        pos = 1
        xs = x_ref[...]
        dx = jnp.zeros_like(xs)
        if has_r:
            dx = refs[pos][...]
            pos += 1
        ins = refs[pos:pos + 2 * npair]
        outs = refs[pos + 2 * npair:]
        rs = lax.rsqrt(jnp.mean(xs * xs, axis=-1, keepdims=True) + EPS)
        xh = xs * rs
        first = pl.program_id(0) == 0
        for pi in range(npair):
            g = ins[2 * pi][...]
            dh = ins[2 * pi + 1][...].astype(F32)
            dgp = jnp.sum(dh * xh, axis=0, keepdims=True)
            dg_ref = outs[1 + pi]

            @pl.when(first)
            def _():
                dg_ref[...] = dgp

            @pl.when(jnp.logical_not(first))
            def _():
                dg_ref[...] += dgp

            dxh = dh * g
            dx = dx + rs * (dxh - xh * jnp.mean(dxh * xh, axis=-1, keepdims=True))
        outs[0][...] = dx

    row = pl.BlockSpec((tm, d), lambda i: (i, 0))
    gsp = pl.BlockSpec((1, d), lambda i: (0, 0))
    ins = [x] + ([dres] if has_r else [])
    in_specs = [row] * (1 + has_r)
    for g, dh in pairs:
        ins += [g.reshape(1, d), dh]
        in_specs += [gsp, row]
    out_shape = [jax.ShapeDtypeStruct((l, d), F32)] + [jax.ShapeDtypeStruct((1, d), F32)] * npair
    return pl.pallas_call(body, name=name, grid=(l // tm,), in_specs=in_specs,
                          out_specs=[row] + [gsp] * npair, out_shape=out_shape,
                          compiler_params=_cparams(("arbitrary",)))(*ins)


def final_loss(name, x, r, tgt, g, tm=512):
    l, d = x.shape
    tm = _blk(l, tm)

    def body(x_ref, r_ref, t_ref, g_ref, loss_ref, dx_ref, dg_ref):
        xs = x_ref[...] + r_ref[...]
        gg = g_ref[...]
        rs = lax.rsqrt(jnp.mean(xs * xs, axis=-1, keepdims=True) + EPS)
        xh = xs * rs
        e = xh * gg - t_ref[...]
        part = 0.5 * jnp.sum(jnp.mean(e * e, axis=-1, keepdims=True), axis=0, keepdims=True)
        dy = e * (1.0 / d)
        dgp = jnp.sum(dy * xh, axis=0, keepdims=True)
        dxh = dy * gg
        dx_ref[...] = rs * (dxh - xh * jnp.mean(dxh * xh, axis=-1, keepdims=True))
        first = pl.program_id(0) == 0

        @pl.when(first)
        def _():
            loss_ref[...] = jnp.broadcast_to(part, loss_ref.shape)
            dg_ref[...] = dgp

        @pl.when(jnp.logical_not(first))
        def _():
            loss_ref[...] += jnp.broadcast_to(part, loss_ref.shape)
            dg_ref[...] += dgp

    row = pl.BlockSpec((tm, d), lambda i: (i, 0))
    gsp = pl.BlockSpec((1, d), lambda i: (0, 0))
    lsp = pl.BlockSpec((8, 128), lambda i: (0, 0))
    return pl.pallas_call(
        body, name=name, grid=(l // tm,), in_specs=[row, row, row, gsp], out_specs=[lsp, row, gsp],
        out_shape=[jax.ShapeDtypeStruct((8, 128), F32), jax.ShapeDtypeStruct((l, d), F32),
                   jax.ShapeDtypeStruct((1, d), F32)],
        compiler_params=_cparams(("arbitrary",)))(x, r, tgt, g.reshape(1, d))


def _sigmoid(x):
    return 0.5 * jnp.tanh(0.5 * x) + 0.5


def glu_fwd(name, z, tm=512):
    l, d2 = z.shape
    d = d2 // 2
    tm = _blk(l, tm)

    def body(z_ref, o_ref):
        o_ref[...] = z_ref[:, :d] * _sigmoid(z_ref[:, d:])

    return pl.pallas_call(body, name=name, grid=(l // tm,),
                          in_specs=[pl.BlockSpec((tm, d2), lambda i: (i, 0))],
                          out_specs=pl.BlockSpec((tm, d), lambda i: (i, 0)),
                          out_shape=jax.ShapeDtypeStruct((l, d), F32),
                          compiler_params=_cparams(("parallel",)))(z)


def glu_bwd(name, z, dm, tm=512):
    l, d2 = z.shape
    d = d2 // 2
    tm = _blk(l, tm)

    def body(z_ref, dm_ref, o_ref):
        sg = _sigmoid(z_ref[:, d:])
        g = dm_ref[...]
        o_ref[:, :d] = (g * sg).astype(BF16)
        o_ref[:, d:] = (g * z_ref[:, :d] * sg * (1.0 - sg)).astype(BF16)

    return pl.pallas_call(body, name=name, grid=(l // tm,),
                          in_specs=[pl.BlockSpec((tm, d2), lambda i: (i, 0)), pl.BlockSpec((tm, d), lambda i: (i, 0))],
                          out_specs=pl.BlockSpec((tm, d2), lambda i: (i, 0)),
                          out_shape=jax.ShapeDtypeStruct((l, d2), BF16),
                          compiler_params=_cparams(("parallel",)))(z, dm)


_GELU_C = math.sqrt(2.0 / math.pi)


def _gelu(y):
    return 0.5 * y * (1.0 + jnp.tanh(_GELU_C * (y + 0.044715 * y * y * y)))


def _gelu_grad(y):
    t = jnp.tanh(_GELU_C * (y + 0.044715 * y * y * y))
    return 0.5 * (1.0 + t) + 0.5 * y * (1.0 - t * t) * _GELU_C * (1.0 + 3.0 * 0.044715 * y * y)


def gelu_bwd(name, ypre, dyg, tm=512):
    l, d = ypre.shape
    tm = _blk(l, tm)

    def body(y_ref, d_ref, o_ref):
        o_ref[...] = (d_ref[...] * _gelu_grad(y_ref[...])).astype(BF16)

    row = pl.BlockSpec((tm, d), lambda i: (i, 0))
    return pl.pallas_call(body, name=name, grid=(l // tm,), in_specs=[row, row], out_specs=row,
                          out_shape=jax.ShapeDtypeStruct((l, d), BF16),
                          compiler_params=_cparams(("parallel",)))(ypre, dyg)


def _shift_down(x, n):
    rolled = pltpu.roll(x, n, axis=0)
    rows = lax.broadcasted_iota(jnp.int32, (8, x.shape[1]), 0)
    return jnp.concatenate([jnp.where(rows >= n, rolled[:8], 0.0), rolled[8:]], axis=0)


def _shift_up(x, n):
    l = x.shape[0]
    rolled = pltpu.roll(x, l - n, axis=0)
    rows = lax.broadcasted_iota(jnp.int32, (8, x.shape[1]), 0)
    return jnp.concatenate([rolled[:l - 8], jnp.where(rows < 8 - n, rolled[l - 8:], 0.0)], axis=0)


def ffn_mid_fwd(name, gu, cw, cb, tc=128):
    _, l, f = gu.shape
    tc = _blk(f, tc)

    def body(gu_ref, w_ref, b_ref, a_ref):
        g = gu_ref[0].astype(F32)
        u = gu_ref[1].astype(F32)
        gc = w_ref[0:1, :] * _shift_down(g, 2) + w_ref[1:2, :] * _shift_down(g, 1) + w_ref[2:3, :] * g + b_ref[...]
        a_ref[...] = (gc * _sigmoid(gc) * u).astype(BF16)

    return pl.pallas_call(
        body, name=name, grid=(f // tc,),
        in_specs=[pl.BlockSpec((2, l, tc), lambda c: (0, 0, c)), pl.BlockSpec((3, tc), lambda c: (0, c)),
                  pl.BlockSpec((1, tc), lambda c: (0, c))],
        out_specs=pl.BlockSpec((l, tc), lambda c: (0, c)),
        out_shape=jax.ShapeDtypeStruct((l, f), BF16),
        compiler_params=_cparams(("parallel",)))(gu, cw, cb)


def ffn_mid_bwd(name, gu, da, cw, cb, tc=128):
    _, l, f = gu.shape
    tc = _blk(f, tc)

    def body(gu_ref, da_ref, w_ref, b_ref, dgu_ref, dw_ref, db_ref):
        g = gu_ref[0].astype(F32)
        u = gu_ref[1].astype(F32)
        g1 = _shift_down(g, 1)
        g2 = _shift_down(g, 2)
        w0, w1, w2 = w_ref[0:1, :], w_ref[1:2, :], w_ref[2:3, :]
        gc = w0 * g2 + w1 * g1 + w2 * g + b_ref[...]
        sg = _sigmoid(gc)
        da = da_ref[...].astype(F32)
        dgu_ref[1] = (da * gc * sg).astype(BF16)
        dgc = da * u * (sg * (1.0 + gc * (1.0 - sg)))
        dgu_ref[0] = (w2 * dgc + w1 * _shift_up(dgc, 1) + w0 * _shift_up(dgc, 2)).astype(BF16)
        dw_ref[0:1, :] = jnp.sum(dgc * g2, axis=0, keepdims=True)
        dw_ref[1:2, :] = jnp.sum(dgc * g1, axis=0, keepdims=True)
        dw_ref[2:3, :] = jnp.sum(dgc * g, axis=0, keepdims=True)
        db_ref[...] = jnp.sum(dgc, axis=0, keepdims=True)

    return pl.pallas_call(
        body, name=name, grid=(f // tc,),
        in_specs=[pl.BlockSpec((2, l, tc), lambda c: (0, 0, c)), pl.BlockSpec((l, tc), lambda c: (0, c)),
                  pl.BlockSpec((3, tc), lambda c: (0, c)), pl.BlockSpec((1, tc), lambda c: (0, c))],
        out_specs=[pl.BlockSpec((2, l, tc), lambda c: (0, 0, c)), pl.BlockSpec((3, tc), lambda c: (0, c)),
                   pl.BlockSpec((1, tc), lambda c: (0, c))],
        out_shape=[jax.ShapeDtypeStruct((2, l, f), BF16), jax.ShapeDtypeStruct((3, f), F32),
                   jax.ShapeDtypeStruct((1, f), F32)],
        compiler_params=_cparams(("parallel",)))(gu, da, cw, cb)


SCAN_LANES = 512


def ssm_discretize(name, a_re, a_im, ldt, b_re, b_im):
    dt = jnp.exp(ldt)
    mag = jnp.exp(a_re * dt)
    ab_re = mag * jnp.cos(a_im * dt)
    ab_im = mag * jnp.sin(a_im * dt)
    den = a_re * a_re + a_im * a_im
    f_re = ((ab_re - 1.0) * a_re + ab_im * a_im) / den
    f_im = (ab_im * a_re - (ab_re - 1.0) * a_im) / den
    return ab_re, ab_im, f_re * b_re - f_im * b_im, f_re * b_im + f_im * b_re


def ssm_disc_fwd(name, a_re, a_im, ldt, b_re, b_im):
    def body(ar, ai, ld, br, bi, o_ar, o_ai, o_br, o_bi):
        r = ssm_discretize(None, ar[...], ai[...], ld[...], br[...], bi[...])
        o_ar[...], o_ai[...], o_br[...], o_bi[...] = r

    sd = jax.ShapeDtypeStruct
    return pl.pallas_call(body, name=name,
                          out_shape=[sd(a_re.shape, F32), sd(a_re.shape, F32), sd(b_re.shape, F32), sd(b_re.shape, F32)],
                          compiler_params=_cparams())(a_re, a_im, ldt, b_re, b_im)


def ssm_disc_bwd(name, a_re, a_im, ldt, b_re, b_im, d_ar, d_ai, d_br, d_bi):
    def body(ar, ai, ld, br, bi, g_ar, g_ai, g_br, g_bi, o_ar, o_ai, o_ld, o_br, o_bi):
        fn = functools.partial(ssm_discretize, None)
        _, vjp = jax.vjp(fn, ar[...], ai[...], ld[...], br[...], bi[...])
        r = vjp((g_ar[...], g_ai[...], g_br[...], g_bi[...]))
        o_ar[...], o_ai[...], o_ld[...], o_br[...], o_bi[...] = r

    sd = jax.ShapeDtypeStruct
    return pl.pallas_call(body, name=name,
                          out_shape=[sd(a_re.shape, F32)] * 3 + [sd(b_re.shape, F32)] * 2,
                          compiler_params=_cparams())(a_re, a_im, ldt, b_re, b_im, d_ar, d_ai, d_br, d_bi)


def _drive(x_ref, wre_ref, wim_ref, dre_ref, dim_ref):
    xb = x_ref[...]
    dre_ref[...] = jnp.dot(xb, wre_ref[...], preferred_element_type=F32)
    dim_ref[...] = jnp.dot(xb, wim_ref[...], preferred_element_type=F32)


def _ssm_dims(l, wre, tj):
    nb, kb, nsb = wre.shape
    jn = l // N_STREAMS
    tj = _blk(jn, tj)
    return nb, kb, nsb, jn, tj, tj * N_STREAMS, jn // tj


def _scan(dre_ref, dim_ref, st_re, st_im, a_ref, tj, reverse, write):
    ns = dre_ref.shape[1]
    cw = min(SCAN_LANES, ns)
    for cb in range(ns // cw):
        sl = slice(cb * cw, (cb + 1) * cw)
        ar = jnp.broadcast_to(a_ref[0:1, sl], (N_STREAMS, cw))
        ai = jnp.broadcast_to(a_ref[1:2, sl], (N_STREAMS, cw))

        def step(jj, carry, sl=sl, ar=ar, ai=ai):
            sr, si = carry
            j = (tj - 1 - jj) if reverse else jj
            off = pl.multiple_of(j * N_STREAMS, N_STREAMS)
            nr = ar * sr - ai * si + dre_ref[pl.ds(off, N_STREAMS), sl]
            ni = ar * si + ai * sr + dim_ref[pl.ds(off, N_STREAMS), sl]
            if write:
                dre_ref[pl.ds(off, N_STREAMS), sl] = nr
                dim_ref[pl.ds(off, N_STREAMS), sl] = ni
            return nr, ni

        sr, si = lax.fori_loop(0, tj, step, (st_re[:, sl], st_im[:, sl]))
        st_re[:, sl] = sr
        st_im[:, sl] = si


def ssm_pass1(name, x, wre, wim, a2, *, reverse, tj=64, rider=None):
    l, d = x.shape
    nb, kb, nsb, jn, tj, r, nblk = _ssm_dims(l, wre, tj)
    ns = nb * nsb

    def body(x_ref, wre_ref, wim_ref, a_ref, cre_ref, cim_ref, dre, dim, st_re, st_im):
        i = pl.program_id(1)

        @pl.when(i == 0)
        def _():
            st_re[...] = jnp.zeros_like(st_re)
            st_im[...] = jnp.zeros_like(st_im)

        _drive(x_ref, wre_ref, wim_ref, dre, dim)
        _scan(dre, dim, st_re, st_im, a_ref, tj, reverse, False)

        @pl.when(i == nblk - 1)
        def _():
            pr, pi = a_ref[0:1, :], a_ref[1:2, :]
            rr, ri = jnp.ones_like(pr), jnp.zeros_like(pr)
            e = jn
            while e:
                if e & 1:
                    rr, ri = rr * pr - ri * pi, rr * pi + ri * pr
                pr, pi = pr * pr - pi * pi, 2.0 * pr * pi
                e >>= 1
            order = range(N_STREAMS - 1, -1, -1) if reverse else range(N_STREAMS)
            cr = jnp.zeros_like(rr)
            ci = jnp.zeros_like(rr)
            for s in order:
                cre_ref[s:s + 1, :] = cr
                cim_ref[s:s + 1, :] = ci
                fr, fi = st_re[s:s + 1, :], st_im[s:s + 1, :]
                cr, ci = fr + rr * cr - ri * ci, fi + rr * ci + ri * cr

    blk = (lambda b, i: (nblk - 1 - i, b)) if reverse else (lambda b, i: (i, b))
    w3 = pl.BlockSpec((None, kb, nsb), lambda b, i: (b, 0, 0))
    st = pl.BlockSpec((N_STREAMS, nsb), lambda b, i: (0, b))
    return _call(
        name, body, grid=(nb, nblk),
        in_specs=[pl.BlockSpec((r, kb), blk), w3, w3, pl.BlockSpec((2, nsb), lambda b, i: (0, b))],
        out_specs=[st, st], out_shape=[jax.ShapeDtypeStruct((N_STREAMS, ns), F32)] * 2,
        scratch=[pltpu.VMEM((r, nsb), F32), pltpu.VMEM((r, nsb), F32),
                 pltpu.VMEM((N_STREAMS, nsb), F32), pltpu.VMEM((N_STREAMS, nsb), F32)],
        sem=("parallel", "arbitrary"), args=(x, wre, wim, a2), rider=rider)


def ssm_fwd2(name, u, bre, bim, a2, init_re, init_im, cre, cim, dskip, *, tj=64, rider=None):
    l, d = u.shape
    nb, kb, nsb, jn, tj, r, nblk = _ssm_dims(l, bre, tj)
    ns = nb * nsb

    def body(u_ref, bre_ref, bim_ref, a_ref, ire_ref, iim_ref, cre_ref, cim_ref, d_ref,
             sre_ref, sim_ref, y_ref, yg_ref, st_re, st_im):
        @pl.when(pl.program_id(1) == 0)
        def _():
            st_re[...] = ire_ref[...]
            st_im[...] = iim_ref[...]

        _drive(u_ref, bre_ref, bim_ref, sre_ref, sim_ref)
        _scan(sre_ref, sim_ref, st_re, st_im, a_ref, tj, False, True)
        y = (jnp.dot(sre_ref[...].astype(BF16), cre_ref[...], preferred_element_type=F32)
             - jnp.dot(sim_ref[...].astype(BF16), cim_ref[...], preferred_element_type=F32)
             + d_ref[...] * u_ref[...].astype(F32))
        y_ref[...] = y
        yg_ref[...] = _gelu(y).astype(BF16)

    rows = lambda w: pl.BlockSpec((r, w), lambda b, i: (i, b))
    w3 = pl.BlockSpec((None, kb, nsb), lambda b, i: (b, 0, 0))
    c3 = pl.BlockSpec((None, nsb, kb), lambda b, i: (b, 0, 0))
    st = pl.BlockSpec((N_STREAMS, nsb), lambda b, i: (0, b))
    return _call(
        name, body, grid=(nb, nblk),
        in_specs=[rows(kb), w3, w3, pl.BlockSpec((2, nsb), lambda b, i: (0, b)), st, st, c3, c3,
                  pl.BlockSpec((1, kb), lambda b, i: (0, b))],
        out_specs=[rows(nsb), rows(nsb), rows(kb), rows(kb)],
        out_shape=[jax.ShapeDtypeStruct((l, ns), F32), jax.ShapeDtypeStruct((l, ns), F32),
                   jax.ShapeDtypeStruct((l, d), F32), jax.ShapeDtypeStruct((l, d), BF16)],
        scratch=[pltpu.VMEM((N_STREAMS, nsb), F32), pltpu.VMEM((N_STREAMS, nsb), F32)],
        sem=("parallel", "arbitrary"), args=(u, bre, bim, a2, init_re, init_im, cre, cim, dskip), rider=rider)


def ssm_bwd2(name, dy, u, sre, sim, ctre, ctim, a2c, init_re, init_im, fre, fim, bre, bim, dskip, *, tj=64, rider=None):
    l, d = u.shape
    nb, kb, nsb, jn, tj, r, nblk = _ssm_dims(l, bre, tj)
    ns = nb * nsb

    def body(dy_ref, u_ref, sre_ref, sim_ref, pre_ref, pim_ref, ctre_ref, ctim_ref, a_ref, ire_ref, iim_ref,
             fre_ref, fim_ref, bre_ref, bim_ref, d_ref,
             du_ref, dbre_ref, dbim_ref, dcre_ref, dcim_ref, dare_ref, daim_ref, dd_ref,
             lre, lim, st_re, st_im):
        i = pl.program_id(1)
        first = i == 0

        @pl.when(first)
        def _():
            st_re[...] = ire_ref[...]
            st_im[...] = iim_ref[...]
            dbre_ref[...] = jnp.zeros_like(dbre_ref)
            dbim_ref[...] = jnp.zeros_like(dbim_ref)
            dcre_ref[...] = jnp.zeros_like(dcre_ref)
            dcim_ref[...] = jnp.zeros_like(dcim_ref)
            dare_ref[...] = jnp.zeros_like(dare_ref)
            daim_ref[...] = jnp.zeros_like(daim_ref)
            dd_ref[...] = jnp.zeros_like(dd_ref)

        _drive(dy_ref, ctre_ref, ctim_ref, lre, lim)
        _scan(lre, lim, st_re, st_im, a_ref, tj, True, True)

        is_t0 = i == nblk - 1
        cw = min(SCAN_LANES, nsb)
        for cb in range(nsb // cw):
            sl = slice(cb * cw, (cb + 1) * cw)
            p_r = jnp.where(is_t0, fre_ref[:, sl], pre_ref[:, sl])
            p_i = jnp.where(is_t0, fim_ref[:, sl], pim_ref[:, sl])
            l_r, l_i = lre[0:N_STREAMS, sl], lim[0:N_STREAMS, sl]
            acc = (l_r * p_r + l_i * p_i, l_i * p_r - l_r * p_i)

            def step(jj, carry, sl=sl):
                a_r, a_i = carry
                off = pl.multiple_of(jj * N_STREAMS, N_STREAMS)
                prev = pl.multiple_of((jj - 1) * N_STREAMS, N_STREAMS)
                l_r, l_i = lre[pl.ds(off, N_STREAMS), sl], lim[pl.ds(off, N_STREAMS), sl]
                p_r, p_i = sre_ref[pl.ds(prev, N_STREAMS), sl], sim_ref[pl.ds(prev, N_STREAMS), sl]
                return a_r + l_r * p_r + l_i * p_i, a_i + l_i * p_r - l_r * p_i

            a_r, a_i = lax.fori_loop(1, tj, step, acc)
            dare_ref[:, sl] += a_r
            daim_ref[:, sl] += a_i

        dyf = dy_ref[...].astype(F32)
        uf = u_ref[...].astype(F32)
        dd_ref[...] += jnp.sum(dyf * uf, axis=0, keepdims=True)
        lrb = lre[...].astype(BF16)
        lib = lim[...].astype(BF16)
        ub = u_ref[...]
        dyb = dy_ref[...]
        dbre_ref[...] += lax.dot_general(ub, lrb, TN, preferred_element_type=F32)
        dbim_ref[...] += lax.dot_general(ub, lib, TN, preferred_element_type=F32)
        dcre_ref[...] += lax.dot_general(sre_ref[...].astype(BF16), dyb, TN, preferred_element_type=F32)
        dcim_ref[...] -= lax.dot_general(sim_ref[...].astype(BF16), dyb, TN, preferred_element_type=F32)
        du = (lax.dot_general(lrb, bre_ref[...], NT, preferred_element_type=F32)
              + lax.dot_general(lib, bim_ref[...], NT, preferred_element_type=F32)
              + d_ref[...] * dyf)
        du_ref[...] = du.astype(BF16)

    rev = lambda w: pl.BlockSpec((r, w), lambda b, i: (nblk - 1 - i, b))
    prev_tile = pl.BlockSpec((N_STREAMS, nsb), lambda b, i: (jnp.maximum((nblk - 1 - i) * tj - 1, 0), b))
    st = pl.BlockSpec((N_STREAMS, nsb), lambda b, i: (0, b))
    w3 = pl.BlockSpec((None, kb, nsb), lambda b, i: (b, 0, 0))
    c3 = pl.BlockSpec((None, nsb, kb), lambda b, i: (b, 0, 0))
    dsp = pl.BlockSpec((1, kb), lambda b, i: (0, b))
    return _call(
        name, body, grid=(nb, nblk),
        in_specs=[rev(kb), rev(kb), rev(nsb), rev(nsb), prev_tile, prev_tile, w3, w3,
                  pl.BlockSpec((2, nsb), lambda b, i: (0, b)), st, st, st, st, w3, w3, dsp],
        out_specs=[rev(kb), w3, w3, c3, c3, st, st, dsp],
        out_shape=[jax.ShapeDtypeStruct((l, d), BF16), jax.ShapeDtypeStruct((nb, kb, nsb), F32),
                   jax.ShapeDtypeStruct((nb, kb, nsb), F32), jax.ShapeDtypeStruct((nb, nsb, kb), F32),
                   jax.ShapeDtypeStruct((nb, nsb, kb), F32), jax.ShapeDtypeStruct((N_STREAMS, ns), F32),
                   jax.ShapeDtypeStruct((N_STREAMS, ns), F32), jax.ShapeDtypeStruct((1, d), F32)],
        scratch=[pltpu.VMEM((r, nsb), F32), pltpu.VMEM((r, nsb), F32),
                 pltpu.VMEM((N_STREAMS, nsb), F32), pltpu.VMEM((N_STREAMS, nsb), F32)],
        sem=("parallel", "arbitrary"),
        args=(dy, u, sre, sim, sre, sim, ctre, ctim, a2c, init_re, init_im, fre, fim, bre, bim, dskip), rider=rider)


ATT_TILE = 256
ATT_HEADS = 4
EXP_ZERO_BELOW = -104.0
LANES = 128


def _split_dot(x, tri):
    hi = x.astype(BF16)
    lo = (x - hi.astype(F32)).astype(BF16)
    return jnp.dot(hi, tri, preferred_element_type=F32) + jnp.dot(lo, tri, preferred_element_type=F32)


def _sb_logs(z, causal):
    sp = jnp.maximum(z, 0.0) + jnp.log(1.0 + jnp.exp(-jnp.abs(z)))
    l1m = -sp
    if causal is not None:
        l1m = jnp.where(causal, l1m, 0.0)
    return z - sp, l1m


def attn_fwd(name, q, kv, t=ATT_TILE, rider=None):
    l, dm = q.shape
    dh = HEAD_DIM
    h = dm // dh
    t = _blk(l, t)
    hb = _blk(h, ATT_HEADS)
    wb = hb * dh
    scale = dh ** -0.5

    def body(q_ref, k_ref, v_ref, o_ref, tot_ref):
        i = pl.program_id(1)
        hd = lambda g: slice(g * dh, (g + 1) * dh)
        qs = [(q_ref[:, hd(g)].astype(F32) * scale).astype(BF16) for g in range(hb)]
        row = lax.broadcasted_iota(jnp.int32, (t, t), 0)
        col = lax.broadcasted_iota(jnp.int32, (t, t), 1)
        tri_gt = jnp.where(row > col, 1.0, 0.0).astype(BF16)
        causal = col < row

        def tile(jb, carry, mask):
            off = pl.multiple_of(jb * t, t)
            heads = range(hb)
            z = [lax.dot_general(qs[g], k_ref[pl.ds(off, t), hd(g)], NT, preferred_element_type=F32) for g in heads]
            logs = [_sb_logs(z[g], mask) for g in heads]
            rem = [_split_dot(logs[g][1], tri_gt) for g in heads]
            w = [jnp.exp(logs[g][0] + rem[g] + carry[g][1]) for g in heads]
            if mask is not None:
                w = [jnp.where(mask, w[g], 0.0) for g in heads]
            pv = [jnp.dot(w[g].astype(BF16), v_ref[pl.ds(off, t), hd(g)], preferred_element_type=F32) for g in heads]
            return tuple((carry[g][0] + pv[g], carry[g][1] + rem[g][:, 0:1] + logs[g][1][:, 0:1]) for g in heads)

        def live(carry):
            top = carry[0][1]
            for g in range(1, hb):
                top = jnp.maximum(top, carry[g][1])
            return (jnp.max(top) > EXP_ZERO_BELOW).astype(jnp.int32)

        def more(c):
            it, alive, _ = c
            return jnp.logical_and(it < i, alive > 0)

        def step(c):
            it, _, carry = c
            carry = tile(i - 1 - it, carry, None)
            return it + 1, live(carry), carry

        carry = tile(i, ((jnp.zeros((t, dh), F32), jnp.zeros((t, 1), F32)),) * hb, causal)
        done, _, carry = lax.while_loop(more, step, (jnp.int32(0), live(carry), carry))
        tot_ref[...] = jnp.zeros_like(tot_ref)
        for g, (acc, run) in enumerate(carry):
            o_ref[:, hd(g)] = acc.astype(BF16)
            tot_ref[:, g:g + 1] = run
        tot_ref[:, hb:hb + 1] = jnp.full((t, 1), done, jnp.int32).astype(F32)

    qsp = pl.BlockSpec((t, wb), lambda hh, i: (i, hh))
    ksp = pl.BlockSpec((None, l, wb), lambda hh, i: (0, 0, hh))
    vsp = pl.BlockSpec((None, l, wb), lambda hh, i: (1, 0, hh))
    tsp = pl.BlockSpec((None, t, LANES), lambda hh, i: (hh, i, 0))
    return _call(name, body, grid=(h // hb, l // t), in_specs=[qsp, ksp, vsp], out_specs=[qsp, tsp],
                 out_shape=[jax.ShapeDtypeStruct((l, dm), BF16), jax.ShapeDtypeStruct((h // hb, l, LANES), F32)],
                 sem=("parallel", "parallel"), args=(q, kv, kv), rider=rider)


def attn_bwd(name, q, kv, tot, do, t=ATT_TILE, rider=None):
    l, dm = q.shape
    dh = HEAD_DIM
    h = dm // dh
    t = _blk(l, t)
    hb = _blk(h, ATT_HEADS)
    wb = hb * dh
    nq = l // t
    scale = dh ** -0.5

    def body(q_ref, k_ref, v_ref, tot_ref, do_ref, dq_ref, dkv_ref, dk_ref, dv_ref):
        i = pl.program_id(1)
        hd = lambda g: slice(g * dh, (g + 1) * dh)

        @pl.when(i == 0)
        def _():
            dk_ref[...] = jnp.zeros_like(dk_ref)
            dv_ref[...] = jnp.zeros_like(dv_ref)

        qs = [(q_ref[:, hd(g)].astype(F32) * scale).astype(BF16) for g in range(hb)]
        dob = [do_ref[:, hd(g)] for g in range(hb)]
        total = [tot_ref[:, g:g + 1] for g in range(hb)]
        row = lax.broadcasted_iota(jnp.int32, (t, t), 0)
        col = lax.broadcasted_iota(jnp.int32, (t, t), 1)
        tri_le = jnp.where(row <= col, 1.0, 0.0).astype(BF16)
        tri_lt = jnp.where(row < col, 1.0, 0.0).astype(BF16)
        causal = col < row

        def tile(jb, carry, mask):
            off = pl.multiple_of(jb * t, t)
            heads = range(hb)
            kj = [k_ref[pl.ds(off, t), hd(g)] for g in heads]
            z = [lax.dot_general(qs[g], kj[g], NT, preferred_element_type=F32) for g in heads]
            dp = [lax.dot_general(dob[g], v_ref[pl.ds(off, t), hd(g)], NT, preferred_element_type=F32) for g in heads]
            logs = [_sb_logs(z[g], mask) for g in heads]
            lpre = [_split_dot(logs[g][1], tri_le) for g in heads]
            w = [jnp.exp(logs[g][0] + (total[g] - carry[g][1] - lpre[g])) for g in heads]
            if mask is not None:
                w = [jnp.where(mask, w[g], 0.0) for g in heads]
            p = [w[g] * dp[g] for g in heads]
            for g in heads:
                dv_ref[pl.ds(off, t), hd(g)] += lax.dot_general(w[g].astype(BF16), dob[g], TN, preferred_element_type=F32)
            qpre = [carry[g][2] + jnp.dot(p[g].astype(BF16), tri_lt, preferred_element_type=F32) for g in heads]
            dz = [p[g] - jnp.exp(logs[g][0]) * (p[g] + qpre[g]) for g in heads]
            if mask is not None:
                dz = [jnp.where(mask, dz[g], 0.0) for g in heads]
            dzb = [dz[g].astype(BF16) for g in heads]
            for g in heads:
                dk_ref[pl.ds(off, t), hd(g)] += lax.dot_general(dzb[g], qs[g], TN, preferred_element_type=F32)
            dq = [carry[g][0] + jnp.dot(dzb[g], kj[g], preferred_element_type=F32) for g in heads]
            return tuple((dq[g], carry[g][1] + lpre[g][:, t - 1:t], qpre[g][:, t - 1:t] + p[g][:, t - 1:t]) for g in heads)

        zero = jnp.zeros((t, 1), F32)
        done = jnp.max(tot_ref[:, hb:hb + 1]).astype(jnp.int32)
        carry = lax.fori_loop(i - done, i, lambda jb, c: tile(jb, c, None), ((jnp.zeros((t, dh), F32), zero, zero),) * hb)
        carry = tile(i, carry, causal)
        for g in range(hb):
            dq_ref[:, hd(g)] = (carry[g][0] * scale).astype(BF16)

        @pl.when(i == nq - 1)
        def _():
            dkv_ref[0] = dk_ref[...].astype(BF16)
            dkv_ref[1] = dv_ref[...].astype(BF16)

    qsp = pl.BlockSpec((t, wb), lambda hh, i: (i, hh))
    ksp = pl.BlockSpec((None, l, wb), lambda hh, i: (0, 0, hh))
    vsp = pl.BlockSpec((None, l, wb), lambda hh, i: (1, 0, hh))
    tsp = pl.BlockSpec((None, t, LANES), lambda hh, i: (hh, i, 0))
    return _call(name, body, grid=(h // hb, nq), in_specs=[qsp, ksp, vsp, tsp, qsp],
                 out_specs=[qsp, pl.BlockSpec((2, l, wb), lambda hh, i: (0, 0, hh))],
                 out_shape=[jax.ShapeDtypeStruct((l, dm), BF16), jax.ShapeDtypeStruct((2, l, dm), BF16)],
                 scratch=[pltpu.VMEM((l, wb), F32), pltpu.VMEM((l, wb), F32)],
                 sem=("parallel", "arbitrary"), args=(q, kv, kv, tot, do), rider=rider)


def adamw(name, w, g, m, v):
    shape = w.shape
    cols = shape[-1]
    rows = w.size // cols
    tr = rows
    if rows % 8 == 0:
        tr = 8 * _blk(rows // 8, 64)
    c1 = 1.0 - ADAM_B1 ** ADAM_STEP
    c2 = 1.0 - ADAM_B2 ** ADAM_STEP

    def body(w_ref, g_ref, m_ref, v_ref, d_ref, nm_ref, nv_ref):
        gg = g_ref[...]
        mm = ADAM_B1 * m_ref[...] + (1.0 - ADAM_B1) * gg
        vv = ADAM_B2 * v_ref[...] + (1.0 - ADAM_B2) * (gg * gg)
        nm_ref[...] = mm
        nv_ref[...] = vv
        d_ref[...] = -ADAM_LR * ((mm / c1) / (jnp.sqrt(vv / c2) + ADAM_EPS) + ADAM_WD * w_ref[...])

    sp = pl.BlockSpec((tr, cols), lambda i: (i, 0))
    sd = jax.ShapeDtypeStruct((rows, cols), F32)
    outs = pl.pallas_call(body, name=name, grid=(rows // tr,), in_specs=[sp] * 4, out_specs=[sp] * 3,
                          out_shape=[sd] * 3, compiler_params=_cparams(("parallel",)))(
        w.reshape(rows, cols), g.reshape(rows, cols), m.reshape(rows, cols), v.reshape(rows, cols))
    return tuple(o.reshape(shape) for o in outs)


def _perm(a):
    l, d = a.shape
    return a.reshape(N_STREAMS, l // N_STREAMS, d).transpose(1, 0, 2).reshape(l, d)


def _unperm(a):
    l, d = a.shape
    return a.reshape(l // N_STREAMS, N_STREAMS, d).transpose(1, 0, 2).reshape(l, d)


def _heads(a):
    l, d = a.shape
    return a.reshape(l, d // HEAD_DIM, HEAD_DIM).transpose(1, 0, 2)


def _unheads(a):
    h, l, dh = a.shape
    return a.transpose(1, 0, 2).reshape(l, h * dh)


def _ssm_layouts(d):
    g = d // SSM_GROUP
    gb = MXU_DIM // SSM_GROUP if d >= MXU_DIM else g
    return g, gb, g // gb


def _b_blocks(bb, d):
    g, gb, nb = _ssm_layouts(d)
    b4 = bb.reshape(SSM_GROUP, nb, gb, STATE)
    eye = jnp.eye(gb, dtype=bb.dtype)
    return jnp.einsum('hbqp,gq->bghqp', b4, eye).reshape(nb, gb * SSM_GROUP, gb * STATE)


def _b_unblocks(db, d):
    g, gb, nb = _ssm_layouts(d)
    eye = jnp.eye(gb, dtype=db.dtype)
    return jnp.einsum('bghqp,gq->hbqp', db.reshape(nb, gb, SSM_GROUP, gb, STATE), eye).reshape(SSM_GROUP, g * STATE)


def _c_blocks(c, d):
    g, gb, nb = _ssm_layouts(d)
    eye = jnp.eye(gb, dtype=c.dtype)
    return jnp.einsum('bghp,gq->bqpgh', c.reshape(nb, gb, SSM_GROUP, STATE), eye).reshape(nb, gb * STATE, gb * SSM_GROUP)


def _c_unblocks(dc, d):
    g, gb, nb = _ssm_layouts(d)
    eye = jnp.eye(gb, dtype=dc.dtype)
    return jnp.einsum('bqpgh,gq->bghp', dc.reshape(nb, gb, STATE, gb, SSM_GROUP), eye).reshape(g, SSM_GROUP, STATE)


class NoComm:
    def __init__(self):
        self.local = {}

    def gather(self, names):
        return None

    def landed(self, names, bufs, p):
        pass

    def reduce(self, names, arrays):
        self.local.update(zip(names, arrays))
        return None

    def reduced(self, names, slots):
        pass


def _ffn_fwd(tag, x, r, gain, p, layer, cm, ride):
    xs, h = add_rmsnorm(f"norm_ffn{tag}", x, r, [gain])
    rider = cm.gather(ride) if ride else None
    gu = mm_up(f"ffn_up{tag}", h, p["wup"][layer], 0, rider=rider)
    if rider is not None:
        gu, landed = gu
        cm.landed(ride, landed, p)
    a = ffn_mid_fwd(f"ffn_mid{tag}", gu, p["conv_w"][layer], p["conv_b"][layer:layer + 1])
    f = mm_nn(f"ffn_down{tag}", a, p["wdown"][layer], F32)
    return xs, h, gu, a, f


def _ffn_bwd(tag, dxo, xs, h, gu, a, gain, p, layer, s, cm, ride):
    wdown = p["wdown"][layer]
    dfb = dxo.astype(BF16)
    f = wdown.shape[0]
    da = mm_nt(f"ffn_down_dx{tag}", dfb, wdown, BF16, tn=f // 2)
    dwdown = mm_tn(f"ffn_down_dw{tag}", a, dfb, tmo=f // 2)
    dgu, dcw, dcb = ffn_mid_bwd(f"ffn_mid_bwd{tag}", gu, da, p["conv_w"][layer], p["conv_b"][layer:layer + 1])
    rider = cm.reduce(*ride) if ride else None
    dh = mm_up_nt(f"ffn_up_dx{tag}", dgu, p["wup"][layer], 0, rider=rider)
    if rider is not None:
        dh, slots = dh
        cm.reduced(ride[0], slots)
    dwup = mm_up_tn(f"ffn_up_dw{tag}", h, dgu, s)
    dxs, dg = norm_bwd(f"norm_ffn_bwd{tag}", xs, dxo, [(gain, dh)])
    return dxs, dg, dwup, dwdown, dcw, dcb


def _local_step(x, tgt, p, cm=None):
    cm = NoComm() if cm is None else cm
    l, d = x.shape
    s = p["wglu4"].shape[0]
    gr = {}

    a_re, a_im = p["a_re"].reshape(1, -1), p["a_im"].reshape(1, -1)
    ldt = jnp.repeat(p["log_dt"].reshape(-1), STATE).reshape(1, -1)
    bk_re = p["b_re"].transpose(2, 0, 1).reshape(SSM_GROUP, -1)
    bk_im = p["b_im"].transpose(2, 0, 1).reshape(SSM_GROUP, -1)
    ab_re, ab_im, bb_re, bb_im = ssm_disc_fwd("ssm_disc", a_re, a_im, ldt, bk_re, bk_im)
    a2 = jnp.concatenate([ab_re, ab_im], axis=0)
    a2c = jnp.concatenate([ab_re, -ab_im], axis=0)
    bre, bim = _b_blocks(bb_re, d).astype(BF16), _b_blocks(bb_im, d).astype(BF16)
    cre, cim = _c_blocks(p["c_re"], d).astype(BF16), _c_blocks(p["c_im"], d).astype(BF16)
    ctre, ctim = cre.transpose(0, 2, 1), -cim.transpose(0, 2, 1)
    dskip = p["d"].reshape(1, d)

    xp = _perm(x)
    (h0p,) = add_rmsnorm("norm_mix0", xp, None, [p["norm_mix"][0]])
    u = mm_nn("ssm_in", h0p, p["win"], BF16)
    rider = cm.gather(["wdown0"])
    res = ssm_pass1("ssm_fwd1", u, bre, bim, a2, reverse=False, rider=rider)
    f_re, f_im = res[0], res[1]
    cm.landed(["wdown0"], res[2:], p)
    rider = cm.gather(["wup0"])
    res = ssm_fwd2("ssm_fwd2", u, bre, bim, a2, f_re, f_im, cre, cim, dskip, rider=rider)
    s_re, s_im, ypre, yg = res[:4]
    cm.landed(["wup0"], res[4:], p)
    z = mm_nn_colshard("ssm_glu", yg, p["wglu4"], F32)
    mix = _unperm(glu_fwd("glu", z))

    x1, h1, gu0, a0, f0 = _ffn_fwd("0", x, mix, p["norm_ffn"][0], p, 0, cm, ["kv", "wq", "wo"])
    x2, hk, h2 = add_rmsnorm("norm_kv_mix1", x1, f0, [p["norm_kv"], p["norm_mix"][1]])
    kvw = p["kvw4"][:, None]
    kv = mm_up("kv_proj", hk, kvw, 0)
    qf = mm_nn("q_proj", h2, p["wq"], BF16)
    rider = cm.gather(["wup1", "wdown1"])
    res = attn_fwd("attn", qf, kv, rider=rider)
    ob, tot = res[0], res[1]
    cm.landed(["wup1", "wdown1"], res[2:], p)
    ao = mm_nn("o_proj", ob, p["wo"], F32)
    x3, h3, gu1, a1, f1 = _ffn_fwd("1", x2, ao, p["norm_ffn"][1], p, 1, cm, None)
    loss, dx4, dg_final = final_loss("final_loss", x3, f1, tgt, p["norm_final"])
    gr["norm_final"] = dg_final.reshape(d)

    dx3, dg_ffn1, dwup1, dwdown1, dcw1, dcb1 = _ffn_bwd("1", dx4, x3, h3, gu1, a1, p["norm_ffn"][1], p, 1, s, cm, None)
    dx3b = dx3.astype(BF16)
    do2 = mm_nt("o_proj_dx", dx3b, p["wo"], BF16)
    dwo = mm_tn("o_proj_dw", ob, dx3b)
    rider = cm.reduce(["wup1", "wdown1"], [dwup1, dwdown1])
    res = attn_bwd("attn_bwd", qf, kv, tot, do2, rider=rider)
    dqf, dkv = res[0], res[1]
    cm.reduced(["wup1", "wdown1"], res[2:])
    dh2 = mm_nt("q_proj_dx", dqf, p["wq"], F32)
    dwq = mm_tn("q_proj_dw", h2, dqf)
    dhk = mm_up_nt("kv_proj_dx", dkv, kvw, 0)
    dwkv = mm_up_tn("kv_proj_dw", hk, dkv, s)
    dx2, dg_mix1, dg_kv = norm_bwd("norm_kv_mix1_bwd", x2, dx3, [(p["norm_mix"][1], dh2), (p["norm_kv"], dhk)])
    gr["norm_kv"] = dg_kv.reshape(d)

    dx1, dg_ffn0, dwup0, dwdown0, dcw0, dcb0 = _ffn_bwd(
        "0", dx2, x1, h1, gu0, a0, p["norm_ffn"][0], p, 0, s, cm, (["wo", "wq", "kv"], [dwo, dwq, dwkv]))
    dx1p = _perm(dx1)
    dz = glu_bwd("glu_bwd", z, dx1p)
    dyg = mm_nt_colshard("ssm_glu_dx", dz, p["wglu4"], F32)
    dwglu = mm_tn_colshard("ssm_glu_dw", yg, dz, s)
    dy = gelu_bwd("gelu_bwd", ypre, dyg)
    i_re, i_im = ssm_pass1("ssm_bwd1", dy, ctre, ctim, a2c, reverse=True)
    rider = cm.reduce(["wup0", "wdown0", "wglu"], [dwup0, dwdown0, dwglu])
    res = ssm_bwd2("ssm_bwd2", dy, u, s_re, s_im, ctre, ctim, a2c, i_re, i_im, f_re, f_im, bre, bim, dskip, rider=rider)
    du, dbre, dbim, dcre, dcim, da_re, da_im, dd = res[:8]
    cm.reduced(["wup0", "wdown0", "wglu"], res[8:])
    dh0p = mm_nt("ssm_in_dx", du, p["win"], F32)
    dwin = mm_tn("ssm_in_dw", h0p, du)
    dxp, dg_mix0 = norm_bwd("norm_mix0_bwd", xp, dx1p, [(p["norm_mix"][0], dh0p)])
    dx = _unperm(dxp)

    g_are, g_aim, g_ldt, g_bre, g_bim = ssm_disc_bwd(
        "ssm_disc_bwd", a_re, a_im, ldt, bk_re, bk_im,
        jnp.sum(da_re, axis=0, keepdims=True), jnp.sum(da_im, axis=0, keepdims=True),
        _b_unblocks(dbre, d), _b_unblocks(dbim, d))
    g = d // SSM_GROUP
    gr["a_re"] = g_are.reshape(g, STATE)
    gr["a_im"] = g_aim.reshape(g, STATE)
    gr["log_dt"] = jnp.sum(g_ldt.reshape(g, STATE), axis=1)
    gr["b_re"] = g_bre.reshape(SSM_GROUP, g, STATE).transpose(1, 2, 0)
    gr["b_im"] = g_bim.reshape(SSM_GROUP, g, STATE).transpose(1, 2, 0)
    gr["c_re"] = _c_unblocks(dcre, d)
    gr["c_im"] = _c_unblocks(dcim, d)
    gr["d"] = dd.reshape(g, SSM_GROUP)
    gr["norm_mix"] = jnp.concatenate([dg_mix0, dg_mix1], axis=0)
    gr["norm_ffn"] = jnp.concatenate([dg_ffn0, dg_ffn1], axis=0)
    gr["conv_w"] = jnp.stack([dcw0, dcw1])
    gr["conv_b"] = jnp.concatenate([dcb0, dcb1], axis=0)
    gr["win"] = dwin
    gr.update(getattr(cm, "local", {}))
    return loss, dx, gr


MESH = pl.DeviceIdType.MESH
N_CHIPS = 4
N_DEV = 8
ANY = pl.BlockSpec(memory_space=pl.ANY)


def _pos():
    x, y, c = lax.axis_index("x"), lax.axis_index("y"), lax.axis_index("c")
    return x, y, c, 2 * x + y


def _other_chips(x, y):
    return [(1 - x, y), (x, 1 - y), (1 - x, 1 - y)]


def _remote(src, dst, send_sem, recv_sem, dev):
    return pltpu.make_async_remote_copy(src_ref=src, dst_ref=dst, send_sem=send_sem, recv_sem=recv_sem,
                                        device_id=dev, device_id_type=MESH)


def cast_into_slot(name, a, chip, dtype):
    r, cdim = a.shape
    tr = 16 * _blk(r // 16, 32) if r % 16 == 0 else r

    def body(m_ref, a_ref, o_ref):
        o_ref[...] = a_ref[...].astype(o_ref.dtype)

    gs = pltpu.PrefetchScalarGridSpec(
        num_scalar_prefetch=1, grid=(r // tr,),
        in_specs=[pl.BlockSpec((tr, cdim), lambda i, m_ref: (i, 0))],
        out_specs=pl.BlockSpec((None, tr, cdim), lambda i, m_ref: (m_ref[0], i, 0)))
    return pl.pallas_call(body, name=name, grid_spec=gs, out_shape=jax.ShapeDtypeStruct((N_CHIPS, r, cdim), dtype),
                          compiler_params=_cparams(("parallel",)))(chip, a)


def gather_weights(name, bufs):
    n = len(bufs)

    def body(*refs):
        _gather_start(refs[n:2 * n], refs[2 * n:])
        _gather_finish(refs[n:2 * n], refs[2 * n:])

    sem = pltpu.SemaphoreType.DMA
    return pl.pallas_call(
        body, name=name, in_specs=[ANY] * n, out_specs=[ANY] * n,
        out_shape=[jax.ShapeDtypeStruct(b.shape, b.dtype) for b in bufs],
        input_output_aliases={w: w for w in range(n)},
        scratch_shapes=[sem((n, 3)), sem((n, 3)), sem((n, 3)), sem((n, 3))],
        compiler_params=pltpu.CompilerParams(has_side_effects=True),
    )(*bufs)


def _half(ref, chip, core):
    hr = ref.shape[1] // 2
    return ref.at[chip, pl.ds(core * hr, hr), :]


def _gather_start(bufs, sems):
    send_a, recv_a = sems[0], sems[1]
    x, y, c, m = _pos()
    for w, buf in enumerate(bufs):
        for j, (px, py) in enumerate(_other_chips(x, y)):
            blk = _half(buf, m, c)
            _remote(blk, blk, send_a.at[w, j], recv_a.at[w, j], (px, py, c)).start()


def _gather_finish(bufs, sems):
    send_a, recv_a, send_b, recv_b = sems
    x, y, c, m = _pos()
    chips = _other_chips(x, y)
    sib = (x, y, 1 - c)
    for j, (px, py) in enumerate(chips):
        for w, buf in enumerate(bufs):
            blk = _half(buf, 2 * px + py, c)
            _remote(blk, blk, send_a.at[w, j], recv_a.at[w, j], (px, py, c)).wait_recv()
            _remote(blk, blk, send_b.at[w, j], recv_b.at[w, j], sib).start()
    for j, (px, py) in enumerate(chips):
        for w, buf in enumerate(bufs):
            blk = _half(buf, 2 * px + py, 1 - c)
            _remote(blk, blk, send_b.at[w, j], recv_b.at[w, j], sib).wait_recv()
    for j, (px, py) in enumerate(chips):
        for w, buf in enumerate(bufs):
            mine, landed = _half(buf, m, c), _half(buf, 2 * px + py, c)
            _remote(mine, mine, send_a.at[w, j], recv_a.at[w, j], (px, py, c)).wait_send()
            _remote(landed, landed, send_b.at[w, j], recv_b.at[w, j], sib).wait_send()


def gather_rider(bufs):
    n = len(bufs)
    return Rider(ins=list(bufs), outs=[jax.ShapeDtypeStruct(b.shape, b.dtype) for b in bufs],
                 alias={w: w for w in range(n)}, sems=[(n, 3)] * 4,
                 start=lambda i, o, s: _gather_start(o, s), finish=lambda i, o, s: _gather_finish(o, s))


def exchange_halves(name, arrs, mine):
    n = len(arrs)

    def body(*refs):
        ins, outs = refs[:n], refs[n:2 * n]
        send, recv = refs[2 * n:]
        x, y, c, _ = _pos()
        sel = c if mine else 1 - c
        cps = []
        for w in range(n):
            hr = ins[w].shape[1] // 2
            cp = _remote(ins[w].at[:, pl.ds(sel * hr, hr), :], outs[w], send.at[w], recv.at[w], (x, y, 1 - c))
            cp.start()
            cps.append(cp)
        for cp in cps:
            cp.wait()

    sem = pltpu.SemaphoreType.DMA
    return pl.pallas_call(
        body, name=name, in_specs=[ANY] * n, out_specs=[ANY] * n,
        out_shape=[jax.ShapeDtypeStruct((a.shape[0], a.shape[1] // 2, a.shape[2]), a.dtype) for a in arrs],
        scratch_shapes=[sem((n,)), sem((n,))],
        compiler_params=pltpu.CompilerParams(has_side_effects=True),
    )(*arrs)


def scatter_to_chips(name, arrs):
    n = len(arrs)

    def body(*refs):
        _scatter(refs[:n], refs[n:2 * n], refs[2 * n:], "start")
        _scatter(refs[:n], refs[n:2 * n], refs[2 * n:], "wait")

    sem = pltpu.SemaphoreType.DMA
    return pl.pallas_call(
        body, name=name, in_specs=[ANY] * n, out_specs=[ANY] * n,
        out_shape=[jax.ShapeDtypeStruct((3,) + a.shape[1:], a.dtype) for a in arrs],
        scratch_shapes=[sem((n, 3)), sem((n, 3))],
        compiler_params=pltpu.CompilerParams(has_side_effects=True),
    )(*arrs)


def _scatter(ins, outs, sems, what):
    send, recv = sems
    x, y, c, _ = _pos()
    for w in range(len(ins)):
        for j, (px, py) in enumerate(_other_chips(x, y)):
            cp = _remote(ins[w].at[2 * px + py], outs[w].at[j], send.at[w, j], recv.at[w, j], (px, py, c))
            if what == "start":
                cp.start()
            else:
                cp.wait()


def scatter_rider(arrs):
    n = len(arrs)
    return Rider(ins=list(arrs), outs=[jax.ShapeDtypeStruct((3,) + a.shape[1:], a.dtype) for a in arrs],
                 alias={}, sems=[(n, 3)] * 2,
                 start=lambda i, o, s: _scatter(i, o, s, "start"), finish=lambda i, o, s: _scatter(i, o, s, "wait"))


def share_halves(name, fulls):
    n = len(fulls)

    def body(*refs):
        outs = refs[n:2 * n]
        send, recv = refs[2 * n:]
        x, y, c, _ = _pos()
        cps = []
        for w in range(n):
            hr = outs[w].shape[0] // 2
            blk = outs[w].at[pl.ds(c * hr, hr), :]
            cp = _remote(blk, blk, send.at[w], recv.at[w], (x, y, 1 - c))
            cp.start()
            cps.append(cp)
        for w, cp in enumerate(cps):
            hr = outs[w].shape[0] // 2
            cp.wait_send()
            blk = outs[w].at[pl.ds((1 - c) * hr, hr), :]
            _remote(blk, blk, send.at[w], recv.at[w], (x, y, 1 - c)).wait_recv()

    sem = pltpu.SemaphoreType.DMA
    return pl.pallas_call(
        body, name=name, in_specs=[ANY] * n, out_specs=[ANY] * n,
        out_shape=[jax.ShapeDtypeStruct(a.shape, a.dtype) for a in fulls],
        input_output_aliases={w: w for w in range(n)},
        scratch_shapes=[sem((n,)), sem((n,))],
        compiler_params=pltpu.CompilerParams(has_side_effects=True),
    )(*fulls)


def add_own_half(name, full, got, core, out_dtype):
    n, r, cdim = full.shape
    hr = r // 2
    tr = 8 * _blk(hr // 8, 32) if out_dtype == F32 else 16 * _blk(hr // 16, 16)
    nbh = hr // tr

    def body(c_ref, f_ref, g_ref, o_ref):
        o_ref[...] = (f_ref[...] + g_ref[...]).astype(o_ref.dtype)

    gs = pltpu.PrefetchScalarGridSpec(
        num_scalar_prefetch=1, grid=(n, nbh),
        in_specs=[pl.BlockSpec((None, tr, cdim), lambda s, i, c_ref: (s, c_ref[0] * nbh + i, 0)),
                  pl.BlockSpec((None, tr, cdim), lambda s, i, c_ref: (s, i, 0))],
        out_specs=pl.BlockSpec((None, tr, cdim), lambda s, i, c_ref: (s, i, 0)))
    return pl.pallas_call(body, name=name, grid_spec=gs, out_shape=jax.ShapeDtypeStruct((n, hr, cdim), out_dtype),
                          compiler_params=_cparams(("parallel", "parallel")))(core, full, got)


def sum_into_half(name, part, slots, chip_core, chip_order):
    _, hr, cdim = part.shape
    unit = 8 if part.dtype == F32 else 16
    tr = unit * _blk(hr // unit, 256 // unit)
    nbh = hr // tr

    def body(mc_ref, p_ref, s_ref, o_ref):
        terms = [p_ref[...].astype(F32)] + [s_ref[k].astype(F32) for k in range(3)]
        if chip_order:
            m = mc_ref[0]
            own, fx, fy, fxy = terms
            terms = [jnp.where((k ^ m) == 0, own, jnp.where((k ^ m) == 2, fx, jnp.where((k ^ m) == 1, fy, fxy)))
                     for k in range(N_CHIPS)]
        o_ref[...] = ((terms[0] + terms[1]) + terms[2]) + terms[3]

    gs = pltpu.PrefetchScalarGridSpec(
        num_scalar_prefetch=1, grid=(nbh,),
        in_specs=[pl.BlockSpec((None, tr, cdim), lambda i, mc: (mc[0], i, 0)),
                  pl.BlockSpec((3, tr, cdim), lambda i, mc: (0, i, 0))],
        out_specs=pl.BlockSpec((tr, cdim), lambda i, mc: (mc[1] * nbh + i, 0)))
    return pl.pallas_call(body, name=name, grid_spec=gs, out_shape=jax.ShapeDtypeStruct((2 * hr, cdim), F32),
                          compiler_params=_cparams(("parallel",)))(chip_core, part, slots)


class StepComm:
    def __init__(self, bufs, core, chip_core):
        self.bufs, self.core, self.chip_core = bufs, core, chip_core
        self.part, self.fulls = {}, {}

    def gather(self, names):
        return gather_rider([self.bufs[n] for n in names])

    def landed(self, names, bufs, p):
        for n, b in zip(names, bufs):
            if n.startswith("wdown"):
                p["wdown"][int(n[-1])] = b.reshape(-1, b.shape[-1])
            elif n.startswith("wup"):
                p["wup"][int(n[-1])] = b[:, None]
            elif n == "kv":
                p["kvw4"] = b
            else:
                p[n] = b.reshape(-1, b.shape[-1])

    def reduce(self, names, arrays, payloads=None):
        payloads = payloads or [BF16] * len(names)
        big = [a if a.ndim == 3 else a.reshape(N_CHIPS, a.shape[0] // N_CHIPS, a.shape[1]) for a in arrays]
        got = exchange_halves("rs_siblings_" + names[0], big, mine=False)
        part = [add_own_half("rs_add_" + n, g, r, self.core, dt) for n, g, r, dt in zip(names, big, got, payloads)]
        self.part.update(zip(names, part))
        return scatter_rider(part)

    def reduced(self, names, slots, chip_order=()):
        for n, s in zip(names, slots):
            self.fulls[n] = sum_into_half("rs_sum_" + n, self.part[n], s, self.chip_core, n in chip_order)

    def finish(self):
        names = list(self.fulls)
        return dict(zip(names, share_halves("rs_share", [self.fulls[n] for n in names])))


WEIGHTS = ('norm_mix', 'norm_ffn', 'norm_kv', 'norm_final', 'ssm_w_in', 'ssm_a_re', 'ssm_a_im', 'ssm_log_dt',
           'ssm_b_re', 'ssm_b_im', 'ssm_c_re', 'ssm_c_im', 'ssm_d', 'ssm_w_glu', 'kv_w', 'attn_w_q', 'attn_w_o',
           'ffn_w_up', 'ffn_conv_w', 'ffn_conv_b', 'ffn_w_down')
SMALL = ('norm_mix', 'norm_ffn', 'norm_kv', 'norm_final', 'ssm_a_re', 'ssm_a_im', 'ssm_log_dt', 'ssm_b_re', 'ssm_b_im',
         'ssm_c_re', 'ssm_c_im', 'ssm_d', 'ffn_conv_w', 'ffn_conv_b')


def _pad_rows(flat, unit):
    n = flat.shape[0]
    total = -(-n // unit) * unit
    return jnp.pad(flat, (0, total - n)).reshape(total // LANES, LANES)


def kernel(x, norm_mix, norm_ffn, norm_kv, norm_final, ssm_w_in, ssm_a_re, ssm_a_im, ssm_log_dt, ssm_b_re, ssm_b_im, ssm_c_re, ssm_c_im, ssm_d, ssm_w_glu, kv_w, attn_w_q, attn_w_o, ffn_w_up, ffn_conv_w, ffn_conv_b, ffn_w_down, loss_target, m_norm_mix, m_norm_ffn, m_norm_kv, m_norm_final, m_ssm_w_in, m_ssm_a_re, m_ssm_a_im, m_ssm_log_dt, m_ssm_b_re, m_ssm_b_im, m_ssm_c_re, m_ssm_c_im, m_ssm_d, m_ssm_w_glu, m_kv_w, m_attn_w_q, m_attn_w_o, m_ffn_w_up, m_ffn_conv_w, m_ffn_conv_b, m_ffn_w_down, v_norm_mix, v_norm_ffn, v_norm_kv, v_norm_final, v_ssm_w_in, v_ssm_a_re, v_ssm_a_im, v_ssm_log_dt, v_ssm_b_re, v_ssm_b_im, v_ssm_c_re, v_ssm_c_im, v_ssm_d, v_ssm_w_glu, v_kv_w, v_attn_w_q, v_attn_w_o, v_ffn_w_up, v_ffn_conv_w, v_ffn_conv_b, v_ffn_w_down):
    a = dict(locals())
    l, d = x.shape[1], x.shape[2]
    f = ffn_conv_b.shape[1]
    fs = f // N_CHIPS
    m = 2 * lax.axis_index("x") + lax.axis_index("y")
    core = lax.axis_index("c").astype(jnp.int32).reshape(1)
    chip = m.astype(jnp.int32).reshape(1)
    chip_core = jnp.concatenate([chip, core])

    shards = {"win": ssm_w_in[0], "wglu": ssm_w_glu[0], "kv": kv_w, "wq": attn_w_q[0], "wo": attn_w_o[0],
              "wup0": ffn_w_up[0], "wup1": ffn_w_up[1], "wdown0": ffn_w_down[0], "wdown1": ffn_w_down[1],
              "convw": _pad_rows(ffn_conv_w.reshape(-1), 16 * LANES)}
    bufs = {k: cast_into_slot(f"cast_{k}", s, chip, F32 if k == "convw" else BF16) for k, s in shards.items()}
    cm = StepComm(bufs, core, chip_core)
    g_in, g_glu, g_cw = gather_weights("gather_first", [bufs["win"], bufs["wglu"], bufs["convw"]])
    conv_w = g_cw.reshape(N_CHIPS, -1)[:, :2 * 3 * fs].reshape(N_CHIPS, 2, 3, fs).transpose(1, 2, 0, 3).reshape(2, 3, f)
    p = dict(
        norm_mix=norm_mix, norm_ffn=norm_ffn, norm_kv=norm_kv, norm_final=norm_final,
        a_re=ssm_a_re[0], a_im=ssm_a_im[0], log_dt=ssm_log_dt[0], b_re=ssm_b_re[0], b_im=ssm_b_im[0],
        c_re=ssm_c_re[0], c_im=ssm_c_im[0], d=ssm_d[0],
        win=g_in.reshape(-1, g_in.shape[-1]), wglu4=g_glu, wup=[None, None], wdown=[None, None],
        conv_w=conv_w, conv_b=ffn_conv_b)

    loss_slab, dx, gr = _local_step(x[0], loss_target[0], p, cm)

    small = {"norm_mix": gr["norm_mix"], "norm_ffn": gr["norm_ffn"], "norm_kv": gr["norm_kv"], "norm_final": gr["norm_final"],
             "ssm_a_re": gr["a_re"], "ssm_a_im": gr["a_im"], "ssm_log_dt": gr["log_dt"], "ssm_b_re": gr["b_re"],
             "ssm_b_im": gr["b_im"], "ssm_c_re": gr["c_re"], "ssm_c_im": gr["c_im"], "ssm_d": gr["d"],
             "ffn_conv_w": gr["conv_w"], "ffn_conv_b": gr["conv_b"]}
    packed = _pad_rows(jnp.concatenate([small[k].reshape(-1) for k in SMALL] + [loss_slab[0, 0:1]]), 16 * LANES)
    last = cm.reduce(["win", "small"], [gr["win"], jnp.broadcast_to(packed, (N_CHIPS,) + packed.shape)], [BF16, F32])
    cm.reduced(["win", "small"], scatter_to_chips("rs_chips_last", last.ins), chip_order=("small",))
    r = cm.finish()
    grads = {"ssm_w_in": r["win"][None], "ssm_w_glu": r["wglu"][None], "kv_w": r["kv"], "attn_w_q": r["wq"][None],
             "attn_w_o": r["wo"][None], "ffn_w_up": jnp.stack([r["wup0"], r["wup1"]]),
             "ffn_w_down": jnp.stack([r["wdown0"], r["wdown1"]])}
    total = r["small"].reshape(-1)
    off = 0
    for k in SMALL:
        n = small[k].size
        full = total[off:off + n].reshape(small[k].shape)
        off += n
        if k == "ffn_conv_w":
            full = lax.dynamic_slice_in_dim(full, m * fs, fs, axis=2)
        grads[k] = full.reshape(a[k].shape)
    loss = total[off]

    outs = {}
    for k in WEIGHTS:
        outs[k] = adamw(f"adamw_{k}", a[k], grads[k], a["m_" + k], a["v_" + k])
    return (loss, dx[None], *[grads[k] for k in WEIGHTS], *[outs[k][0] for k in WEIGHTS],
            *[outs[k][1] for k in WEIGHTS], *[outs[k][2] for k in WEIGHTS])
```

```python
import functools
import math

import jax
import jax.numpy as jnp
from jax import lax
from jax.experimental import pallas as pl
from jax.experimental.pallas import tpu as pltpu

F32 = jnp.float32
BF16 = jnp.bfloat16

EPS = 1e-6
SSM_GROUP = 16
STATE = 64
HEAD_DIM = 64
N_STREAMS = 8
MXU_DIM = 256
VMEM_LIMIT = 56 * 1024 * 1024

ADAM_LR = 0.001
ADAM_B1 = 0.9
ADAM_B2 = 0.999
ADAM_EPS = 1e-08
ADAM_WD = 0.01
ADAM_STEP = 10


def _cparams(sem=None):
    return pltpu.CompilerParams(dimension_semantics=sem, vmem_limit_bytes=VMEM_LIMIT)


class Rider:
    def __init__(self, ins, outs, alias, sems, start, finish):
        self.ins, self.outs, self.alias, self.sems, self.start, self.finish = ins, outs, alias, sems, start, finish


def _call(name, body, *, grid, in_specs, out_specs, out_shape, args, scratch=(), sem=None, rider=None):
    if rider is None:
        return pl.pallas_call(body, name=name, grid=grid, in_specs=in_specs, out_specs=out_specs, out_shape=out_shape,
                              scratch_shapes=list(scratch), compiler_params=_cparams(sem))(*args)
    nin, nout, nscr = len(in_specs), len(out_specs), len(scratch)
    nri, nro = len(rider.ins), len(rider.outs)
    steps = math.prod(grid)

    def hosted(*refs):
        ins, refs = refs[:nin], refs[nin:]
        r_in, refs = refs[:nri], refs[nri:]
        outs, refs = refs[:nout], refs[nout:]
        r_out, refs = refs[:nro], refs[nro:]
        scr, sems = refs[:nscr], refs[nscr:]
        lin = 0
        for ax, size in enumerate(grid):
            lin = lin * size + pl.program_id(ax)

        @pl.when(lin == 0)
        def _():
            rider.start(r_in, r_out, sems)

        body(*ins, *outs, *scr)

        @pl.when(lin == steps - 1)
        def _():
            rider.finish(r_in, r_out, sems)

    any_spec = pl.BlockSpec(memory_space=pl.ANY)
    dma = pltpu.SemaphoreType.DMA
    return pl.pallas_call(
        hosted, name=name, grid=grid, in_specs=list(in_specs) + [any_spec] * nri,
        out_specs=list(out_specs) + [any_spec] * nro, out_shape=list(out_shape) + list(rider.outs),
        input_output_aliases={nin + i: nout + o for i, o in rider.alias.items()},
        scratch_shapes=list(scratch) + [dma(s) for s in rider.sems],
        compiler_params=pltpu.CompilerParams(dimension_semantics=("arbitrary",) * len(grid),
                                             vmem_limit_bytes=VMEM_LIMIT, has_side_effects=True),
    )(*args, *rider.ins)


def _blk(n, want):
    b = min(n, want)
    while n % b:
        b -= 1
    return b


NN = (((1,), (0,)), ((), ()))
NT = (((1,), (1,)), ((), ()))
TN = (((0,), (0,)), ((), ()))


def _matmul(name, a, b, *, a_spec, b_spec, o_spec, out_shape, grid, dn, nk, out_dtype, rider=None):
    nax = len(grid)

    def body(a_ref, b_ref, o_ref, *scr):
        part = lax.dot_general(a_ref[...], b_ref[...], dn, preferred_element_type=F32)
        if nk == 1:
            o_ref[...] = part.astype(o_ref.dtype)
            return
        acc = scr[0] if scr else o_ref
        k = pl.program_id(nax - 1)

        @pl.when(k == 0)
        def _():
            acc[...] = part

        @pl.when(k > 0)
        def _():
            acc[...] += part

        if scr:
            @pl.when(k == nk - 1)
            def _():
                o_ref[...] = acc[...].astype(o_ref.dtype)

    scratch = []
    if nk > 1 and out_dtype != F32:
        blk = tuple(d for d in o_spec.block_shape if d is not None)
        scratch = [pltpu.VMEM(blk, F32)]
    sem = ("parallel",) * (nax - 1) + ("arbitrary",)
    res = _call(name, body, grid=grid, in_specs=[a_spec, b_spec], out_specs=[o_spec],
                out_shape=[jax.ShapeDtypeStruct(out_shape, out_dtype)], scratch=scratch, sem=sem, args=(a, b), rider=rider)
    return res[0] if rider is None else (res[0], res[1:])


MM_TM = 1024
MM_TK = 2048


def mm_nn(name, a, w, out_dtype, tm=MM_TM):
    m, k = a.shape
    n = w.shape[1]
    tm = _blk(m, tm)
    return _matmul(name, a, w, a_spec=pl.BlockSpec((tm, k), lambda i, kk: (i, 0)),
                   b_spec=pl.BlockSpec((k, n), lambda i, kk: (0, 0)),
                   o_spec=pl.BlockSpec((tm, n), lambda i, kk: (i, 0)),
                   out_shape=(m, n), grid=(m // tm, 1), dn=NN, nk=1, out_dtype=out_dtype)


def mm_nt(name, a, w, out_dtype, tm=MM_TM, tn=None):
    m, n = a.shape
    k = w.shape[0]
    tm = _blk(m, tm)
    tn = k if tn is None else tn
    return _matmul(name, a, w, a_spec=pl.BlockSpec((tm, n), lambda i, j, kk: (i, 0)),
                   b_spec=pl.BlockSpec((tn, n), lambda i, j, kk: (j, 0)),
                   o_spec=pl.BlockSpec((tm, tn), lambda i, j, kk: (i, j)),
                   out_shape=(m, k), grid=(m // tm, k // tn, 1), dn=NT, nk=1, out_dtype=out_dtype)


def mm_tn(name, a, b, tmo=None, tk=MM_TK):
    l, m = a.shape
    n = b.shape[1]
    tk = _blk(l, tk)
    tmo = m if tmo is None else tmo
    nk = l // tk
    return _matmul(name, a, b, a_spec=pl.BlockSpec((tk, tmo), lambda i, kk: (kk, i)),
                   b_spec=pl.BlockSpec((tk, n), lambda i, kk: (kk, 0)),
                   o_spec=pl.BlockSpec((tmo, n), lambda i, kk: (i, 0)),
                   out_shape=(m, n), grid=(m // tmo, nk), dn=TN, nk=nk, out_dtype=F32)


def mm_nn_colshard(name, a, w4, out_dtype, tm=MM_TM):
    m, k = a.shape
    s, _, ns = w4.shape
    tm = _blk(m, tm)
    return _matmul(name, a, w4, a_spec=pl.BlockSpec((tm, k), lambda i, j, kk: (i, 0)),
                   b_spec=pl.BlockSpec((None, k, ns), lambda i, j, kk: (j, 0, 0)),
                   o_spec=pl.BlockSpec((tm, ns), lambda i, j, kk: (i, j)),
                   out_shape=(m, s * ns), grid=(m // tm, s, 1), dn=NN, nk=1, out_dtype=out_dtype)


def mm_nt_colshard(name, a, w4, out_dtype, tm=MM_TM):
    m, _ = a.shape
    s, k, ns = w4.shape
    tm = _blk(m, tm)
    return _matmul(name, a, w4, a_spec=pl.BlockSpec((tm, ns), lambda i, kk: (i, kk)),
                   b_spec=pl.BlockSpec((None, k, ns), lambda i, kk: (kk, 0, 0)),
                   o_spec=pl.BlockSpec((tm, k), lambda i, kk: (i, 0)),
                   out_shape=(m, k), grid=(m // tm, s), dn=NT, nk=s, out_dtype=out_dtype)


def mm_tn_colshard(name, a, b, s, tk=MM_TK):
    l, k = a.shape
    ns = b.shape[1] // s
    tk = _blk(l, tk)
    nk = l // tk
    return _matmul(name, a, b, a_spec=pl.BlockSpec((tk, k), lambda j, kk: (kk, 0)),
                   b_spec=pl.BlockSpec((tk, ns), lambda j, kk: (kk, j)),
                   o_spec=pl.BlockSpec((None, k, ns), lambda j, kk: (j, 0, 0)),
                   out_shape=(s, k, ns), grid=(s, nk), dn=TN, nk=nk, out_dtype=F32)


def mm_up(name, h, wup4, layer, tm=MM_TM, rider=None):
    m, k = h.shape
    s, _, _, ns = wup4.shape
    half = s // 2
    tm = _blk(m, tm)
    return _matmul(name, h, wup4, a_spec=pl.BlockSpec((tm, k), lambda i, j, kk: (i, 0)),
                   b_spec=pl.BlockSpec((None, None, k, ns), lambda i, j, kk: (j, layer, 0, 0)),
                   o_spec=pl.BlockSpec((None, tm, ns), lambda i, j, kk: (j // half, i, j % half)),
                   out_shape=(2, m, half * ns), grid=(m // tm, s, 1), dn=NN, nk=1, out_dtype=BF16, rider=rider)


def mm_up_nt(name, dgu, wup4, layer, tm=MM_TM, rider=None):
    _, m, _ = dgu.shape
    s, _, k, ns = wup4.shape
    half = s // 2
    tm = _blk(m, tm)
    return _matmul(name, dgu, wup4,
                   a_spec=pl.BlockSpec((None, tm, ns), lambda i, kk: (kk // half, i, kk % half)),
                   b_spec=pl.BlockSpec((None, None, k, ns), lambda i, kk: (kk, layer, 0, 0)),
                   o_spec=pl.BlockSpec((tm, k), lambda i, kk: (i, 0)),
                   out_shape=(m, k), grid=(m // tm, s), dn=NT, nk=s, out_dtype=F32, rider=rider)


def mm_up_tn(name, h, dgu, s, tk=MM_TK):
    l, k = h.shape
    half = s // 2
    ns = dgu.shape[2] // half
    tk = _blk(l, tk)
    nk = l // tk
    return _matmul(name, h, dgu, a_spec=pl.BlockSpec((tk, k), lambda j, kk: (kk, 0)),
                   b_spec=pl.BlockSpec((None, tk, ns), lambda j, kk: (j // half, kk, j % half)),
                   o_spec=pl.BlockSpec((None, k, ns), lambda j, kk: (j, 0, 0)),
                   out_shape=(s, k, ns), grid=(s, nk), dn=TN, nk=nk, out_dtype=F32)


def add_rmsnorm(name, x, r, gains, tm=512):
    l, d = x.shape
    tm = _blk(l, tm)
    ng = len(gains)
    has_r = r is not None

    def body(*refs):
        x_ref = refs[0]
        pos = 1
        xs = x_ref[...]
        if has_r:
            xs = xs + refs[pos][...]
            pos += 1
        g_refs = refs[pos:pos + ng]
        outs = refs[pos + ng:]
        o = 0
        if has_r:
            outs[0][...] = xs
            o = 1
        xh = xs * lax.rsqrt(jnp.mean(xs * xs, axis=-1, keepdims=True) + EPS)
        for gi in range(ng):
            outs[o + gi][...] = (xh * g_refs[gi][...]).astype(BF16)

    row = pl.BlockSpec((tm, d), lambda i: (i, 0))
    gsp = pl.BlockSpec((1, d), lambda i: (0, 0))
    ins = [x] + ([r] if has_r else []) + [g.reshape(1, d) for g in gains]
    in_specs = [row] * (1 + has_r) + [gsp] * ng
    out_shape = ([jax.ShapeDtypeStruct((l, d), F32)] if has_r else []) + [jax.ShapeDtypeStruct((l, d), BF16)] * ng
    return pl.pallas_call(body, name=name, grid=(l // tm,), in_specs=in_specs,
                          out_specs=[row] * len(out_shape), out_shape=out_shape,
                          compiler_params=_cparams(("parallel",)))(*ins)


def norm_bwd(name, x, dres, pairs, tm=512, bf16_copy=False):
    l, d = x.shape
    tm = _blk(l, tm)
    npair = len(pairs)
    has_r = dres is not None

    def body(*refs):
        x_ref = refs[0]
        pos = 1
        xs = x_ref[...]
        dx = jnp.zeros_like(xs)
        if has_r:
            dx = refs[pos][...]
            pos += 1
        ins = refs[pos:pos + 2 * npair]
        outs = refs[pos + 2 * npair:]
        rs = lax.rsqrt(jnp.mean(xs * xs, axis=-1, keepdims=True) + EPS)
        xh = xs * rs
        first = pl.program_id(0) == 0
        for pi in range(npair):
            g = ins[2 * pi][...]
            dh = ins[2 * pi + 1][...].astype(F32)
            dgp = jnp.sum(dh * xh, axis=0, keepdims=True)
            dg_ref = outs[1 + pi]

            @pl.when(first)
            def _():
                dg_ref[...] = dgp

            @pl.when(jnp.logical_not(first))
            def _():
                dg_ref[...] += dgp

            dxh = dh * g
            dx = dx + rs * (dxh - xh * jnp.mean(dxh * xh, axis=-1, keepdims=True))
        outs[0][...] = dx
        if bf16_copy:
            outs[1 + npair][...] = dx.astype(BF16)

    row = pl.BlockSpec((tm, d), lambda i: (i, 0))
    gsp = pl.BlockSpec((1, d), lambda i: (0, 0))
    ins = [x] + ([dres] if has_r else [])
    in_specs = [row] * (1 + has_r)
    for g, dh in pairs:
        ins += [g.reshape(1, d), dh]
        in_specs += [gsp, row]
    out_shape = [jax.ShapeDtypeStruct((l, d), F32)] + [jax.ShapeDtypeStruct((1, d), F32)] * npair
    out_specs = [row] + [gsp] * npair
    if bf16_copy:
        out_shape.append(jax.ShapeDtypeStruct((l, d), BF16))
        out_specs.append(row)
    return pl.pallas_call(body, name=name, grid=(l // tm,), in_specs=in_specs,
                          out_specs=out_specs, out_shape=out_shape,
                          compiler_params=_cparams(("arbitrary",)))(*ins)


def final_loss(name, x, r, tgt, g, tm=512):
    l, d = x.shape
    tm = _blk(l, tm)

    def body(x_ref, r_ref, t_ref, g_ref, loss_ref, dx_ref, dg_ref, dxb_ref):
        xs = x_ref[...] + r_ref[...]
        gg = g_ref[...]
        rs = lax.rsqrt(jnp.mean(xs * xs, axis=-1, keepdims=True) + EPS)
        xh = xs * rs
        e = xh * gg - t_ref[...]
        part = 0.5 * jnp.sum(jnp.mean(e * e, axis=-1, keepdims=True), axis=0, keepdims=True)
        dy = e * (1.0 / d)
        dgp = jnp.sum(dy * xh, axis=0, keepdims=True)
        dxh = dy * gg
        dx = rs * (dxh - xh * jnp.mean(dxh * xh, axis=-1, keepdims=True))
        dx_ref[...] = dx
        dxb_ref[...] = dx.astype(BF16)
        first = pl.program_id(0) == 0

        @pl.when(first)
        def _():
            loss_ref[...] = jnp.broadcast_to(part, loss_ref.shape)
            dg_ref[...] = dgp

        @pl.when(jnp.logical_not(first))
        def _():
            loss_ref[...] += jnp.broadcast_to(part, loss_ref.shape)
            dg_ref[...] += dgp

    row = pl.BlockSpec((tm, d), lambda i: (i, 0))
    gsp = pl.BlockSpec((1, d), lambda i: (0, 0))
    lsp = pl.BlockSpec((8, 128), lambda i: (0, 0))
    return pl.pallas_call(
        body, name=name, grid=(l // tm,), in_specs=[row, row, row, gsp], out_specs=[lsp, row, gsp, row],
        out_shape=[jax.ShapeDtypeStruct((8, 128), F32), jax.ShapeDtypeStruct((l, d), F32),
                   jax.ShapeDtypeStruct((1, d), F32), jax.ShapeDtypeStruct((l, d), BF16)],
        compiler_params=_cparams(("arbitrary",)))(x, r, tgt, g.reshape(1, d))


def _sigmoid(x):
    return 0.5 * jnp.tanh(0.5 * x) + 0.5


def glu_fwd(name, z, tm=512):
    l, d2 = z.shape
    d = d2 // 2
    tm = _blk(l, tm)

    def body(z_ref, o_ref):
        o_ref[...] = z_ref[:, :d] * _sigmoid(z_ref[:, d:])

    return pl.pallas_call(body, name=name, grid=(l // tm,),
                          in_specs=[pl.BlockSpec((tm, d2), lambda i: (i, 0))],
                          out_specs=pl.BlockSpec((tm, d), lambda i: (i, 0)),
                          out_shape=jax.ShapeDtypeStruct((l, d), F32),
                          compiler_params=_cparams(("parallel",)))(z)


def glu_bwd(name, z, dm, tm=512):
    l, d2 = z.shape
    d = d2 // 2
    tm = _blk(l, tm)

    def body(z_ref, dm_ref, o_ref):
        sg = _sigmoid(z_ref[:, d:])
        g = dm_ref[...]
        o_ref[:, :d] = (g * sg).astype(BF16)
        o_ref[:, d:] = (g * z_ref[:, :d] * sg * (1.0 - sg)).astype(BF16)

    return pl.pallas_call(body, name=name, grid=(l // tm,),
                          in_specs=[pl.BlockSpec((tm, d2), lambda i: (i, 0)), pl.BlockSpec((tm, d), lambda i: (i, 0))],
                          out_specs=pl.BlockSpec((tm, d2), lambda i: (i, 0)),
                          out_shape=jax.ShapeDtypeStruct((l, d2), BF16),
                          compiler_params=_cparams(("parallel",)))(z, dm)


_GELU_C = math.sqrt(2.0 / math.pi)


def _gelu(y):
    return 0.5 * y * (1.0 + jnp.tanh(_GELU_C * (y + 0.044715 * y * y * y)))


def _gelu_grad(y):
    t = jnp.tanh(_GELU_C * (y + 0.044715 * y * y * y))
    return 0.5 * (1.0 + t) + 0.5 * y * (1.0 - t * t) * _GELU_C * (1.0 + 3.0 * 0.044715 * y * y)


def gelu_bwd(name, ypre, dyg, tm=512):
    l, d = ypre.shape
    tm = _blk(l, tm)

    def body(y_ref, d_ref, o_ref):
        o_ref[...] = (d_ref[...] * _gelu_grad(y_ref[...])).astype(BF16)

    row = pl.BlockSpec((tm, d), lambda i: (i, 0))
    return pl.pallas_call(body, name=name, grid=(l // tm,), in_specs=[row, row], out_specs=row,
                          out_shape=jax.ShapeDtypeStruct((l, d), BF16),
                          compiler_params=_cparams(("parallel",)))(ypre, dyg)


FFN_ROWS = 128
HALO = 16


def _taps_back(tail, g):
    ext = jnp.concatenate([tail, g], axis=0)
    return pltpu.roll(ext, 1, axis=0)[HALO:], pltpu.roll(ext, 2, axis=0)[HALO:]


def ffn_mid_fwd(name, gu, cw, cb, tc=128):
    _, l, f = gu.shape
    tc = _blk(f, tc)

    def body(gu_ref, w_ref, b_ref, a_ref):
        g = gu_ref[0].astype(F32)
        u = gu_ref[1].astype(F32)
        g1, g2 = _taps_back(jnp.zeros((HALO, tc), F32), g)
        gc = w_ref[0:1, :] * g2 + w_ref[1:2, :] * g1 + w_ref[2:3, :] * g + b_ref[...]
        a_ref[...] = (gc * _sigmoid(gc) * u).astype(BF16)

    return pl.pallas_call(
        body, name=name, grid=(f // tc,),
        in_specs=[pl.BlockSpec((2, l, tc), lambda c: (0, 0, c)), pl.BlockSpec((3, tc), lambda c: (0, c)),
                  pl.BlockSpec((1, tc), lambda c: (0, c))],
        out_specs=pl.BlockSpec((l, tc), lambda c: (0, c)),
        out_shape=jax.ShapeDtypeStruct((l, f), BF16),
        compiler_params=_cparams(("parallel",)))(gu, cw, cb)


def ffn_mid_bwd(name, gu, da, cw, cb, tc=128):
    _, l, f = gu.shape
    tc = _blk(f, tc)
    rc = _blk(l, FFN_ROWS)
    nchunk = l // rc

    def body(gu_ref, da_ref, w_ref, b_ref, dgu_ref, dw_ref, db_ref):
        w0, w1, w2, b = w_ref[0:1, :], w_ref[1:2, :], w_ref[2:3, :], b_ref[...]

        def fold(x):
            return jnp.sum(x.reshape(rc // 8, 8, tc), axis=0)

        def chunk(k, carry):
            head, s0, s1, s2, sb = carry
            ci = nchunk - 1 - k
            r0 = pl.multiple_of(ci * rc, rc)
            t0 = pl.multiple_of(jnp.maximum(r0 - HALO, 0), HALO)
            tail = jnp.where(ci > 0, gu_ref[0, pl.ds(t0, HALO), :].astype(F32), 0.0)
            g = gu_ref[0, pl.ds(r0, rc), :].astype(F32)
            u = gu_ref[1, pl.ds(r0, rc), :].astype(F32)
            da = da_ref[pl.ds(r0, rc), :].astype(F32)
            g1, g2 = _taps_back(tail, g)
            gc = w0 * g2 + w1 * g1 + w2 * g + b
            sg = _sigmoid(gc)
            dgu_ref[1, pl.ds(r0, rc), :] = (da * gc * sg).astype(BF16)
            dgc = da * u * (sg * (1.0 + gc * (1.0 - sg)))
            ext = jnp.concatenate([dgc, head], axis=0)
            up1 = pltpu.roll(ext, rc + HALO - 1, axis=0)[:rc]
            up2 = pltpu.roll(ext, rc + HALO - 2, axis=0)[:rc]
            dgu_ref[0, pl.ds(r0, rc), :] = (w2 * dgc + w1 * up1 + w0 * up2).astype(BF16)
            return dgc[:HALO], s0 + fold(dgc * g2), s1 + fold(dgc * g1), s2 + fold(dgc * g), sb + fold(dgc)

        z8 = jnp.zeros((8, tc), F32)
        _, s0, s1, s2, sb = lax.fori_loop(0, nchunk, chunk, (jnp.zeros((HALO, tc), F32), z8, z8, z8, z8))
        dw_ref[0:1, :] = jnp.sum(s0, axis=0, keepdims=True)
        dw_ref[1:2, :] = jnp.sum(s1, axis=0, keepdims=True)
        dw_ref[2:3, :] = jnp.sum(s2, axis=0, keepdims=True)
        db_ref[...] = jnp.sum(sb, axis=0, keepdims=True)

    return pl.pallas_call(
        body, name=name, grid=(f // tc,),
        in_specs=[pl.BlockSpec((2, l, tc), lambda c: (0, 0, c)), pl.BlockSpec((l, tc), lambda c: (0, c)),
                  pl.BlockSpec((3, tc), lambda c: (0, c)), pl.BlockSpec((1, tc), lambda c: (0, c))],
        out_specs=[pl.BlockSpec((2, l, tc), lambda c: (0, 0, c)), pl.BlockSpec((3, tc), lambda c: (0, c)),
                   pl.BlockSpec((1, tc), lambda c: (0, c))],
        out_shape=[jax.ShapeDtypeStruct((2, l, f), BF16), jax.ShapeDtypeStruct((3, f), F32),
                   jax.ShapeDtypeStruct((1, f), F32)],
        compiler_params=_cparams(("parallel",)))(gu, da, cw, cb)


SCAN_LANES = 512


def ssm_discretize(name, a_re, a_im, ldt, b_re, b_im):
    dt = jnp.exp(ldt)
    mag = jnp.exp(a_re * dt)
    ab_re = mag * jnp.cos(a_im * dt)
    ab_im = mag * jnp.sin(a_im * dt)
    den = a_re * a_re + a_im * a_im
    f_re = ((ab_re - 1.0) * a_re + ab_im * a_im) / den
    f_im = (ab_im * a_re - (ab_re - 1.0) * a_im) / den
    return ab_re, ab_im, f_re * b_re - f_im * b_im, f_re * b_im + f_im * b_re


def ssm_disc_fwd(name, a_re, a_im, ldt, b_re, b_im):
    def body(ar, ai, ld, br, bi, o_ar, o_ai, o_br, o_bi):
        r = ssm_discretize(None, ar[...], ai[...], ld[...], br[...], bi[...])
        o_ar[...], o_ai[...], o_br[...], o_bi[...] = r

    sd = jax.ShapeDtypeStruct
    return pl.pallas_call(body, name=name,
                          out_shape=[sd(a_re.shape, F32), sd(a_re.shape, F32), sd(b_re.shape, F32), sd(b_re.shape, F32)],
                          compiler_params=_cparams())(a_re, a_im, ldt, b_re, b_im)


def ssm_disc_bwd(name, a_re, a_im, ldt, b_re, b_im, d_ar, d_ai, d_br, d_bi):
    def body(ar, ai, ld, br, bi, g_ar, g_ai, g_br, g_bi, o_ar, o_ai, o_ld, o_br, o_bi):
        fn = functools.partial(ssm_discretize, None)
        _, vjp = jax.vjp(fn, ar[...], ai[...], ld[...], br[...], bi[...])
        r = vjp((g_ar[...], g_ai[...], g_br[...], g_bi[...]))
        o_ar[...], o_ai[...], o_ld[...], o_br[...], o_bi[...] = r

    sd = jax.ShapeDtypeStruct
    return pl.pallas_call(body, name=name,
                          out_shape=[sd(a_re.shape, F32)] * 3 + [sd(b_re.shape, F32)] * 2,
                          compiler_params=_cparams())(a_re, a_im, ldt, b_re, b_im, d_ar, d_ai, d_br, d_bi)


def _drive(x_ref, wre_ref, wim_ref, dre_ref, dim_ref):
    xb = x_ref[...]
    dre_ref[...] = jnp.dot(xb, wre_ref[...], preferred_element_type=F32)
    dim_ref[...] = jnp.dot(xb, wim_ref[...], preferred_element_type=F32)


def _ssm_dims(l, wre, tj):
    nb, kb, nsb = wre.shape
    jn = l // N_STREAMS
    tj = _blk(jn, tj)
    return nb, kb, nsb, jn, tj, tj * N_STREAMS, jn // tj


def _scan(dre_ref, dim_ref, st_re, st_im, a_ref, tj, reverse, write):
    ns = dre_ref.shape[1]
    cw = min(SCAN_LANES, ns)
    for cb in range(ns // cw):
        sl = slice(cb * cw, (cb + 1) * cw)
        ar = jnp.broadcast_to(a_ref[0:1, sl], (N_STREAMS, cw))
        ai = jnp.broadcast_to(a_ref[1:2, sl], (N_STREAMS, cw))

        def step(jj, carry, sl=sl, ar=ar, ai=ai):
            sr, si = carry
            j = (tj - 1 - jj) if reverse else jj
            off = pl.multiple_of(j * N_STREAMS, N_STREAMS)
            nr = ar * sr - ai * si + dre_ref[pl.ds(off, N_STREAMS), sl]
            ni = ar * si + ai * sr + dim_ref[pl.ds(off, N_STREAMS), sl]
            if write:
                dre_ref[pl.ds(off, N_STREAMS), sl] = nr
                dim_ref[pl.ds(off, N_STREAMS), sl] = ni
            return nr, ni

        sr, si = lax.fori_loop(0, tj, step, (st_re[:, sl], st_im[:, sl]))
        st_re[:, sl] = sr
        st_im[:, sl] = si


def ssm_pass1(name, x, wre, wim, a2, *, reverse, tj=64, rider=None):
    l, d = x.shape
    nb, kb, nsb, jn, tj, r, nblk = _ssm_dims(l, wre, tj)
    ns = nb * nsb

    def body(x_ref, wre_ref, wim_ref, a_ref, cre_ref, cim_ref, dre, dim, st_re, st_im):
        i = pl.program_id(1)

        @pl.when(i == 0)
        def _():
            st_re[...] = jnp.zeros_like(st_re)
            st_im[...] = jnp.zeros_like(st_im)

        _drive(x_ref, wre_ref, wim_ref, dre, dim)
        _scan(dre, dim, st_re, st_im, a_ref, tj, reverse, False)

        @pl.when(i == nblk - 1)
        def _():
            pr, pi = a_ref[0:1, :], a_ref[1:2, :]
            rr, ri = jnp.ones_like(pr), jnp.zeros_like(pr)
            e = jn
            while e:
                if e & 1:
                    rr, ri = rr * pr - ri * pi, rr * pi + ri * pr
                pr, pi = pr * pr - pi * pi, 2.0 * pr * pi
                e >>= 1
            order = range(N_STREAMS - 1, -1, -1) if reverse else range(N_STREAMS)
            cr = jnp.zeros_like(rr)
            ci = jnp.zeros_like(rr)
            for s in order:
                cre_ref[s:s + 1, :] = cr
                cim_ref[s:s + 1, :] = ci
                fr, fi = st_re[s:s + 1, :], st_im[s:s + 1, :]
                cr, ci = fr + rr * cr - ri * ci, fi + rr * ci + ri * cr

    blk = (lambda b, i: (nblk - 1 - i, b)) if reverse else (lambda b, i: (i, b))
    w3 = pl.BlockSpec((None, kb, nsb), lambda b, i: (b, 0, 0))
    st = pl.BlockSpec((N_STREAMS, nsb), lambda b, i: (0, b))
    return _call(
        name, body, grid=(nb, nblk),
        in_specs=[pl.BlockSpec((r, kb), blk), w3, w3, pl.BlockSpec((2, nsb), lambda b, i: (0, b))],
        out_specs=[st, st], out_shape=[jax.ShapeDtypeStruct((N_STREAMS, ns), F32)] * 2,
        scratch=[pltpu.VMEM((r, nsb), F32), pltpu.VMEM((r, nsb), F32),
                 pltpu.VMEM((N_STREAMS, nsb), F32), pltpu.VMEM((N_STREAMS, nsb), F32)],
        sem=("parallel", "arbitrary"), args=(x, wre, wim, a2), rider=rider)


def ssm_fwd2(name, u, bre, bim, a2, init_re, init_im, cre, cim, dskip, *, tj=64, rider=None):
    l, d = u.shape
    nb, kb, nsb, jn, tj, r, nblk = _ssm_dims(l, bre, tj)
    ns = nb * nsb

    def body(u_ref, bre_ref, bim_ref, a_ref, ire_ref, iim_ref, cre_ref, cim_ref, d_ref,
             sre_ref, sim_ref, y_ref, yg_ref, st_re, st_im):
        @pl.when(pl.program_id(1) == 0)
        def _():
            st_re[...] = ire_ref[...]
            st_im[...] = iim_ref[...]

        _drive(u_ref, bre_ref, bim_ref, sre_ref, sim_ref)
        _scan(sre_ref, sim_ref, st_re, st_im, a_ref, tj, False, True)
        y = (jnp.dot(sre_ref[...].astype(BF16), cre_ref[...], preferred_element_type=F32)
             - jnp.dot(sim_ref[...].astype(BF16), cim_ref[...], preferred_element_type=F32)
             + d_ref[...] * u_ref[...].astype(F32))
        y_ref[...] = y
        yg_ref[...] = _gelu(y).astype(BF16)

    rows = lambda w: pl.BlockSpec((r, w), lambda b, i: (i, b))
    w3 = pl.BlockSpec((None, kb, nsb), lambda b, i: (b, 0, 0))
    c3 = pl.BlockSpec((None, nsb, kb), lambda b, i: (b, 0, 0))
    st = pl.BlockSpec((N_STREAMS, nsb), lambda b, i: (0, b))
    return _call(
        name, body, grid=(nb, nblk),
        in_specs=[rows(kb), w3, w3, pl.BlockSpec((2, nsb), lambda b, i: (0, b)), st, st, c3, c3,
                  pl.BlockSpec((1, kb), lambda b, i: (0, b))],
        out_specs=[rows(nsb), rows(nsb), rows(kb), rows(kb)],
        out_shape=[jax.ShapeDtypeStruct((l, ns), F32), jax.ShapeDtypeStruct((l, ns), F32),
                   jax.ShapeDtypeStruct((l, d), F32), jax.ShapeDtypeStruct((l, d), BF16)],
        scratch=[pltpu.VMEM((N_STREAMS, nsb), F32), pltpu.VMEM((N_STREAMS, nsb), F32)],
        sem=("parallel", "arbitrary"), args=(u, bre, bim, a2, init_re, init_im, cre, cim, dskip), rider=rider)


def ssm_bwd2(name, dy, u, sre, sim, ctre, ctim, a2c, init_re, init_im, fre, fim, bre, bim, dskip, *, tj=64, rider=None):
    l, d = u.shape
    nb, kb, nsb, jn, tj, r, nblk = _ssm_dims(l, bre, tj)
    ns = nb * nsb

    def body(dy_ref, u_ref, sre_ref, sim_ref, pre_ref, pim_ref, ctre_ref, ctim_ref, a_ref, ire_ref, iim_ref,
             fre_ref, fim_ref, bre_ref, bim_ref, d_ref,
             du_ref, dbre_ref, dbim_ref, dcre_ref, dcim_ref, dare_ref, daim_ref, dd_ref,
             lre, lim, st_re, st_im):
        i = pl.program_id(1)
        first = i == 0

        @pl.when(first)
        def _():
            st_re[...] = ire_ref[...]
            st_im[...] = iim_ref[...]
            dbre_ref[...] = jnp.zeros_like(dbre_ref)
            dbim_ref[...] = jnp.zeros_like(dbim_ref)
            dcre_ref[...] = jnp.zeros_like(dcre_ref)
            dcim_ref[...] = jnp.zeros_like(dcim_ref)
            dare_ref[...] = jnp.zeros_like(dare_ref)
            daim_ref[...] = jnp.zeros_like(daim_ref)
            dd_ref[...] = jnp.zeros_like(dd_ref)

        _drive(dy_ref, ctre_ref, ctim_ref, lre, lim)
        _scan(lre, lim, st_re, st_im, a_ref, tj, True, True)

        is_t0 = i == nblk - 1
        cw = min(SCAN_LANES, nsb)
        for cb in range(nsb // cw):
            sl = slice(cb * cw, (cb + 1) * cw)
            p_r = jnp.where(is_t0, fre_ref[:, sl], pre_ref[:, sl])
            p_i = jnp.where(is_t0, fim_ref[:, sl], pim_ref[:, sl])
            l_r, l_i = lre[0:N_STREAMS, sl], lim[0:N_STREAMS, sl]
            acc = (l_r * p_r + l_i * p_i, l_i * p_r - l_r * p_i)

            def step(jj, carry, sl=sl):
                a_r, a_i = carry
                off = pl.multiple_of(jj * N_STREAMS, N_STREAMS)
                prev = pl.multiple_of((jj - 1) * N_STREAMS, N_STREAMS)
                l_r, l_i = lre[pl.ds(off, N_STREAMS), sl], lim[pl.ds(off, N_STREAMS), sl]
                p_r, p_i = sre_ref[pl.ds(prev, N_STREAMS), sl], sim_ref[pl.ds(prev, N_STREAMS), sl]
                return a_r + l_r * p_r + l_i * p_i, a_i + l_i * p_r - l_r * p_i

            a_r, a_i = lax.fori_loop(1, tj, step, acc)
            dare_ref[:, sl] += a_r
            daim_ref[:, sl] += a_i

        dyf = dy_ref[...].astype(F32)
        uf = u_ref[...].astype(F32)
        dd_ref[...] += jnp.sum(dyf * uf, axis=0, keepdims=True)
        lrb = lre[...].astype(BF16)
        lib = lim[...].astype(BF16)
        ub = u_ref[...]
        dyb = dy_ref[...]
        dbre_ref[...] += lax.dot_general(ub, lrb, TN, preferred_element_type=F32)
        dbim_ref[...] += lax.dot_general(ub, lib, TN, preferred_element_type=F32)
        dcre_ref[...] += lax.dot_general(sre_ref[...].astype(BF16), dyb, TN, preferred_element_type=F32)
        dcim_ref[...] -= lax.dot_general(sim_ref[...].astype(BF16), dyb, TN, preferred_element_type=F32)
        du = (lax.dot_general(lrb, bre_ref[...], NT, preferred_element_type=F32)
              + lax.dot_general(lib, bim_ref[...], NT, preferred_element_type=F32)
              + d_ref[...] * dyf)
        du_ref[...] = du.astype(BF16)

    rev = lambda w: pl.BlockSpec((r, w), lambda b, i: (nblk - 1 - i, b))
    prev_tile = pl.BlockSpec((N_STREAMS, nsb), lambda b, i: (jnp.maximum((nblk - 1 - i) * tj - 1, 0), b))
    st = pl.BlockSpec((N_STREAMS, nsb), lambda b, i: (0, b))
    w3 = pl.BlockSpec((None, kb, nsb), lambda b, i: (b, 0, 0))
    c3 = pl.BlockSpec((None, nsb, kb), lambda b, i: (b, 0, 0))
    dsp = pl.BlockSpec((1, kb), lambda b, i: (0, b))
    return _call(
        name, body, grid=(nb, nblk),
        in_specs=[rev(kb), rev(kb), rev(nsb), rev(nsb), prev_tile, prev_tile, w3, w3,
                  pl.BlockSpec((2, nsb), lambda b, i: (0, b)), st, st, st, st, w3, w3, dsp],
        out_specs=[rev(kb), w3, w3, c3, c3, st, st, dsp],
        out_shape=[jax.ShapeDtypeStruct((l, d), BF16), jax.ShapeDtypeStruct((nb, kb, nsb), F32),
                   jax.ShapeDtypeStruct((nb, kb, nsb), F32), jax.ShapeDtypeStruct((nb, nsb, kb), F32),
                   jax.ShapeDtypeStruct((nb, nsb, kb), F32), jax.ShapeDtypeStruct((N_STREAMS, ns), F32),
                   jax.ShapeDtypeStruct((N_STREAMS, ns), F32), jax.ShapeDtypeStruct((1, d), F32)],
        scratch=[pltpu.VMEM((r, nsb), F32), pltpu.VMEM((r, nsb), F32),
                 pltpu.VMEM((N_STREAMS, nsb), F32), pltpu.VMEM((N_STREAMS, nsb), F32)],
        sem=("parallel", "arbitrary"),
        args=(dy, u, sre, sim, sre, sim, ctre, ctim, a2c, init_re, init_im, fre, fim, bre, bim, dskip), rider=rider)


ATT_TILE = 256
ATT_HEADS = 4
EXP_ZERO_BELOW = -104.0
LANES = 128


def _split_dot(x, tri):
    hi = x.astype(BF16)
    lo = (x - hi.astype(F32)).astype(BF16)
    return jnp.dot(hi, tri, preferred_element_type=F32) + jnp.dot(lo, tri, preferred_element_type=F32)


def _sb_logs(z, causal):
    sp = jnp.maximum(z, 0.0) + jnp.log(1.0 + jnp.exp(-jnp.abs(z)))
    l1m = -sp
    if causal is not None:
        l1m = jnp.where(causal, l1m, 0.0)
    return z - sp, l1m


def attn_fwd(name, q, kv, t=ATT_TILE, rider=None):
    l, dm = q.shape
    dh = HEAD_DIM
    h = dm // dh
    t = _blk(l, t)
    hb = _blk(h, ATT_HEADS)
    wb = hb * dh
    scale = dh ** -0.5

    def body(q_ref, k_ref, v_ref, o_ref, tot_ref):
        i = pl.program_id(1)
        hd = lambda g: slice(g * dh, (g + 1) * dh)
        qs = [(q_ref[:, hd(g)].astype(F32) * scale).astype(BF16) for g in range(hb)]
        row = lax.broadcasted_iota(jnp.int32, (t, t), 0)
        col = lax.broadcasted_iota(jnp.int32, (t, t), 1)
        tri_gt = jnp.where(row > col, 1.0, 0.0).astype(BF16)
        causal = col < row

        def tile(jb, carry, mask):
            off = pl.multiple_of(jb * t, t)
            heads = range(hb)
            z = [lax.dot_general(qs[g], k_ref[pl.ds(off, t), hd(g)], NT, preferred_element_type=F32) for g in heads]
            logs = [_sb_logs(z[g], mask) for g in heads]
            rem = [_split_dot(logs[g][1], tri_gt) for g in heads]
            w = [jnp.exp(logs[g][0] + rem[g] + carry[g][1]) for g in heads]
            if mask is not None:
                w = [jnp.where(mask, w[g], 0.0) for g in heads]
            pv = [jnp.dot(w[g].astype(BF16), v_ref[pl.ds(off, t), hd(g)], preferred_element_type=F32) for g in heads]
            return tuple((carry[g][0] + pv[g], carry[g][1] + rem[g][:, 0:1] + logs[g][1][:, 0:1]) for g in heads)

        def live(carry):
            top = carry[0][1]
            for g in range(1, hb):
                top = jnp.maximum(top, carry[g][1])
            return (jnp.max(top) > EXP_ZERO_BELOW).astype(jnp.int32)

        def more(c):
            it, alive, _ = c
            return jnp.logical_and(it < i, alive > 0)

        def step(c):
            it, _, carry = c
            carry = tile(i - 1 - it, carry, None)
            return it + 1, live(carry), carry

        carry = tile(i, ((jnp.zeros((t, dh), F32), jnp.zeros((t, 1), F32)),) * hb, causal)
        done, _, carry = lax.while_loop(more, step, (jnp.int32(0), live(carry), carry))
        tot_ref[...] = jnp.zeros_like(tot_ref)
        for g, (acc, run) in enumerate(carry):
            o_ref[:, hd(g)] = acc.astype(BF16)
            tot_ref[:, g:g + 1] = run
        tot_ref[:, hb:hb + 1] = jnp.full((t, 1), done, jnp.int32).astype(F32)

    qsp = pl.BlockSpec((t, wb), lambda hh, i: (i, hh))
    ksp = pl.BlockSpec((None, l, wb), lambda hh, i: (0, 0, hh))
    vsp = pl.BlockSpec((None, l, wb), lambda hh, i: (1, 0, hh))
    tsp = pl.BlockSpec((None, t, LANES), lambda hh, i: (hh, i, 0))
    return _call(name, body, grid=(h // hb, l // t), in_specs=[qsp, ksp, vsp], out_specs=[qsp, tsp],
                 out_shape=[jax.ShapeDtypeStruct((l, dm), BF16), jax.ShapeDtypeStruct((h // hb, l, LANES), F32)],
                 sem=("parallel", "parallel"), args=(q, kv, kv), rider=rider)


def attn_bwd(name, q, kv, tot, do, t=ATT_TILE, rider=None):
    l, dm = q.shape
    dh = HEAD_DIM
    h = dm // dh
    t = _blk(l, t)
    hb = _blk(h, ATT_HEADS)
    wb = hb * dh
    nq = l // t
    scale = dh ** -0.5

    def body(q_ref, k_ref, v_ref, tot_ref, do_ref, dq_ref, dkv_ref, dk_ref, dv_ref):
        i = pl.program_id(1)
        hd = lambda g: slice(g * dh, (g + 1) * dh)

        @pl.when(i == 0)
        def _():
            dk_ref[...] = jnp.zeros_like(dk_ref)
            dv_ref[...] = jnp.zeros_like(dv_ref)

        qs = [(q_ref[:, hd(g)].astype(F32) * scale).astype(BF16) for g in range(hb)]
        dob = [do_ref[:, hd(g)] for g in range(hb)]
        total = [tot_ref[:, g:g + 1] for g in range(hb)]
        row = lax.broadcasted_iota(jnp.int32, (t, t), 0)
        col = lax.broadcasted_iota(jnp.int32, (t, t), 1)
        tri_le = jnp.where(row <= col, 1.0, 0.0).astype(BF16)
        tri_lt = jnp.where(row < col, 1.0, 0.0).astype(BF16)
        causal = col < row

        def tile(jb, carry, mask):
            off = pl.multiple_of(jb * t, t)
            heads = range(hb)
            kj = [k_ref[pl.ds(off, t), hd(g)] for g in heads]
            z = [lax.dot_general(qs[g], kj[g], NT, preferred_element_type=F32) for g in heads]
            dp = [lax.dot_general(dob[g], v_ref[pl.ds(off, t), hd(g)], NT, preferred_element_type=F32) for g in heads]
            logs = [_sb_logs(z[g], mask) for g in heads]
            lpre = [_split_dot(logs[g][1], tri_le) for g in heads]
            w = [jnp.exp(logs[g][0] + (total[g] - carry[g][1] - lpre[g])) for g in heads]
            if mask is not None:
                w = [jnp.where(mask, w[g], 0.0) for g in heads]
            p = [w[g] * dp[g] for g in heads]
            for g in heads:
                dv_ref[pl.ds(off, t), hd(g)] += lax.dot_general(w[g].astype(BF16), dob[g], TN, preferred_element_type=F32)
            qpre = [carry[g][2] + jnp.dot(p[g].astype(BF16), tri_lt, preferred_element_type=F32) for g in heads]
            dz = [p[g] - jnp.exp(logs[g][0]) * (p[g] + qpre[g]) for g in heads]
            if mask is not None:
                dz = [jnp.where(mask, dz[g], 0.0) for g in heads]
            dzb = [dz[g].astype(BF16) for g in heads]
            for g in heads:
                dk_ref[pl.ds(off, t), hd(g)] += lax.dot_general(dzb[g], qs[g], TN, preferred_element_type=F32)
            dq = [carry[g][0] + jnp.dot(dzb[g], kj[g], preferred_element_type=F32) for g in heads]
            return tuple((dq[g], carry[g][1] + lpre[g][:, t - 1:t], qpre[g][:, t - 1:t] + p[g][:, t - 1:t]) for g in heads)

        zero = jnp.zeros((t, 1), F32)
        done = jnp.max(tot_ref[:, hb:hb + 1]).astype(jnp.int32)
        carry = lax.fori_loop(i - done, i, lambda jb, c: tile(jb, c, None), ((jnp.zeros((t, dh), F32), zero, zero),) * hb)
        carry = tile(i, carry, causal)
        for g in range(hb):
            dq_ref[:, hd(g)] = (carry[g][0] * scale).astype(BF16)

        @pl.when(i == nq - 1)
        def _():
            dkv_ref[0] = dk_ref[...].astype(BF16)
            dkv_ref[1] = dv_ref[...].astype(BF16)

    qsp = pl.BlockSpec((t, wb), lambda hh, i: (i, hh))
    ksp = pl.BlockSpec((None, l, wb), lambda hh, i: (0, 0, hh))
    vsp = pl.BlockSpec((None, l, wb), lambda hh, i: (1, 0, hh))
    tsp = pl.BlockSpec((None, t, LANES), lambda hh, i: (hh, i, 0))
    return _call(name, body, grid=(h // hb, nq), in_specs=[qsp, ksp, vsp, tsp, qsp],
                 out_specs=[qsp, pl.BlockSpec((2, l, wb), lambda hh, i: (0, 0, hh))],
                 out_shape=[jax.ShapeDtypeStruct((l, dm), BF16), jax.ShapeDtypeStruct((2, l, dm), BF16)],
                 scratch=[pltpu.VMEM((l, wb), F32), pltpu.VMEM((l, wb), F32)],
                 sem=("parallel", "arbitrary"), args=(q, kv, kv, tot, do), rider=rider)


def adamw(name, w, g, m, v):
    shape = w.shape
    cols = shape[-1]
    rows = w.size // cols
    tr = rows
    if rows % 8 == 0:
        tr = 8 * _blk(rows // 8, 64)
    c1 = 1.0 - ADAM_B1 ** ADAM_STEP
    c2 = 1.0 - ADAM_B2 ** ADAM_STEP

    def body(w_ref, g_ref, m_ref, v_ref, d_ref, nm_ref, nv_ref):
        gg = g_ref[...]
        mm = ADAM_B1 * m_ref[...] + (1.0 - ADAM_B1) * gg
        vv = ADAM_B2 * v_ref[...] + (1.0 - ADAM_B2) * (gg * gg)
        nm_ref[...] = mm
        nv_ref[...] = vv
        d_ref[...] = -ADAM_LR * ((mm / c1) / (jnp.sqrt(vv / c2) + ADAM_EPS) + ADAM_WD * w_ref[...])

    sp = pl.BlockSpec((tr, cols), lambda i: (i, 0))
    sd = jax.ShapeDtypeStruct((rows, cols), F32)
    outs = pl.pallas_call(body, name=name, grid=(rows // tr,), in_specs=[sp] * 4, out_specs=[sp] * 3,
                          out_shape=[sd] * 3, compiler_params=_cparams(("parallel",)))(
        w.reshape(rows, cols), g.reshape(rows, cols), m.reshape(rows, cols), v.reshape(rows, cols))
    return tuple(o.reshape(shape) for o in outs)


def _perm(a):
    l, d = a.shape
    return a.reshape(N_STREAMS, l // N_STREAMS, d).transpose(1, 0, 2).reshape(l, d)


def _unperm(a):
    l, d = a.shape
    return a.reshape(l // N_STREAMS, N_STREAMS, d).transpose(1, 0, 2).reshape(l, d)


def _heads(a):
    l, d = a.shape
    return a.reshape(l, d // HEAD_DIM, HEAD_DIM).transpose(1, 0, 2)


def _unheads(a):
    h, l, dh = a.shape
    return a.transpose(1, 0, 2).reshape(l, h * dh)


def _ssm_layouts(d):
    g = d // SSM_GROUP
    gb = MXU_DIM // SSM_GROUP if d >= MXU_DIM else g
    return g, gb, g // gb


def _b_blocks(bb, d):
    g, gb, nb = _ssm_layouts(d)
    b4 = bb.reshape(SSM_GROUP, nb, gb, STATE)
    eye = jnp.eye(gb, dtype=bb.dtype)
    return jnp.einsum('hbqp,gq->bghqp', b4, eye).reshape(nb, gb * SSM_GROUP, gb * STATE)


def _b_unblocks(db, d):
    g, gb, nb = _ssm_layouts(d)
    eye = jnp.eye(gb, dtype=db.dtype)
    return jnp.einsum('bghqp,gq->hbqp', db.reshape(nb, gb, SSM_GROUP, gb, STATE), eye).reshape(SSM_GROUP, g * STATE)


def _c_blocks(c, d):
    g, gb, nb = _ssm_layouts(d)
    eye = jnp.eye(gb, dtype=c.dtype)
    return jnp.einsum('bghp,gq->bqpgh', c.reshape(nb, gb, SSM_GROUP, STATE), eye).reshape(nb, gb * STATE, gb * SSM_GROUP)


def _c_unblocks(dc, d):
    g, gb, nb = _ssm_layouts(d)
    eye = jnp.eye(gb, dtype=dc.dtype)
    return jnp.einsum('bqpgh,gq->bghp', dc.reshape(nb, gb, STATE, gb, SSM_GROUP), eye).reshape(g, SSM_GROUP, STATE)


class NoComm:
    def __init__(self):
        self.local = {}

    def gather(self, names):
        return None

    def landed(self, names, bufs, p):
        pass

    def reduce(self, names, arrays):
        self.local.update(zip(names, arrays))
        return None

    def reduced(self, names, slots):
        pass


def _ffn_fwd(tag, x, r, gain, p, layer, cm, ride):
    xs, h = add_rmsnorm(f"norm_ffn{tag}", x, r, [gain])
    rider = cm.gather(ride) if ride else None
    gu = mm_up(f"ffn_up{tag}", h, p["wup"][layer], 0, rider=rider)
    if rider is not None:
        gu, landed = gu
        cm.landed(ride, landed, p)
    a = ffn_mid_fwd(f"ffn_mid{tag}", gu, p["conv_w"][layer], p["conv_b"][layer:layer + 1])
    f = mm_nn(f"ffn_down{tag}", a, p["wdown"][layer], F32)
    return xs, h, gu, a, f


def _ffn_bwd(tag, dxo, dfb, xs, h, gu, a, gain, p, layer, s, cm, ride):
    wdown = p["wdown"][layer]
    f = wdown.shape[0]
    da = mm_nt(f"ffn_down_dx{tag}", dfb, wdown, BF16, tn=f // 2)
    dwdown = mm_tn(f"ffn_down_dw{tag}", a, dfb, tmo=f // 2)
    dgu, dcw, dcb = ffn_mid_bwd(f"ffn_mid_bwd{tag}", gu, da, p["conv_w"][layer], p["conv_b"][layer:layer + 1])
    rider = cm.reduce(*ride) if ride else None
    dh = mm_up_nt(f"ffn_up_dx{tag}", dgu, p["wup"][layer], 0, rider=rider)
    if rider is not None:
        dh, slots = dh
        cm.reduced(ride[0], slots)
    dwup = mm_up_tn(f"ffn_up_dw{tag}", h, dgu, s)
    dxs, dg, dxsb = norm_bwd(f"norm_ffn_bwd{tag}", xs, dxo, [(gain, dh)], bf16_copy=True)
    return dxs, dxsb, dg, dwup, dwdown, dcw, dcb


def _local_step(x, tgt, p, cm=None):
    cm = NoComm() if cm is None else cm
    l, d = x.shape
    s = p["wglu4"].shape[0]
    gr = {}

    a_re, a_im = p["a_re"].reshape(1, -1), p["a_im"].reshape(1, -1)
    ldt = jnp.repeat(p["log_dt"].reshape(-1), STATE).reshape(1, -1)
    bk_re = p["b_re"].transpose(2, 0, 1).reshape(SSM_GROUP, -1)
    bk_im = p["b_im"].transpose(2, 0, 1).reshape(SSM_GROUP, -1)
    ab_re, ab_im, bb_re, bb_im = ssm_disc_fwd("ssm_disc", a_re, a_im, ldt, bk_re, bk_im)
    a2 = jnp.concatenate([ab_re, ab_im], axis=0)
    a2c = jnp.concatenate([ab_re, -ab_im], axis=0)
    bre, bim = _b_blocks(bb_re, d).astype(BF16), _b_blocks(bb_im, d).astype(BF16)
    cre, cim = _c_blocks(p["c_re"], d).astype(BF16), _c_blocks(p["c_im"], d).astype(BF16)
    ctre, ctim = cre.transpose(0, 2, 1), -cim.transpose(0, 2, 1)
    dskip = p["d"].reshape(1, d)

    xp = _perm(x)
    (h0p,) = add_rmsnorm("norm_mix0", xp, None, [p["norm_mix"][0]])
    u = mm_nn("ssm_in", h0p, p["win"], BF16)
    rider = cm.gather(["wdown0"])
    res = ssm_pass1("ssm_fwd1", u, bre, bim, a2, reverse=False, rider=rider)
    f_re, f_im = res[0], res[1]
    cm.landed(["wdown0"], res[2:], p)
    rider = cm.gather(["wup0"])
    res = ssm_fwd2("ssm_fwd2", u, bre, bim, a2, f_re, f_im, cre, cim, dskip, rider=rider)
    s_re, s_im, ypre, yg = res[:4]
    cm.landed(["wup0"], res[4:], p)
    z = mm_nn_colshard("ssm_glu", yg, p["wglu4"], F32)
    mix = _unperm(glu_fwd("glu", z))

    x1, h1, gu0, a0, f0 = _ffn_fwd("0", x, mix, p["norm_ffn"][0], p, 0, cm, ["kv", "wq", "wo"])
    x2, hk, h2 = add_rmsnorm("norm_kv_mix1", x1, f0, [p["norm_kv"], p["norm_mix"][1]])
    kvw = p["kvw4"][:, None]
    kv = mm_up("kv_proj", hk, kvw, 0)
    qf = mm_nn("q_proj", h2, p["wq"], BF16)
    rider = cm.gather(["wup1", "wdown1"])
    res = attn_fwd("attn", qf, kv, rider=rider)
    ob, tot = res[0], res[1]
    cm.landed(["wup1", "wdown1"], res[2:], p)
    ao = mm_nn("o_proj", ob, p["wo"], F32)
    x3, h3, gu1, a1, f1 = _ffn_fwd("1", x2, ao, p["norm_ffn"][1], p, 1, cm, None)
    loss, dx4, dg_final, dx4b = final_loss("final_loss", x3, f1, tgt, p["norm_final"])
    gr["norm_final"] = dg_final.reshape(d)

    dx3, dx3b, dg_ffn1, dwup1, dwdown1, dcw1, dcb1 = _ffn_bwd(
        "1", dx4, dx4b, x3, h3, gu1, a1, p["norm_ffn"][1], p, 1, s, cm, None)
    do2 = mm_nt("o_proj_dx", dx3b, p["wo"], BF16)
    dwo = mm_tn("o_proj_dw", ob, dx3b)
    rider = cm.reduce(["wup1", "wdown1"], [dwup1, dwdown1])
    res = attn_bwd("attn_bwd", qf, kv, tot, do2, rider=rider)
    dqf, dkv = res[0], res[1]
    cm.reduced(["wup1", "wdown1"], res[2:])
    dh2 = mm_nt("q_proj_dx", dqf, p["wq"], F32)
    dwq = mm_tn("q_proj_dw", h2, dqf)
    dhk = mm_up_nt("kv_proj_dx", dkv, kvw, 0)
    dwkv = mm_up_tn("kv_proj_dw", hk, dkv, s)
    dx2, dg_mix1, dg_kv, dx2b = norm_bwd("norm_kv_mix1_bwd", x2, dx3, [(p["norm_mix"][1], dh2), (p["norm_kv"], dhk)],
                                         bf16_copy=True)
    gr["norm_kv"] = dg_kv.reshape(d)

    dx1, _, dg_ffn0, dwup0, dwdown0, dcw0, dcb0 = _ffn_bwd(
        "0", dx2, dx2b, x1, h1, gu0, a0, p["norm_ffn"][0], p, 0, s, cm, (["wo", "wq", "kv"], [dwo, dwq, dwkv]))
    dx1p = _perm(dx1)
    dz = glu_bwd("glu_bwd", z, dx1p)
    dyg = mm_nt_colshard("ssm_glu_dx", dz, p["wglu4"], F32)
    dwglu = mm_tn_colshard("ssm_glu_dw", yg, dz, s)
    dy = gelu_bwd("gelu_bwd", ypre, dyg)
    i_re, i_im = ssm_pass1("ssm_bwd1", dy, ctre, ctim, a2c, reverse=True)
    rider = cm.reduce(["wup0", "wdown0", "wglu"], [dwup0, dwdown0, dwglu])
    res = ssm_bwd2("ssm_bwd2", dy, u, s_re, s_im, ctre, ctim, a2c, i_re, i_im, f_re, f_im, bre, bim, dskip, rider=rider)
    du, dbre, dbim, dcre, dcim, da_re, da_im, dd = res[:8]
    cm.reduced(["wup0", "wdown0", "wglu"], res[8:])
    dh0p = mm_nt("ssm_in_dx", du, p["win"], F32)
    dwin = mm_tn("ssm_in_dw", h0p, du)
    dxp, dg_mix0 = norm_bwd("norm_mix0_bwd", xp, dx1p, [(p["norm_mix"][0], dh0p)])
    dx = _unperm(dxp)

    g_are, g_aim, g_ldt, g_bre, g_bim = ssm_disc_bwd(
        "ssm_disc_bwd", a_re, a_im, ldt, bk_re, bk_im,
        jnp.sum(da_re, axis=0, keepdims=True), jnp.sum(da_im, axis=0, keepdims=True),
        _b_unblocks(dbre, d), _b_unblocks(dbim, d))
    g = d // SSM_GROUP
    gr["a_re"] = g_are.reshape(g, STATE)
    gr["a_im"] = g_aim.reshape(g, STATE)
    gr["log_dt"] = jnp.sum(g_ldt.reshape(g, STATE), axis=1)
    gr["b_re"] = g_bre.reshape(SSM_GROUP, g, STATE).transpose(1, 2, 0)
    gr["b_im"] = g_bim.reshape(SSM_GROUP, g, STATE).transpose(1, 2, 0)
    gr["c_re"] = _c_unblocks(dcre, d)
    gr["c_im"] = _c_unblocks(dcim, d)
    gr["d"] = dd.reshape(g, SSM_GROUP)
    gr["norm_mix"] = jnp.concatenate([dg_mix0, dg_mix1], axis=0)
    gr["norm_ffn"] = jnp.concatenate([dg_ffn0, dg_ffn1], axis=0)
    gr["conv_w"] = jnp.stack([dcw0, dcw1])
    gr["conv_b"] = jnp.concatenate([dcb0, dcb1], axis=0)
    gr["win"] = dwin
    gr.update(getattr(cm, "local", {}))
    return loss, dx, gr


MESH = pl.DeviceIdType.MESH
N_CHIPS = 4
N_DEV = 8
ANY = pl.BlockSpec(memory_space=pl.ANY)


def _pos():
    x, y, c = lax.axis_index("x"), lax.axis_index("y"), lax.axis_index("c")
    return x, y, c, 2 * x + y


def _other_chips(x, y):
    return [(1 - x, y), (x, 1 - y), (1 - x, 1 - y)]


def _remote(src, dst, send_sem, recv_sem, dev):
    return pltpu.make_async_remote_copy(src_ref=src, dst_ref=dst, send_sem=send_sem, recv_sem=recv_sem,
                                        device_id=dev, device_id_type=MESH)


def cast_into_slot(name, a, chip, dtype):
    r, cdim = a.shape
    tr = 16 * _blk(r // 16, 32) if r % 16 == 0 else r

    def body(m_ref, a_ref, o_ref):
        o_ref[...] = a_ref[...].astype(o_ref.dtype)

    gs = pltpu.PrefetchScalarGridSpec(
        num_scalar_prefetch=1, grid=(r // tr,),
        in_specs=[pl.BlockSpec((tr, cdim), lambda i, m_ref: (i, 0))],
        out_specs=pl.BlockSpec((None, tr, cdim), lambda i, m_ref: (m_ref[0], i, 0)))
    return pl.pallas_call(body, name=name, grid_spec=gs, out_shape=jax.ShapeDtypeStruct((N_CHIPS, r, cdim), dtype),
                          compiler_params=_cparams(("parallel",)))(chip, a)


def gather_weights(name, bufs):
    n = len(bufs)

    def body(*refs):
        _gather_start(refs[n:2 * n], refs[2 * n:])
        _gather_finish(refs[n:2 * n], refs[2 * n:])

    sem = pltpu.SemaphoreType.DMA
    return pl.pallas_call(
        body, name=name, in_specs=[ANY] * n, out_specs=[ANY] * n,
        out_shape=[jax.ShapeDtypeStruct(b.shape, b.dtype) for b in bufs],
        input_output_aliases={w: w for w in range(n)},
        scratch_shapes=[sem((n, 3)), sem((n, 3)), sem((n, 3)), sem((n, 3))],
        compiler_params=pltpu.CompilerParams(has_side_effects=True),
    )(*bufs)


def _half(ref, chip, core):
    hr = ref.shape[1] // 2
    return ref.at[chip, pl.ds(core * hr, hr), :]


def _gather_start(bufs, sems):
    send_a, recv_a = sems[0], sems[1]
    x, y, c, m = _pos()
    for w, buf in enumerate(bufs):
        for j, (px, py) in enumerate(_other_chips(x, y)):
            blk = _half(buf, m, c)
            _remote(blk, blk, send_a.at[w, j], recv_a.at[w, j], (px, py, c)).start()


def _gather_finish(bufs, sems):
    send_a, recv_a, send_b, recv_b = sems
    x, y, c, m = _pos()
    chips = _other_chips(x, y)
    sib = (x, y, 1 - c)
    for j, (px, py) in enumerate(chips):
        for w, buf in enumerate(bufs):
            blk = _half(buf, 2 * px + py, c)
            _remote(blk, blk, send_a.at[w, j], recv_a.at[w, j], (px, py, c)).wait_recv()
            _remote(blk, blk, send_b.at[w, j], recv_b.at[w, j], sib).start()
    for j, (px, py) in enumerate(chips):
        for w, buf in enumerate(bufs):
            blk = _half(buf, 2 * px + py, 1 - c)
            _remote(blk, blk, send_b.at[w, j], recv_b.at[w, j], sib).wait_recv()
    for j, (px, py) in enumerate(chips):
        for w, buf in enumerate(bufs):
            mine, landed = _half(buf, m, c), _half(buf, 2 * px + py, c)
            _remote(mine, mine, send_a.at[w, j], recv_a.at[w, j], (px, py, c)).wait_send()
            _remote(landed, landed, send_b.at[w, j], recv_b.at[w, j], sib).wait_send()


def gather_rider(bufs):
    n = len(bufs)
    return Rider(ins=list(bufs), outs=[jax.ShapeDtypeStruct(b.shape, b.dtype) for b in bufs],
                 alias={w: w for w in range(n)}, sems=[(n, 3)] * 4,
                 start=lambda i, o, s: _gather_start(o, s), finish=lambda i, o, s: _gather_finish(o, s))


def exchange_halves(name, arrs, mine):
    n = len(arrs)

    def body(*refs):
        ins, outs = refs[:n], refs[n:2 * n]
        send, recv = refs[2 * n:]
        x, y, c, _ = _pos()
        sel = c if mine else 1 - c
        cps = []
        for w in range(n):
            hr = ins[w].shape[1] // 2
            cp = _remote(ins[w].at[:, pl.ds(sel * hr, hr), :], outs[w], send.at[w], recv.at[w], (x, y, 1 - c))
            cp.start()
            cps.append(cp)
        for cp in cps:
            cp.wait()

    sem = pltpu.SemaphoreType.DMA
    return pl.pallas_call(
        body, name=name, in_specs=[ANY] * n, out_specs=[ANY] * n,
        out_shape=[jax.ShapeDtypeStruct((a.shape[0], a.shape[1] // 2, a.shape[2]), a.dtype) for a in arrs],
        scratch_shapes=[sem((n,)), sem((n,))],
        compiler_params=pltpu.CompilerParams(has_side_effects=True),
    )(*arrs)


def scatter_to_chips(name, arrs):
    n = len(arrs)

    def body(*refs):
        _scatter(refs[:n], refs[n:2 * n], refs[2 * n:], "start")
        _scatter(refs[:n], refs[n:2 * n], refs[2 * n:], "wait")

    sem = pltpu.SemaphoreType.DMA
    return pl.pallas_call(
        body, name=name, in_specs=[ANY] * n, out_specs=[ANY] * n,
        out_shape=[jax.ShapeDtypeStruct((3,) + a.shape[1:], a.dtype) for a in arrs],
        scratch_shapes=[sem((n, 3)), sem((n, 3))],
        compiler_params=pltpu.CompilerParams(has_side_effects=True),
    )(*arrs)


def _scatter(ins, outs, sems, what):
    send, recv = sems
    x, y, c, _ = _pos()
    for w in range(len(ins)):
        for j, (px, py) in enumerate(_other_chips(x, y)):
            cp = _remote(ins[w].at[2 * px + py], outs[w].at[j], send.at[w, j], recv.at[w, j], (px, py, c))
            if what == "start":
                cp.start()
            else:
                cp.wait()


def scatter_rider(arrs):
    n = len(arrs)
    return Rider(ins=list(arrs), outs=[jax.ShapeDtypeStruct((3,) + a.shape[1:], a.dtype) for a in arrs],
                 alias={}, sems=[(n, 3)] * 2,
                 start=lambda i, o, s: _scatter(i, o, s, "start"), finish=lambda i, o, s: _scatter(i, o, s, "wait"))


def share_halves(name, fulls):
    n = len(fulls)

    def body(*refs):
        outs = refs[n:2 * n]
        send, recv = refs[2 * n:]
        x, y, c, _ = _pos()
        cps = []
        for w in range(n):
            hr = outs[w].shape[0] // 2
            blk = outs[w].at[pl.ds(c * hr, hr), :]
            cp = _remote(blk, blk, send.at[w], recv.at[w], (x, y, 1 - c))
            cp.start()
            cps.append(cp)
        for w, cp in enumerate(cps):
            hr = outs[w].shape[0] // 2
            cp.wait_send()
            blk = outs[w].at[pl.ds((1 - c) * hr, hr), :]
            _remote(blk, blk, send.at[w], recv.at[w], (x, y, 1 - c)).wait_recv()

    sem = pltpu.SemaphoreType.DMA
    return pl.pallas_call(
        body, name=name, in_specs=[ANY] * n, out_specs=[ANY] * n,
        out_shape=[jax.ShapeDtypeStruct(a.shape, a.dtype) for a in fulls],
        input_output_aliases={w: w for w in range(n)},
        scratch_shapes=[sem((n,)), sem((n,))],
        compiler_params=pltpu.CompilerParams(has_side_effects=True),
    )(*fulls)


def add_own_half(name, full, got, core, out_dtype):
    n, r, cdim = full.shape
    hr = r // 2
    tr = 8 * _blk(hr // 8, 32) if out_dtype == F32 else 16 * _blk(hr // 16, 16)
    nbh = hr // tr

    def body(c_ref, f_ref, g_ref, o_ref):
        o_ref[...] = (f_ref[...] + g_ref[...]).astype(o_ref.dtype)

    gs = pltpu.PrefetchScalarGridSpec(
        num_scalar_prefetch=1, grid=(n, nbh),
        in_specs=[pl.BlockSpec((None, tr, cdim), lambda s, i, c_ref: (s, c_ref[0] * nbh + i, 0)),
                  pl.BlockSpec((None, tr, cdim), lambda s, i, c_ref: (s, i, 0))],
        out_specs=pl.BlockSpec((None, tr, cdim), lambda s, i, c_ref: (s, i, 0)))
    return pl.pallas_call(body, name=name, grid_spec=gs, out_shape=jax.ShapeDtypeStruct((n, hr, cdim), out_dtype),
                          compiler_params=_cparams(("parallel", "parallel")))(core, full, got)


def sum_into_half(name, part, slots, chip_core, chip_order):
    _, hr, cdim = part.shape
    unit = 8 if part.dtype == F32 else 16
    tr = unit * _blk(hr // unit, 256 // unit)
    nbh = hr // tr

    def body(mc_ref, p_ref, s_ref, o_ref):
        terms = [p_ref[...].astype(F32)] + [s_ref[k].astype(F32) for k in range(3)]
        if chip_order:
            m = mc_ref[0]
            own, fx, fy, fxy = terms
            terms = [jnp.where((k ^ m) == 0, own, jnp.where((k ^ m) == 2, fx, jnp.where((k ^ m) == 1, fy, fxy)))
                     for k in range(N_CHIPS)]
        o_ref[...] = ((terms[0] + terms[1]) + terms[2]) + terms[3]

    gs = pltpu.PrefetchScalarGridSpec(
        num_scalar_prefetch=1, grid=(nbh,),
        in_specs=[pl.BlockSpec((None, tr, cdim), lambda i, mc: (mc[0], i, 0)),
                  pl.BlockSpec((3, tr, cdim), lambda i, mc: (0, i, 0))],
        out_specs=pl.BlockSpec((tr, cdim), lambda i, mc: (mc[1] * nbh + i, 0)))
    return pl.pallas_call(body, name=name, grid_spec=gs, out_shape=jax.ShapeDtypeStruct((2 * hr, cdim), F32),
                          compiler_params=_cparams(("parallel",)))(chip_core, part, slots)


class StepComm:
    def __init__(self, bufs, core, chip_core):
        self.bufs, self.core, self.chip_core = bufs, core, chip_core
        self.part, self.fulls = {}, {}

    def gather(self, names):
        return gather_rider([self.bufs[n] for n in names])

    def landed(self, names, bufs, p):
        for n, b in zip(names, bufs):
            if n.startswith("wdown"):
                p["wdown"][int(n[-1])] = b.reshape(-1, b.shape[-1])
            elif n.startswith("wup"):
                p["wup"][int(n[-1])] = b[:, None]
            elif n == "kv":
                p["kvw4"] = b
            else:
                p[n] = b.reshape(-1, b.shape[-1])

    def reduce(self, names, arrays, payloads=None):
        payloads = payloads or [BF16] * len(names)
        big = [a if a.ndim == 3 else a.reshape(N_CHIPS, a.shape[0] // N_CHIPS, a.shape[1]) for a in arrays]
        got = exchange_halves("rs_siblings_" + names[0], big, mine=False)
        part = [add_own_half("rs_add_" + n, g, r, self.core, dt) for n, g, r, dt in zip(names, big, got, payloads)]
        self.part.update(zip(names, part))
        return scatter_rider(part)

    def reduced(self, names, slots, chip_order=()):
        for n, s in zip(names, slots):
            self.fulls[n] = sum_into_half("rs_sum_" + n, self.part[n], s, self.chip_core, n in chip_order)

    def finish(self):
        names = list(self.fulls)
        return dict(zip(names, share_halves("rs_share", [self.fulls[n] for n in names])))


WEIGHTS = ('norm_mix', 'norm_ffn', 'norm_kv', 'norm_final', 'ssm_w_in', 'ssm_a_re', 'ssm_a_im', 'ssm_log_dt',
           'ssm_b_re', 'ssm_b_im', 'ssm_c_re', 'ssm_c_im', 'ssm_d', 'ssm_w_glu', 'kv_w', 'attn_w_q', 'attn_w_o',
           'ffn_w_up', 'ffn_conv_w', 'ffn_conv_b', 'ffn_w_down')
SMALL = ('norm_mix', 'norm_ffn', 'norm_kv', 'norm_final', 'ssm_a_re', 'ssm_a_im', 'ssm_log_dt', 'ssm_b_re', 'ssm_b_im',
         'ssm_c_re', 'ssm_c_im', 'ssm_d', 'ffn_conv_w', 'ffn_conv_b')


def _pad_rows(flat, unit):
    n = flat.shape[0]
    total = -(-n // unit) * unit
    return jnp.pad(flat, (0, total - n)).reshape(total // LANES, LANES)


def kernel(x, norm_mix, norm_ffn, norm_kv, norm_final, ssm_w_in, ssm_a_re, ssm_a_im, ssm_log_dt, ssm_b_re, ssm_b_im, ssm_c_re, ssm_c_im, ssm_d, ssm_w_glu, kv_w, attn_w_q, attn_w_o, ffn_w_up, ffn_conv_w, ffn_conv_b, ffn_w_down, loss_target, m_norm_mix, m_norm_ffn, m_norm_kv, m_norm_final, m_ssm_w_in, m_ssm_a_re, m_ssm_a_im, m_ssm_log_dt, m_ssm_b_re, m_ssm_b_im, m_ssm_c_re, m_ssm_c_im, m_ssm_d, m_ssm_w_glu, m_kv_w, m_attn_w_q, m_attn_w_o, m_ffn_w_up, m_ffn_conv_w, m_ffn_conv_b, m_ffn_w_down, v_norm_mix, v_norm_ffn, v_norm_kv, v_norm_final, v_ssm_w_in, v_ssm_a_re, v_ssm_a_im, v_ssm_log_dt, v_ssm_b_re, v_ssm_b_im, v_ssm_c_re, v_ssm_c_im, v_ssm_d, v_ssm_w_glu, v_kv_w, v_attn_w_q, v_attn_w_o, v_ffn_w_up, v_ffn_conv_w, v_ffn_conv_b, v_ffn_w_down):
    a = dict(locals())
    l, d = x.shape[1], x.shape[2]
    f = ffn_conv_b.shape[1]
    fs = f // N_CHIPS
    m = 2 * lax.axis_index("x") + lax.axis_index("y")
    core = lax.axis_index("c").astype(jnp.int32).reshape(1)
    chip = m.astype(jnp.int32).reshape(1)
    chip_core = jnp.concatenate([chip, core])

    shards = {"win": ssm_w_in[0], "wglu": ssm_w_glu[0], "kv": kv_w, "wq": attn_w_q[0], "wo": attn_w_o[0],
              "wup0": ffn_w_up[0], "wup1": ffn_w_up[1], "wdown0": ffn_w_down[0], "wdown1": ffn_w_down[1],
              "convw": _pad_rows(ffn_conv_w.reshape(-1), 16 * LANES)}
    bufs = {k: cast_into_slot(f"cast_{k}", s, chip, F32 if k == "convw" else BF16) for k, s in shards.items()}
    cm = StepComm(bufs, core, chip_core)
    g_in, g_glu, g_cw = gather_weights("gather_first", [bufs["win"], bufs["wglu"], bufs["convw"]])
    conv_w = g_cw.reshape(N_CHIPS, -1)[:, :2 * 3 * fs].reshape(N_CHIPS, 2, 3, fs).transpose(1, 2, 0, 3).reshape(2, 3, f)
    p = dict(
        norm_mix=norm_mix, norm_ffn=norm_ffn, norm_kv=norm_kv, norm_final=norm_final,
        a_re=ssm_a_re[0], a_im=ssm_a_im[0], log_dt=ssm_log_dt[0], b_re=ssm_b_re[0], b_im=ssm_b_im[0],
        c_re=ssm_c_re[0], c_im=ssm_c_im[0], d=ssm_d[0],
        win=g_in.reshape(-1, g_in.shape[-1]), wglu4=g_glu, wup=[None, None], wdown=[None, None],
        conv_w=conv_w, conv_b=ffn_conv_b)

    loss_slab, dx, gr = _local_step(x[0], loss_target[0], p, cm)

    small = {"norm_mix": gr["norm_mix"], "norm_ffn": gr["norm_ffn"], "norm_kv": gr["norm_kv"], "norm_final": gr["norm_final"],
             "ssm_a_re": gr["a_re"], "ssm_a_im": gr["a_im"], "ssm_log_dt": gr["log_dt"], "ssm_b_re": gr["b_re"],
             "ssm_b_im": gr["b_im"], "ssm_c_re": gr["c_re"], "ssm_c_im": gr["c_im"], "ssm_d": gr["d"],
             "ffn_conv_w": gr["conv_w"], "ffn_conv_b": gr["conv_b"]}
    packed = _pad_rows(jnp.concatenate([small[k].reshape(-1) for k in SMALL] + [loss_slab[0, 0:1]]), 16 * LANES)
    last = cm.reduce(["win", "small"], [gr["win"], jnp.broadcast_to(packed, (N_CHIPS,) + packed.shape)], [BF16, F32])
    cm.reduced(["win", "small"], scatter_to_chips("rs_chips_last", last.ins), chip_order=("small",))
    r = cm.finish()
    grads = {"ssm_w_in": r["win"][None], "ssm_w_glu": r["wglu"][None], "kv_w": r["kv"], "attn_w_q": r["wq"][None],
             "attn_w_o": r["wo"][None], "ffn_w_up": jnp.stack([r["wup0"], r["wup1"]]),
             "ffn_w_down": jnp.stack([r["wdown0"], r["wdown1"]])}
    total = r["small"].reshape(-1)
    off = 0
    for k in SMALL:
        n = small[k].size
        full = total[off:off + n].reshape(small[k].shape)
        off += n
        if k == "ffn_conv_w":
            full = lax.dynamic_slice_in_dim(full, m * fs, fs, axis=2)
        grads[k] = full.reshape(a[k].shape)
    loss = total[off]

    outs = {}
    for k in WEIGHTS:
        outs[k] = adamw(f"adamw_{k}", a[k], grads[k], a["m_" + k], a["v_" + k])
    return (loss, dx[None], *[grads[k] for k in WEIGHTS], *[outs[k][0] for k in WEIGHTS],
            *[outs[k][1] for k in WEIGHTS], *[outs[k][2] for k in WEIGHTS])
```

```python
import functools
import math

import jax
import jax.numpy as jnp
from jax import lax
from jax.experimental import pallas as pl
from jax.experimental.pallas import tpu as pltpu

F32 = jnp.float32
BF16 = jnp.bfloat16

EPS = 1e-6
SSM_GROUP = 16
STATE = 64
HEAD_DIM = 64
N_STREAMS = 8
MXU_DIM = 256
VMEM_LIMIT = 56 * 1024 * 1024

ADAM_LR = 0.001
ADAM_B1 = 0.9
ADAM_B2 = 0.999
ADAM_EPS = 1e-08
ADAM_WD = 0.01
ADAM_STEP = 10


def _cparams(sem=None):
    return pltpu.CompilerParams(dimension_semantics=sem, vmem_limit_bytes=VMEM_LIMIT)


class Rider:
    def __init__(self, ins, outs, alias, sems, start, finish):
        self.ins, self.outs, self.alias, self.sems, self.start, self.finish = ins, outs, alias, sems, start, finish


def _call(name, body, *, grid, in_specs, out_specs, out_shape, args, scratch=(), sem=None, rider=None):
    if rider is None:
        return pl.pallas_call(body, name=name, grid=grid, in_specs=in_specs, out_specs=out_specs, out_shape=out_shape,
                              scratch_shapes=list(scratch), compiler_params=_cparams(sem))(*args)
    nin, nout, nscr = len(in_specs), len(out_specs), len(scratch)
    nri, nro = len(rider.ins), len(rider.outs)
    steps = math.prod(grid)

    def hosted(*refs):
        ins, refs = refs[:nin], refs[nin:]
        r_in, refs = refs[:nri], refs[nri:]
        outs, refs = refs[:nout], refs[nout:]
        r_out, refs = refs[:nro], refs[nro:]
        scr, sems = refs[:nscr], refs[nscr:]
        lin = 0
        for ax, size in enumerate(grid):
            lin = lin * size + pl.program_id(ax)

        @pl.when(lin == 0)
        def _():
            rider.start(r_in, r_out, sems)

        body(*ins, *outs, *scr)

        @pl.when(lin == steps - 1)
        def _():
            rider.finish(r_in, r_out, sems)

    any_spec = pl.BlockSpec(memory_space=pl.ANY)
    dma = pltpu.SemaphoreType.DMA
    return pl.pallas_call(
        hosted, name=name, grid=grid, in_specs=list(in_specs) + [any_spec] * nri,
        out_specs=list(out_specs) + [any_spec] * nro, out_shape=list(out_shape) + list(rider.outs),
        input_output_aliases={nin + i: nout + o for i, o in rider.alias.items()},
        scratch_shapes=list(scratch) + [dma(s) for s in rider.sems],
        compiler_params=pltpu.CompilerParams(dimension_semantics=("arbitrary",) * len(grid),
                                             vmem_limit_bytes=VMEM_LIMIT, has_side_effects=True),
    )(*args, *rider.ins)


def _blk(n, want):
    b = min(n, want)
    while n % b:
        b -= 1
    return b


NN = (((1,), (0,)), ((), ()))
NT = (((1,), (1,)), ((), ()))
TN = (((0,), (0,)), ((), ()))


def _matmul(name, a, b, *, a_spec, b_spec, o_spec, out_shape, grid, dn, nk, out_dtype, rider=None):
    nax = len(grid)

    def body(a_ref, b_ref, o_ref, *scr):
        part = lax.dot_general(a_ref[...], b_ref[...], dn, preferred_element_type=F32)
        if nk == 1:
            o_ref[...] = part.astype(o_ref.dtype)
            return
        acc = scr[0] if scr else o_ref
        k = pl.program_id(nax - 1)

        @pl.when(k == 0)
        def _():
            acc[...] = part

        @pl.when(k > 0)
        def _():
            acc[...] += part

        if scr:
            @pl.when(k == nk - 1)
            def _():
                o_ref[...] = acc[...].astype(o_ref.dtype)

    scratch = []
    if nk > 1 and out_dtype != F32:
        blk = tuple(d for d in o_spec.block_shape if d is not None)
        scratch = [pltpu.VMEM(blk, F32)]
    sem = ("parallel",) * (nax - 1) + ("arbitrary",)
    res = _call(name, body, grid=grid, in_specs=[a_spec, b_spec], out_specs=[o_spec],
                out_shape=[jax.ShapeDtypeStruct(out_shape, out_dtype)], scratch=scratch, sem=sem, args=(a, b), rider=rider)
    return res[0] if rider is None else (res[0], res[1:])


MM_TM = 1024
MM_TK = 2048


def mm_nn(name, a, w, out_dtype, tm=MM_TM):
    m, k = a.shape
    n = w.shape[1]
    tm = _blk(m, tm)
    return _matmul(name, a, w, a_spec=pl.BlockSpec((tm, k), lambda i, kk: (i, 0)),
                   b_spec=pl.BlockSpec((k, n), lambda i, kk: (0, 0)),
                   o_spec=pl.BlockSpec((tm, n), lambda i, kk: (i, 0)),
                   out_shape=(m, n), grid=(m // tm, 1), dn=NN, nk=1, out_dtype=out_dtype)


def mm_nt(name, a, w, out_dtype, tm=MM_TM, tn=None):
    m, n = a.shape
    k = w.shape[0]
    tm = _blk(m, tm)
    tn = k if tn is None else tn
    return _matmul(name, a, w, a_spec=pl.BlockSpec((tm, n), lambda i, j, kk: (i, 0)),
                   b_spec=pl.BlockSpec((tn, n), lambda i, j, kk: (j, 0)),
                   o_spec=pl.BlockSpec((tm, tn), lambda i, j, kk: (i, j)),
                   out_shape=(m, k), grid=(m // tm, k // tn, 1), dn=NT, nk=1, out_dtype=out_dtype)


def mm_tn(name, a, b, tmo=None, tk=MM_TK):
    l, m = a.shape
    n = b.shape[1]
    tk = _blk(l, tk)
    tmo = m if tmo is None else tmo
    nk = l // tk
    return _matmul(name, a, b, a_spec=pl.BlockSpec((tk, tmo), lambda i, kk: (kk, i)),
                   b_spec=pl.BlockSpec((tk, n), lambda i, kk: (kk, 0)),
                   o_spec=pl.BlockSpec((tmo, n), lambda i, kk: (i, 0)),
                   out_shape=(m, n), grid=(m // tmo, nk), dn=TN, nk=nk, out_dtype=F32)


def mm_nn_colshard(name, a, w4, out_dtype, tm=MM_TM):
    m, k = a.shape
    s, _, ns = w4.shape
    tm = _blk(m, tm)
    return _matmul(name, a, w4, a_spec=pl.BlockSpec((tm, k), lambda i, j, kk: (i, 0)),
                   b_spec=pl.BlockSpec((None, k, ns), lambda i, j, kk: (j, 0, 0)),
                   o_spec=pl.BlockSpec((tm, ns), lambda i, j, kk: (i, j)),
                   out_shape=(m, s * ns), grid=(m // tm, s, 1), dn=NN, nk=1, out_dtype=out_dtype)


def mm_nt_colshard(name, a, w4, out_dtype, tm=MM_TM):
    m, _ = a.shape
    s, k, ns = w4.shape
    tm = _blk(m, tm)
    return _matmul(name, a, w4, a_spec=pl.BlockSpec((tm, ns), lambda i, kk: (i, kk)),
                   b_spec=pl.BlockSpec((None, k, ns), lambda i, kk: (kk, 0, 0)),
                   o_spec=pl.BlockSpec((tm, k), lambda i, kk: (i, 0)),
                   out_shape=(m, k), grid=(m // tm, s), dn=NT, nk=s, out_dtype=out_dtype)


def mm_tn_colshard(name, a, b, s, tk=MM_TK):
    l, k = a.shape
    ns = b.shape[1] // s
    tk = _blk(l, tk)
    nk = l // tk
    return _matmul(name, a, b, a_spec=pl.BlockSpec((tk, k), lambda j, kk: (kk, 0)),
                   b_spec=pl.BlockSpec((tk, ns), lambda j, kk: (kk, j)),
                   o_spec=pl.BlockSpec((None, k, ns), lambda j, kk: (j, 0, 0)),
                   out_shape=(s, k, ns), grid=(s, nk), dn=TN, nk=nk, out_dtype=F32)


def mm_up(name, h, wup4, layer, tm=MM_TM, rider=None):
    m, k = h.shape
    s, _, _, ns = wup4.shape
    half = s // 2
    tm = _blk(m, tm)
    return _matmul(name, h, wup4, a_spec=pl.BlockSpec((tm, k), lambda i, j, kk: (i, 0)),
                   b_spec=pl.BlockSpec((None, None, k, ns), lambda i, j, kk: (j, layer, 0, 0)),
                   o_spec=pl.BlockSpec((None, tm, ns), lambda i, j, kk: (j // half, i, j % half)),
                   out_shape=(2, m, half * ns), grid=(m // tm, s, 1), dn=NN, nk=1, out_dtype=BF16, rider=rider)


def mm_up_nt(name, dgu, wup4, layer, tm=MM_TM, rider=None):
    _, m, _ = dgu.shape
    s, _, k, ns = wup4.shape
    half = s // 2
    tm = _blk(m, tm)
    return _matmul(name, dgu, wup4,
                   a_spec=pl.BlockSpec((None, tm, ns), lambda i, kk: (kk // half, i, kk % half)),
                   b_spec=pl.BlockSpec((None, None, k, ns), lambda i, kk: (kk, layer, 0, 0)),
                   o_spec=pl.BlockSpec((tm, k), lambda i, kk: (i, 0)),
                   out_shape=(m, k), grid=(m // tm, s), dn=NT, nk=s, out_dtype=F32, rider=rider)


def mm_up_tn(name, h, dgu, s, tk=MM_TK):
    l, k = h.shape
    half = s // 2
    ns = dgu.shape[2] // half
    tk = _blk(l, tk)
    nk = l // tk
    return _matmul(name, h, dgu, a_spec=pl.BlockSpec((tk, k), lambda j, kk: (kk, 0)),
                   b_spec=pl.BlockSpec((None, tk, ns), lambda j, kk: (j // half, kk, j % half)),
                   o_spec=pl.BlockSpec((None, k, ns), lambda j, kk: (j, 0, 0)),
                   out_shape=(s, k, ns), grid=(s, nk), dn=TN, nk=nk, out_dtype=F32)


def add_rmsnorm(name, x, r, gains, tm=512):
    l, d = x.shape
    tm = _blk(l, tm)
    ng = len(gains)
    has_r = r is not None

    def body(*refs):
        x_ref = refs[0]
        pos = 1
        xs = x_ref[...]
        if has_r:
            xs = xs + refs[pos][...]
            pos += 1
        g_refs = refs[pos:pos + ng]
        outs = refs[pos + ng:]
        o = 0
        if has_r:
            outs[0][...] = xs
            o = 1
        xh = xs * lax.rsqrt(jnp.mean(xs * xs, axis=-1, keepdims=True) + EPS)
        for gi in range(ng):
            outs[o + gi][...] = (xh * g_refs[gi][...]).astype(BF16)

    row = pl.BlockSpec((tm, d), lambda i: (i, 0))
    gsp = pl.BlockSpec((1, d), lambda i: (0, 0))
    ins = [x] + ([r] if has_r else []) + [g.reshape(1, d) for g in gains]
    in_specs = [row] * (1 + has_r) + [gsp] * ng
    out_shape = ([jax.ShapeDtypeStruct((l, d), F32)] if has_r else []) + [jax.ShapeDtypeStruct((l, d), BF16)] * ng
    return pl.pallas_call(body, name=name, grid=(l // tm,), in_specs=in_specs,
                          out_specs=[row] * len(out_shape), out_shape=out_shape,
                          compiler_params=_cparams(("parallel",)))(*ins)


def norm_bwd(name, x, dres, pairs, tm=512, bf16_copy=False, rider=None):
    l, d = x.shape
    tm = _blk(l, tm)
    npair = len(pairs)
    has_r = dres is not None

    def body(*refs):
        x_ref = refs[0]
        pos = 1
        xs = x_ref[...]
        dx = jnp.zeros_like(xs)
        if has_r:
            dx = refs[pos][...]
            pos += 1
        ins = refs[pos:pos + 2 * npair]
        outs = refs[pos + 2 * npair:]
        rs = lax.rsqrt(jnp.mean(xs * xs, axis=-1, keepdims=True) + EPS)
        xh = xs * rs
        first = pl.program_id(0) == 0
        for pi in range(npair):
            g = ins[2 * pi][...]
            dh = ins[2 * pi + 1][...].astype(F32)
            dgp = jnp.sum(dh * xh, axis=0, keepdims=True)
            dg_ref = outs[1 + pi]

            @pl.when(first)
            def _():
                dg_ref[...] = dgp

            @pl.when(jnp.logical_not(first))
            def _():
                dg_ref[...] += dgp

            dxh = dh * g
            dx = dx + rs * (dxh - xh * jnp.mean(dxh * xh, axis=-1, keepdims=True))
        outs[0][...] = dx
        if bf16_copy:
            outs[1 + npair][...] = dx.astype(BF16)

    row = pl.BlockSpec((tm, d), lambda i: (i, 0))
    gsp = pl.BlockSpec((1, d), lambda i: (0, 0))
    ins = [x] + ([dres] if has_r else [])
    in_specs = [row] * (1 + has_r)
    for g, dh in pairs:
        ins += [g.reshape(1, d), dh]
        in_specs += [gsp, row]
    out_shape = [jax.ShapeDtypeStruct((l, d), F32)] + [jax.ShapeDtypeStruct((1, d), F32)] * npair
    out_specs = [row] + [gsp] * npair
    if bf16_copy:
        out_shape.append(jax.ShapeDtypeStruct((l, d), BF16))
        out_specs.append(row)
    return _call(name, body, grid=(l // tm,), in_specs=in_specs, out_specs=out_specs, out_shape=out_shape,
                 sem=("arbitrary",), args=ins, rider=rider)


def final_loss(name, x, r, tgt, g, tm=512):
    l, d = x.shape
    tm = _blk(l, tm)

    def body(x_ref, r_ref, t_ref, g_ref, loss_ref, dx_ref, dg_ref, dxb_ref):
        xs = x_ref[...] + r_ref[...]
        gg = g_ref[...]
        rs = lax.rsqrt(jnp.mean(xs * xs, axis=-1, keepdims=True) + EPS)
        xh = xs * rs
        e = xh * gg - t_ref[...]
        part = 0.5 * jnp.sum(jnp.mean(e * e, axis=-1, keepdims=True), axis=0, keepdims=True)
        dy = e * (1.0 / d)
        dgp = jnp.sum(dy * xh, axis=0, keepdims=True)
        dxh = dy * gg
        dx = rs * (dxh - xh * jnp.mean(dxh * xh, axis=-1, keepdims=True))
        dx_ref[...] = dx
        dxb_ref[...] = dx.astype(BF16)
        first = pl.program_id(0) == 0

        @pl.when(first)
        def _():
            loss_ref[...] = jnp.broadcast_to(part, loss_ref.shape)
            dg_ref[...] = dgp

        @pl.when(jnp.logical_not(first))
        def _():
            loss_ref[...] += jnp.broadcast_to(part, loss_ref.shape)
            dg_ref[...] += dgp

    row = pl.BlockSpec((tm, d), lambda i: (i, 0))
    gsp = pl.BlockSpec((1, d), lambda i: (0, 0))
    lsp = pl.BlockSpec((8, 128), lambda i: (0, 0))
    return pl.pallas_call(
        body, name=name, grid=(l // tm,), in_specs=[row, row, row, gsp], out_specs=[lsp, row, gsp, row],
        out_shape=[jax.ShapeDtypeStruct((8, 128), F32), jax.ShapeDtypeStruct((l, d), F32),
                   jax.ShapeDtypeStruct((1, d), F32), jax.ShapeDtypeStruct((l, d), BF16)],
        compiler_params=_cparams(("arbitrary",)))(x, r, tgt, g.reshape(1, d))


def _sigmoid(x):
    return 0.5 * jnp.tanh(0.5 * x) + 0.5


def glu_fwd(name, z, tm=512):
    l, d2 = z.shape
    d = d2 // 2
    tm = _blk(l, tm)

    def body(z_ref, o_ref):
        o_ref[...] = z_ref[:, :d] * _sigmoid(z_ref[:, d:])

    return pl.pallas_call(body, name=name, grid=(l // tm,),
                          in_specs=[pl.BlockSpec((tm, d2), lambda i: (i, 0))],
                          out_specs=pl.BlockSpec((tm, d), lambda i: (i, 0)),
                          out_shape=jax.ShapeDtypeStruct((l, d), F32),
                          compiler_params=_cparams(("parallel",)))(z)


def glu_bwd(name, z, dm, tm=512):
    l, d2 = z.shape
    d = d2 // 2
    tm = _blk(l, tm)

    def body(z_ref, dm_ref, o_ref):
        sg = _sigmoid(z_ref[:, d:])
        g = dm_ref[...]
        o_ref[:, :d] = (g * sg).astype(BF16)
        o_ref[:, d:] = (g * z_ref[:, :d] * sg * (1.0 - sg)).astype(BF16)

    return pl.pallas_call(body, name=name, grid=(l // tm,),
                          in_specs=[pl.BlockSpec((tm, d2), lambda i: (i, 0)), pl.BlockSpec((tm, d), lambda i: (i, 0))],
                          out_specs=pl.BlockSpec((tm, d2), lambda i: (i, 0)),
                          out_shape=jax.ShapeDtypeStruct((l, d2), BF16),
                          compiler_params=_cparams(("parallel",)))(z, dm)


_GELU_C = math.sqrt(2.0 / math.pi)


def _gelu(y):
    return 0.5 * y * (1.0 + jnp.tanh(_GELU_C * (y + 0.044715 * y * y * y)))


def _gelu_grad(y):
    t = jnp.tanh(_GELU_C * (y + 0.044715 * y * y * y))
    return 0.5 * (1.0 + t) + 0.5 * y * (1.0 - t * t) * _GELU_C * (1.0 + 3.0 * 0.044715 * y * y)


def gelu_bwd(name, ypre, dyg, tm=512):
    l, d = ypre.shape
    tm = _blk(l, tm)

    def body(y_ref, d_ref, o_ref):
        o_ref[...] = (d_ref[...] * _gelu_grad(y_ref[...])).astype(BF16)

    row = pl.BlockSpec((tm, d), lambda i: (i, 0))
    return pl.pallas_call(body, name=name, grid=(l // tm,), in_specs=[row, row], out_specs=row,
                          out_shape=jax.ShapeDtypeStruct((l, d), BF16),
                          compiler_params=_cparams(("parallel",)))(ypre, dyg)


FFN_ROWS = 128
HALO = 16


def _taps_back(tail, g):
    ext = jnp.concatenate([tail, g], axis=0)
    return pltpu.roll(ext, 1, axis=0)[HALO:], pltpu.roll(ext, 2, axis=0)[HALO:]


def ffn_mid_fwd(name, gu, cw, cb, tc=128):
    _, l, f = gu.shape
    tc = _blk(f, tc)

    def body(gu_ref, w_ref, b_ref, a_ref):
        g = gu_ref[0].astype(F32)
        u = gu_ref[1].astype(F32)
        g1, g2 = _taps_back(jnp.zeros((HALO, tc), F32), g)
        gc = w_ref[0:1, :] * g2 + w_ref[1:2, :] * g1 + w_ref[2:3, :] * g + b_ref[...]
        a_ref[...] = (gc * _sigmoid(gc) * u).astype(BF16)

    return pl.pallas_call(
        body, name=name, grid=(f // tc,),
        in_specs=[pl.BlockSpec((2, l, tc), lambda c: (0, 0, c)), pl.BlockSpec((3, tc), lambda c: (0, c)),
                  pl.BlockSpec((1, tc), lambda c: (0, c))],
        out_specs=pl.BlockSpec((l, tc), lambda c: (0, c)),
        out_shape=jax.ShapeDtypeStruct((l, f), BF16),
        compiler_params=_cparams(("parallel",)))(gu, cw, cb)


def ffn_mid_bwd(name, gu, da, cw, cb, tc=128):
    _, l, f = gu.shape
    tc = _blk(f, tc)
    rc = _blk(l, FFN_ROWS)
    nchunk = l // rc

    def body(gu_ref, da_ref, w_ref, b_ref, dgu_ref, dw_ref, db_ref):
        w0, w1, w2, b = w_ref[0:1, :], w_ref[1:2, :], w_ref[2:3, :], b_ref[...]

        def fold(x):
            return jnp.sum(x.reshape(rc // 8, 8, tc), axis=0)

        def chunk(k, carry):
            head, s0, s1, s2, sb = carry
            ci = nchunk - 1 - k
            r0 = pl.multiple_of(ci * rc, rc)
            t0 = pl.multiple_of(jnp.maximum(r0 - HALO, 0), HALO)
            tail = jnp.where(ci > 0, gu_ref[0, pl.ds(t0, HALO), :].astype(F32), 0.0)
            g = gu_ref[0, pl.ds(r0, rc), :].astype(F32)
            u = gu_ref[1, pl.ds(r0, rc), :].astype(F32)
            da = da_ref[pl.ds(r0, rc), :].astype(F32)
            g1, g2 = _taps_back(tail, g)
            gc = w0 * g2 + w1 * g1 + w2 * g + b
            sg = _sigmoid(gc)
            dgu_ref[1, pl.ds(r0, rc), :] = (da * gc * sg).astype(BF16)
            dgc = da * u * (sg * (1.0 + gc * (1.0 - sg)))
            ext = jnp.concatenate([dgc, head], axis=0)
            up1 = pltpu.roll(ext, rc + HALO - 1, axis=0)[:rc]
            up2 = pltpu.roll(ext, rc + HALO - 2, axis=0)[:rc]
            dgu_ref[0, pl.ds(r0, rc), :] = (w2 * dgc + w1 * up1 + w0 * up2).astype(BF16)
            return dgc[:HALO], s0 + fold(dgc * g2), s1 + fold(dgc * g1), s2 + fold(dgc * g), sb + fold(dgc)

        z8 = jnp.zeros((8, tc), F32)
        _, s0, s1, s2, sb = lax.fori_loop(0, nchunk, chunk, (jnp.zeros((HALO, tc), F32), z8, z8, z8, z8))
        dw_ref[0:1, :] = jnp.sum(s0, axis=0, keepdims=True)
        dw_ref[1:2, :] = jnp.sum(s1, axis=0, keepdims=True)
        dw_ref[2:3, :] = jnp.sum(s2, axis=0, keepdims=True)
        db_ref[...] = jnp.sum(sb, axis=0, keepdims=True)

    return pl.pallas_call(
        body, name=name, grid=(f // tc,),
        in_specs=[pl.BlockSpec((2, l, tc), lambda c: (0, 0, c)), pl.BlockSpec((l, tc), lambda c: (0, c)),
                  pl.BlockSpec((3, tc), lambda c: (0, c)), pl.BlockSpec((1, tc), lambda c: (0, c))],
        out_specs=[pl.BlockSpec((2, l, tc), lambda c: (0, 0, c)), pl.BlockSpec((3, tc), lambda c: (0, c)),
                   pl.BlockSpec((1, tc), lambda c: (0, c))],
        out_shape=[jax.ShapeDtypeStruct((2, l, f), BF16), jax.ShapeDtypeStruct((3, f), F32),
                   jax.ShapeDtypeStruct((1, f), F32)],
        compiler_params=_cparams(("parallel",)))(gu, da, cw, cb)


SCAN_LANES = 512


def ssm_discretize(name, a_re, a_im, ldt, b_re, b_im):
    dt = jnp.exp(ldt)
    mag = jnp.exp(a_re * dt)
    ab_re = mag * jnp.cos(a_im * dt)
    ab_im = mag * jnp.sin(a_im * dt)
    den = a_re * a_re + a_im * a_im
    f_re = ((ab_re - 1.0) * a_re + ab_im * a_im) / den
    f_im = (ab_im * a_re - (ab_re - 1.0) * a_im) / den
    return ab_re, ab_im, f_re * b_re - f_im * b_im, f_re * b_im + f_im * b_re


def ssm_disc_fwd(name, a_re, a_im, ldt, b_re, b_im):
    def body(ar, ai, ld, br, bi, o_ar, o_ai, o_br, o_bi):
        r = ssm_discretize(None, ar[...], ai[...], ld[...], br[...], bi[...])
        o_ar[...], o_ai[...], o_br[...], o_bi[...] = r

    sd = jax.ShapeDtypeStruct
    return pl.pallas_call(body, name=name,
                          out_shape=[sd(a_re.shape, F32), sd(a_re.shape, F32), sd(b_re.shape, F32), sd(b_re.shape, F32)],
                          compiler_params=_cparams())(a_re, a_im, ldt, b_re, b_im)


def ssm_disc_bwd(name, a_re, a_im, ldt, b_re, b_im, d_ar, d_ai, d_br, d_bi):
    def body(ar, ai, ld, br, bi, g_ar, g_ai, g_br, g_bi, o_ar, o_ai, o_ld, o_br, o_bi):
        fn = functools.partial(ssm_discretize, None)
        _, vjp = jax.vjp(fn, ar[...], ai[...], ld[...], br[...], bi[...])
        r = vjp((g_ar[...], g_ai[...], g_br[...], g_bi[...]))
        o_ar[...], o_ai[...], o_ld[...], o_br[...], o_bi[...] = r

    sd = jax.ShapeDtypeStruct
    return pl.pallas_call(body, name=name,
                          out_shape=[sd(a_re.shape, F32)] * 3 + [sd(b_re.shape, F32)] * 2,
                          compiler_params=_cparams())(a_re, a_im, ldt, b_re, b_im, d_ar, d_ai, d_br, d_bi)


def _drive(x_ref, wre_ref, wim_ref, dre_ref, dim_ref):
    xb = x_ref[...]
    dre_ref[...] = jnp.dot(xb, wre_ref[...], preferred_element_type=F32)
    dim_ref[...] = jnp.dot(xb, wim_ref[...], preferred_element_type=F32)


def _ssm_dims(l, wre, tj):
    nb, kb, nsb = wre.shape
    jn = l // N_STREAMS
    tj = _blk(jn, tj)
    return nb, kb, nsb, jn, tj, tj * N_STREAMS, jn // tj


def _scan(dre_ref, dim_ref, st_re, st_im, a_ref, tj, reverse, write):
    ns = dre_ref.shape[1]
    cw = min(SCAN_LANES, ns)
    for cb in range(ns // cw):
        sl = slice(cb * cw, (cb + 1) * cw)
        ar = jnp.broadcast_to(a_ref[0:1, sl], (N_STREAMS, cw))
        ai = jnp.broadcast_to(a_ref[1:2, sl], (N_STREAMS, cw))

        def step(jj, carry, sl=sl, ar=ar, ai=ai):
            sr, si = carry
            j = (tj - 1 - jj) if reverse else jj
            off = pl.multiple_of(j * N_STREAMS, N_STREAMS)
            nr = ar * sr - ai * si + dre_ref[pl.ds(off, N_STREAMS), sl]
            ni = ar * si + ai * sr + dim_ref[pl.ds(off, N_STREAMS), sl]
            if write:
                dre_ref[pl.ds(off, N_STREAMS), sl] = nr
                dim_ref[pl.ds(off, N_STREAMS), sl] = ni
            return nr, ni

        sr, si = lax.fori_loop(0, tj, step, (st_re[:, sl], st_im[:, sl]))
        st_re[:, sl] = sr
        st_im[:, sl] = si


def ssm_pass1(name, x, wre, wim, a2, *, reverse, tj=64, rider=None):
    l, d = x.shape
    nb, kb, nsb, jn, tj, r, nblk = _ssm_dims(l, wre, tj)
    ns = nb * nsb

    def body(x_ref, wre_ref, wim_ref, a_ref, cre_ref, cim_ref, dre, dim, st_re, st_im):
        i = pl.program_id(1)

        @pl.when(i == 0)
        def _():
            st_re[...] = jnp.zeros_like(st_re)
            st_im[...] = jnp.zeros_like(st_im)

        _drive(x_ref, wre_ref, wim_ref, dre, dim)
        _scan(dre, dim, st_re, st_im, a_ref, tj, reverse, False)

        @pl.when(i == nblk - 1)
        def _():
            pr, pi = a_ref[0:1, :], a_ref[1:2, :]
            rr, ri = jnp.ones_like(pr), jnp.zeros_like(pr)
            e = jn
            while e:
                if e & 1:
                    rr, ri = rr * pr - ri * pi, rr * pi + ri * pr
                pr, pi = pr * pr - pi * pi, 2.0 * pr * pi
                e >>= 1
            order = range(N_STREAMS - 1, -1, -1) if reverse else range(N_STREAMS)
            cr = jnp.zeros_like(rr)
            ci = jnp.zeros_like(rr)
            for s in order:
                cre_ref[s:s + 1, :] = cr
                cim_ref[s:s + 1, :] = ci
                fr, fi = st_re[s:s + 1, :], st_im[s:s + 1, :]
                cr, ci = fr + rr * cr - ri * ci, fi + rr * ci + ri * cr

    blk = (lambda b, i: (nblk - 1 - i, b)) if reverse else (lambda b, i: (i, b))
    w3 = pl.BlockSpec((None, kb, nsb), lambda b, i: (b, 0, 0))
    st = pl.BlockSpec((N_STREAMS, nsb), lambda b, i: (0, b))
    return _call(
        name, body, grid=(nb, nblk),
        in_specs=[pl.BlockSpec((r, kb), blk), w3, w3, pl.BlockSpec((2, nsb), lambda b, i: (0, b))],
        out_specs=[st, st], out_shape=[jax.ShapeDtypeStruct((N_STREAMS, ns), F32)] * 2,
        scratch=[pltpu.VMEM((r, nsb), F32), pltpu.VMEM((r, nsb), F32),
                 pltpu.VMEM((N_STREAMS, nsb), F32), pltpu.VMEM((N_STREAMS, nsb), F32)],
        sem=("parallel", "arbitrary"), args=(x, wre, wim, a2), rider=rider)


def ssm_fwd2(name, u, bre, bim, a2, init_re, init_im, cre, cim, dskip, *, tj=64, rider=None):
    l, d = u.shape
    nb, kb, nsb, jn, tj, r, nblk = _ssm_dims(l, bre, tj)
    ns = nb * nsb

    def body(u_ref, bre_ref, bim_ref, a_ref, ire_ref, iim_ref, cre_ref, cim_ref, d_ref,
             sre_ref, sim_ref, y_ref, yg_ref, st_re, st_im):
        @pl.when(pl.program_id(1) == 0)
        def _():
            st_re[...] = ire_ref[...]
            st_im[...] = iim_ref[...]

        _drive(u_ref, bre_ref, bim_ref, sre_ref, sim_ref)
        _scan(sre_ref, sim_ref, st_re, st_im, a_ref, tj, False, True)
        y = (jnp.dot(sre_ref[...].astype(BF16), cre_ref[...], preferred_element_type=F32)
             - jnp.dot(sim_ref[...].astype(BF16), cim_ref[...], preferred_element_type=F32)
             + d_ref[...] * u_ref[...].astype(F32))
        y_ref[...] = y
        yg_ref[...] = _gelu(y).astype(BF16)

    rows = lambda w: pl.BlockSpec((r, w), lambda b, i: (i, b))
    w3 = pl.BlockSpec((None, kb, nsb), lambda b, i: (b, 0, 0))
    c3 = pl.BlockSpec((None, nsb, kb), lambda b, i: (b, 0, 0))
    st = pl.BlockSpec((N_STREAMS, nsb), lambda b, i: (0, b))
    return _call(
        name, body, grid=(nb, nblk),
        in_specs=[rows(kb), w3, w3, pl.BlockSpec((2, nsb), lambda b, i: (0, b)), st, st, c3, c3,
                  pl.BlockSpec((1, kb), lambda b, i: (0, b))],
        out_specs=[rows(nsb), rows(nsb), rows(kb), rows(kb)],
        out_shape=[jax.ShapeDtypeStruct((l, ns), F32), jax.ShapeDtypeStruct((l, ns), F32),
                   jax.ShapeDtypeStruct((l, d), F32), jax.ShapeDtypeStruct((l, d), BF16)],
        scratch=[pltpu.VMEM((N_STREAMS, nsb), F32), pltpu.VMEM((N_STREAMS, nsb), F32)],
        sem=("parallel", "arbitrary"), args=(u, bre, bim, a2, init_re, init_im, cre, cim, dskip), rider=rider)


def ssm_bwd2(name, dy, u, sre, sim, ctre, ctim, a2c, init_re, init_im, fre, fim, bre, bim, dskip, *, tj=64, rider=None):
    l, d = u.shape
    nb, kb, nsb, jn, tj, r, nblk = _ssm_dims(l, bre, tj)
    ns = nb * nsb

    def body(dy_ref, u_ref, sre_ref, sim_ref, pre_ref, pim_ref, ctre_ref, ctim_ref, a_ref, ire_ref, iim_ref,
             fre_ref, fim_ref, bre_ref, bim_ref, d_ref,
             du_ref, dbre_ref, dbim_ref, dcre_ref, dcim_ref, dare_ref, daim_ref, dd_ref,
             lre, lim, st_re, st_im):
        i = pl.program_id(1)
        first = i == 0

        @pl.when(first)
        def _():
            st_re[...] = ire_ref[...]
            st_im[...] = iim_ref[...]
            dbre_ref[...] = jnp.zeros_like(dbre_ref)
            dbim_ref[...] = jnp.zeros_like(dbim_ref)
            dcre_ref[...] = jnp.zeros_like(dcre_ref)
            dcim_ref[...] = jnp.zeros_like(dcim_ref)
            dare_ref[...] = jnp.zeros_like(dare_ref)
            daim_ref[...] = jnp.zeros_like(daim_ref)
            dd_ref[...] = jnp.zeros_like(dd_ref)

        _drive(dy_ref, ctre_ref, ctim_ref, lre, lim)
        _scan(lre, lim, st_re, st_im, a_ref, tj, True, True)

        is_t0 = i == nblk - 1
        cw = min(SCAN_LANES, nsb)
        for cb in range(nsb // cw):
            sl = slice(cb * cw, (cb + 1) * cw)
            p_r = jnp.where(is_t0, fre_ref[:, sl], pre_ref[:, sl])
            p_i = jnp.where(is_t0, fim_ref[:, sl], pim_ref[:, sl])
            l_r, l_i = lre[0:N_STREAMS, sl], lim[0:N_STREAMS, sl]
            acc = (l_r * p_r + l_i * p_i, l_i * p_r - l_r * p_i)

            def step(jj, carry, sl=sl):
                a_r, a_i = carry
                off = pl.multiple_of(jj * N_STREAMS, N_STREAMS)
                prev = pl.multiple_of((jj - 1) * N_STREAMS, N_STREAMS)
                l_r, l_i = lre[pl.ds(off, N_STREAMS), sl], lim[pl.ds(off, N_STREAMS), sl]
                p_r, p_i = sre_ref[pl.ds(prev, N_STREAMS), sl], sim_ref[pl.ds(prev, N_STREAMS), sl]
                return a_r + l_r * p_r + l_i * p_i, a_i + l_i * p_r - l_r * p_i

            a_r, a_i = lax.fori_loop(1, tj, step, acc)
            dare_ref[:, sl] += a_r
            daim_ref[:, sl] += a_i

        dyf = dy_ref[...].astype(F32)
        uf = u_ref[...].astype(F32)
        dd_ref[...] += jnp.sum(dyf * uf, axis=0, keepdims=True)
        lrb = lre[...].astype(BF16)
        lib = lim[...].astype(BF16)
        ub = u_ref[...]
        dyb = dy_ref[...]
        dbre_ref[...] += lax.dot_general(ub, lrb, TN, preferred_element_type=F32)
        dbim_ref[...] += lax.dot_general(ub, lib, TN, preferred_element_type=F32)
        dcre_ref[...] += lax.dot_general(sre_ref[...].astype(BF16), dyb, TN, preferred_element_type=F32)
        dcim_ref[...] -= lax.dot_general(sim_ref[...].astype(BF16), dyb, TN, preferred_element_type=F32)
        du = (lax.dot_general(lrb, bre_ref[...], NT, preferred_element_type=F32)
              + lax.dot_general(lib, bim_ref[...], NT, preferred_element_type=F32)
              + d_ref[...] * dyf)
        du_ref[...] = du.astype(BF16)

    rev = lambda w: pl.BlockSpec((r, w), lambda b, i: (nblk - 1 - i, b))
    prev_tile = pl.BlockSpec((N_STREAMS, nsb), lambda b, i: (jnp.maximum((nblk - 1 - i) * tj - 1, 0), b))
    st = pl.BlockSpec((N_STREAMS, nsb), lambda b, i: (0, b))
    w3 = pl.BlockSpec((None, kb, nsb), lambda b, i: (b, 0, 0))
    c3 = pl.BlockSpec((None, nsb, kb), lambda b, i: (b, 0, 0))
    dsp = pl.BlockSpec((1, kb), lambda b, i: (0, b))
    return _call(
        name, body, grid=(nb, nblk),
        in_specs=[rev(kb), rev(kb), rev(nsb), rev(nsb), prev_tile, prev_tile, w3, w3,
                  pl.BlockSpec((2, nsb), lambda b, i: (0, b)), st, st, st, st, w3, w3, dsp],
        out_specs=[rev(kb), w3, w3, c3, c3, st, st, dsp],
        out_shape=[jax.ShapeDtypeStruct((l, d), BF16), jax.ShapeDtypeStruct((nb, kb, nsb), F32),
                   jax.ShapeDtypeStruct((nb, kb, nsb), F32), jax.ShapeDtypeStruct((nb, nsb, kb), F32),
                   jax.ShapeDtypeStruct((nb, nsb, kb), F32), jax.ShapeDtypeStruct((N_STREAMS, ns), F32),
                   jax.ShapeDtypeStruct((N_STREAMS, ns), F32), jax.ShapeDtypeStruct((1, d), F32)],
        scratch=[pltpu.VMEM((r, nsb), F32), pltpu.VMEM((r, nsb), F32),
                 pltpu.VMEM((N_STREAMS, nsb), F32), pltpu.VMEM((N_STREAMS, nsb), F32)],
        sem=("parallel", "arbitrary"),
        args=(dy, u, sre, sim, sre, sim, ctre, ctim, a2c, init_re, init_im, fre, fim, bre, bim, dskip), rider=rider)


ATT_TILE = 256
ATT_HEADS = 4
EXP_ZERO_BELOW = -104.0
LANES = 128


def _split_dot(x, tri):
    hi = x.astype(BF16)
    lo = (x - hi.astype(F32)).astype(BF16)
    return jnp.dot(hi, tri, preferred_element_type=F32) + jnp.dot(lo, tri, preferred_element_type=F32)


def _sb_logs(z, causal):
    sp = jnp.maximum(z, 0.0) + jnp.log(1.0 + jnp.exp(-jnp.abs(z)))
    l1m = -sp
    if causal is not None:
        l1m = jnp.where(causal, l1m, 0.0)
    return z - sp, l1m


def attn_fwd(name, q, kv, t=ATT_TILE, rider=None):
    l, dm = q.shape
    dh = HEAD_DIM
    h = dm // dh
    t = _blk(l, t)
    hb = _blk(h, ATT_HEADS)
    wb = hb * dh
    scale = dh ** -0.5

    def body(q_ref, k_ref, v_ref, o_ref, tot_ref):
        i = pl.program_id(1)
        hd = lambda g: slice(g * dh, (g + 1) * dh)
        qs = [(q_ref[:, hd(g)].astype(F32) * scale).astype(BF16) for g in range(hb)]
        row = lax.broadcasted_iota(jnp.int32, (t, t), 0)
        col = lax.broadcasted_iota(jnp.int32, (t, t), 1)
        tri_gt = jnp.where(row > col, 1.0, 0.0).astype(BF16)
        causal = col < row

        def tile(jb, carry, mask):
            off = pl.multiple_of(jb * t, t)
            heads = range(hb)
            z = [lax.dot_general(qs[g], k_ref[pl.ds(off, t), hd(g)], NT, preferred_element_type=F32) for g in heads]
            logs = [_sb_logs(z[g], mask) for g in heads]
            rem = [_split_dot(logs[g][1], tri_gt) for g in heads]
            w = [jnp.exp(logs[g][0] + rem[g] + carry[g][1]) for g in heads]
            if mask is not None:
                w = [jnp.where(mask, w[g], 0.0) for g in heads]
            pv = [jnp.dot(w[g].astype(BF16), v_ref[pl.ds(off, t), hd(g)], preferred_element_type=F32) for g in heads]
            return tuple((carry[g][0] + pv[g], carry[g][1] + rem[g][:, 0:1] + logs[g][1][:, 0:1]) for g in heads)

        def live(carry):
            top = carry[0][1]
            for g in range(1, hb):
                top = jnp.maximum(top, carry[g][1])
            return (jnp.max(top) > EXP_ZERO_BELOW).astype(jnp.int32)

        def more(c):
            it, alive, _ = c
            return jnp.logical_and(it < i, alive > 0)

        def step(c):
            it, _, carry = c
            carry = tile(i - 1 - it, carry, None)
            return it + 1, live(carry), carry

        carry = tile(i, ((jnp.zeros((t, dh), F32), jnp.zeros((t, 1), F32)),) * hb, causal)
        done, _, carry = lax.while_loop(more, step, (jnp.int32(0), live(carry), carry))
        tot_ref[...] = jnp.zeros_like(tot_ref)
        for g, (acc, run) in enumerate(carry):
            o_ref[:, hd(g)] = acc.astype(BF16)
            tot_ref[:, g:g + 1] = run
        tot_ref[:, hb:hb + 1] = jnp.full((t, 1), done, jnp.int32).astype(F32)

    qsp = pl.BlockSpec((t, wb), lambda hh, i: (i, hh))
    ksp = pl.BlockSpec((None, l, wb), lambda hh, i: (0, 0, hh))
    vsp = pl.BlockSpec((None, l, wb), lambda hh, i: (1, 0, hh))
    tsp = pl.BlockSpec((None, t, LANES), lambda hh, i: (hh, i, 0))
    return _call(name, body, grid=(h // hb, l // t), in_specs=[qsp, ksp, vsp], out_specs=[qsp, tsp],
                 out_shape=[jax.ShapeDtypeStruct((l, dm), BF16), jax.ShapeDtypeStruct((h // hb, l, LANES), F32)],
                 sem=("parallel", "parallel"), args=(q, kv, kv), rider=rider)


def attn_bwd(name, q, kv, tot, do, t=ATT_TILE, rider=None):
    l, dm = q.shape
    dh = HEAD_DIM
    h = dm // dh
    t = _blk(l, t)
    hb = _blk(h, ATT_HEADS)
    wb = hb * dh
    nq = l // t
    scale = dh ** -0.5

    def body(q_ref, k_ref, v_ref, tot_ref, do_ref, dq_ref, dkv_ref, dk_ref, dv_ref):
        i = pl.program_id(1)
        hd = lambda g: slice(g * dh, (g + 1) * dh)

        @pl.when(i == 0)
        def _():
            dk_ref[...] = jnp.zeros_like(dk_ref)
            dv_ref[...] = jnp.zeros_like(dv_ref)

        qs = [(q_ref[:, hd(g)].astype(F32) * scale).astype(BF16) for g in range(hb)]
        dob = [do_ref[:, hd(g)] for g in range(hb)]
        total = [tot_ref[:, g:g + 1] for g in range(hb)]
        row = lax.broadcasted_iota(jnp.int32, (t, t), 0)
        col = lax.broadcasted_iota(jnp.int32, (t, t), 1)
        tri_le = jnp.where(row <= col, 1.0, 0.0).astype(BF16)
        tri_lt = jnp.where(row < col, 1.0, 0.0).astype(BF16)
        causal = col < row

        def tile(jb, carry, mask):
            off = pl.multiple_of(jb * t, t)
            heads = range(hb)
            kj = [k_ref[pl.ds(off, t), hd(g)] for g in heads]
            z = [lax.dot_general(qs[g], kj[g], NT, preferred_element_type=F32) for g in heads]
            dp = [lax.dot_general(dob[g], v_ref[pl.ds(off, t), hd(g)], NT, preferred_element_type=F32) for g in heads]
            logs = [_sb_logs(z[g], mask) for g in heads]
            lpre = [_split_dot(logs[g][1], tri_le) for g in heads]
            w = [jnp.exp(logs[g][0] + (total[g] - carry[g][1] - lpre[g])) for g in heads]
            if mask is not None:
                w = [jnp.where(mask, w[g], 0.0) for g in heads]
            p = [w[g] * dp[g] for g in heads]
            for g in heads:
                dv_ref[pl.ds(off, t), hd(g)] += lax.dot_general(w[g].astype(BF16), dob[g], TN, preferred_element_type=F32)
            qpre = [carry[g][2] + jnp.dot(p[g].astype(BF16), tri_lt, preferred_element_type=F32) for g in heads]
            dz = [p[g] - jnp.exp(logs[g][0]) * (p[g] + qpre[g]) for g in heads]
            if mask is not None:
                dz = [jnp.where(mask, dz[g], 0.0) for g in heads]
            dzb = [dz[g].astype(BF16) for g in heads]
            for g in heads:
                dk_ref[pl.ds(off, t), hd(g)] += lax.dot_general(dzb[g], qs[g], TN, preferred_element_type=F32)
            dq = [carry[g][0] + jnp.dot(dzb[g], kj[g], preferred_element_type=F32) for g in heads]
            return tuple((dq[g], carry[g][1] + lpre[g][:, t - 1:t], qpre[g][:, t - 1:t] + p[g][:, t - 1:t]) for g in heads)

        zero = jnp.zeros((t, 1), F32)
        done = jnp.max(tot_ref[:, hb:hb + 1]).astype(jnp.int32)
        carry = lax.fori_loop(i - done, i, lambda jb, c: tile(jb, c, None), ((jnp.zeros((t, dh), F32), zero, zero),) * hb)
        carry = tile(i, carry, causal)
        for g in range(hb):
            dq_ref[:, hd(g)] = (carry[g][0] * scale).astype(BF16)

        @pl.when(i == nq - 1)
        def _():
            dkv_ref[0] = dk_ref[...].astype(BF16)
            dkv_ref[1] = dv_ref[...].astype(BF16)

    qsp = pl.BlockSpec((t, wb), lambda hh, i: (i, hh))
    ksp = pl.BlockSpec((None, l, wb), lambda hh, i: (0, 0, hh))
    vsp = pl.BlockSpec((None, l, wb), lambda hh, i: (1, 0, hh))
    tsp = pl.BlockSpec((None, t, LANES), lambda hh, i: (hh, i, 0))
    return _call(name, body, grid=(h // hb, nq), in_specs=[qsp, ksp, vsp, tsp, qsp],
                 out_specs=[qsp, pl.BlockSpec((2, l, wb), lambda hh, i: (0, 0, hh))],
                 out_shape=[jax.ShapeDtypeStruct((l, dm), BF16), jax.ShapeDtypeStruct((2, l, dm), BF16)],
                 scratch=[pltpu.VMEM((l, wb), F32), pltpu.VMEM((l, wb), F32)],
                 sem=("parallel", "arbitrary"), args=(q, kv, kv, tot, do), rider=rider)


def adamw(name, w, g, m, v):
    shape = w.shape
    cols = shape[-1]
    rows = w.size // cols
    tr = rows
    if rows % 8 == 0:
        tr = 8 * _blk(rows // 8, 64)
    c1 = 1.0 - ADAM_B1 ** ADAM_STEP
    c2 = 1.0 - ADAM_B2 ** ADAM_STEP

    def body(w_ref, g_ref, m_ref, v_ref, d_ref, nm_ref, nv_ref):
        gg = g_ref[...]
        mm = ADAM_B1 * m_ref[...] + (1.0 - ADAM_B1) * gg
        vv = ADAM_B2 * v_ref[...] + (1.0 - ADAM_B2) * (gg * gg)
        nm_ref[...] = mm
        nv_ref[...] = vv
        d_ref[...] = -ADAM_LR * ((mm / c1) / (jnp.sqrt(vv / c2) + ADAM_EPS) + ADAM_WD * w_ref[...])

    sp = pl.BlockSpec((tr, cols), lambda i: (i, 0))
    sd = jax.ShapeDtypeStruct((rows, cols), F32)
    outs = pl.pallas_call(body, name=name, grid=(rows // tr,), in_specs=[sp] * 4, out_specs=[sp] * 3,
                          out_shape=[sd] * 3, compiler_params=_cparams(("parallel",)))(
        w.reshape(rows, cols), g.reshape(rows, cols), m.reshape(rows, cols), v.reshape(rows, cols))
    return tuple(o.reshape(shape) for o in outs)


def _perm(a):
    l, d = a.shape
    return a.reshape(N_STREAMS, l // N_STREAMS, d).transpose(1, 0, 2).reshape(l, d)


def _unperm(a):
    l, d = a.shape
    return a.reshape(l // N_STREAMS, N_STREAMS, d).transpose(1, 0, 2).reshape(l, d)


def _ssm_layouts(d):
    g = d // SSM_GROUP
    gb = MXU_DIM // SSM_GROUP if d >= MXU_DIM else g
    return g, gb, g // gb


def _b_blocks(bb, d):
    g, gb, nb = _ssm_layouts(d)
    b4 = bb.reshape(SSM_GROUP, nb, gb, STATE)
    eye = jnp.eye(gb, dtype=bb.dtype)
    return jnp.einsum('hbqp,gq->bghqp', b4, eye).reshape(nb, gb * SSM_GROUP, gb * STATE)


def _b_unblocks(db, d):
    g, gb, nb = _ssm_layouts(d)
    eye = jnp.eye(gb, dtype=db.dtype)
    return jnp.einsum('bghqp,gq->hbqp', db.reshape(nb, gb, SSM_GROUP, gb, STATE), eye).reshape(SSM_GROUP, g * STATE)


def _c_blocks(c, d):
    g, gb, nb = _ssm_layouts(d)
    eye = jnp.eye(gb, dtype=c.dtype)
    return jnp.einsum('bghp,gq->bqpgh', c.reshape(nb, gb, SSM_GROUP, STATE), eye).reshape(nb, gb * STATE, gb * SSM_GROUP)


def _c_unblocks(dc, d):
    g, gb, nb = _ssm_layouts(d)
    eye = jnp.eye(gb, dtype=dc.dtype)
    return jnp.einsum('bqpgh,gq->bghp', dc.reshape(nb, gb, STATE, gb, SSM_GROUP), eye).reshape(g, SSM_GROUP, STATE)


class NoComm:
    def __init__(self):
        self.local = {}

    def gather(self, names):
        return None

    def landed(self, names, bufs, p):
        pass

    def swap(self, names, arrays):
        self.local.update(zip(names, arrays))
        return None

    def swapped(self, names, got):
        pass

    def chips(self, names):
        return None

    def reduced(self, names, slots):
        pass


def _ffn_fwd(tag, x, r, gain, p, layer, cm, ride):
    xs, h = add_rmsnorm(f"norm_ffn{tag}", x, r, [gain])
    rider = cm.gather(ride) if ride else None
    gu = mm_up(f"ffn_up{tag}", h, p["wup"][layer], 0, rider=rider)
    if rider is not None:
        gu, landed = gu
        cm.landed(ride, landed, p)
    a = ffn_mid_fwd(f"ffn_mid{tag}", gu, p["conv_w"][layer], p["conv_b"][layer:layer + 1])
    f = mm_nn(f"ffn_down{tag}", a, p["wdown"][layer], F32)
    return xs, h, gu, a, f


def _ffn_bwd(tag, dxo, dfb, xs, h, gu, a, gain, p, layer, s, cm, ride):
    wdown = p["wdown"][layer]
    f = wdown.shape[0]
    da = mm_nt(f"ffn_down_dx{tag}", dfb, wdown, BF16, tn=f // 2)
    dwdown = mm_tn(f"ffn_down_dw{tag}", a, dfb, tmo=f // 2)
    dgu, dcw, dcb = ffn_mid_bwd(f"ffn_mid_bwd{tag}", gu, da, p["conv_w"][layer], p["conv_b"][layer:layer + 1])
    rider = cm.chips(ride) if ride else None
    dh = mm_up_nt(f"ffn_up_dx{tag}", dgu, p["wup"][layer], 0, rider=rider)
    if rider is not None:
        dh, slots = dh
        cm.reduced(ride, slots)
    dwup = mm_up_tn(f"ffn_up_dw{tag}", h, dgu, s)
    own = [f"wup{layer}", f"wdown{layer}"]
    res = norm_bwd(f"norm_ffn_bwd{tag}", xs, dxo, [(gain, dh)], bf16_copy=True, rider=cm.swap(own, [dwup, dwdown]))
    cm.swapped(own, res[3:])
    return res[0], res[2], res[1], dcw, dcb


def _local_step(x, tgt, p, cm=None):
    cm = NoComm() if cm is None else cm
    l, d = x.shape
    s = N_CHIPS
    gr = {}

    a_re, a_im = p["a_re"].reshape(1, -1), p["a_im"].reshape(1, -1)
    ldt = jnp.repeat(p["log_dt"].reshape(-1), STATE).reshape(1, -1)
    bk_re = p["b_re"].transpose(2, 0, 1).reshape(SSM_GROUP, -1)
    bk_im = p["b_im"].transpose(2, 0, 1).reshape(SSM_GROUP, -1)
    ab_re, ab_im, bb_re, bb_im = ssm_disc_fwd("ssm_disc", a_re, a_im, ldt, bk_re, bk_im)
    a2 = jnp.concatenate([ab_re, ab_im], axis=0)
    a2c = jnp.concatenate([ab_re, -ab_im], axis=0)
    bre, bim = _b_blocks(bb_re, d).astype(BF16), _b_blocks(bb_im, d).astype(BF16)
    cre, cim = _c_blocks(p["c_re"], d).astype(BF16), _c_blocks(p["c_im"], d).astype(BF16)
    ctre, ctim = cre.transpose(0, 2, 1), -cim.transpose(0, 2, 1)
    dskip = p["d"].reshape(1, d)

    xp = _perm(x)
    (h0p,) = add_rmsnorm("norm_mix0", xp, None, [p["norm_mix"][0]])
    u = mm_nn("ssm_in", h0p, p["win"], BF16)
    rider = cm.gather(["wglu", "wdown0"])
    res = ssm_pass1("ssm_fwd1", u, bre, bim, a2, reverse=False, rider=rider)
    f_re, f_im = res[0], res[1]
    cm.landed(["wglu", "wdown0"], res[2:], p)
    rider = cm.gather(["wup0"])
    res = ssm_fwd2("ssm_fwd2", u, bre, bim, a2, f_re, f_im, cre, cim, dskip, rider=rider)
    s_re, s_im, ypre, yg = res[:4]
    cm.landed(["wup0"], res[4:], p)
    z = mm_nn_colshard("ssm_glu", yg, p["wglu4"], F32)
    mix = _unperm(glu_fwd("glu", z))

    x1, h1, gu0, a0, f0 = _ffn_fwd("0", x, mix, p["norm_ffn"][0], p, 0, cm, ["kv", "wq", "wo"])
    x2, hk, h2 = add_rmsnorm("norm_kv_mix1", x1, f0, [p["norm_kv"], p["norm_mix"][1]])
    kvw = p["kvw4"][:, None]
    kv = mm_up("kv_proj", hk, kvw, 0)
    qf = mm_nn("q_proj", h2, p["wq"], BF16)
    rider = cm.gather(["wup1", "wdown1"])
    res = attn_fwd("attn", qf, kv, rider=rider)
    ob, tot = res[0], res[1]
    cm.landed(["wup1", "wdown1"], res[2:], p)
    ao = mm_nn("o_proj", ob, p["wo"], F32)
    x3, h3, gu1, a1, f1 = _ffn_fwd("1", x2, ao, p["norm_ffn"][1], p, 1, cm, None)
    loss, dx4, dg_final, dx4b = final_loss("final_loss", x3, f1, tgt, p["norm_final"])
    gr["norm_final"] = dg_final.reshape(d)

    dx3, dx3b, dg_ffn1, dcw1, dcb1 = _ffn_bwd("1", dx4, dx4b, x3, h3, gu1, a1, p["norm_ffn"][1], p, 1, s, cm, None)
    do2 = mm_nt("o_proj_dx", dx3b, p["wo"], BF16)
    dwo = mm_tn("o_proj_dw", ob, dx3b)
    rider = cm.chips(["wup1", "wdown1"])
    res = attn_bwd("attn_bwd", qf, kv, tot, do2, rider=rider)
    dqf, dkv = res[0], res[1]
    cm.reduced(["wup1", "wdown1"], res[2:])
    dh2 = mm_nt("q_proj_dx", dqf, p["wq"], F32)
    dwq = mm_tn("q_proj_dw", h2, dqf)
    dhk = mm_up_nt("kv_proj_dx", dkv, kvw, 0)
    dwkv = mm_up_tn("kv_proj_dw", hk, dkv, s)
    att = ["wo", "wq", "kv"]
    res = norm_bwd("norm_kv_mix1_bwd", x2, dx3, [(p["norm_mix"][1], dh2), (p["norm_kv"], dhk)], bf16_copy=True,
                   rider=cm.swap(att, [dwo, dwq, dwkv]))
    dx2, dg_mix1, dg_kv, dx2b = res[:4]
    cm.swapped(att, res[4:])
    gr["norm_kv"] = dg_kv.reshape(d)

    dx1, _, dg_ffn0, dcw0, dcb0 = _ffn_bwd("0", dx2, dx2b, x1, h1, gu0, a0, p["norm_ffn"][0], p, 0, s, cm, att)
    dx1p = _perm(dx1)
    dz = glu_bwd("glu_bwd", z, dx1p)
    dyg = mm_nt_colshard("ssm_glu_dx", dz, p["wglu4"], F32)
    dwglu = mm_tn_colshard("ssm_glu_dw", yg, dz, s)
    dy = gelu_bwd("gelu_bwd", ypre, dyg)
    res = ssm_pass1("ssm_bwd1", dy, ctre, ctim, a2c, reverse=True, rider=cm.swap(["wglu"], [dwglu]))
    i_re, i_im = res[0], res[1]
    cm.swapped(["wglu"], res[2:])
    rider = cm.chips(["wup0", "wdown0", "wglu"])
    res = ssm_bwd2("ssm_bwd2", dy, u, s_re, s_im, ctre, ctim, a2c, i_re, i_im, f_re, f_im, bre, bim, dskip, rider=rider)
    du, dbre, dbim, dcre, dcim, da_re, da_im, dd = res[:8]
    cm.reduced(["wup0", "wdown0", "wglu"], res[8:])
    dh0p = mm_nt("ssm_in_dx", du, p["win"], F32)
    dwin = mm_tn("ssm_in_dw", h0p, du)
    dxp, dg_mix0 = norm_bwd("norm_mix0_bwd", xp, dx1p, [(p["norm_mix"][0], dh0p)])
    dx = _unperm(dxp)

    g_are, g_aim, g_ldt, g_bre, g_bim = ssm_disc_bwd(
        "ssm_disc_bwd", a_re, a_im, ldt, bk_re, bk_im,
        jnp.sum(da_re, axis=0, keepdims=True), jnp.sum(da_im, axis=0, keepdims=True),
        _b_unblocks(dbre, d), _b_unblocks(dbim, d))
    g = d // SSM_GROUP
    gr["a_re"] = g_are.reshape(g, STATE)
    gr["a_im"] = g_aim.reshape(g, STATE)
    gr["log_dt"] = jnp.sum(g_ldt.reshape(g, STATE), axis=1)
    gr["b_re"] = g_bre.reshape(SSM_GROUP, g, STATE).transpose(1, 2, 0)
    gr["b_im"] = g_bim.reshape(SSM_GROUP, g, STATE).transpose(1, 2, 0)
    gr["c_re"] = _c_unblocks(dcre, d)
    gr["c_im"] = _c_unblocks(dcim, d)
    gr["d"] = dd.reshape(g, SSM_GROUP)
    gr["norm_mix"] = jnp.concatenate([dg_mix0, dg_mix1], axis=0)
    gr["norm_ffn"] = jnp.concatenate([dg_ffn0, dg_ffn1], axis=0)
    gr["conv_w"] = jnp.stack([dcw0, dcw1])
    gr["conv_b"] = jnp.concatenate([dcb0, dcb1], axis=0)
    gr["win"] = dwin
    gr.update(getattr(cm, "local", {}))
    return loss, dx, gr


MESH = pl.DeviceIdType.MESH
N_CHIPS = 4
N_DEV = 8
ANY = pl.BlockSpec(memory_space=pl.ANY)


def _pos():
    x, y, c = lax.axis_index("x"), lax.axis_index("y"), lax.axis_index("c")
    return x, y, c, 2 * x + y


def _other_chips(x, y):
    return [(1 - x, y), (x, 1 - y), (1 - x, 1 - y)]


def _remote(src, dst, send_sem, recv_sem, dev):
    return pltpu.make_async_remote_copy(src_ref=src, dst_ref=dst, send_sem=send_sem, recv_sem=recv_sem,
                                        device_id=dev, device_id_type=MESH)


def cast_into_slot(name, a, chip, dtype):
    r, cdim = a.shape
    tr = 16 * _blk(r // 16, 32) if r % 16 == 0 else r

    def body(m_ref, a_ref, o_ref):
        o_ref[...] = a_ref[...].astype(o_ref.dtype)

    gs = pltpu.PrefetchScalarGridSpec(
        num_scalar_prefetch=1, grid=(r // tr,),
        in_specs=[pl.BlockSpec((tr, cdim), lambda i, m_ref: (i, 0))],
        out_specs=pl.BlockSpec((None, tr, cdim), lambda i, m_ref: (m_ref[0], i, 0)))
    return pl.pallas_call(body, name=name, grid_spec=gs, out_shape=jax.ShapeDtypeStruct((N_CHIPS, r, cdim), dtype),
                          compiler_params=_cparams(("parallel",)))(chip, a)


def gather_weights(name, bufs):
    n = len(bufs)

    def body(*refs):
        _gather_start(refs[n:2 * n], refs[2 * n:])
        _gather_finish(refs[n:2 * n], refs[2 * n:])

    sem = pltpu.SemaphoreType.DMA
    return pl.pallas_call(
        body, name=name, in_specs=[ANY] * n, out_specs=[ANY] * n,
        out_shape=[jax.ShapeDtypeStruct(b.shape, b.dtype) for b in bufs],
        input_output_aliases={w: w for w in range(n)},
        scratch_shapes=[sem((n, 3)), sem((n, 3)), sem((n, 3)), sem((n, 3))],
        compiler_params=pltpu.CompilerParams(has_side_effects=True),
    )(*bufs)


def _half(ref, chip, core):
    hr = ref.shape[1] // 2
    return ref.at[chip, pl.ds(core * hr, hr), :]


def _gather_start(bufs, sems):
    send_a, recv_a = sems[0], sems[1]
    x, y, c, m = _pos()
    for w, buf in enumerate(bufs):
        for j, (px, py) in enumerate(_other_chips(x, y)):
            blk = _half(buf, m, c)
            _remote(blk, blk, send_a.at[w, j], recv_a.at[w, j], (px, py, c)).start()


def _gather_finish(bufs, sems):
    send_a, recv_a, send_b, recv_b = sems
    x, y, c, m = _pos()
    chips = _other_chips(x, y)
    sib = (x, y, 1 - c)
    for j, (px, py) in enumerate(chips):
        for w, buf in enumerate(bufs):
            blk = _half(buf, 2 * px + py, c)
            _remote(blk, blk, send_a.at[w, j], recv_a.at[w, j], (px, py, c)).wait_recv()
            _remote(blk, blk, send_b.at[w, j], recv_b.at[w, j], sib).start()
    for j, (px, py) in enumerate(chips):
        for w, buf in enumerate(bufs):
            blk = _half(buf, 2 * px + py, 1 - c)
            _remote(blk, blk, send_b.at[w, j], recv_b.at[w, j], sib).wait_recv()
    for j, (px, py) in enumerate(chips):
        for w, buf in enumerate(bufs):
            mine, landed = _half(buf, m, c), _half(buf, 2 * px + py, c)
            _remote(mine, mine, send_a.at[w, j], recv_a.at[w, j], (px, py, c)).wait_send()
            _remote(landed, landed, send_b.at[w, j], recv_b.at[w, j], sib).wait_send()


def gather_rider(bufs):
    n = len(bufs)
    return Rider(ins=list(bufs), outs=[jax.ShapeDtypeStruct(b.shape, b.dtype) for b in bufs],
                 alias={w: w for w in range(n)}, sems=[(n, 3)] * 4,
                 start=lambda i, o, s: _gather_start(o, s), finish=lambda i, o, s: _gather_finish(o, s))


def exchange_halves(name, arrs):
    n = len(arrs)

    def body(*refs):
        _exchange(refs[:n], refs[n:2 * n], refs[2 * n:], "start")
        _exchange(refs[:n], refs[n:2 * n], refs[2 * n:], "wait")

    sem = pltpu.SemaphoreType.DMA
    return pl.pallas_call(
        body, name=name, in_specs=[ANY] * n, out_specs=[ANY] * n, out_shape=_exchange_shapes(arrs),
        scratch_shapes=[sem((n,)), sem((n,))],
        compiler_params=pltpu.CompilerParams(has_side_effects=True),
    )(*arrs)


def _exchange_shapes(arrs):
    return [jax.ShapeDtypeStruct((a.shape[0], a.shape[1] // 2, a.shape[2]), a.dtype) for a in arrs]


def _exchange(ins, outs, sems, what):
    send, recv = sems
    x, y, c, _ = _pos()
    for w in range(len(ins)):
        hr = ins[w].shape[1] // 2
        cp = _remote(ins[w].at[:, pl.ds((1 - c) * hr, hr), :], outs[w], send.at[w], recv.at[w], (x, y, 1 - c))
        if what == "start":
            cp.start()
        else:
            cp.wait()


def exchange_rider(arrs):
    n = len(arrs)
    return Rider(ins=list(arrs), outs=_exchange_shapes(arrs), alias={}, sems=[(n,)] * 2,
                 start=lambda i, o, s: _exchange(i, o, s, "start"), finish=lambda i, o, s: _exchange(i, o, s, "wait"))


def scatter_to_chips(name, arrs):
    n = len(arrs)

    def body(*refs):
        _scatter(refs[:n], refs[n:2 * n], refs[2 * n:], "start")
        _scatter(refs[:n], refs[n:2 * n], refs[2 * n:], "wait")

    sem = pltpu.SemaphoreType.DMA
    return pl.pallas_call(
        body, name=name, in_specs=[ANY] * n, out_specs=[ANY] * n,
        out_shape=[jax.ShapeDtypeStruct((3,) + a.shape[1:], a.dtype) for a in arrs],
        scratch_shapes=[sem((n, 3)), sem((n, 3))],
        compiler_params=pltpu.CompilerParams(has_side_effects=True),
    )(*arrs)


def _scatter(ins, outs, sems, what):
    send, recv = sems
    x, y, c, _ = _pos()
    for w in range(len(ins)):
        for j, (px, py) in enumerate(_other_chips(x, y)):
            cp = _remote(ins[w].at[2 * px + py], outs[w].at[j], send.at[w, j], recv.at[w, j], (px, py, c))
            if what == "start":
                cp.start()
            else:
                cp.wait()


def scatter_rider(arrs):
    n = len(arrs)
    return Rider(ins=list(arrs), outs=[jax.ShapeDtypeStruct((3,) + a.shape[1:], a.dtype) for a in arrs],
                 alias={}, sems=[(n, 3)] * 2,
                 start=lambda i, o, s: _scatter(i, o, s, "start"), finish=lambda i, o, s: _scatter(i, o, s, "wait"))


def share_halves(name, fulls):
    n = len(fulls)

    def body(*refs):
        outs = refs[n:2 * n]
        send, recv = refs[2 * n:]
        x, y, c, _ = _pos()
        cps = []
        for w in range(n):
            hr = outs[w].shape[0] // 2
            blk = outs[w].at[pl.ds(c * hr, hr), :]
            cp = _remote(blk, blk, send.at[w], recv.at[w], (x, y, 1 - c))
            cp.start()
            cps.append(cp)
        for w, cp in enumerate(cps):
            hr = outs[w].shape[0] // 2
            cp.wait_send()
            blk = outs[w].at[pl.ds((1 - c) * hr, hr), :]
            _remote(blk, blk, send.at[w], recv.at[w], (x, y, 1 - c)).wait_recv()

    sem = pltpu.SemaphoreType.DMA
    return pl.pallas_call(
        body, name=name, in_specs=[ANY] * n, out_specs=[ANY] * n,
        out_shape=[jax.ShapeDtypeStruct(a.shape, a.dtype) for a in fulls],
        input_output_aliases={w: w for w in range(n)},
        scratch_shapes=[sem((n,)), sem((n,))],
        compiler_params=pltpu.CompilerParams(has_side_effects=True),
    )(*fulls)


def add_own_half(name, full, got, core, out_dtype):
    n, r, cdim = full.shape
    hr = r // 2
    tr = 8 * _blk(hr // 8, 32) if out_dtype == F32 else 16 * _blk(hr // 16, 16)
    nbh = hr // tr

    def body(c_ref, f_ref, g_ref, o_ref):
        o_ref[...] = (f_ref[...] + g_ref[...]).astype(o_ref.dtype)

    gs = pltpu.PrefetchScalarGridSpec(
        num_scalar_prefetch=1, grid=(n, nbh),
        in_specs=[pl.BlockSpec((None, tr, cdim), lambda s, i, c_ref: (s, c_ref[0] * nbh + i, 0)),
                  pl.BlockSpec((None, tr, cdim), lambda s, i, c_ref: (s, i, 0))],
        out_specs=pl.BlockSpec((None, tr, cdim), lambda s, i, c_ref: (s, i, 0)))
    return pl.pallas_call(body, name=name, grid_spec=gs, out_shape=jax.ShapeDtypeStruct((n, hr, cdim), out_dtype),
                          compiler_params=_cparams(("parallel", "parallel")))(core, full, got)


def sum_into_half(name, part, slots, chip_core, chip_order):
    _, hr, cdim = part.shape
    unit = 8 if part.dtype == F32 else 16
    tr = unit * _blk(hr // unit, 256 // unit)
    nbh = hr // tr

    def body(mc_ref, p_ref, s_ref, o_ref):
        terms = [p_ref[...].astype(F32)] + [s_ref[k].astype(F32) for k in range(3)]
        if chip_order:
            m = mc_ref[0]
            own, fx, fy, fxy = terms
            terms = [jnp.where((k ^ m) == 0, own, jnp.where((k ^ m) == 2, fx, jnp.where((k ^ m) == 1, fy, fxy)))
                     for k in range(N_CHIPS)]
        o_ref[...] = ((terms[0] + terms[1]) + terms[2]) + terms[3]

    gs = pltpu.PrefetchScalarGridSpec(
        num_scalar_prefetch=1, grid=(nbh,),
        in_specs=[pl.BlockSpec((None, tr, cdim), lambda i, mc: (mc[0], i, 0)),
                  pl.BlockSpec((3, tr, cdim), lambda i, mc: (0, i, 0))],
        out_specs=pl.BlockSpec((tr, cdim), lambda i, mc: (mc[1] * nbh + i, 0)))
    return pl.pallas_call(body, name=name, grid_spec=gs, out_shape=jax.ShapeDtypeStruct((2 * hr, cdim), F32),
                          compiler_params=_cparams(("parallel",)))(chip_core, part, slots)


class StepComm:
    def __init__(self, bufs, core, chip_core):
        self.bufs, self.core, self.chip_core = bufs, core, chip_core
        self.whole, self.part, self.fulls = {}, {}, {}

    def gather(self, names):
        return gather_rider([self.bufs[n] for n in names])

    def landed(self, names, bufs, p):
        for n, b in zip(names, bufs):
            if n.startswith("wdown"):
                p["wdown"][int(n[-1])] = b.reshape(-1, b.shape[-1])
            elif n.startswith("wup"):
                p["wup"][int(n[-1])] = b[:, None]
            elif n == "kv":
                p["kvw4"] = b
            elif n == "wglu":
                p["wglu4"] = b
            else:
                p[n] = b.reshape(-1, b.shape[-1])

    def swap(self, names, arrays):
        big = [a if a.ndim == 3 else a.reshape(N_CHIPS, a.shape[0] // N_CHIPS, a.shape[1]) for a in arrays]
        self.whole.update(zip(names, big))
        return exchange_rider(big)

    def swapped(self, names, got, payloads=None):
        payloads = payloads or [BF16] * len(names)
        for n, r, dt in zip(names, got, payloads):
            self.part[n] = add_own_half("rs_add_" + n, self.whole.pop(n), r, self.core, dt)

    def chips(self, names):
        return scatter_rider([self.part[n] for n in names])

    def reduce(self, names, arrays, payloads=None):
        rider = self.swap(names, arrays)
        self.swapped(names, exchange_halves("rs_siblings_" + names[0], rider.ins), payloads)
        return self.chips(names)

    def reduced(self, names, slots, chip_order=()):
        for n, s in zip(names, slots):
            self.fulls[n] = sum_into_half("rs_sum_" + n, self.part[n], s, self.chip_core, n in chip_order)

    def finish(self):
        names = list(self.fulls)
        return dict(zip(names, share_halves("rs_share", [self.fulls[n] for n in names])))


WEIGHTS = ('norm_mix', 'norm_ffn', 'norm_kv', 'norm_final', 'ssm_w_in', 'ssm_a_re', 'ssm_a_im', 'ssm_log_dt',
           'ssm_b_re', 'ssm_b_im', 'ssm_c_re', 'ssm_c_im', 'ssm_d', 'ssm_w_glu', 'kv_w', 'attn_w_q', 'attn_w_o',
           'ffn_w_up', 'ffn_conv_w', 'ffn_conv_b', 'ffn_w_down')
SMALL = ('norm_mix', 'norm_ffn', 'norm_kv', 'norm_final', 'ssm_a_re', 'ssm_a_im', 'ssm_log_dt', 'ssm_b_re', 'ssm_b_im',
         'ssm_c_re', 'ssm_c_im', 'ssm_d', 'ffn_conv_w', 'ffn_conv_b')


def _pad_rows(flat, unit):
    n = flat.shape[0]
    total = -(-n // unit) * unit
    return jnp.pad(flat, (0, total - n)).reshape(total // LANES, LANES)


def kernel(x, norm_mix, norm_ffn, norm_kv, norm_final, ssm_w_in, ssm_a_re, ssm_a_im, ssm_log_dt, ssm_b_re, ssm_b_im, ssm_c_re, ssm_c_im, ssm_d, ssm_w_glu, kv_w, attn_w_q, attn_w_o, ffn_w_up, ffn_conv_w, ffn_conv_b, ffn_w_down, loss_target, m_norm_mix, m_norm_ffn, m_norm_kv, m_norm_final, m_ssm_w_in, m_ssm_a_re, m_ssm_a_im, m_ssm_log_dt, m_ssm_b_re, m_ssm_b_im, m_ssm_c_re, m_ssm_c_im, m_ssm_d, m_ssm_w_glu, m_kv_w, m_attn_w_q, m_attn_w_o, m_ffn_w_up, m_ffn_conv_w, m_ffn_conv_b, m_ffn_w_down, v_norm_mix, v_norm_ffn, v_norm_kv, v_norm_final, v_ssm_w_in, v_ssm_a_re, v_ssm_a_im, v_ssm_log_dt, v_ssm_b_re, v_ssm_b_im, v_ssm_c_re, v_ssm_c_im, v_ssm_d, v_ssm_w_glu, v_kv_w, v_attn_w_q, v_attn_w_o, v_ffn_w_up, v_ffn_conv_w, v_ffn_conv_b, v_ffn_w_down):
    a = dict(locals())
    l, d = x.shape[1], x.shape[2]
    f = ffn_conv_b.shape[1]
    fs = f // N_CHIPS
    m = 2 * lax.axis_index("x") + lax.axis_index("y")
    core = lax.axis_index("c").astype(jnp.int32).reshape(1)
    chip = m.astype(jnp.int32).reshape(1)
    chip_core = jnp.concatenate([chip, core])

    shards = {"win": ssm_w_in[0], "wglu": ssm_w_glu[0], "kv": kv_w, "wq": attn_w_q[0], "wo": attn_w_o[0],
              "wup0": ffn_w_up[0], "wup1": ffn_w_up[1], "wdown0": ffn_w_down[0], "wdown1": ffn_w_down[1],
              "convw": _pad_rows(ffn_conv_w.reshape(-1), 16 * LANES)}
    bufs = {k: cast_into_slot(f"cast_{k}", s, chip, F32 if k == "convw" else BF16) for k, s in shards.items()}
    cm = StepComm(bufs, core, chip_core)
    g_in, g_cw = gather_weights("gather_first", [bufs["win"], bufs["convw"]])
    conv_w = g_cw.reshape(N_CHIPS, -1)[:, :2 * 3 * fs].reshape(N_CHIPS, 2, 3, fs).transpose(1, 2, 0, 3).reshape(2, 3, f)
    p = dict(
        norm_mix=norm_mix, norm_ffn=norm_ffn, norm_kv=norm_kv, norm_final=norm_final,
        a_re=ssm_a_re[0], a_im=ssm_a_im[0], log_dt=ssm_log_dt[0], b_re=ssm_b_re[0], b_im=ssm_b_im[0],
        c_re=ssm_c_re[0], c_im=ssm_c_im[0], d=ssm_d[0],
        win=g_in.reshape(-1, g_in.shape[-1]), wup=[None, None], wdown=[None, None],
        conv_w=conv_w, conv_b=ffn_conv_b)

    loss_slab, dx, gr = _local_step(x[0], loss_target[0], p, cm)

    small = {"norm_mix": gr["norm_mix"], "norm_ffn": gr["norm_ffn"], "norm_kv": gr["norm_kv"], "norm_final": gr["norm_final"],
             "ssm_a_re": gr["a_re"], "ssm_a_im": gr["a_im"], "ssm_log_dt": gr["log_dt"], "ssm_b_re": gr["b_re"],
             "ssm_b_im": gr["b_im"], "ssm_c_re": gr["c_re"], "ssm_c_im": gr["c_im"], "ssm_d": gr["d"],
             "ffn_conv_w": gr["conv_w"], "ffn_conv_b": gr["conv_b"]}
    packed = _pad_rows(jnp.concatenate([small[k].reshape(-1) for k in SMALL] + [loss_slab[0, 0:1]]), 16 * LANES)
    last = cm.reduce(["win", "small"], [gr["win"], jnp.broadcast_to(packed, (N_CHIPS,) + packed.shape)], [BF16, F32])
    cm.reduced(["win", "small"], scatter_to_chips("rs_chips_last", last.ins), chip_order=("small",))
    r = cm.finish()
    grads = {"ssm_w_in": r["win"][None], "ssm_w_glu": r["wglu"][None], "kv_w": r["kv"], "attn_w_q": r["wq"][None],
             "attn_w_o": r["wo"][None], "ffn_w_up": jnp.stack([r["wup0"], r["wup1"]]),
             "ffn_w_down": jnp.stack([r["wdown0"], r["wdown1"]])}
    total = r["small"].reshape(-1)
    off = 0
    for k in SMALL:
        n = small[k].size
        full = total[off:off + n].reshape(small[k].shape)
        off += n
        if k == "ffn_conv_w":
            full = lax.dynamic_slice_in_dim(full, m * fs, fs, axis=2)
        grads[k] = full.reshape(a[k].shape)
    loss = total[off]

    outs = {}
    for k in WEIGHTS:
        outs[k] = adamw(f"adamw_{k}", a[k], grads[k], a["m_" + k], a["v_" + k])
    return (loss, dx[None], *[grads[k] for k in WEIGHTS], *[outs[k][0] for k in WEIGHTS],
            *[outs[k][1] for k in WEIGHTS], *[outs[k][2] for k in WEIGHTS])
```

```python
import functools
import math

import jax
import jax.numpy as jnp
from jax import lax
from jax.experimental import pallas as pl
from jax.experimental.pallas import tpu as pltpu

F32 = jnp.float32
BF16 = jnp.bfloat16

EPS = 1e-6
SSM_GROUP = 16
STATE = 64
HEAD_DIM = 64
N_STREAMS = 8
MXU_DIM = 256
VMEM_LIMIT = 56 * 1024 * 1024

ADAM_LR = 0.001
ADAM_B1 = 0.9
ADAM_B2 = 0.999
ADAM_EPS = 1e-08
ADAM_WD = 0.01
ADAM_STEP = 10


def _cparams(sem=None):
    return pltpu.CompilerParams(dimension_semantics=sem, vmem_limit_bytes=VMEM_LIMIT)


class Rider:
    def __init__(self, ins, outs, alias, sems, start, finish):
        self.ins, self.outs, self.alias, self.sems, self.start, self.finish = ins, outs, alias, sems, start, finish


def _call(name, body, *, grid, in_specs, out_specs, out_shape, args, scratch=(), sem=None, rider=None):
    if rider is None:
        return pl.pallas_call(body, name=name, grid=grid, in_specs=in_specs, out_specs=out_specs, out_shape=out_shape,
                              scratch_shapes=list(scratch), compiler_params=_cparams(sem))(*args)
    nin, nout, nscr = len(in_specs), len(out_specs), len(scratch)
    nri, nro = len(rider.ins), len(rider.outs)
    steps = math.prod(grid)

    def hosted(*refs):
        ins, refs = refs[:nin], refs[nin:]
        r_in, refs = refs[:nri], refs[nri:]
        outs, refs = refs[:nout], refs[nout:]
        r_out, refs = refs[:nro], refs[nro:]
        scr, sems = refs[:nscr], refs[nscr:]
        lin = 0
        for ax, size in enumerate(grid):
            lin = lin * size + pl.program_id(ax)

        @pl.when(lin == 0)
        def _():
            rider.start(r_in, r_out, sems)

        body(*ins, *outs, *scr)

        @pl.when(lin == steps - 1)
        def _():
            rider.finish(r_in, r_out, sems)

    any_spec = pl.BlockSpec(memory_space=pl.ANY)
    dma = pltpu.SemaphoreType.DMA
    return pl.pallas_call(
        hosted, name=name, grid=grid, in_specs=list(in_specs) + [any_spec] * nri,
        out_specs=list(out_specs) + [any_spec] * nro, out_shape=list(out_shape) + list(rider.outs),
        input_output_aliases={nin + i: nout + o for i, o in rider.alias.items()},
        scratch_shapes=list(scratch) + [dma(s) for s in rider.sems],
        compiler_params=pltpu.CompilerParams(dimension_semantics=("arbitrary",) * len(grid),
                                             vmem_limit_bytes=VMEM_LIMIT, has_side_effects=True),
    )(*args, *rider.ins)


def _blk(n, want):
    b = min(n, want)
    while n % b:
        b -= 1
    return b


NN = (((1,), (0,)), ((), ()))
NT = (((1,), (1,)), ((), ()))
TN = (((0,), (0,)), ((), ()))


def _matmul(name, a, b, *, a_spec, b_spec, o_spec, out_shape, grid, dn, nk, out_dtype, rider=None):
    nax = len(grid)

    def body(a_ref, b_ref, o_ref, *scr):
        part = lax.dot_general(a_ref[...], b_ref[...], dn, preferred_element_type=F32)
        if nk == 1:
            o_ref[...] = part.astype(o_ref.dtype)
            return
        acc = scr[0] if scr else o_ref
        k = pl.program_id(nax - 1)

        @pl.when(k == 0)
        def _():
            acc[...] = part

        @pl.when(k > 0)
        def _():
            acc[...] += part

        if scr:
            @pl.when(k == nk - 1)
            def _():
                o_ref[...] = acc[...].astype(o_ref.dtype)

    scratch = []
    if nk > 1 and out_dtype != F32:
        blk = tuple(d for d in o_spec.block_shape if d is not None)
        scratch = [pltpu.VMEM(blk, F32)]
    sem = ("parallel",) * (nax - 1) + ("arbitrary",)
    res = _call(name, body, grid=grid, in_specs=[a_spec, b_spec], out_specs=[o_spec],
                out_shape=[jax.ShapeDtypeStruct(out_shape, out_dtype)], scratch=scratch, sem=sem, args=(a, b), rider=rider)
    return res[0] if rider is None else (res[0], res[1:])


MM_TM = 1024
MM_TK = 2048


def mm_nn(name, a, w, out_dtype, tm=MM_TM):
    m, k = a.shape
    n = w.shape[1]
    tm = _blk(m, tm)
    return _matmul(name, a, w, a_spec=pl.BlockSpec((tm, k), lambda i, kk: (i, 0)),
                   b_spec=pl.BlockSpec((k, n), lambda i, kk: (0, 0)),
                   o_spec=pl.BlockSpec((tm, n), lambda i, kk: (i, 0)),
                   out_shape=(m, n), grid=(m // tm, 1), dn=NN, nk=1, out_dtype=out_dtype)


def mm_nt(name, a, w, out_dtype, tm=MM_TM, tn=None):
    m, n = a.shape
    k = w.shape[0]
    tm = _blk(m, tm)
    tn = k if tn is None else tn
    return _matmul(name, a, w, a_spec=pl.BlockSpec((tm, n), lambda i, j, kk: (i, 0)),
                   b_spec=pl.BlockSpec((tn, n), lambda i, j, kk: (j, 0)),
                   o_spec=pl.BlockSpec((tm, tn), lambda i, j, kk: (i, j)),
                   out_shape=(m, k), grid=(m // tm, k // tn, 1), dn=NT, nk=1, out_dtype=out_dtype)


def mm_tn(name, a, b, tmo=None, tk=MM_TK):
    l, m = a.shape
    n = b.shape[1]
    tk = _blk(l, tk)
    tmo = m if tmo is None else tmo
    nk = l // tk
    return _matmul(name, a, b, a_spec=pl.BlockSpec((tk, tmo), lambda i, kk: (kk, i)),
                   b_spec=pl.BlockSpec((tk, n), lambda i, kk: (kk, 0)),
                   o_spec=pl.BlockSpec((tmo, n), lambda i, kk: (i, 0)),
                   out_shape=(m, n), grid=(m // tmo, nk), dn=TN, nk=nk, out_dtype=F32)


def mm_nn_colshard(name, a, w4, out_dtype, tm=MM_TM):
    m, k = a.shape
    s, _, ns = w4.shape
    tm = _blk(m, tm)
    return _matmul(name, a, w4, a_spec=pl.BlockSpec((tm, k), lambda i, j, kk: (i, 0)),
                   b_spec=pl.BlockSpec((None, k, ns), lambda i, j, kk: (j, 0, 0)),
                   o_spec=pl.BlockSpec((tm, ns), lambda i, j, kk: (i, j)),
                   out_shape=(m, s * ns), grid=(m // tm, s, 1), dn=NN, nk=1, out_dtype=out_dtype)


def mm_nt_colshard(name, a, w4, out_dtype, tm=MM_TM):
    m, _ = a.shape
    s, k, ns = w4.shape
    tm = _blk(m, tm)
    return _matmul(name, a, w4, a_spec=pl.BlockSpec((tm, ns), lambda i, kk: (i, kk)),
                   b_spec=pl.BlockSpec((None, k, ns), lambda i, kk: (kk, 0, 0)),
                   o_spec=pl.BlockSpec((tm, k), lambda i, kk: (i, 0)),
                   out_shape=(m, k), grid=(m // tm, s), dn=NT, nk=s, out_dtype=out_dtype)


def mm_tn_colshard(name, a, b, s, tk=MM_TK):
    l, k = a.shape
    ns = b.shape[1] // s
    tk = _blk(l, tk)
    nk = l // tk
    return _matmul(name, a, b, a_spec=pl.BlockSpec((tk, k), lambda j, kk: (kk, 0)),
                   b_spec=pl.BlockSpec((tk, ns), lambda j, kk: (kk, j)),
                   o_spec=pl.BlockSpec((None, k, ns), lambda j, kk: (j, 0, 0)),
                   out_shape=(s, k, ns), grid=(s, nk), dn=TN, nk=nk, out_dtype=F32)


def mm_up(name, h, wup4, layer, tm=MM_TM, rider=None):
    m, k = h.shape
    s, _, _, ns = wup4.shape
    half = s // 2
    tm = _blk(m, tm)
    return _matmul(name, h, wup4, a_spec=pl.BlockSpec((tm, k), lambda i, j, kk: (i, 0)),
                   b_spec=pl.BlockSpec((None, None, k, ns), lambda i, j, kk: (j, layer, 0, 0)),
                   o_spec=pl.BlockSpec((None, tm, ns), lambda i, j, kk: (j // half, i, j % half)),
                   out_shape=(2, m, half * ns), grid=(m // tm, s, 1), dn=NN, nk=1, out_dtype=BF16, rider=rider)


def mm_up_nt(name, dgu, wup4, layer, tm=MM_TM, rider=None):
    _, m, _ = dgu.shape
    s, _, k, ns = wup4.shape
    half = s // 2
    tm = _blk(m, tm)
    return _matmul(name, dgu, wup4,
                   a_spec=pl.BlockSpec((None, tm, ns), lambda i, kk: (kk // half, i, kk % half)),
                   b_spec=pl.BlockSpec((None, None, k, ns), lambda i, kk: (kk, layer, 0, 0)),
                   o_spec=pl.BlockSpec((tm, k), lambda i, kk: (i, 0)),
                   out_shape=(m, k), grid=(m // tm, s), dn=NT, nk=s, out_dtype=F32, rider=rider)


def mm_up_tn(name, h, dgu, s, tk=MM_TK):
    l, k = h.shape
    half = s // 2
    ns = dgu.shape[2] // half
    tk = _blk(l, tk)
    nk = l // tk
    return _matmul(name, h, dgu, a_spec=pl.BlockSpec((tk, k), lambda j, kk: (kk, 0)),
                   b_spec=pl.BlockSpec((None, tk, ns), lambda j, kk: (j // half, kk, j % half)),
                   o_spec=pl.BlockSpec((None, k, ns), lambda j, kk: (j, 0, 0)),
                   out_shape=(s, k, ns), grid=(s, nk), dn=TN, nk=nk, out_dtype=F32)


def add_rmsnorm(name, x, r, gains, tm=512):
    l, d = x.shape
    tm = _blk(l, tm)
    ng = len(gains)
    has_r = r is not None

    def body(*refs):
        x_ref = refs[0]
        pos = 1
        xs = x_ref[...]
        if has_r:
            xs = xs + refs[pos][...]
            pos += 1
        g_refs = refs[pos:pos + ng]
        outs = refs[pos + ng:]
        o = 0
        if has_r:
            outs[0][...] = xs
            o = 1
        xh = xs * lax.rsqrt(jnp.mean(xs * xs, axis=-1, keepdims=True) + EPS)
        for gi in range(ng):
            outs[o + gi][...] = (xh * g_refs[gi][...]).astype(BF16)

    row = pl.BlockSpec((tm, d), lambda i: (i, 0))
    gsp = pl.BlockSpec((1, d), lambda i: (0, 0))
    ins = [x] + ([r] if has_r else []) + [g.reshape(1, d) for g in gains]
    in_specs = [row] * (1 + has_r) + [gsp] * ng
    out_shape = ([jax.ShapeDtypeStruct((l, d), F32)] if has_r else []) + [jax.ShapeDtypeStruct((l, d), BF16)] * ng
    return pl.pallas_call(body, name=name, grid=(l // tm,), in_specs=in_specs,
                          out_specs=[row] * len(out_shape), out_shape=out_shape,
                          compiler_params=_cparams(("parallel",)))(*ins)


def norm_bwd(name, x, dres, pairs, tm=512, bf16_copy=False, rider=None):
    l, d = x.shape
    tm = _blk(l, tm)
    npair = len(pairs)
    has_r = dres is not None

    def body(*refs):
        x_ref = refs[0]
        pos = 1
        xs = x_ref[...]
        dx = jnp.zeros_like(xs)
        if has_r:
            dx = refs[pos][...]
            pos += 1
        ins = refs[pos:pos + 2 * npair]
        outs = refs[pos + 2 * npair:]
        rs = lax.rsqrt(jnp.mean(xs * xs, axis=-1, keepdims=True) + EPS)
        xh = xs * rs
        first = pl.program_id(0) == 0
        for pi in range(npair):
            g = ins[2 * pi][...]
            dh = ins[2 * pi + 1][...].astype(F32)
            dgp = jnp.sum(dh * xh, axis=0, keepdims=True)
            dg_ref = outs[1 + pi]

            @pl.when(first)
            def _():
                dg_ref[...] = dgp

            @pl.when(jnp.logical_not(first))
            def _():
                dg_ref[...] += dgp

            dxh = dh * g
            dx = dx + rs * (dxh - xh * jnp.mean(dxh * xh, axis=-1, keepdims=True))
        outs[0][...] = dx
        if bf16_copy:
            outs[1 + npair][...] = dx.astype(BF16)

    row = pl.BlockSpec((tm, d), lambda i: (i, 0))
    gsp = pl.BlockSpec((1, d), lambda i: (0, 0))
    ins = [x] + ([dres] if has_r else [])
    in_specs = [row] * (1 + has_r)
    for g, dh in pairs:
        ins += [g.reshape(1, d), dh]
        in_specs += [gsp, row]
    out_shape = [jax.ShapeDtypeStruct((l, d), F32)] + [jax.ShapeDtypeStruct((1, d), F32)] * npair
    out_specs = [row] + [gsp] * npair
    if bf16_copy:
        out_shape.append(jax.ShapeDtypeStruct((l, d), BF16))
        out_specs.append(row)
    return _call(name, body, grid=(l // tm,), in_specs=in_specs, out_specs=out_specs, out_shape=out_shape,
                 sem=("arbitrary",), args=ins, rider=rider)


def final_loss(name, x, r, tgt, g, tm=512):
    l, d = x.shape
    tm = _blk(l, tm)

    def body(x_ref, r_ref, t_ref, g_ref, loss_ref, dx_ref, dg_ref, dxb_ref):
        xs = x_ref[...] + r_ref[...]
        gg = g_ref[...]
        rs = lax.rsqrt(jnp.mean(xs * xs, axis=-1, keepdims=True) + EPS)
        xh = xs * rs
        e = xh * gg - t_ref[...]
        part = 0.5 * jnp.sum(jnp.mean(e * e, axis=-1, keepdims=True), axis=0, keepdims=True)
        dy = e * (1.0 / d)
        dgp = jnp.sum(dy * xh, axis=0, keepdims=True)
        dxh = dy * gg
        dx = rs * (dxh - xh * jnp.mean(dxh * xh, axis=-1, keepdims=True))
        dx_ref[...] = dx
        dxb_ref[...] = dx.astype(BF16)
        first = pl.program_id(0) == 0

        @pl.when(first)
        def _():
            loss_ref[...] = jnp.broadcast_to(part, loss_ref.shape)
            dg_ref[...] = dgp

        @pl.when(jnp.logical_not(first))
        def _():
            loss_ref[...] += jnp.broadcast_to(part, loss_ref.shape)
            dg_ref[...] += dgp

    row = pl.BlockSpec((tm, d), lambda i: (i, 0))
    gsp = pl.BlockSpec((1, d), lambda i: (0, 0))
    lsp = pl.BlockSpec((8, 128), lambda i: (0, 0))
    return pl.pallas_call(
        body, name=name, grid=(l // tm,), in_specs=[row, row, row, gsp], out_specs=[lsp, row, gsp, row],
        out_shape=[jax.ShapeDtypeStruct((8, 128), F32), jax.ShapeDtypeStruct((l, d), F32),
                   jax.ShapeDtypeStruct((1, d), F32), jax.ShapeDtypeStruct((l, d), BF16)],
        compiler_params=_cparams(("arbitrary",)))(x, r, tgt, g.reshape(1, d))


def _sigmoid(x):
    return 0.5 * jnp.tanh(0.5 * x) + 0.5


def glu_fwd(name, z, tm=512):
    l, d2 = z.shape
    d = d2 // 2
    tm = _blk(l, tm)

    def body(z_ref, o_ref):
        o_ref[...] = z_ref[:, :d] * _sigmoid(z_ref[:, d:])

    return pl.pallas_call(body, name=name, grid=(l // tm,),
                          in_specs=[pl.BlockSpec((tm, d2), lambda i: (i, 0))],
                          out_specs=pl.BlockSpec((tm, d), lambda i: (i, 0)),
                          out_shape=jax.ShapeDtypeStruct((l, d), F32),
                          compiler_params=_cparams(("parallel",)))(z)


def glu_bwd(name, z, dm, tm=512):
    l, d2 = z.shape
    d = d2 // 2
    tm = _blk(l, tm)

    def body(z_ref, dm_ref, o_ref):
        sg = _sigmoid(z_ref[:, d:])
        g = dm_ref[...]
        o_ref[:, :d] = (g * sg).astype(BF16)
        o_ref[:, d:] = (g * z_ref[:, :d] * sg * (1.0 - sg)).astype(BF16)

    return pl.pallas_call(body, name=name, grid=(l // tm,),
                          in_specs=[pl.BlockSpec((tm, d2), lambda i: (i, 0)), pl.BlockSpec((tm, d), lambda i: (i, 0))],
                          out_specs=pl.BlockSpec((tm, d2), lambda i: (i, 0)),
                          out_shape=jax.ShapeDtypeStruct((l, d2), BF16),
                          compiler_params=_cparams(("parallel",)))(z, dm)


_GELU_C = math.sqrt(2.0 / math.pi)


def _gelu(y):
    return 0.5 * y * (1.0 + jnp.tanh(_GELU_C * (y + 0.044715 * y * y * y)))


def _gelu_grad(y):
    t = jnp.tanh(_GELU_C * (y + 0.044715 * y * y * y))
    return 0.5 * (1.0 + t) + 0.5 * y * (1.0 - t * t) * _GELU_C * (1.0 + 3.0 * 0.044715 * y * y)


def gelu_bwd(name, ypre, dyg, tm=512):
    l, d = ypre.shape
    tm = _blk(l, tm)

    def body(y_ref, d_ref, o_ref):
        o_ref[...] = (d_ref[...] * _gelu_grad(y_ref[...])).astype(BF16)

    row = pl.BlockSpec((tm, d), lambda i: (i, 0))
    return pl.pallas_call(body, name=name, grid=(l // tm,), in_specs=[row, row], out_specs=row,
                          out_shape=jax.ShapeDtypeStruct((l, d), BF16),
                          compiler_params=_cparams(("parallel",)))(ypre, dyg)


FFN_ROWS = 128
HALO = 16


def _taps_back(tail, g):
    ext = jnp.concatenate([tail, g], axis=0)
    return pltpu.roll(ext, 1, axis=0)[HALO:], pltpu.roll(ext, 2, axis=0)[HALO:]


def ffn_mid_fwd(name, gu, cw, cb, tc=128):
    _, l, f = gu.shape
    tc = _blk(f, tc)

    def body(gu_ref, w_ref, b_ref, a_ref):
        g = gu_ref[0].astype(F32)
        u = gu_ref[1].astype(F32)
        g1, g2 = _taps_back(jnp.zeros((HALO, tc), F32), g)
        gc = w_ref[0:1, :] * g2 + w_ref[1:2, :] * g1 + w_ref[2:3, :] * g + b_ref[...]
        a_ref[...] = (gc * _sigmoid(gc) * u).astype(BF16)

    return pl.pallas_call(
        body, name=name, grid=(f // tc,),
        in_specs=[pl.BlockSpec((2, l, tc), lambda c: (0, 0, c)), pl.BlockSpec((3, tc), lambda c: (0, c)),
                  pl.BlockSpec((1, tc), lambda c: (0, c))],
        out_specs=pl.BlockSpec((l, tc), lambda c: (0, c)),
        out_shape=jax.ShapeDtypeStruct((l, f), BF16),
        compiler_params=_cparams(("parallel",)))(gu, cw, cb)


def ffn_mid_bwd(name, gu, da, cw, cb, tc=128):
    _, l, f = gu.shape
    tc = _blk(f, tc)
    rc = _blk(l, FFN_ROWS)
    nchunk = l // rc

    def body(gu_ref, da_ref, w_ref, b_ref, dgu_ref, dw_ref, db_ref):
        w0, w1, w2, b = w_ref[0:1, :], w_ref[1:2, :], w_ref[2:3, :], b_ref[...]

        def fold(x):
            return jnp.sum(x.reshape(rc // 8, 8, tc), axis=0)

        def chunk(k, carry):
            head, s0, s1, s2, sb = carry
            ci = nchunk - 1 - k
            r0 = pl.multiple_of(ci * rc, rc)
            t0 = pl.multiple_of(jnp.maximum(r0 - HALO, 0), HALO)
            tail = jnp.where(ci > 0, gu_ref[0, pl.ds(t0, HALO), :].astype(F32), 0.0)
            g = gu_ref[0, pl.ds(r0, rc), :].astype(F32)
            u = gu_ref[1, pl.ds(r0, rc), :].astype(F32)
            da = da_ref[pl.ds(r0, rc), :].astype(F32)
            g1, g2 = _taps_back(tail, g)
            gc = w0 * g2 + w1 * g1 + w2 * g + b
            sg = _sigmoid(gc)
            dgu_ref[1, pl.ds(r0, rc), :] = (da * gc * sg).astype(BF16)
            dgc = da * u * (sg * (1.0 + gc * (1.0 - sg)))
            ext = jnp.concatenate([dgc, head], axis=0)
            up1 = pltpu.roll(ext, rc + HALO - 1, axis=0)[:rc]
            up2 = pltpu.roll(ext, rc + HALO - 2, axis=0)[:rc]
            dgu_ref[0, pl.ds(r0, rc), :] = (w2 * dgc + w1 * up1 + w0 * up2).astype(BF16)
            return dgc[:HALO], s0 + fold(dgc * g2), s1 + fold(dgc * g1), s2 + fold(dgc * g), sb + fold(dgc)

        z8 = jnp.zeros((8, tc), F32)
        _, s0, s1, s2, sb = lax.fori_loop(0, nchunk, chunk, (jnp.zeros((HALO, tc), F32), z8, z8, z8, z8))
        dw_ref[0:1, :] = jnp.sum(s0, axis=0, keepdims=True)
        dw_ref[1:2, :] = jnp.sum(s1, axis=0, keepdims=True)
        dw_ref[2:3, :] = jnp.sum(s2, axis=0, keepdims=True)
        db_ref[...] = jnp.sum(sb, axis=0, keepdims=True)

    return pl.pallas_call(
        body, name=name, grid=(f // tc,),
        in_specs=[pl.BlockSpec((2, l, tc), lambda c: (0, 0, c)), pl.BlockSpec((l, tc), lambda c: (0, c)),
                  pl.BlockSpec((3, tc), lambda c: (0, c)), pl.BlockSpec((1, tc), lambda c: (0, c))],
        out_specs=[pl.BlockSpec((2, l, tc), lambda c: (0, 0, c)), pl.BlockSpec((3, tc), lambda c: (0, c)),
                   pl.BlockSpec((1, tc), lambda c: (0, c))],
        out_shape=[jax.ShapeDtypeStruct((2, l, f), BF16), jax.ShapeDtypeStruct((3, f), F32),
                   jax.ShapeDtypeStruct((1, f), F32)],
        compiler_params=_cparams(("parallel",)))(gu, da, cw, cb)


SCAN_LANES = 512


def ssm_discretize(name, a_re, a_im, ldt, b_re, b_im):
    dt = jnp.exp(ldt)
    mag = jnp.exp(a_re * dt)
    ab_re = mag * jnp.cos(a_im * dt)
    ab_im = mag * jnp.sin(a_im * dt)
    den = a_re * a_re + a_im * a_im
    f_re = ((ab_re - 1.0) * a_re + ab_im * a_im) / den
    f_im = (ab_im * a_re - (ab_re - 1.0) * a_im) / den
    return ab_re, ab_im, f_re * b_re - f_im * b_im, f_re * b_im + f_im * b_re


def ssm_disc_fwd(name, a_re, a_im, ldt, b_re, b_im):
    def body(ar, ai, ld, br, bi, o_ar, o_ai, o_br, o_bi):
        r = ssm_discretize(None, ar[...], ai[...], ld[...], br[...], bi[...])
        o_ar[...], o_ai[...], o_br[...], o_bi[...] = r

    sd = jax.ShapeDtypeStruct
    return pl.pallas_call(body, name=name,
                          out_shape=[sd(a_re.shape, F32), sd(a_re.shape, F32), sd(b_re.shape, F32), sd(b_re.shape, F32)],
                          compiler_params=_cparams())(a_re, a_im, ldt, b_re, b_im)


def ssm_disc_bwd(name, a_re, a_im, ldt, b_re, b_im, d_ar, d_ai, d_br, d_bi):
    def body(ar, ai, ld, br, bi, g_ar, g_ai, g_br, g_bi, o_ar, o_ai, o_ld, o_br, o_bi):
        fn = functools.partial(ssm_discretize, None)
        _, vjp = jax.vjp(fn, ar[...], ai[...], ld[...], br[...], bi[...])
        r = vjp((g_ar[...], g_ai[...], g_br[...], g_bi[...]))
        o_ar[...], o_ai[...], o_ld[...], o_br[...], o_bi[...] = r

    sd = jax.ShapeDtypeStruct
    return pl.pallas_call(body, name=name,
                          out_shape=[sd(a_re.shape, F32)] * 3 + [sd(b_re.shape, F32)] * 2,
                          compiler_params=_cparams())(a_re, a_im, ldt, b_re, b_im, d_ar, d_ai, d_br, d_bi)


def _drive(x_ref, wre_ref, wim_ref, dre_ref, dim_ref):
    xb = x_ref[...]
    dre_ref[...] = jnp.dot(xb, wre_ref[...], preferred_element_type=F32)
    dim_ref[...] = jnp.dot(xb, wim_ref[...], preferred_element_type=F32)


def _ssm_dims(l, wre, tj):
    nb, kb, nsb = wre.shape
    jn = l // N_STREAMS
    tj = _blk(jn, tj)
    return nb, kb, nsb, jn, tj, tj * N_STREAMS, jn // tj


def _scan(dre_ref, dim_ref, st_re, st_im, a_ref, tj, reverse, write):
    ns = dre_ref.shape[1]
    cw = min(SCAN_LANES, ns)
    for cb in range(ns // cw):
        sl = slice(cb * cw, (cb + 1) * cw)
        ar = jnp.broadcast_to(a_ref[0:1, sl], (N_STREAMS, cw))
        ai = jnp.broadcast_to(a_ref[1:2, sl], (N_STREAMS, cw))

        def step(jj, carry, sl=sl, ar=ar, ai=ai):
            sr, si = carry
            j = (tj - 1 - jj) if reverse else jj
            off = pl.multiple_of(j * N_STREAMS, N_STREAMS)
            nr = ar * sr - ai * si + dre_ref[pl.ds(off, N_STREAMS), sl]
            ni = ar * si + ai * sr + dim_ref[pl.ds(off, N_STREAMS), sl]
            if write:
                dre_ref[pl.ds(off, N_STREAMS), sl] = nr
                dim_ref[pl.ds(off, N_STREAMS), sl] = ni
            return nr, ni

        sr, si = lax.fori_loop(0, tj, step, (st_re[:, sl], st_im[:, sl]))
        st_re[:, sl] = sr
        st_im[:, sl] = si


def ssm_pass1(name, x, wre, wim, a2, *, reverse, tj=64, rider=None):
    l, d = x.shape
    nb, kb, nsb, jn, tj, r, nblk = _ssm_dims(l, wre, tj)
    ns = nb * nsb

    def body(x_ref, wre_ref, wim_ref, a_ref, cre_ref, cim_ref, dre, dim, st_re, st_im):
        i = pl.program_id(1)

        @pl.when(i == 0)
        def _():
            st_re[...] = jnp.zeros_like(st_re)
            st_im[...] = jnp.zeros_like(st_im)

        _drive(x_ref, wre_ref, wim_ref, dre, dim)
        _scan(dre, dim, st_re, st_im, a_ref, tj, reverse, False)

        @pl.when(i == nblk - 1)
        def _():
            pr, pi = a_ref[0:1, :], a_ref[1:2, :]
            rr, ri = jnp.ones_like(pr), jnp.zeros_like(pr)
            e = jn
            while e:
                if e & 1:
                    rr, ri = rr * pr - ri * pi, rr * pi + ri * pr
                pr, pi = pr * pr - pi * pi, 2.0 * pr * pi
                e >>= 1
            order = range(N_STREAMS - 1, -1, -1) if reverse else range(N_STREAMS)
            cr = jnp.zeros_like(rr)
            ci = jnp.zeros_like(rr)
            for s in order:
                cre_ref[s:s + 1, :] = cr
                cim_ref[s:s + 1, :] = ci
                fr, fi = st_re[s:s + 1, :], st_im[s:s + 1, :]
                cr, ci = fr + rr * cr - ri * ci, fi + rr * ci + ri * cr

    blk = (lambda b, i: (nblk - 1 - i, b)) if reverse else (lambda b, i: (i, b))
    w3 = pl.BlockSpec((None, kb, nsb), lambda b, i: (b, 0, 0))
    st = pl.BlockSpec((N_STREAMS, nsb), lambda b, i: (0, b))
    return _call(
        name, body, grid=(nb, nblk),
        in_specs=[pl.BlockSpec((r, kb), blk), w3, w3, pl.BlockSpec((2, nsb), lambda b, i: (0, b))],
        out_specs=[st, st], out_shape=[jax.ShapeDtypeStruct((N_STREAMS, ns), F32)] * 2,
        scratch=[pltpu.VMEM((r, nsb), F32), pltpu.VMEM((r, nsb), F32),
                 pltpu.VMEM((N_STREAMS, nsb), F32), pltpu.VMEM((N_STREAMS, nsb), F32)],
        sem=("parallel", "arbitrary"), args=(x, wre, wim, a2), rider=rider)


def ssm_fwd2(name, u, bre, bim, a2, init_re, init_im, cre, cim, dskip, *, tj=64, rider=None):
    l, d = u.shape
    nb, kb, nsb, jn, tj, r, nblk = _ssm_dims(l, bre, tj)
    ns = nb * nsb

    def body(u_ref, bre_ref, bim_ref, a_ref, ire_ref, iim_ref, cre_ref, cim_ref, d_ref,
             sre_ref, sim_ref, y_ref, yg_ref, dre, dim, st_re, st_im):
        @pl.when(pl.program_id(1) == 0)
        def _():
            st_re[...] = ire_ref[...]
            st_im[...] = iim_ref[...]

        _drive(u_ref, bre_ref, bim_ref, dre, dim)
        _scan(dre, dim, st_re, st_im, a_ref, tj, False, True)
        srb, sib = dre[...].astype(BF16), dim[...].astype(BF16)
        sre_ref[...] = srb
        sim_ref[...] = sib
        y = (jnp.dot(srb, cre_ref[...], preferred_element_type=F32)
             - jnp.dot(sib, cim_ref[...], preferred_element_type=F32)
             + d_ref[...] * u_ref[...].astype(F32))
        y_ref[...] = y
        yg_ref[...] = _gelu(y).astype(BF16)

    rows = lambda w: pl.BlockSpec((r, w), lambda b, i: (i, b))
    w3 = pl.BlockSpec((None, kb, nsb), lambda b, i: (b, 0, 0))
    c3 = pl.BlockSpec((None, nsb, kb), lambda b, i: (b, 0, 0))
    st = pl.BlockSpec((N_STREAMS, nsb), lambda b, i: (0, b))
    return _call(
        name, body, grid=(nb, nblk),
        in_specs=[rows(kb), w3, w3, pl.BlockSpec((2, nsb), lambda b, i: (0, b)), st, st, c3, c3,
                  pl.BlockSpec((1, kb), lambda b, i: (0, b))],
        out_specs=[rows(nsb), rows(nsb), rows(kb), rows(kb)],
        out_shape=[jax.ShapeDtypeStruct((l, ns), BF16), jax.ShapeDtypeStruct((l, ns), BF16),
                   jax.ShapeDtypeStruct((l, d), F32), jax.ShapeDtypeStruct((l, d), BF16)],
        scratch=[pltpu.VMEM((r, nsb), F32), pltpu.VMEM((r, nsb), F32),
                 pltpu.VMEM((N_STREAMS, nsb), F32), pltpu.VMEM((N_STREAMS, nsb), F32)],
        sem=("parallel", "arbitrary"), args=(u, bre, bim, a2, init_re, init_im, cre, cim, dskip), rider=rider)


def ssm_bwd2(name, dy, u, sre, sim, ctre, ctim, a2c, init_re, init_im, fre, fim, bre, bim, dskip, *, tj=64, rider=None):
    l, d = u.shape
    nb, kb, nsb, jn, tj, r, nblk = _ssm_dims(l, bre, tj)
    ns = nb * nsb

    def body(dy_ref, u_ref, sre_ref, sim_ref, pre_ref, pim_ref, ctre_ref, ctim_ref, a_ref, ire_ref, iim_ref,
             fre_ref, fim_ref, bre_ref, bim_ref, d_ref,
             du_ref, dbre_ref, dbim_ref, dcre_ref, dcim_ref, dare_ref, daim_ref, dd_ref,
             lre, lim, sf_re, sf_im, st_re, st_im):
        i = pl.program_id(1)
        first = i == 0
        sf_re[...] = sre_ref[...].astype(F32)
        sf_im[...] = sim_ref[...].astype(F32)

        @pl.when(first)
        def _():
            st_re[...] = ire_ref[...]
            st_im[...] = iim_ref[...]
            dbre_ref[...] = jnp.zeros_like(dbre_ref)
            dbim_ref[...] = jnp.zeros_like(dbim_ref)
            dcre_ref[...] = jnp.zeros_like(dcre_ref)
            dcim_ref[...] = jnp.zeros_like(dcim_ref)
            dare_ref[...] = jnp.zeros_like(dare_ref)
            daim_ref[...] = jnp.zeros_like(daim_ref)
            dd_ref[...] = jnp.zeros_like(dd_ref)

        _drive(dy_ref, ctre_ref, ctim_ref, lre, lim)
        _scan(lre, lim, st_re, st_im, a_ref, tj, True, True)

        is_t0 = i == nblk - 1
        cw = min(SCAN_LANES, nsb)
        for cb in range(nsb // cw):
            sl = slice(cb * cw, (cb + 1) * cw)
            p_r = jnp.where(is_t0, fre_ref[:, sl], pre_ref[:, sl].astype(F32)[N_STREAMS:])
            p_i = jnp.where(is_t0, fim_ref[:, sl], pim_ref[:, sl].astype(F32)[N_STREAMS:])
            l_r, l_i = lre[0:N_STREAMS, sl], lim[0:N_STREAMS, sl]
            acc = (l_r * p_r + l_i * p_i, l_i * p_r - l_r * p_i)

            def step(jj, carry, sl=sl):
                a_r, a_i = carry
                off = pl.multiple_of(jj * N_STREAMS, N_STREAMS)
                prev = pl.multiple_of((jj - 1) * N_STREAMS, N_STREAMS)
                l_r, l_i = lre[pl.ds(off, N_STREAMS), sl], lim[pl.ds(off, N_STREAMS), sl]
                p_r, p_i = sf_re[pl.ds(prev, N_STREAMS), sl], sf_im[pl.ds(prev, N_STREAMS), sl]
                return a_r + l_r * p_r + l_i * p_i, a_i + l_i * p_r - l_r * p_i

            a_r, a_i = lax.fori_loop(1, tj, step, acc)
            dare_ref[:, sl] += a_r
            daim_ref[:, sl] += a_i

        dyf = dy_ref[...].astype(F32)
        uf = u_ref[...].astype(F32)
        dd_ref[...] += jnp.sum(dyf * uf, axis=0, keepdims=True)
        lrb = lre[...].astype(BF16)
        lib = lim[...].astype(BF16)
        ub = u_ref[...]
        dyb = dy_ref[...]
        dbre_ref[...] += lax.dot_general(ub, lrb, TN, preferred_element_type=F32)
        dbim_ref[...] += lax.dot_general(ub, lib, TN, preferred_element_type=F32)
        dcre_ref[...] += lax.dot_general(sre_ref[...].astype(BF16), dyb, TN, preferred_element_type=F32)
        dcim_ref[...] -= lax.dot_general(sim_ref[...].astype(BF16), dyb, TN, preferred_element_type=F32)
        du = (lax.dot_general(lrb, bre_ref[...], NT, preferred_element_type=F32)
              + lax.dot_general(lib, bim_ref[...], NT, preferred_element_type=F32)
              + d_ref[...] * dyf)
        du_ref[...] = du.astype(BF16)

    rev = lambda w: pl.BlockSpec((r, w), lambda b, i: (nblk - 1 - i, b))
    assert tj % 2 == 0 or nblk == 1
    prev_tile = pl.BlockSpec((2 * N_STREAMS, nsb), lambda b, i: (jnp.maximum((nblk - 1 - i) * tj, 2) // 2 - 1, b))
    st = pl.BlockSpec((N_STREAMS, nsb), lambda b, i: (0, b))
    w3 = pl.BlockSpec((None, kb, nsb), lambda b, i: (b, 0, 0))
    c3 = pl.BlockSpec((None, nsb, kb), lambda b, i: (b, 0, 0))
    dsp = pl.BlockSpec((1, kb), lambda b, i: (0, b))
    return _call(
        name, body, grid=(nb, nblk),
        in_specs=[rev(kb), rev(kb), rev(nsb), rev(nsb), prev_tile, prev_tile, w3, w3,
                  pl.BlockSpec((2, nsb), lambda b, i: (0, b)), st, st, st, st, w3, w3, dsp],
        out_specs=[rev(kb), w3, w3, c3, c3, st, st, dsp],
        out_shape=[jax.ShapeDtypeStruct((l, d), BF16), jax.ShapeDtypeStruct((nb, kb, nsb), F32),
                   jax.ShapeDtypeStruct((nb, kb, nsb), F32), jax.ShapeDtypeStruct((nb, nsb, kb), F32),
                   jax.ShapeDtypeStruct((nb, nsb, kb), F32), jax.ShapeDtypeStruct((N_STREAMS, ns), F32),
                   jax.ShapeDtypeStruct((N_STREAMS, ns), F32), jax.ShapeDtypeStruct((1, d), F32)],
        scratch=[pltpu.VMEM((r, nsb), F32), pltpu.VMEM((r, nsb), F32), pltpu.VMEM((r, nsb), F32), pltpu.VMEM((r, nsb), F32),
                 pltpu.VMEM((N_STREAMS, nsb), F32), pltpu.VMEM((N_STREAMS, nsb), F32)],
        sem=("parallel", "arbitrary"),
        args=(dy, u, sre, sim, sre, sim, ctre, ctim, a2c, init_re, init_im, fre, fim, bre, bim, dskip), rider=rider)


ATT_TILE = 256
ATT_HEADS = 4
EXP_ZERO_BELOW = -104.0
LANES = 128


def _split_dot(x, tri):
    hi = x.astype(BF16)
    lo = (x - hi.astype(F32)).astype(BF16)
    return jnp.dot(hi, tri, preferred_element_type=F32) + jnp.dot(lo, tri, preferred_element_type=F32)


def _sb_logs(z, causal):
    sp = jnp.maximum(z, 0.0) + jnp.log(1.0 + jnp.exp(-jnp.abs(z)))
    l1m = -sp
    if causal is not None:
        l1m = jnp.where(causal, l1m, 0.0)
    return z - sp, l1m


def attn_fwd(name, q, kv, t=ATT_TILE, rider=None):
    l, dm = q.shape
    dh = HEAD_DIM
    h = dm // dh
    t = _blk(l, t)
    hb = _blk(h, ATT_HEADS)
    wb = hb * dh
    scale = dh ** -0.5

    def body(q_ref, k_ref, v_ref, o_ref, tot_ref):
        i = pl.program_id(1)
        hd = lambda g: slice(g * dh, (g + 1) * dh)
        qs = [(q_ref[:, hd(g)].astype(F32) * scale).astype(BF16) for g in range(hb)]
        row = lax.broadcasted_iota(jnp.int32, (t, t), 0)
        col = lax.broadcasted_iota(jnp.int32, (t, t), 1)
        tri_gt = jnp.where(row > col, 1.0, 0.0).astype(BF16)
        causal = col < row

        def tile(jb, carry, mask):
            off = pl.multiple_of(jb * t, t)
            heads = range(hb)
            z = [lax.dot_general(qs[g], k_ref[pl.ds(off, t), hd(g)], NT, preferred_element_type=F32) for g in heads]
            logs = [_sb_logs(z[g], mask) for g in heads]
            rem = [_split_dot(logs[g][1], tri_gt) for g in heads]
            w = [jnp.exp(logs[g][0] + rem[g] + carry[g][1]) for g in heads]
            if mask is not None:
                w = [jnp.where(mask, w[g], 0.0) for g in heads]
            pv = [jnp.dot(w[g].astype(BF16), v_ref[pl.ds(off, t), hd(g)], preferred_element_type=F32) for g in heads]
            return tuple((carry[g][0] + pv[g], carry[g][1] + rem[g][:, 0:1] + logs[g][1][:, 0:1]) for g in heads)

        def live(carry):
            top = carry[0][1]
            for g in range(1, hb):
                top = jnp.maximum(top, carry[g][1])
            return (jnp.max(top) > EXP_ZERO_BELOW).astype(jnp.int32)

        def more(c):
            it, alive, _ = c
            return jnp.logical_and(it < i, alive > 0)

        def step(c):
            it, _, carry = c
            carry = tile(i - 1 - it, carry, None)
            return it + 1, live(carry), carry

        carry = tile(i, ((jnp.zeros((t, dh), F32), jnp.zeros((t, 1), F32)),) * hb, causal)
        done, _, carry = lax.while_loop(more, step, (jnp.int32(0), live(carry), carry))
        tot_ref[...] = jnp.zeros_like(tot_ref)
        for g, (acc, run) in enumerate(carry):
            o_ref[:, hd(g)] = acc.astype(BF16)
            tot_ref[:, g:g + 1] = run
        tot_ref[:, hb:hb + 1] = jnp.full((t, 1), done, jnp.int32).astype(F32)

    qsp = pl.BlockSpec((t, wb), lambda hh, i: (i, hh))
    ksp = pl.BlockSpec((None, l, wb), lambda hh, i: (0, 0, hh))
    vsp = pl.BlockSpec((None, l, wb), lambda hh, i: (1, 0, hh))
    tsp = pl.BlockSpec((None, t, LANES), lambda hh, i: (hh, i, 0))
    return _call(name, body, grid=(h // hb, l // t), in_specs=[qsp, ksp, vsp], out_specs=[qsp, tsp],
                 out_shape=[jax.ShapeDtypeStruct((l, dm), BF16), jax.ShapeDtypeStruct((h // hb, l, LANES), F32)],
                 sem=("parallel", "parallel"), args=(q, kv, kv), rider=rider)


def attn_bwd(name, q, kv, tot, do, t=ATT_TILE, rider=None):
    l, dm = q.shape
    dh = HEAD_DIM
    h = dm // dh
    t = _blk(l, t)
    hb = _blk(h, ATT_HEADS)
    wb = hb * dh
    nq = l // t
    scale = dh ** -0.5

    def body(q_ref, k_ref, v_ref, tot_ref, do_ref, dq_ref, dkv_ref, dk_ref, dv_ref):
        i = pl.program_id(1)
        hd = lambda g: slice(g * dh, (g + 1) * dh)

        @pl.when(i == 0)
        def _():
            dk_ref[...] = jnp.zeros_like(dk_ref)
            dv_ref[...] = jnp.zeros_like(dv_ref)

        qs = [(q_ref[:, hd(g)].astype(F32) * scale).astype(BF16) for g in range(hb)]
        dob = [do_ref[:, hd(g)] for g in range(hb)]
        total = [tot_ref[:, g:g + 1] for g in range(hb)]
        row = lax.broadcasted_iota(jnp.int32, (t, t), 0)
        col = lax.broadcasted_iota(jnp.int32, (t, t), 1)
        tri_le = jnp.where(row <= col, 1.0, 0.0).astype(BF16)
        tri_lt = jnp.where(row < col, 1.0, 0.0).astype(BF16)
        causal = col < row

        def tile(jb, carry, mask):
            off = pl.multiple_of(jb * t, t)
            heads = range(hb)
            kj = [k_ref[pl.ds(off, t), hd(g)] for g in heads]
            z = [lax.dot_general(qs[g], kj[g], NT, preferred_element_type=F32) for g in heads]
            dp = [lax.dot_general(dob[g], v_ref[pl.ds(off, t), hd(g)], NT, preferred_element_type=F32) for g in heads]
            logs = [_sb_logs(z[g], mask) for g in heads]
            lpre = [_split_dot(logs[g][1], tri_le) for g in heads]
            w = [jnp.exp(logs[g][0] + (total[g] - carry[g][1] - lpre[g])) for g in heads]
            if mask is not None:
                w = [jnp.where(mask, w[g], 0.0) for g in heads]
            p = [w[g] * dp[g] for g in heads]
            for g in heads:
                dv_ref[pl.ds(off, t), hd(g)] += lax.dot_general(w[g].astype(BF16), dob[g], TN, preferred_element_type=F32)
            qpre = [carry[g][2] + jnp.dot(p[g].astype(BF16), tri_lt, preferred_element_type=F32) for g in heads]
            dz = [p[g] - jnp.exp(logs[g][0]) * (p[g] + qpre[g]) for g in heads]
            if mask is not None:
                dz = [jnp.where(mask, dz[g], 0.0) for g in heads]
            dzb = [dz[g].astype(BF16) for g in heads]
            for g in heads:
                dk_ref[pl.ds(off, t), hd(g)] += lax.dot_general(dzb[g], qs[g], TN, preferred_element_type=F32)
            dq = [carry[g][0] + jnp.dot(dzb[g], kj[g], preferred_element_type=F32) for g in heads]
            return tuple((dq[g], carry[g][1] + lpre[g][:, t - 1:t], qpre[g][:, t - 1:t] + p[g][:, t - 1:t]) for g in heads)

        zero = jnp.zeros((t, 1), F32)
        done = jnp.max(tot_ref[:, hb:hb + 1]).astype(jnp.int32)
        carry = lax.fori_loop(i - done, i, lambda jb, c: tile(jb, c, None), ((jnp.zeros((t, dh), F32), zero, zero),) * hb)
        carry = tile(i, carry, causal)
        for g in range(hb):
            dq_ref[:, hd(g)] = (carry[g][0] * scale).astype(BF16)

        @pl.when(i == nq - 1)
        def _():
            dkv_ref[0] = dk_ref[...].astype(BF16)
            dkv_ref[1] = dv_ref[...].astype(BF16)

    qsp = pl.BlockSpec((t, wb), lambda hh, i: (i, hh))
    ksp = pl.BlockSpec((None, l, wb), lambda hh, i: (0, 0, hh))
    vsp = pl.BlockSpec((None, l, wb), lambda hh, i: (1, 0, hh))
    tsp = pl.BlockSpec((None, t, LANES), lambda hh, i: (hh, i, 0))
    return _call(name, body, grid=(h // hb, nq), in_specs=[qsp, ksp, vsp, tsp, qsp],
                 out_specs=[qsp, pl.BlockSpec((2, l, wb), lambda hh, i: (0, 0, hh))],
                 out_shape=[jax.ShapeDtypeStruct((l, dm), BF16), jax.ShapeDtypeStruct((2, l, dm), BF16)],
                 scratch=[pltpu.VMEM((l, wb), F32), pltpu.VMEM((l, wb), F32)],
                 sem=("parallel", "arbitrary"), args=(q, kv, kv, tot, do), rider=rider)


def adamw(name, w, g, m, v):
    shape = w.shape
    cols = shape[-1]
    rows = w.size // cols
    tr = rows
    if rows % 8 == 0:
        tr = 8 * _blk(rows // 8, 64)
    c1 = 1.0 - ADAM_B1 ** ADAM_STEP
    c2 = 1.0 - ADAM_B2 ** ADAM_STEP

    def body(w_ref, g_ref, m_ref, v_ref, d_ref, nm_ref, nv_ref):
        gg = g_ref[...]
        mm = ADAM_B1 * m_ref[...] + (1.0 - ADAM_B1) * gg
        vv = ADAM_B2 * v_ref[...] + (1.0 - ADAM_B2) * (gg * gg)
        nm_ref[...] = mm
        nv_ref[...] = vv
        d_ref[...] = -ADAM_LR * ((mm / c1) / (jnp.sqrt(vv / c2) + ADAM_EPS) + ADAM_WD * w_ref[...])

    sp = pl.BlockSpec((tr, cols), lambda i: (i, 0))
    sd = jax.ShapeDtypeStruct((rows, cols), F32)
    outs = pl.pallas_call(body, name=name, grid=(rows // tr,), in_specs=[sp] * 4, out_specs=[sp] * 3,
                          out_shape=[sd] * 3, compiler_params=_cparams(("parallel",)))(
        w.reshape(rows, cols), g.reshape(rows, cols), m.reshape(rows, cols), v.reshape(rows, cols))
    return tuple(o.reshape(shape) for o in outs)


def _perm(a):
    l, d = a.shape
    return a.reshape(N_STREAMS, l // N_STREAMS, d).transpose(1, 0, 2).reshape(l, d)


def _unperm(a):
    l, d = a.shape
    return a.reshape(l // N_STREAMS, N_STREAMS, d).transpose(1, 0, 2).reshape(l, d)


def _ssm_layouts(d):
    g = d // SSM_GROUP
    gb = MXU_DIM // SSM_GROUP if d >= MXU_DIM else g
    return g, gb, g // gb


def _b_blocks(bb, d):
    g, gb, nb = _ssm_layouts(d)
    b4 = bb.reshape(SSM_GROUP, nb, gb, STATE)
    eye = jnp.eye(gb, dtype=bb.dtype)
    return jnp.einsum('hbqp,gq->bghqp', b4, eye).reshape(nb, gb * SSM_GROUP, gb * STATE)


def _b_unblocks(db, d):
    g, gb, nb = _ssm_layouts(d)
    eye = jnp.eye(gb, dtype=db.dtype)
    return jnp.einsum('bghqp,gq->hbqp', db.reshape(nb, gb, SSM_GROUP, gb, STATE), eye).reshape(SSM_GROUP, g * STATE)


def _c_blocks(c, d):
    g, gb, nb = _ssm_layouts(d)
    eye = jnp.eye(gb, dtype=c.dtype)
    return jnp.einsum('bghp,gq->bqpgh', c.reshape(nb, gb, SSM_GROUP, STATE), eye).reshape(nb, gb * STATE, gb * SSM_GROUP)


def _c_unblocks(dc, d):
    g, gb, nb = _ssm_layouts(d)
    eye = jnp.eye(gb, dtype=dc.dtype)
    return jnp.einsum('bqpgh,gq->bghp', dc.reshape(nb, gb, STATE, gb, SSM_GROUP), eye).reshape(g, SSM_GROUP, STATE)


class NoComm:
    def __init__(self):
        self.local = {}

    def gather(self, names):
        return None

    def landed(self, names, bufs, p):
        pass

    def swap(self, names, arrays):
        self.local.update(zip(names, arrays))
        return None

    def swapped(self, names, got):
        pass

    def chips(self, names):
        return None

    def reduced(self, names, slots):
        pass


def _ffn_fwd(tag, x, r, gain, p, layer, cm, ride):
    xs, h = add_rmsnorm(f"norm_ffn{tag}", x, r, [gain])
    rider = cm.gather(ride) if ride else None
    gu = mm_up(f"ffn_up{tag}", h, p["wup"][layer], 0, rider=rider)
    if rider is not None:
        gu, landed = gu
        cm.landed(ride, landed, p)
    a = ffn_mid_fwd(f"ffn_mid{tag}", gu, p["conv_w"][layer], p["conv_b"][layer:layer + 1])
    f = mm_nn(f"ffn_down{tag}", a, p["wdown"][layer], F32)
    return xs, h, gu, a, f


def _ffn_bwd(tag, dxo, dfb, xs, h, gu, a, gain, p, layer, s, cm, ride):
    wdown = p["wdown"][layer]
    f = wdown.shape[0]
    da = mm_nt(f"ffn_down_dx{tag}", dfb, wdown, BF16, tn=f // 2)
    dwdown = mm_tn(f"ffn_down_dw{tag}", a, dfb, tmo=f // 2)
    dgu, dcw, dcb = ffn_mid_bwd(f"ffn_mid_bwd{tag}", gu, da, p["conv_w"][layer], p["conv_b"][layer:layer + 1])
    rider = cm.chips(ride) if ride else None
    dh = mm_up_nt(f"ffn_up_dx{tag}", dgu, p["wup"][layer], 0, rider=rider)
    if rider is not None:
        dh, slots = dh
        cm.reduced(ride, slots)
    dwup = mm_up_tn(f"ffn_up_dw{tag}", h, dgu, s)
    own = [f"wup{layer}", f"wdown{layer}"]
    res = norm_bwd(f"norm_ffn_bwd{tag}", xs, dxo, [(gain, dh)], bf16_copy=True, rider=cm.swap(own, [dwup, dwdown]))
    cm.swapped(own, res[3:])
    return res[0], res[2], res[1], dcw, dcb


def _local_step(x, tgt, p, cm=None):
    cm = NoComm() if cm is None else cm
    l, d = x.shape
    s = N_CHIPS
    gr = {}

    a_re, a_im = p["a_re"].reshape(1, -1), p["a_im"].reshape(1, -1)
    ldt = jnp.repeat(p["log_dt"].reshape(-1), STATE).reshape(1, -1)
    bk_re = p["b_re"].transpose(2, 0, 1).reshape(SSM_GROUP, -1)
    bk_im = p["b_im"].transpose(2, 0, 1).reshape(SSM_GROUP, -1)
    ab_re, ab_im, bb_re, bb_im = ssm_disc_fwd("ssm_disc", a_re, a_im, ldt, bk_re, bk_im)
    a2 = jnp.concatenate([ab_re, ab_im], axis=0)
    a2c = jnp.concatenate([ab_re, -ab_im], axis=0)
    bre, bim = _b_blocks(bb_re, d).astype(BF16), _b_blocks(bb_im, d).astype(BF16)
    cre, cim = _c_blocks(p["c_re"], d).astype(BF16), _c_blocks(p["c_im"], d).astype(BF16)
    ctre, ctim = cre.transpose(0, 2, 1), -cim.transpose(0, 2, 1)
    dskip = p["d"].reshape(1, d)

    xp = _perm(x)
    (h0p,) = add_rmsnorm("norm_mix0", xp, None, [p["norm_mix"][0]])
    u = mm_nn("ssm_in", h0p, p["win"], BF16)
    rider = cm.gather(["wglu", "wdown0"])
    res = ssm_pass1("ssm_fwd1", u, bre, bim, a2, reverse=False, rider=rider)
    f_re, f_im = res[0], res[1]
    cm.landed(["wglu", "wdown0"], res[2:], p)
    rider = cm.gather(["wup0"])
    res = ssm_fwd2("ssm_fwd2", u, bre, bim, a2, f_re, f_im, cre, cim, dskip, rider=rider)
    s_re, s_im, ypre, yg = res[:4]
    cm.landed(["wup0"], res[4:], p)
    z = mm_nn_colshard("ssm_glu", yg, p["wglu4"], F32)
    mix = _unperm(glu_fwd("glu", z))

    x1, h1, gu0, a0, f0 = _ffn_fwd("0", x, mix, p["norm_ffn"][0], p, 0, cm, ["kv", "wq", "wo"])
    x2, hk, h2 = add_rmsnorm("norm_kv_mix1", x1, f0, [p["norm_kv"], p["norm_mix"][1]])
    kvw = p["kvw4"][:, None]
    kv = mm_up("kv_proj", hk, kvw, 0)
    qf = mm_nn("q_proj", h2, p["wq"], BF16)
    rider = cm.gather(["wup1", "wdown1"])
    res = attn_fwd("attn", qf, kv, rider=rider)
    ob, tot = res[0], res[1]
    cm.landed(["wup1", "wdown1"], res[2:], p)
    ao = mm_nn("o_proj", ob, p["wo"], F32)
    x3, h3, gu1, a1, f1 = _ffn_fwd("1", x2, ao, p["norm_ffn"][1], p, 1, cm, None)
    loss, dx4, dg_final, dx4b = final_loss("final_loss", x3, f1, tgt, p["norm_final"])
    gr["norm_final"] = dg_final.reshape(d)

    dx3, dx3b, dg_ffn1, dcw1, dcb1 = _ffn_bwd("1", dx4, dx4b, x3, h3, gu1, a1, p["norm_ffn"][1], p, 1, s, cm, None)
    do2 = mm_nt("o_proj_dx", dx3b, p["wo"], BF16)
    dwo = mm_tn("o_proj_dw", ob, dx3b)
    rider = cm.chips(["wup1", "wdown1"])
    res = attn_bwd("attn_bwd", qf, kv, tot, do2, rider=rider)
    dqf, dkv = res[0], res[1]
    cm.reduced(["wup1", "wdown1"], res[2:])
    dh2 = mm_nt("q_proj_dx", dqf, p["wq"], F32)
    dwq = mm_tn("q_proj_dw", h2, dqf)
    dhk = mm_up_nt("kv_proj_dx", dkv, kvw, 0)
    dwkv = mm_up_tn("kv_proj_dw", hk, dkv, s)
    att = ["wo", "wq", "kv"]
    res = norm_bwd("norm_kv_mix1_bwd", x2, dx3, [(p["norm_mix"][1], dh2), (p["norm_kv"], dhk)], bf16_copy=True,
                   rider=cm.swap(att, [dwo, dwq, dwkv]))
    dx2, dg_mix1, dg_kv, dx2b = res[:4]
    cm.swapped(att, res[4:])
    gr["norm_kv"] = dg_kv.reshape(d)

    dx1, _, dg_ffn0, dcw0, dcb0 = _ffn_bwd("0", dx2, dx2b, x1, h1, gu0, a0, p["norm_ffn"][0], p, 0, s, cm, att)
    dx1p = _perm(dx1)
    dz = glu_bwd("glu_bwd", z, dx1p)
    dyg = mm_nt_colshard("ssm_glu_dx", dz, p["wglu4"], F32)
    dwglu = mm_tn_colshard("ssm_glu_dw", yg, dz, s)
    dy = gelu_bwd("gelu_bwd", ypre, dyg)
    res = ssm_pass1("ssm_bwd1", dy, ctre, ctim, a2c, reverse=True, rider=cm.swap(["wglu"], [dwglu]))
    i_re, i_im = res[0], res[1]
    cm.swapped(["wglu"], res[2:])
    rider = cm.chips(["wup0", "wdown0", "wglu"])
    res = ssm_bwd2("ssm_bwd2", dy, u, s_re, s_im, ctre, ctim, a2c, i_re, i_im, f_re, f_im, bre, bim, dskip, rider=rider)
    du, dbre, dbim, dcre, dcim, da_re, da_im, dd = res[:8]
    cm.reduced(["wup0", "wdown0", "wglu"], res[8:])
    dh0p = mm_nt("ssm_in_dx", du, p["win"], F32)
    dwin = mm_tn("ssm_in_dw", h0p, du)
    dxp, dg_mix0 = norm_bwd("norm_mix0_bwd", xp, dx1p, [(p["norm_mix"][0], dh0p)])
    dx = _unperm(dxp)

    g_are, g_aim, g_ldt, g_bre, g_bim = ssm_disc_bwd(
        "ssm_disc_bwd", a_re, a_im, ldt, bk_re, bk_im,
        jnp.sum(da_re, axis=0, keepdims=True), jnp.sum(da_im, axis=0, keepdims=True),
        _b_unblocks(dbre, d), _b_unblocks(dbim, d))
    g = d // SSM_GROUP
    gr["a_re"] = g_are.reshape(g, STATE)
    gr["a_im"] = g_aim.reshape(g, STATE)
    gr["log_dt"] = jnp.sum(g_ldt.reshape(g, STATE), axis=1)
    gr["b_re"] = g_bre.reshape(SSM_GROUP, g, STATE).transpose(1, 2, 0)
    gr["b_im"] = g_bim.reshape(SSM_GROUP, g, STATE).transpose(1, 2, 0)
    gr["c_re"] = _c_unblocks(dcre, d)
    gr["c_im"] = _c_unblocks(dcim, d)
    gr["d"] = dd.reshape(g, SSM_GROUP)
    gr["norm_mix"] = jnp.concatenate([dg_mix0, dg_mix1], axis=0)
    gr["norm_ffn"] = jnp.concatenate([dg_ffn0, dg_ffn1], axis=0)
    gr["conv_w"] = jnp.stack([dcw0, dcw1])
    gr["conv_b"] = jnp.concatenate([dcb0, dcb1], axis=0)
    gr["win"] = dwin
    gr.update(getattr(cm, "local", {}))
    return loss, dx, gr


MESH = pl.DeviceIdType.MESH
N_CHIPS = 4
N_DEV = 8
ANY = pl.BlockSpec(memory_space=pl.ANY)


def _pos():
    x, y, c = lax.axis_index("x"), lax.axis_index("y"), lax.axis_index("c")
    return x, y, c, 2 * x + y


def _other_chips(x, y):
    return [(1 - x, y), (x, 1 - y), (1 - x, 1 - y)]


def _remote(src, dst, send_sem, recv_sem, dev):
    return pltpu.make_async_remote_copy(src_ref=src, dst_ref=dst, send_sem=send_sem, recv_sem=recv_sem,
                                        device_id=dev, device_id_type=MESH)


def cast_into_slot(name, a, chip, dtype):
    r, cdim = a.shape
    tr = 16 * _blk(r // 16, 32) if r % 16 == 0 else r

    def body(m_ref, a_ref, o_ref):
        o_ref[...] = a_ref[...].astype(o_ref.dtype)

    gs = pltpu.PrefetchScalarGridSpec(
        num_scalar_prefetch=1, grid=(r // tr,),
        in_specs=[pl.BlockSpec((tr, cdim), lambda i, m_ref: (i, 0))],
        out_specs=pl.BlockSpec((None, tr, cdim), lambda i, m_ref: (m_ref[0], i, 0)))
    return pl.pallas_call(body, name=name, grid_spec=gs, out_shape=jax.ShapeDtypeStruct((N_CHIPS, r, cdim), dtype),
                          compiler_params=_cparams(("parallel",)))(chip, a)


def gather_weights(name, bufs):
    n = len(bufs)

    def body(*refs):
        _gather_start(refs[n:2 * n], refs[2 * n:])
        _gather_finish(refs[n:2 * n], refs[2 * n:])

    sem = pltpu.SemaphoreType.DMA
    return pl.pallas_call(
        body, name=name, in_specs=[ANY] * n, out_specs=[ANY] * n,
        out_shape=[jax.ShapeDtypeStruct(b.shape, b.dtype) for b in bufs],
        input_output_aliases={w: w for w in range(n)},
        scratch_shapes=[sem((n, 3)), sem((n, 3)), sem((n, 3)), sem((n, 3))],
        compiler_params=pltpu.CompilerParams(has_side_effects=True),
    )(*bufs)


def _half(ref, chip, core):
    hr = ref.shape[1] // 2
    return ref.at[chip, pl.ds(core * hr, hr), :]


def _gather_start(bufs, sems):
    send_a, recv_a = sems[0], sems[1]
    x, y, c, m = _pos()
    for w, buf in enumerate(bufs):
        for j, (px, py) in enumerate(_other_chips(x, y)):
            blk = _half(buf, m, c)
            _remote(blk, blk, send_a.at[w, j], recv_a.at[w, j], (px, py, c)).start()


def _gather_finish(bufs, sems):
    send_a, recv_a, send_b, recv_b = sems
    x, y, c, m = _pos()
    chips = _other_chips(x, y)
    sib = (x, y, 1 - c)
    for j, (px, py) in enumerate(chips):
        for w, buf in enumerate(bufs):
            blk = _half(buf, 2 * px + py, c)
            _remote(blk, blk, send_a.at[w, j], recv_a.at[w, j], (px, py, c)).wait_recv()
            _remote(blk, blk, send_b.at[w, j], recv_b.at[w, j], sib).start()
    for j, (px, py) in enumerate(chips):
        for w, buf in enumerate(bufs):
            blk = _half(buf, 2 * px + py, 1 - c)
            _remote(blk, blk, send_b.at[w, j], recv_b.at[w, j], sib).wait_recv()
    for j, (px, py) in enumerate(chips):
        for w, buf in enumerate(bufs):
            mine, landed = _half(buf, m, c), _half(buf, 2 * px + py, c)
            _remote(mine, mine, send_a.at[w, j], recv_a.at[w, j], (px, py, c)).wait_send()
            _remote(landed, landed, send_b.at[w, j], recv_b.at[w, j], sib).wait_send()


def gather_rider(bufs):
    n = len(bufs)
    return Rider(ins=list(bufs), outs=[jax.ShapeDtypeStruct(b.shape, b.dtype) for b in bufs],
                 alias={w: w for w in range(n)}, sems=[(n, 3)] * 4,
                 start=lambda i, o, s: _gather_start(o, s), finish=lambda i, o, s: _gather_finish(o, s))


def exchange_halves(name, arrs):
    n = len(arrs)

    def body(*refs):
        _exchange(refs[:n], refs[n:2 * n], refs[2 * n:], "start")
        _exchange(refs[:n], refs[n:2 * n], refs[2 * n:], "wait")

    sem = pltpu.SemaphoreType.DMA
    return pl.pallas_call(
        body, name=name, in_specs=[ANY] * n, out_specs=[ANY] * n, out_shape=_exchange_shapes(arrs),
        scratch_shapes=[sem((n,)), sem((n,))],
        compiler_params=pltpu.CompilerParams(has_side_effects=True),
    )(*arrs)


def _exchange_shapes(arrs):
    return [jax.ShapeDtypeStruct((a.shape[0], a.shape[1] // 2, a.shape[2]), a.dtype) for a in arrs]


def _exchange(ins, outs, sems, what):
    send, recv = sems
    x, y, c, _ = _pos()
    for w in range(len(ins)):
        hr = ins[w].shape[1] // 2
        cp = _remote(ins[w].at[:, pl.ds((1 - c) * hr, hr), :], outs[w], send.at[w], recv.at[w], (x, y, 1 - c))
        if what == "start":
            cp.start()
        else:
            cp.wait()


def exchange_rider(arrs):
    n = len(arrs)
    return Rider(ins=list(arrs), outs=_exchange_shapes(arrs), alias={}, sems=[(n,)] * 2,
                 start=lambda i, o, s: _exchange(i, o, s, "start"), finish=lambda i, o, s: _exchange(i, o, s, "wait"))


def scatter_to_chips(name, arrs):
    n = len(arrs)

    def body(*refs):
        _scatter(refs[:n], refs[n:2 * n], refs[2 * n:], "start")
        _scatter(refs[:n], refs[n:2 * n], refs[2 * n:], "wait")

    sem = pltpu.SemaphoreType.DMA
    return pl.pallas_call(
        body, name=name, in_specs=[ANY] * n, out_specs=[ANY] * n,
        out_shape=[jax.ShapeDtypeStruct((3,) + a.shape[1:], a.dtype) for a in arrs],
        scratch_shapes=[sem((n, 3)), sem((n, 3))],
        compiler_params=pltpu.CompilerParams(has_side_effects=True),
    )(*arrs)


def _scatter(ins, outs, sems, what):
    send, recv = sems
    x, y, c, _ = _pos()
    for w in range(len(ins)):
        for j, (px, py) in enumerate(_other_chips(x, y)):
            cp = _remote(ins[w].at[2 * px + py], outs[w].at[j], send.at[w, j], recv.at[w, j], (px, py, c))
            if what == "start":
                cp.start()
            else:
                cp.wait()


def scatter_rider(arrs):
    n = len(arrs)
    return Rider(ins=list(arrs), outs=[jax.ShapeDtypeStruct((3,) + a.shape[1:], a.dtype) for a in arrs],
                 alias={}, sems=[(n, 3)] * 2,
                 start=lambda i, o, s: _scatter(i, o, s, "start"), finish=lambda i, o, s: _scatter(i, o, s, "wait"))


def share_halves(name, fulls):
    n = len(fulls)

    def body(*refs):
        outs = refs[n:2 * n]
        send, recv = refs[2 * n:]
        x, y, c, _ = _pos()
        cps = []
        for w in range(n):
            hr = outs[w].shape[0] // 2
            blk = outs[w].at[pl.ds(c * hr, hr), :]
            cp = _remote(blk, blk, send.at[w], recv.at[w], (x, y, 1 - c))
            cp.start()
            cps.append(cp)
        for w, cp in enumerate(cps):
            hr = outs[w].shape[0] // 2
            cp.wait_send()
            blk = outs[w].at[pl.ds((1 - c) * hr, hr), :]
            _remote(blk, blk, send.at[w], recv.at[w], (x, y, 1 - c)).wait_recv()

    sem = pltpu.SemaphoreType.DMA
    return pl.pallas_call(
        body, name=name, in_specs=[ANY] * n, out_specs=[ANY] * n,
        out_shape=[jax.ShapeDtypeStruct(a.shape, a.dtype) for a in fulls],
        input_output_aliases={w: w for w in range(n)},
        scratch_shapes=[sem((n,)), sem((n,))],
        compiler_params=pltpu.CompilerParams(has_side_effects=True),
    )(*fulls)


def add_own_half(name, full, got, core, out_dtype):
    n, r, cdim = full.shape
    hr = r // 2
    tr = 8 * _blk(hr // 8, 32) if out_dtype == F32 else 16 * _blk(hr // 16, 16)
    nbh = hr // tr

    def body(c_ref, f_ref, g_ref, o_ref):
        o_ref[...] = (f_ref[...] + g_ref[...]).astype(o_ref.dtype)

    gs = pltpu.PrefetchScalarGridSpec(
        num_scalar_prefetch=1, grid=(n, nbh),
        in_specs=[pl.BlockSpec((None, tr, cdim), lambda s, i, c_ref: (s, c_ref[0] * nbh + i, 0)),
                  pl.BlockSpec((None, tr, cdim), lambda s, i, c_ref: (s, i, 0))],
        out_specs=pl.BlockSpec((None, tr, cdim), lambda s, i, c_ref: (s, i, 0)))
    return pl.pallas_call(body, name=name, grid_spec=gs, out_shape=jax.ShapeDtypeStruct((n, hr, cdim), out_dtype),
                          compiler_params=_cparams(("parallel", "parallel")))(core, full, got)


def sum_into_half(name, part, slots, chip_core, chip_order):
    _, hr, cdim = part.shape
    unit = 8 if part.dtype == F32 else 16
    tr = unit * _blk(hr // unit, 256 // unit)
    nbh = hr // tr

    def body(mc_ref, p_ref, s_ref, o_ref):
        terms = [p_ref[...].astype(F32)] + [s_ref[k].astype(F32) for k in range(3)]
        if chip_order:
            m = mc_ref[0]
            own, fx, fy, fxy = terms
            terms = [jnp.where((k ^ m) == 0, own, jnp.where((k ^ m) == 2, fx, jnp.where((k ^ m) == 1, fy, fxy)))
                     for k in range(N_CHIPS)]
        o_ref[...] = ((terms[0] + terms[1]) + terms[2]) + terms[3]

    gs = pltpu.PrefetchScalarGridSpec(
        num_scalar_prefetch=1, grid=(nbh,),
        in_specs=[pl.BlockSpec((None, tr, cdim), lambda i, mc: (mc[0], i, 0)),
                  pl.BlockSpec((3, tr, cdim), lambda i, mc: (0, i, 0))],
        out_specs=pl.BlockSpec((tr, cdim), lambda i, mc: (mc[1] * nbh + i, 0)))
    return pl.pallas_call(body, name=name, grid_spec=gs, out_shape=jax.ShapeDtypeStruct((2 * hr, cdim), F32),
                          compiler_params=_cparams(("parallel",)))(chip_core, part, slots)


class StepComm:
    def __init__(self, bufs, core, chip_core):
        self.bufs, self.core, self.chip_core = bufs, core, chip_core
        self.whole, self.part, self.fulls = {}, {}, {}

    def gather(self, names):
        return gather_rider([self.bufs[n] for n in names])

    def landed(self, names, bufs, p):
        for n, b in zip(names, bufs):
            if n.startswith("wdown"):
                p["wdown"][int(n[-1])] = b.reshape(-1, b.shape[-1])
            elif n.startswith("wup"):
                p["wup"][int(n[-1])] = b[:, None]
            elif n == "kv":
                p["kvw4"] = b
            elif n == "wglu":
                p["wglu4"] = b
            else:
                p[n] = b.reshape(-1, b.shape[-1])

    def swap(self, names, arrays):
        big = [a if a.ndim == 3 else a.reshape(N_CHIPS, a.shape[0] // N_CHIPS, a.shape[1]) for a in arrays]
        self.whole.update(zip(names, big))
        return exchange_rider(big)

    def swapped(self, names, got, payloads=None):
        payloads = payloads or [BF16] * len(names)
        for n, r, dt in zip(names, got, payloads):
            self.part[n] = add_own_half("rs_add_" + n, self.whole.pop(n), r, self.core, dt)

    def chips(self, names):
        return scatter_rider([self.part[n] for n in names])

    def reduce(self, names, arrays, payloads=None):
        rider = self.swap(names, arrays)
        self.swapped(names, exchange_halves("rs_siblings_" + names[0], rider.ins), payloads)
        return self.chips(names)

    def reduced(self, names, slots, chip_order=()):
        for n, s in zip(names, slots):
            self.fulls[n] = sum_into_half("rs_sum_" + n, self.part[n], s, self.chip_core, n in chip_order)

    def finish(self):
        names = list(self.fulls)
        return dict(zip(names, share_halves("rs_share", [self.fulls[n] for n in names])))


WEIGHTS = ('norm_mix', 'norm_ffn', 'norm_kv', 'norm_final', 'ssm_w_in', 'ssm_a_re', 'ssm_a_im', 'ssm_log_dt',
           'ssm_b_re', 'ssm_b_im', 'ssm_c_re', 'ssm_c_im', 'ssm_d', 'ssm_w_glu', 'kv_w', 'attn_w_q', 'attn_w_o',
           'ffn_w_up', 'ffn_conv_w', 'ffn_conv_b', 'ffn_w_down')
SMALL = ('norm_mix', 'norm_ffn', 'norm_kv', 'norm_final', 'ssm_a_re', 'ssm_a_im', 'ssm_log_dt', 'ssm_b_re', 'ssm_b_im',
         'ssm_c_re', 'ssm_c_im', 'ssm_d', 'ffn_conv_w', 'ffn_conv_b')


def _pad_rows(flat, unit):
    n = flat.shape[0]
    total = -(-n // unit) * unit
    return jnp.pad(flat, (0, total - n)).reshape(total // LANES, LANES)


def kernel(x, norm_mix, norm_ffn, norm_kv, norm_final, ssm_w_in, ssm_a_re, ssm_a_im, ssm_log_dt, ssm_b_re, ssm_b_im, ssm_c_re, ssm_c_im, ssm_d, ssm_w_glu, kv_w, attn_w_q, attn_w_o, ffn_w_up, ffn_conv_w, ffn_conv_b, ffn_w_down, loss_target, m_norm_mix, m_norm_ffn, m_norm_kv, m_norm_final, m_ssm_w_in, m_ssm_a_re, m_ssm_a_im, m_ssm_log_dt, m_ssm_b_re, m_ssm_b_im, m_ssm_c_re, m_ssm_c_im, m_ssm_d, m_ssm_w_glu, m_kv_w, m_attn_w_q, m_attn_w_o, m_ffn_w_up, m_ffn_conv_w, m_ffn_conv_b, m_ffn_w_down, v_norm_mix, v_norm_ffn, v_norm_kv, v_norm_final, v_ssm_w_in, v_ssm_a_re, v_ssm_a_im, v_ssm_log_dt, v_ssm_b_re, v_ssm_b_im, v_ssm_c_re, v_ssm_c_im, v_ssm_d, v_ssm_w_glu, v_kv_w, v_attn_w_q, v_attn_w_o, v_ffn_w_up, v_ffn_conv_w, v_ffn_conv_b, v_ffn_w_down):
    a = dict(locals())
    l, d = x.shape[1], x.shape[2]
    f = ffn_conv_b.shape[1]
    fs = f // N_CHIPS
    m = 2 * lax.axis_index("x") + lax.axis_index("y")
    core = lax.axis_index("c").astype(jnp.int32).reshape(1)
    chip = m.astype(jnp.int32).reshape(1)
    chip_core = jnp.concatenate([chip, core])

    shards = {"win": ssm_w_in[0], "wglu": ssm_w_glu[0], "kv": kv_w, "wq": attn_w_q[0], "wo": attn_w_o[0],
              "wup0": ffn_w_up[0], "wup1": ffn_w_up[1], "wdown0": ffn_w_down[0], "wdown1": ffn_w_down[1],
              "convw": _pad_rows(ffn_conv_w.reshape(-1), 16 * LANES)}
    bufs = {k: cast_into_slot(f"cast_{k}", s, chip, F32 if k == "convw" else BF16) for k, s in shards.items()}
    cm = StepComm(bufs, core, chip_core)
    g_in, g_cw = gather_weights("gather_first", [bufs["win"], bufs["convw"]])
    conv_w = g_cw.reshape(N_CHIPS, -1)[:, :2 * 3 * fs].reshape(N_CHIPS, 2, 3, fs).transpose(1, 2, 0, 3).reshape(2, 3, f)
    p = dict(
        norm_mix=norm_mix, norm_ffn=norm_ffn, norm_kv=norm_kv, norm_final=norm_final,
        a_re=ssm_a_re[0], a_im=ssm_a_im[0], log_dt=ssm_log_dt[0], b_re=ssm_b_re[0], b_im=ssm_b_im[0],
        c_re=ssm_c_re[0], c_im=ssm_c_im[0], d=ssm_d[0],
        win=g_in.reshape(-1, g_in.shape[-1]), wup=[None, None], wdown=[None, None],
        conv_w=conv_w, conv_b=ffn_conv_b)

    loss_slab, dx, gr = _local_step(x[0], loss_target[0], p, cm)

    small = {"norm_mix": gr["norm_mix"], "norm_ffn": gr["norm_ffn"], "norm_kv": gr["norm_kv"], "norm_final": gr["norm_final"],
             "ssm_a_re": gr["a_re"], "ssm_a_im": gr["a_im"], "ssm_log_dt": gr["log_dt"], "ssm_b_re": gr["b_re"],
             "ssm_b_im": gr["b_im"], "ssm_c_re": gr["c_re"], "ssm_c_im": gr["c_im"], "ssm_d": gr["d"],
             "ffn_conv_w": gr["conv_w"], "ffn_conv_b": gr["conv_b"]}
    packed = _pad_rows(jnp.concatenate([small[k].reshape(-1) for k in SMALL] + [loss_slab[0, 0:1]]), 16 * LANES)
    last = cm.reduce(["win", "small"], [gr["win"], jnp.broadcast_to(packed, (N_CHIPS,) + packed.shape)], [BF16, F32])
    cm.reduced(["win", "small"], scatter_to_chips("rs_chips_last", last.ins), chip_order=("small",))
    r = cm.finish()
    grads = {"ssm_w_in": r["win"][None], "ssm_w_glu": r["wglu"][None], "kv_w": r["kv"], "attn_w_q": r["wq"][None],
             "attn_w_o": r["wo"][None], "ffn_w_up": jnp.stack([r["wup0"], r["wup1"]]),
             "ffn_w_down": jnp.stack([r["wdown0"], r["wdown1"]])}
    total = r["small"].reshape(-1)
    off = 0
    for k in SMALL:
        n = small[k].size
        full = total[off:off + n].reshape(small[k].shape)
        off += n
        if k == "ffn_conv_w":
            full = lax.dynamic_slice_in_dim(full, m * fs, fs, axis=2)
        grads[k] = full.reshape(a[k].shape)
    loss = total[off]

    outs = {}
    for k in WEIGHTS:
        outs[k] = adamw(f"adamw_{k}", a[k], grads[k], a["m_" + k], a["v_" + k])
    return (loss, dx[None], *[grads[k] for k in WEIGHTS], *[outs[k][0] for k in WEIGHTS],
            *[outs[k][1] for k in WEIGHTS], *[outs[k][2] for k in WEIGHTS])
```

```python
import functools
import math

import jax
import jax.numpy as jnp
from jax import lax
from jax.experimental import pallas as pl
from jax.experimental.pallas import tpu as pltpu

F32 = jnp.float32
BF16 = jnp.bfloat16

EPS = 1e-6
SSM_GROUP = 16
STATE = 64
HEAD_DIM = 64
N_STREAMS = 8
MXU_DIM = 256
VMEM_LIMIT = 56 * 1024 * 1024

ADAM_LR = 0.001
ADAM_B1 = 0.9
ADAM_B2 = 0.999
ADAM_EPS = 1e-08
ADAM_WD = 0.01
ADAM_STEP = 10


def _cparams(sem=None):
    return pltpu.CompilerParams(dimension_semantics=sem, vmem_limit_bytes=VMEM_LIMIT)


class Rider:
    def __init__(self, ins, outs, alias, sems, start, finish):
        self.ins, self.outs, self.alias, self.sems, self.start, self.finish = ins, outs, alias, sems, start, finish


def _call(name, body, *, grid, in_specs, out_specs, out_shape, args, scratch=(), sem=None, rider=None):
    if rider is None:
        return pl.pallas_call(body, name=name, grid=grid, in_specs=in_specs, out_specs=out_specs, out_shape=out_shape,
                              scratch_shapes=list(scratch), compiler_params=_cparams(sem))(*args)
    nin, nout, nscr = len(in_specs), len(out_specs), len(scratch)
    nri, nro = len(rider.ins), len(rider.outs)
    steps = math.prod(grid)

    def hosted(*refs):
        ins, refs = refs[:nin], refs[nin:]
        r_in, refs = refs[:nri], refs[nri:]
        outs, refs = refs[:nout], refs[nout:]
        r_out, refs = refs[:nro], refs[nro:]
        scr, sems = refs[:nscr], refs[nscr:]
        lin = 0
        for ax, size in enumerate(grid):
            lin = lin * size + pl.program_id(ax)

        @pl.when(lin == 0)
        def _():
            rider.start(r_in, r_out, sems)

        body(*ins, *outs, *scr)

        @pl.when(lin == steps - 1)
        def _():
            rider.finish(r_in, r_out, sems)

    any_spec = pl.BlockSpec(memory_space=pl.ANY)
    dma = pltpu.SemaphoreType.DMA
    return pl.pallas_call(
        hosted, name=name, grid=grid, in_specs=list(in_specs) + [any_spec] * nri,
        out_specs=list(out_specs) + [any_spec] * nro, out_shape=list(out_shape) + list(rider.outs),
        input_output_aliases={nin + i: nout + o for i, o in rider.alias.items()},
        scratch_shapes=list(scratch) + [dma(s) for s in rider.sems],
        compiler_params=pltpu.CompilerParams(dimension_semantics=("arbitrary",) * len(grid),
                                             vmem_limit_bytes=VMEM_LIMIT, has_side_effects=True),
    )(*args, *rider.ins)


def _blk(n, want):
    b = min(n, want)
    while n % b:
        b -= 1
    return b


NN = (((1,), (0,)), ((), ()))
NT = (((1,), (1,)), ((), ()))
TN = (((0,), (0,)), ((), ()))


def _matmul(name, a, b, *, a_spec, b_spec, o_spec, out_shape, grid, dn, nk, out_dtype, rider=None):
    nax = len(grid)

    def body(a_ref, b_ref, o_ref, *scr):
        part = lax.dot_general(a_ref[...], b_ref[...], dn, preferred_element_type=F32)
        if nk == 1:
            o_ref[...] = part.astype(o_ref.dtype)
            return
        acc = scr[0] if scr else o_ref
        k = pl.program_id(nax - 1)

        @pl.when(k == 0)
        def _():
            acc[...] = part

        @pl.when(k > 0)
        def _():
            acc[...] += part

        if scr:
            @pl.when(k == nk - 1)
            def _():
                o_ref[...] = acc[...].astype(o_ref.dtype)

    scratch = []
    if nk > 1 and out_dtype != F32:
        blk = tuple(d for d in o_spec.block_shape if d is not None)
        scratch = [pltpu.VMEM(blk, F32)]
    sem = ("parallel",) * (nax - 1) + ("arbitrary",)
    res = _call(name, body, grid=grid, in_specs=[a_spec, b_spec], out_specs=[o_spec],
                out_shape=[jax.ShapeDtypeStruct(out_shape, out_dtype)], scratch=scratch, sem=sem, args=(a, b), rider=rider)
    return res[0] if rider is None else (res[0], res[1:])


MM_TM = 1024
MM_TK = 2048


def mm_nn(name, a, w, out_dtype, tm=MM_TM):
    m, k = a.shape
    n = w.shape[1]
    tm = _blk(m, tm)
    return _matmul(name, a, w, a_spec=pl.BlockSpec((tm, k), lambda i, kk: (i, 0)),
                   b_spec=pl.BlockSpec((k, n), lambda i, kk: (0, 0)),
                   o_spec=pl.BlockSpec((tm, n), lambda i, kk: (i, 0)),
                   out_shape=(m, n), grid=(m // tm, 1), dn=NN, nk=1, out_dtype=out_dtype)


def mm_nt(name, a, w, out_dtype, tm=MM_TM, tn=None):
    m, n = a.shape
    k = w.shape[0]
    tm = _blk(m, tm)
    tn = k if tn is None else tn
    return _matmul(name, a, w, a_spec=pl.BlockSpec((tm, n), lambda i, j, kk: (i, 0)),
                   b_spec=pl.BlockSpec((tn, n), lambda i, j, kk: (j, 0)),
                   o_spec=pl.BlockSpec((tm, tn), lambda i, j, kk: (i, j)),
                   out_shape=(m, k), grid=(m // tm, k // tn, 1), dn=NT, nk=1, out_dtype=out_dtype)


def mm_tn(name, a, b, tmo=None, tk=MM_TK):
    l, m = a.shape
    n = b.shape[1]
    tk = _blk(l, tk)
    tmo = m if tmo is None else tmo
    nk = l // tk
    return _matmul(name, a, b, a_spec=pl.BlockSpec((tk, tmo), lambda i, kk: (kk, i)),
                   b_spec=pl.BlockSpec((tk, n), lambda i, kk: (kk, 0)),
                   o_spec=pl.BlockSpec((tmo, n), lambda i, kk: (i, 0)),
                   out_shape=(m, n), grid=(m // tmo, nk), dn=TN, nk=nk, out_dtype=F32)


def mm_nn_colshard(name, a, w4, out_dtype, tm=MM_TM):
    m, k = a.shape
    s, _, ns = w4.shape
    tm = _blk(m, tm)
    return _matmul(name, a, w4, a_spec=pl.BlockSpec((tm, k), lambda i, j, kk: (i, 0)),
                   b_spec=pl.BlockSpec((None, k, ns), lambda i, j, kk: (j, 0, 0)),
                   o_spec=pl.BlockSpec((tm, ns), lambda i, j, kk: (i, j)),
                   out_shape=(m, s * ns), grid=(m // tm, s, 1), dn=NN, nk=1, out_dtype=out_dtype)


def mm_nt_colshard(name, a, w4, out_dtype, tm=MM_TM):
    m, _ = a.shape
    s, k, ns = w4.shape
    tm = _blk(m, tm)
    return _matmul(name, a, w4, a_spec=pl.BlockSpec((tm, ns), lambda i, kk: (i, kk)),
                   b_spec=pl.BlockSpec((None, k, ns), lambda i, kk: (kk, 0, 0)),
                   o_spec=pl.BlockSpec((tm, k), lambda i, kk: (i, 0)),
                   out_shape=(m, k), grid=(m // tm, s), dn=NT, nk=s, out_dtype=out_dtype)


def mm_tn_colshard(name, a, b, s, tk=MM_TK):
    l, k = a.shape
    ns = b.shape[1] // s
    tk = _blk(l, tk)
    nk = l // tk
    return _matmul(name, a, b, a_spec=pl.BlockSpec((tk, k), lambda j, kk: (kk, 0)),
                   b_spec=pl.BlockSpec((tk, ns), lambda j, kk: (kk, j)),
                   o_spec=pl.BlockSpec((None, k, ns), lambda j, kk: (j, 0, 0)),
                   out_shape=(s, k, ns), grid=(s, nk), dn=TN, nk=nk, out_dtype=F32)


def mm_up(name, h, wup4, layer, tm=MM_TM, rider=None):
    m, k = h.shape
    s, _, _, ns = wup4.shape
    half = s // 2
    tm = _blk(m, tm)
    return _matmul(name, h, wup4, a_spec=pl.BlockSpec((tm, k), lambda i, j, kk: (i, 0)),
                   b_spec=pl.BlockSpec((None, None, k, ns), lambda i, j, kk: (j, layer, 0, 0)),
                   o_spec=pl.BlockSpec((None, tm, ns), lambda i, j, kk: (j // half, i, j % half)),
                   out_shape=(2, m, half * ns), grid=(m // tm, s, 1), dn=NN, nk=1, out_dtype=BF16, rider=rider)


def mm_up_nt(name, dgu, wup4, layer, tm=MM_TM, rider=None):
    _, m, _ = dgu.shape
    s, _, k, ns = wup4.shape
    half = s // 2
    tm = _blk(m, tm)
    return _matmul(name, dgu, wup4,
                   a_spec=pl.BlockSpec((None, tm, ns), lambda i, kk: (kk // half, i, kk % half)),
                   b_spec=pl.BlockSpec((None, None, k, ns), lambda i, kk: (kk, layer, 0, 0)),
                   o_spec=pl.BlockSpec((tm, k), lambda i, kk: (i, 0)),
                   out_shape=(m, k), grid=(m // tm, s), dn=NT, nk=s, out_dtype=F32, rider=rider)


def mm_up_tn(name, h, dgu, s, tk=MM_TK):
    l, k = h.shape
    half = s // 2
    ns = dgu.shape[2] // half
    tk = _blk(l, tk)
    nk = l // tk
    return _matmul(name, h, dgu, a_spec=pl.BlockSpec((tk, k), lambda j, kk: (kk, 0)),
                   b_spec=pl.BlockSpec((None, tk, ns), lambda j, kk: (j // half, kk, j % half)),
                   o_spec=pl.BlockSpec((None, k, ns), lambda j, kk: (j, 0, 0)),
                   out_shape=(s, k, ns), grid=(s, nk), dn=TN, nk=nk, out_dtype=F32)


def add_rmsnorm(name, x, r, gains, tm=512):
    l, d = x.shape
    tm = _blk(l, tm)
    ng = len(gains)
    has_r = r is not None

    def body(*refs):
        x_ref = refs[0]
        pos = 1
        xs = x_ref[...]
        if has_r:
            xs = xs + refs[pos][...]
            pos += 1
        g_refs = refs[pos:pos + ng]
        outs = refs[pos + ng:]
        o = 0
        if has_r:
            outs[0][...] = xs
            o = 1
        xh = xs * lax.rsqrt(jnp.mean(xs * xs, axis=-1, keepdims=True) + EPS)
        for gi in range(ng):
            outs[o + gi][...] = (xh * g_refs[gi][...]).astype(BF16)

    row = pl.BlockSpec((tm, d), lambda i: (i, 0))
    gsp = pl.BlockSpec((1, d), lambda i: (0, 0))
    ins = [x] + ([r] if has_r else []) + [g.reshape(1, d) for g in gains]
    in_specs = [row] * (1 + has_r) + [gsp] * ng
    out_shape = ([jax.ShapeDtypeStruct((l, d), F32)] if has_r else []) + [jax.ShapeDtypeStruct((l, d), BF16)] * ng
    return pl.pallas_call(body, name=name, grid=(l // tm,), in_specs=in_specs,
                          out_specs=[row] * len(out_shape), out_shape=out_shape,
                          compiler_params=_cparams(("parallel",)))(*ins)


def norm_bwd(name, x, dres, pairs, tm=512, bf16_copy=False, rider=None):
    l, d = x.shape
    tm = _blk(l, tm)
    npair = len(pairs)
    has_r = dres is not None

    def body(*refs):
        x_ref = refs[0]
        pos = 1
        xs = x_ref[...]
        dx = jnp.zeros_like(xs)
        if has_r:
            dx = refs[pos][...]
            pos += 1
        ins = refs[pos:pos + 2 * npair]
        outs = refs[pos + 2 * npair:]
        rs = lax.rsqrt(jnp.mean(xs * xs, axis=-1, keepdims=True) + EPS)
        xh = xs * rs
        first = pl.program_id(0) == 0
        for pi in range(npair):
            g = ins[2 * pi][...]
            dh = ins[2 * pi + 1][...].astype(F32)
            dgp = jnp.sum(dh * xh, axis=0, keepdims=True)
            dg_ref = outs[1 + pi]

            @pl.when(first)
            def _():
                dg_ref[...] = dgp

            @pl.when(jnp.logical_not(first))
            def _():
                dg_ref[...] += dgp

            dxh = dh * g
            dx = dx + rs * (dxh - xh * jnp.mean(dxh * xh, axis=-1, keepdims=True))
        outs[0][...] = dx
        if bf16_copy:
            outs[1 + npair][...] = dx.astype(BF16)

    row = pl.BlockSpec((tm, d), lambda i: (i, 0))
    gsp = pl.BlockSpec((1, d), lambda i: (0, 0))
    ins = [x] + ([dres] if has_r else [])
    in_specs = [row] * (1 + has_r)
    for g, dh in pairs:
        ins += [g.reshape(1, d), dh]
        in_specs += [gsp, row]
    out_shape = [jax.ShapeDtypeStruct((l, d), F32)] + [jax.ShapeDtypeStruct((1, d), F32)] * npair
    out_specs = [row] + [gsp] * npair
    if bf16_copy:
        out_shape.append(jax.ShapeDtypeStruct((l, d), BF16))
        out_specs.append(row)
    return _call(name, body, grid=(l // tm,), in_specs=in_specs, out_specs=out_specs, out_shape=out_shape,
                 sem=("arbitrary",), args=ins, rider=rider)


def final_loss(name, x, r, tgt, g, tm=512):
    l, d = x.shape
    tm = _blk(l, tm)

    def body(x_ref, r_ref, t_ref, g_ref, loss_ref, dx_ref, dg_ref, dxb_ref):
        xs = x_ref[...] + r_ref[...]
        gg = g_ref[...]
        rs = lax.rsqrt(jnp.mean(xs * xs, axis=-1, keepdims=True) + EPS)
        xh = xs * rs
        e = xh * gg - t_ref[...]
        part = 0.5 * jnp.sum(jnp.mean(e * e, axis=-1, keepdims=True), axis=0, keepdims=True)
        dy = e * (1.0 / d)
        dgp = jnp.sum(dy * xh, axis=0, keepdims=True)
        dxh = dy * gg
        dx = rs * (dxh - xh * jnp.mean(dxh * xh, axis=-1, keepdims=True))
        dx_ref[...] = dx
        dxb_ref[...] = dx.astype(BF16)
        first = pl.program_id(0) == 0

        @pl.when(first)
        def _():
            loss_ref[...] = jnp.broadcast_to(part, loss_ref.shape)
            dg_ref[...] = dgp

        @pl.when(jnp.logical_not(first))
        def _():
            loss_ref[...] += jnp.broadcast_to(part, loss_ref.shape)
            dg_ref[...] += dgp

    row = pl.BlockSpec((tm, d), lambda i: (i, 0))
    gsp = pl.BlockSpec((1, d), lambda i: (0, 0))
    lsp = pl.BlockSpec((8, 128), lambda i: (0, 0))
    return pl.pallas_call(
        body, name=name, grid=(l // tm,), in_specs=[row, row, row, gsp], out_specs=[lsp, row, gsp, row],
        out_shape=[jax.ShapeDtypeStruct((8, 128), F32), jax.ShapeDtypeStruct((l, d), F32),
                   jax.ShapeDtypeStruct((1, d), F32), jax.ShapeDtypeStruct((l, d), BF16)],
        compiler_params=_cparams(("arbitrary",)))(x, r, tgt, g.reshape(1, d))


def _sigmoid(x):
    return 0.5 * jnp.tanh(0.5 * x) + 0.5


def glu_fwd(name, z, tm=512):
    l, d2 = z.shape
    d = d2 // 2
    tm = _blk(l, tm)

    def body(z_ref, o_ref):
        o_ref[...] = z_ref[:, :d] * _sigmoid(z_ref[:, d:])

    return pl.pallas_call(body, name=name, grid=(l // tm,),
                          in_specs=[pl.BlockSpec((tm, d2), lambda i: (i, 0))],
                          out_specs=pl.BlockSpec((tm, d), lambda i: (i, 0)),
                          out_shape=jax.ShapeDtypeStruct((l, d), F32),
                          compiler_params=_cparams(("parallel",)))(z)


def glu_bwd(name, z, dm, tm=512):
    l, d2 = z.shape
    d = d2 // 2
    tm = _blk(l, tm)

    def body(z_ref, dm_ref, o_ref):
        sg = _sigmoid(z_ref[:, d:])
        g = dm_ref[...]
        o_ref[:, :d] = (g * sg).astype(BF16)
        o_ref[:, d:] = (g * z_ref[:, :d] * sg * (1.0 - sg)).astype(BF16)

    return pl.pallas_call(body, name=name, grid=(l // tm,),
                          in_specs=[pl.BlockSpec((tm, d2), lambda i: (i, 0)), pl.BlockSpec((tm, d), lambda i: (i, 0))],
                          out_specs=pl.BlockSpec((tm, d2), lambda i: (i, 0)),
                          out_shape=jax.ShapeDtypeStruct((l, d2), BF16),
                          compiler_params=_cparams(("parallel",)))(z, dm)


_GELU_C = math.sqrt(2.0 / math.pi)


def _gelu(y):
    return 0.5 * y * (1.0 + jnp.tanh(_GELU_C * (y + 0.044715 * y * y * y)))


def _gelu_grad(y):
    t = jnp.tanh(_GELU_C * (y + 0.044715 * y * y * y))
    return 0.5 * (1.0 + t) + 0.5 * y * (1.0 - t * t) * _GELU_C * (1.0 + 3.0 * 0.044715 * y * y)


def gelu_bwd(name, ypre, dyg, tm=512):
    l, d = ypre.shape
    tm = _blk(l, tm)

    def body(y_ref, d_ref, o_ref):
        o_ref[...] = (d_ref[...] * _gelu_grad(y_ref[...])).astype(BF16)

    row = pl.BlockSpec((tm, d), lambda i: (i, 0))
    return pl.pallas_call(body, name=name, grid=(l // tm,), in_specs=[row, row], out_specs=row,
                          out_shape=jax.ShapeDtypeStruct((l, d), BF16),
                          compiler_params=_cparams(("parallel",)))(ypre, dyg)


FFN_ROWS = 128
HALO = 16


def _taps_back(tail, g):
    ext = jnp.concatenate([tail, g], axis=0)
    return pltpu.roll(ext, 1, axis=0)[HALO:], pltpu.roll(ext, 2, axis=0)[HALO:]


def ffn_mid_fwd(name, gu, cw, cb, tc=128):
    _, l, f = gu.shape
    tc = _blk(f, tc)

    def body(gu_ref, w_ref, b_ref, a_ref):
        g = gu_ref[0].astype(F32)
        u = gu_ref[1].astype(F32)
        g1, g2 = _taps_back(jnp.zeros((HALO, tc), F32), g)
        gc = w_ref[0:1, :] * g2 + w_ref[1:2, :] * g1 + w_ref[2:3, :] * g + b_ref[...]
        a_ref[...] = (gc * _sigmoid(gc) * u).astype(BF16)

    return pl.pallas_call(
        body, name=name, grid=(f // tc,),
        in_specs=[pl.BlockSpec((2, l, tc), lambda c: (0, 0, c)), pl.BlockSpec((3, tc), lambda c: (0, c)),
                  pl.BlockSpec((1, tc), lambda c: (0, c))],
        out_specs=pl.BlockSpec((l, tc), lambda c: (0, c)),
        out_shape=jax.ShapeDtypeStruct((l, f), BF16),
        compiler_params=_cparams(("parallel",)))(gu, cw, cb)


def ffn_mid_bwd(name, gu, da, cw, cb, tc=128):
    _, l, f = gu.shape
    tc = _blk(f, tc)
    rc = _blk(l, FFN_ROWS)
    nchunk = l // rc

    def body(gu_ref, da_ref, w_ref, b_ref, dgu_ref, dw_ref, db_ref):
        w0, w1, w2, b = w_ref[0:1, :], w_ref[1:2, :], w_ref[2:3, :], b_ref[...]

        def fold(x):
            return jnp.sum(x.reshape(rc // 8, 8, tc), axis=0)

        def chunk(k, carry):
            head, s0, s1, s2, sb = carry
            ci = nchunk - 1 - k
            r0 = pl.multiple_of(ci * rc, rc)
            t0 = pl.multiple_of(jnp.maximum(r0 - HALO, 0), HALO)
            tail = jnp.where(ci > 0, gu_ref[0, pl.ds(t0, HALO), :].astype(F32), 0.0)
            g = gu_ref[0, pl.ds(r0, rc), :].astype(F32)
            u = gu_ref[1, pl.ds(r0, rc), :].astype(F32)
            da = da_ref[pl.ds(r0, rc), :].astype(F32)
            g1, g2 = _taps_back(tail, g)
            gc = w0 * g2 + w1 * g1 + w2 * g + b
            sg = _sigmoid(gc)
            dgu_ref[1, pl.ds(r0, rc), :] = (da * gc * sg).astype(BF16)
            dgc = da * u * (sg * (1.0 + gc * (1.0 - sg)))
            ext = jnp.concatenate([dgc, head], axis=0)
            up1 = pltpu.roll(ext, rc + HALO - 1, axis=0)[:rc]
            up2 = pltpu.roll(ext, rc + HALO - 2, axis=0)[:rc]
            dgu_ref[0, pl.ds(r0, rc), :] = (w2 * dgc + w1 * up1 + w0 * up2).astype(BF16)
            return dgc[:HALO], s0 + fold(dgc * g2), s1 + fold(dgc * g1), s2 + fold(dgc * g), sb + fold(dgc)

        z8 = jnp.zeros((8, tc), F32)
        _, s0, s1, s2, sb = lax.fori_loop(0, nchunk, chunk, (jnp.zeros((HALO, tc), F32), z8, z8, z8, z8))
        dw_ref[0:1, :] = jnp.sum(s0, axis=0, keepdims=True)
        dw_ref[1:2, :] = jnp.sum(s1, axis=0, keepdims=True)
        dw_ref[2:3, :] = jnp.sum(s2, axis=0, keepdims=True)
        db_ref[...] = jnp.sum(sb, axis=0, keepdims=True)

    return pl.pallas_call(
        body, name=name, grid=(f // tc,),
        in_specs=[pl.BlockSpec((2, l, tc), lambda c: (0, 0, c)), pl.BlockSpec((l, tc), lambda c: (0, c)),
                  pl.BlockSpec((3, tc), lambda c: (0, c)), pl.BlockSpec((1, tc), lambda c: (0, c))],
        out_specs=[pl.BlockSpec((2, l, tc), lambda c: (0, 0, c)), pl.BlockSpec((3, tc), lambda c: (0, c)),
                   pl.BlockSpec((1, tc), lambda c: (0, c))],
        out_shape=[jax.ShapeDtypeStruct((2, l, f), BF16), jax.ShapeDtypeStruct((3, f), F32),
                   jax.ShapeDtypeStruct((1, f), F32)],
        compiler_params=_cparams(("parallel",)))(gu, da, cw, cb)


SCAN_LANES = 512


def ssm_discretize(name, a_re, a_im, ldt, b_re, b_im):
    dt = jnp.exp(ldt)
    mag = jnp.exp(a_re * dt)
    ab_re = mag * jnp.cos(a_im * dt)
    ab_im = mag * jnp.sin(a_im * dt)
    den = a_re * a_re + a_im * a_im
    f_re = ((ab_re - 1.0) * a_re + ab_im * a_im) / den
    f_im = (ab_im * a_re - (ab_re - 1.0) * a_im) / den
    return ab_re, ab_im, f_re * b_re - f_im * b_im, f_re * b_im + f_im * b_re


def ssm_disc_fwd(name, a_re, a_im, ldt, b_re, b_im):
    def body(ar, ai, ld, br, bi, o_ar, o_ai, o_br, o_bi):
        r = ssm_discretize(None, ar[...], ai[...], ld[...], br[...], bi[...])
        o_ar[...], o_ai[...], o_br[...], o_bi[...] = r

    sd = jax.ShapeDtypeStruct
    return pl.pallas_call(body, name=name,
                          out_shape=[sd(a_re.shape, F32), sd(a_re.shape, F32), sd(b_re.shape, F32), sd(b_re.shape, F32)],
                          compiler_params=_cparams())(a_re, a_im, ldt, b_re, b_im)


def ssm_disc_bwd(name, a_re, a_im, ldt, b_re, b_im, d_ar, d_ai, d_br, d_bi):
    def body(ar, ai, ld, br, bi, g_ar, g_ai, g_br, g_bi, o_ar, o_ai, o_ld, o_br, o_bi):
        fn = functools.partial(ssm_discretize, None)
        _, vjp = jax.vjp(fn, ar[...], ai[...], ld[...], br[...], bi[...])
        r = vjp((g_ar[...], g_ai[...], g_br[...], g_bi[...]))
        o_ar[...], o_ai[...], o_ld[...], o_br[...], o_bi[...] = r

    sd = jax.ShapeDtypeStruct
    return pl.pallas_call(body, name=name,
                          out_shape=[sd(a_re.shape, F32)] * 3 + [sd(b_re.shape, F32)] * 2,
                          compiler_params=_cparams())(a_re, a_im, ldt, b_re, b_im, d_ar, d_ai, d_br, d_bi)


def _drive(x_ref, wre_ref, wim_ref, dre_ref, dim_ref):
    xb = x_ref[...]
    dre_ref[...] = jnp.dot(xb, wre_ref[...], preferred_element_type=F32)
    dim_ref[...] = jnp.dot(xb, wim_ref[...], preferred_element_type=F32)


def _ssm_dims(l, wre, tj):
    nb, kb, nsb = wre.shape
    jn = l // N_STREAMS
    tj = _blk(jn, tj)
    return nb, kb, nsb, jn, tj, tj * N_STREAMS, jn // tj


def _scan(dre_ref, dim_ref, st_re, st_im, a_ref, tj, reverse, write):
    ns = dre_ref.shape[1]
    cw = min(SCAN_LANES, ns)
    for cb in range(ns // cw):
        sl = slice(cb * cw, (cb + 1) * cw)
        ar = jnp.broadcast_to(a_ref[0:1, sl], (N_STREAMS, cw))
        ai = jnp.broadcast_to(a_ref[1:2, sl], (N_STREAMS, cw))

        def step(jj, carry, sl=sl, ar=ar, ai=ai):
            sr, si = carry
            j = (tj - 1 - jj) if reverse else jj
            off = pl.multiple_of(j * N_STREAMS, N_STREAMS)
            nr = ar * sr - ai * si + dre_ref[pl.ds(off, N_STREAMS), sl]
            ni = ar * si + ai * sr + dim_ref[pl.ds(off, N_STREAMS), sl]
            if write:
                dre_ref[pl.ds(off, N_STREAMS), sl] = nr
                dim_ref[pl.ds(off, N_STREAMS), sl] = ni
            return nr, ni

        sr, si = lax.fori_loop(0, tj, step, (st_re[:, sl], st_im[:, sl]))
        st_re[:, sl] = sr
        st_im[:, sl] = si


def ssm_pass1(name, x, wre, wim, a2, *, reverse, tj=128, rider=None):
    l, d = x.shape
    nb, kb, nsb, jn, tj, r, nblk = _ssm_dims(l, wre, tj)
    ns = nb * nsb

    def body(x_ref, wre_ref, wim_ref, a_ref, cre_ref, cim_ref, dre, dim, st_re, st_im):
        i = pl.program_id(1)

        @pl.when(i == 0)
        def _():
            st_re[...] = jnp.zeros_like(st_re)
            st_im[...] = jnp.zeros_like(st_im)

        _drive(x_ref, wre_ref, wim_ref, dre, dim)
        _scan(dre, dim, st_re, st_im, a_ref, tj, reverse, False)

        @pl.when(i == nblk - 1)
        def _():
            pr, pi = a_ref[0:1, :], a_ref[1:2, :]
            rr, ri = jnp.ones_like(pr), jnp.zeros_like(pr)
            e = jn
            while e:
                if e & 1:
                    rr, ri = rr * pr - ri * pi, rr * pi + ri * pr
                pr, pi = pr * pr - pi * pi, 2.0 * pr * pi
                e >>= 1
            order = range(N_STREAMS - 1, -1, -1) if reverse else range(N_STREAMS)
            cr = jnp.zeros_like(rr)
            ci = jnp.zeros_like(rr)
            for s in order:
                cre_ref[s:s + 1, :] = cr
                cim_ref[s:s + 1, :] = ci
                fr, fi = st_re[s:s + 1, :], st_im[s:s + 1, :]
                cr, ci = fr + rr * cr - ri * ci, fi + rr * ci + ri * cr

    blk = (lambda b, i: (nblk - 1 - i, b)) if reverse else (lambda b, i: (i, b))
    w3 = pl.BlockSpec((None, kb, nsb), lambda b, i: (b, 0, 0))
    st = pl.BlockSpec((N_STREAMS, nsb), lambda b, i: (0, b))
    return _call(
        name, body, grid=(nb, nblk),
        in_specs=[pl.BlockSpec((r, kb), blk), w3, w3, pl.BlockSpec((2, nsb), lambda b, i: (0, b))],
        out_specs=[st, st], out_shape=[jax.ShapeDtypeStruct((N_STREAMS, ns), F32)] * 2,
        scratch=[pltpu.VMEM((r, nsb), F32), pltpu.VMEM((r, nsb), F32),
                 pltpu.VMEM((N_STREAMS, nsb), F32), pltpu.VMEM((N_STREAMS, nsb), F32)],
        sem=("parallel", "arbitrary"), args=(x, wre, wim, a2), rider=rider)


def ssm_fwd2(name, u, bre, bim, a2, init_re, init_im, cre, cim, dskip, *, tj=128, rider=None):
    l, d = u.shape
    nb, kb, nsb, jn, tj, r, nblk = _ssm_dims(l, bre, tj)
    ns = nb * nsb

    def body(u_ref, bre_ref, bim_ref, a_ref, ire_ref, iim_ref, cre_ref, cim_ref, d_ref,
             sre_ref, sim_ref, y_ref, yg_ref, dre, dim, st_re, st_im):
        @pl.when(pl.program_id(1) == 0)
        def _():
            st_re[...] = ire_ref[...]
            st_im[...] = iim_ref[...]

        _drive(u_ref, bre_ref, bim_ref, dre, dim)
        _scan(dre, dim, st_re, st_im, a_ref, tj, False, True)
        srb, sib = dre[...].astype(BF16), dim[...].astype(BF16)
        sre_ref[...] = srb
        sim_ref[...] = sib
        y = (jnp.dot(srb, cre_ref[...], preferred_element_type=F32)
             - jnp.dot(sib, cim_ref[...], preferred_element_type=F32)
             + d_ref[...] * u_ref[...].astype(F32))
        y_ref[...] = y
        yg_ref[...] = _gelu(y).astype(BF16)

    rows = lambda w: pl.BlockSpec((r, w), lambda b, i: (i, b))
    w3 = pl.BlockSpec((None, kb, nsb), lambda b, i: (b, 0, 0))
    c3 = pl.BlockSpec((None, nsb, kb), lambda b, i: (b, 0, 0))
    st = pl.BlockSpec((N_STREAMS, nsb), lambda b, i: (0, b))
    return _call(
        name, body, grid=(nb, nblk),
        in_specs=[rows(kb), w3, w3, pl.BlockSpec((2, nsb), lambda b, i: (0, b)), st, st, c3, c3,
                  pl.BlockSpec((1, kb), lambda b, i: (0, b))],
        out_specs=[rows(nsb), rows(nsb), rows(kb), rows(kb)],
        out_shape=[jax.ShapeDtypeStruct((l, ns), BF16), jax.ShapeDtypeStruct((l, ns), BF16),
                   jax.ShapeDtypeStruct((l, d), F32), jax.ShapeDtypeStruct((l, d), BF16)],
        scratch=[pltpu.VMEM((r, nsb), F32), pltpu.VMEM((r, nsb), F32),
                 pltpu.VMEM((N_STREAMS, nsb), F32), pltpu.VMEM((N_STREAMS, nsb), F32)],
        sem=("parallel", "arbitrary"), args=(u, bre, bim, a2, init_re, init_im, cre, cim, dskip), rider=rider)


def ssm_bwd2(name, dy, u, sre, sim, ctre, ctim, a2c, init_re, init_im, fre, fim, bre, bim, dskip, *, tj=128, rider=None):
    l, d = u.shape
    nb, kb, nsb, jn, tj, r, nblk = _ssm_dims(l, bre, tj)
    ns = nb * nsb

    def body(dy_ref, u_ref, sre_ref, sim_ref, pre_ref, pim_ref, ctre_ref, ctim_ref, a_ref, ire_ref, iim_ref,
             fre_ref, fim_ref, bre_ref, bim_ref, d_ref,
             du_ref, dbre_ref, dbim_ref, dcre_ref, dcim_ref, dare_ref, daim_ref, dd_ref,
             lre, lim, sf_re, sf_im, st_re, st_im):
        i = pl.program_id(1)
        first = i == 0
        sf_re[...] = sre_ref[...].astype(F32)
        sf_im[...] = sim_ref[...].astype(F32)

        @pl.when(first)
        def _():
            st_re[...] = ire_ref[...]
            st_im[...] = iim_ref[...]
            dbre_ref[...] = jnp.zeros_like(dbre_ref)
            dbim_ref[...] = jnp.zeros_like(dbim_ref)
            dcre_ref[...] = jnp.zeros_like(dcre_ref)
            dcim_ref[...] = jnp.zeros_like(dcim_ref)
            dare_ref[...] = jnp.zeros_like(dare_ref)
            daim_ref[...] = jnp.zeros_like(daim_ref)
            dd_ref[...] = jnp.zeros_like(dd_ref)

        _drive(dy_ref, ctre_ref, ctim_ref, lre, lim)
        _scan(lre, lim, st_re, st_im, a_ref, tj, True, True)

        is_t0 = i == nblk - 1
        cw = min(SCAN_LANES, nsb)
        for cb in range(nsb // cw):
            sl = slice(cb * cw, (cb + 1) * cw)
            p_r = jnp.where(is_t0, fre_ref[:, sl], pre_ref[:, sl].astype(F32)[N_STREAMS:])
            p_i = jnp.where(is_t0, fim_ref[:, sl], pim_ref[:, sl].astype(F32)[N_STREAMS:])
            l_r, l_i = lre[0:N_STREAMS, sl], lim[0:N_STREAMS, sl]
            acc = (l_r * p_r + l_i * p_i, l_i * p_r - l_r * p_i)

            def step(jj, carry, sl=sl):
                a_r, a_i = carry
                off = pl.multiple_of(jj * N_STREAMS, N_STREAMS)
                prev = pl.multiple_of((jj - 1) * N_STREAMS, N_STREAMS)
                l_r, l_i = lre[pl.ds(off, N_STREAMS), sl], lim[pl.ds(off, N_STREAMS), sl]
                p_r, p_i = sf_re[pl.ds(prev, N_STREAMS), sl], sf_im[pl.ds(prev, N_STREAMS), sl]
                return a_r + l_r * p_r + l_i * p_i, a_i + l_i * p_r - l_r * p_i

            a_r, a_i = lax.fori_loop(1, tj, step, acc)
            dare_ref[:, sl] += a_r
            daim_ref[:, sl] += a_i

        dyf = dy_ref[...].astype(F32)
        uf = u_ref[...].astype(F32)
        dd_ref[...] += jnp.sum(dyf * uf, axis=0, keepdims=True)
        lrb = lre[...].astype(BF16)
        lib = lim[...].astype(BF16)
        ub = u_ref[...]
        dyb = dy_ref[...]
        dbre_ref[...] += lax.dot_general(ub, lrb, TN, preferred_element_type=F32)
        dbim_ref[...] += lax.dot_general(ub, lib, TN, preferred_element_type=F32)
        dcre_ref[...] += lax.dot_general(sre_ref[...].astype(BF16), dyb, TN, preferred_element_type=F32)
        dcim_ref[...] -= lax.dot_general(sim_ref[...].astype(BF16), dyb, TN, preferred_element_type=F32)
        du = (lax.dot_general(lrb, bre_ref[...], NT, preferred_element_type=F32)
              + lax.dot_general(lib, bim_ref[...], NT, preferred_element_type=F32)
              + d_ref[...] * dyf)
        du_ref[...] = du.astype(BF16)

    rev = lambda w: pl.BlockSpec((r, w), lambda b, i: (nblk - 1 - i, b))
    assert tj % 2 == 0 or nblk == 1
    prev_tile = pl.BlockSpec((2 * N_STREAMS, nsb), lambda b, i: (jnp.maximum((nblk - 1 - i) * tj, 2) // 2 - 1, b))
    st = pl.BlockSpec((N_STREAMS, nsb), lambda b, i: (0, b))
    w3 = pl.BlockSpec((None, kb, nsb), lambda b, i: (b, 0, 0))
    c3 = pl.BlockSpec((None, nsb, kb), lambda b, i: (b, 0, 0))
    dsp = pl.BlockSpec((1, kb), lambda b, i: (0, b))
    return _call(
        name, body, grid=(nb, nblk),
        in_specs=[rev(kb), rev(kb), rev(nsb), rev(nsb), prev_tile, prev_tile, w3, w3,
                  pl.BlockSpec((2, nsb), lambda b, i: (0, b)), st, st, st, st, w3, w3, dsp],
        out_specs=[rev(kb), w3, w3, c3, c3, st, st, dsp],
        out_shape=[jax.ShapeDtypeStruct((l, d), BF16), jax.ShapeDtypeStruct((nb, kb, nsb), F32),
                   jax.ShapeDtypeStruct((nb, kb, nsb), F32), jax.ShapeDtypeStruct((nb, nsb, kb), F32),
                   jax.ShapeDtypeStruct((nb, nsb, kb), F32), jax.ShapeDtypeStruct((N_STREAMS, ns), F32),
                   jax.ShapeDtypeStruct((N_STREAMS, ns), F32), jax.ShapeDtypeStruct((1, d), F32)],
        scratch=[pltpu.VMEM((r, nsb), F32), pltpu.VMEM((r, nsb), F32), pltpu.VMEM((r, nsb), F32), pltpu.VMEM((r, nsb), F32),
                 pltpu.VMEM((N_STREAMS, nsb), F32), pltpu.VMEM((N_STREAMS, nsb), F32)],
        sem=("parallel", "arbitrary"),
        args=(dy, u, sre, sim, sre, sim, ctre, ctim, a2c, init_re, init_im, fre, fim, bre, bim, dskip), rider=rider)


ATT_TILE = 256
ATT_HEADS = 4
EXP_ZERO_BELOW = -104.0
LANES = 128


def _split_dot(x, tri):
    hi = x.astype(BF16)
    lo = (x - hi.astype(F32)).astype(BF16)
    return jnp.dot(hi, tri, preferred_element_type=F32) + jnp.dot(lo, tri, preferred_element_type=F32)


def _sb_logs(z, causal):
    sp = jnp.maximum(z, 0.0) + jnp.log(1.0 + jnp.exp(-jnp.abs(z)))
    l1m = -sp
    if causal is not None:
        l1m = jnp.where(causal, l1m, 0.0)
    return z - sp, l1m


def attn_fwd(name, q, kv, t=ATT_TILE, rider=None):
    l, dm = q.shape
    dh = HEAD_DIM
    h = dm // dh
    t = _blk(l, t)
    hb = _blk(h, ATT_HEADS)
    wb = hb * dh
    scale = dh ** -0.5

    def body(q_ref, k_ref, v_ref, o_ref, tot_ref):
        i = pl.program_id(1)
        hd = lambda g: slice(g * dh, (g + 1) * dh)
        qs = [(q_ref[:, hd(g)].astype(F32) * scale).astype(BF16) for g in range(hb)]
        row = lax.broadcasted_iota(jnp.int32, (t, t), 0)
        col = lax.broadcasted_iota(jnp.int32, (t, t), 1)
        tri_gt = jnp.where(row > col, 1.0, 0.0).astype(BF16)
        causal = col < row

        def tile(jb, carry, mask):
            off = pl.multiple_of(jb * t, t)
            heads = range(hb)
            z = [lax.dot_general(qs[g], k_ref[pl.ds(off, t), hd(g)], NT, preferred_element_type=F32) for g in heads]
            logs = [_sb_logs(z[g], mask) for g in heads]
            rem = [_split_dot(logs[g][1], tri_gt) for g in heads]
            w = [jnp.exp(logs[g][0] + rem[g] + carry[g][1]) for g in heads]
            if mask is not None:
                w = [jnp.where(mask, w[g], 0.0) for g in heads]
            pv = [jnp.dot(w[g].astype(BF16), v_ref[pl.ds(off, t), hd(g)], preferred_element_type=F32) for g in heads]
            return tuple((carry[g][0] + pv[g], carry[g][1] + rem[g][:, 0:1] + logs[g][1][:, 0:1]) for g in heads)

        def live(carry):
            top = carry[0][1]
            for g in range(1, hb):
                top = jnp.maximum(top, carry[g][1])
            return (jnp.max(top) > EXP_ZERO_BELOW).astype(jnp.int32)

        def more(c):
            it, alive, _ = c
            return jnp.logical_and(it < i, alive > 0)

        def step(c):
            it, _, carry = c
            carry = tile(i - 1 - it, carry, None)
            return it + 1, live(carry), carry

        carry = tile(i, ((jnp.zeros((t, dh), F32), jnp.zeros((t, 1), F32)),) * hb, causal)
        done, _, carry = lax.while_loop(more, step, (jnp.int32(0), live(carry), carry))
        tot_ref[...] = jnp.zeros_like(tot_ref)
        for g, (acc, run) in enumerate(carry):
            o_ref[:, hd(g)] = acc.astype(BF16)
            tot_ref[:, g:g + 1] = run
        tot_ref[:, hb:hb + 1] = jnp.full((t, 1), done, jnp.int32).astype(F32)

    qsp = pl.BlockSpec((t, wb), lambda hh, i: (i, hh))
    ksp = pl.BlockSpec((None, l, wb), lambda hh, i: (0, 0, hh))
    vsp = pl.BlockSpec((None, l, wb), lambda hh, i: (1, 0, hh))
    tsp = pl.BlockSpec((None, t, LANES), lambda hh, i: (hh, i, 0))
    return _call(name, body, grid=(h // hb, l // t), in_specs=[qsp, ksp, vsp], out_specs=[qsp, tsp],
                 out_shape=[jax.ShapeDtypeStruct((l, dm), BF16), jax.ShapeDtypeStruct((h // hb, l, LANES), F32)],
                 sem=("parallel", "parallel"), args=(q, kv, kv), rider=rider)


def attn_bwd(name, q, kv, tot, do, t=ATT_TILE, rider=None):
    l, dm = q.shape
    dh = HEAD_DIM
    h = dm // dh
    t = _blk(l, t)
    hb = _blk(h, ATT_HEADS)
    wb = hb * dh
    nq = l // t
    scale = dh ** -0.5

    def body(q_ref, k_ref, v_ref, tot_ref, do_ref, dq_ref, dkv_ref, dk_ref, dv_ref):
        i = pl.program_id(1)
        hd = lambda g: slice(g * dh, (g + 1) * dh)

        @pl.when(i == 0)
        def _():
            dk_ref[...] = jnp.zeros_like(dk_ref)
            dv_ref[...] = jnp.zeros_like(dv_ref)

        qs = [(q_ref[:, hd(g)].astype(F32) * scale).astype(BF16) for g in range(hb)]
        dob = [do_ref[:, hd(g)] for g in range(hb)]
        total = [tot_ref[:, g:g + 1] for g in range(hb)]
        row = lax.broadcasted_iota(jnp.int32, (t, t), 0)
        col = lax.broadcasted_iota(jnp.int32, (t, t), 1)
        tri_le = jnp.where(row <= col, 1.0, 0.0).astype(BF16)
        tri_lt = jnp.where(row < col, 1.0, 0.0).astype(BF16)
        causal = col < row

        def tile(jb, carry, mask):
            off = pl.multiple_of(jb * t, t)
            heads = range(hb)
            kj = [k_ref[pl.ds(off, t), hd(g)] for g in heads]
            z = [lax.dot_general(qs[g], kj[g], NT, preferred_element_type=F32) for g in heads]
            dp = [lax.dot_general(dob[g], v_ref[pl.ds(off, t), hd(g)], NT, preferred_element_type=F32) for g in heads]
            logs = [_sb_logs(z[g], mask) for g in heads]
            lpre = [_split_dot(logs[g][1], tri_le) for g in heads]
            w = [jnp.exp(logs[g][0] + (total[g] - carry[g][1] - lpre[g])) for g in heads]
            if mask is not None:
                w = [jnp.where(mask, w[g], 0.0) for g in heads]
            p = [w[g] * dp[g] for g in heads]
            for g in heads:
                dv_ref[pl.ds(off, t), hd(g)] += lax.dot_general(w[g].astype(BF16), dob[g], TN, preferred_element_type=F32)
            qpre = [carry[g][2] + jnp.dot(p[g].astype(BF16), tri_lt, preferred_element_type=F32) for g in heads]
            dz = [p[g] - jnp.exp(logs[g][0]) * (p[g] + qpre[g]) for g in heads]
            if mask is not None:
                dz = [jnp.where(mask, dz[g], 0.0) for g in heads]
            dzb = [dz[g].astype(BF16) for g in heads]
            for g in heads:
                dk_ref[pl.ds(off, t), hd(g)] += lax.dot_general(dzb[g], qs[g], TN, preferred_element_type=F32)
            dq = [carry[g][0] + jnp.dot(dzb[g], kj[g], preferred_element_type=F32) for g in heads]
            return tuple((dq[g], carry[g][1] + lpre[g][:, t - 1:t], qpre[g][:, t - 1:t] + p[g][:, t - 1:t]) for g in heads)

        zero = jnp.zeros((t, 1), F32)
        done = jnp.max(tot_ref[:, hb:hb + 1]).astype(jnp.int32)
        carry = lax.fori_loop(i - done, i, lambda jb, c: tile(jb, c, None), ((jnp.zeros((t, dh), F32), zero, zero),) * hb)
        carry = tile(i, carry, causal)
        for g in range(hb):
            dq_ref[:, hd(g)] = (carry[g][0] * scale).astype(BF16)

        @pl.when(i == nq - 1)
        def _():
            dkv_ref[0] = dk_ref[...].astype(BF16)
            dkv_ref[1] = dv_ref[...].astype(BF16)

    qsp = pl.BlockSpec((t, wb), lambda hh, i: (i, hh))
    ksp = pl.BlockSpec((None, l, wb), lambda hh, i: (0, 0, hh))
    vsp = pl.BlockSpec((None, l, wb), lambda hh, i: (1, 0, hh))
    tsp = pl.BlockSpec((None, t, LANES), lambda hh, i: (hh, i, 0))
    return _call(name, body, grid=(h // hb, nq), in_specs=[qsp, ksp, vsp, tsp, qsp],
                 out_specs=[qsp, pl.BlockSpec((2, l, wb), lambda hh, i: (0, 0, hh))],
                 out_shape=[jax.ShapeDtypeStruct((l, dm), BF16), jax.ShapeDtypeStruct((2, l, dm), BF16)],
                 scratch=[pltpu.VMEM((l, wb), F32), pltpu.VMEM((l, wb), F32)],
                 sem=("parallel", "arbitrary"), args=(q, kv, kv, tot, do), rider=rider)


def adamw(name, w, g, m, v):
    shape = w.shape
    cols = shape[-1]
    rows = w.size // cols
    tr = rows
    if rows % 8 == 0:
        tr = 8 * _blk(rows // 8, 64)
    c1 = 1.0 - ADAM_B1 ** ADAM_STEP
    c2 = 1.0 - ADAM_B2 ** ADAM_STEP

    def body(w_ref, g_ref, m_ref, v_ref, d_ref, nm_ref, nv_ref):
        gg = g_ref[...]
        mm = ADAM_B1 * m_ref[...] + (1.0 - ADAM_B1) * gg
        vv = ADAM_B2 * v_ref[...] + (1.0 - ADAM_B2) * (gg * gg)
        nm_ref[...] = mm
        nv_ref[...] = vv
        d_ref[...] = -ADAM_LR * ((mm / c1) / (jnp.sqrt(vv / c2) + ADAM_EPS) + ADAM_WD * w_ref[...])

    sp = pl.BlockSpec((tr, cols), lambda i: (i, 0))
    sd = jax.ShapeDtypeStruct((rows, cols), F32)
    outs = pl.pallas_call(body, name=name, grid=(rows // tr,), in_specs=[sp] * 4, out_specs=[sp] * 3,
                          out_shape=[sd] * 3, compiler_params=_cparams(("parallel",)))(
        w.reshape(rows, cols), g.reshape(rows, cols), m.reshape(rows, cols), v.reshape(rows, cols))
    return tuple(o.reshape(shape) for o in outs)


def _perm(a):
    l, d = a.shape
    return a.reshape(N_STREAMS, l // N_STREAMS, d).transpose(1, 0, 2).reshape(l, d)


def _unperm(a):
    l, d = a.shape
    return a.reshape(l // N_STREAMS, N_STREAMS, d).transpose(1, 0, 2).reshape(l, d)


def _ssm_layouts(d):
    g = d // SSM_GROUP
    gb = MXU_DIM // SSM_GROUP if d >= MXU_DIM else g
    return g, gb, g // gb


def _b_blocks(bb, d):
    g, gb, nb = _ssm_layouts(d)
    b4 = bb.reshape(SSM_GROUP, nb, gb, STATE)
    eye = jnp.eye(gb, dtype=bb.dtype)
    return jnp.einsum('hbqp,gq->bghqp', b4, eye).reshape(nb, gb * SSM_GROUP, gb * STATE)


def _b_unblocks(db, d):
    g, gb, nb = _ssm_layouts(d)
    eye = jnp.eye(gb, dtype=db.dtype)
    return jnp.einsum('bghqp,gq->hbqp', db.reshape(nb, gb, SSM_GROUP, gb, STATE), eye).reshape(SSM_GROUP, g * STATE)


def _c_blocks(c, d):
    g, gb, nb = _ssm_layouts(d)
    eye = jnp.eye(gb, dtype=c.dtype)
    return jnp.einsum('bghp,gq->bqpgh', c.reshape(nb, gb, SSM_GROUP, STATE), eye).reshape(nb, gb * STATE, gb * SSM_GROUP)


def _c_unblocks(dc, d):
    g, gb, nb = _ssm_layouts(d)
    eye = jnp.eye(gb, dtype=dc.dtype)
    return jnp.einsum('bqpgh,gq->bghp', dc.reshape(nb, gb, STATE, gb, SSM_GROUP), eye).reshape(g, SSM_GROUP, STATE)


class NoComm:
    def __init__(self):
        self.local = {}

    def gather(self, names):
        return None

    def landed(self, names, bufs, p):
        pass

    def swap(self, names, arrays):
        self.local.update(zip(names, arrays))
        return None

    def swapped(self, names, got):
        pass

    def chips(self, names):
        return None

    def reduced(self, names, slots):
        pass


def _ffn_fwd(tag, x, r, gain, p, layer, cm, ride):
    xs, h = add_rmsnorm(f"norm_ffn{tag}", x, r, [gain])
    rider = cm.gather(ride) if ride else None
    gu = mm_up(f"ffn_up{tag}", h, p["wup"][layer], 0, rider=rider)
    if rider is not None:
        gu, landed = gu
        cm.landed(ride, landed, p)
    a = ffn_mid_fwd(f"ffn_mid{tag}", gu, p["conv_w"][layer], p["conv_b"][layer:layer + 1])
    f = mm_nn(f"ffn_down{tag}", a, p["wdown"][layer], F32)
    return xs, h, gu, a, f


def _ffn_bwd(tag, dxo, dfb, xs, h, gu, a, gain, p, layer, s, cm, ride):
    wdown = p["wdown"][layer]
    f = wdown.shape[0]
    da = mm_nt(f"ffn_down_dx{tag}", dfb, wdown, BF16, tn=f // 2)
    dwdown = mm_tn(f"ffn_down_dw{tag}", a, dfb, tmo=f // 2)
    dgu, dcw, dcb = ffn_mid_bwd(f"ffn_mid_bwd{tag}", gu, da, p["conv_w"][layer], p["conv_b"][layer:layer + 1])
    rider = cm.chips(ride) if ride else None
    dh = mm_up_nt(f"ffn_up_dx{tag}", dgu, p["wup"][layer], 0, rider=rider)
    if rider is not None:
        dh, slots = dh
        cm.reduced(ride, slots)
    dwup = mm_up_tn(f"ffn_up_dw{tag}", h, dgu, s)
    own = [f"wup{layer}", f"wdown{layer}"]
    res = norm_bwd(f"norm_ffn_bwd{tag}", xs, dxo, [(gain, dh)], bf16_copy=True, rider=cm.swap(own, [dwup, dwdown]))
    cm.swapped(own, res[3:])
    return res[0], res[2], res[1], dcw, dcb


def _local_step(x, tgt, p, cm=None):
    cm = NoComm() if cm is None else cm
    l, d = x.shape
    s = N_CHIPS
    gr = {}

    a_re, a_im = p["a_re"].reshape(1, -1), p["a_im"].reshape(1, -1)
    ldt = jnp.repeat(p["log_dt"].reshape(-1), STATE).reshape(1, -1)
    bk_re = p["b_re"].transpose(2, 0, 1).reshape(SSM_GROUP, -1)
    bk_im = p["b_im"].transpose(2, 0, 1).reshape(SSM_GROUP, -1)
    ab_re, ab_im, bb_re, bb_im = ssm_disc_fwd("ssm_disc", a_re, a_im, ldt, bk_re, bk_im)
    a2 = jnp.concatenate([ab_re, ab_im], axis=0)
    a2c = jnp.concatenate([ab_re, -ab_im], axis=0)
    bre, bim = _b_blocks(bb_re, d).astype(BF16), _b_blocks(bb_im, d).astype(BF16)
    cre, cim = _c_blocks(p["c_re"], d).astype(BF16), _c_blocks(p["c_im"], d).astype(BF16)
    ctre, ctim = cre.transpose(0, 2, 1), -cim.transpose(0, 2, 1)
    dskip = p["d"].reshape(1, d)

    xp = _perm(x)
    (h0p,) = add_rmsnorm("norm_mix0", xp, None, [p["norm_mix"][0]])
    u = mm_nn("ssm_in", h0p, p["win"], BF16)
    rider = cm.gather(["wglu", "wdown0"])
    res = ssm_pass1("ssm_fwd1", u, bre, bim, a2, reverse=False, rider=rider)
    f_re, f_im = res[0], res[1]
    cm.landed(["wglu", "wdown0"], res[2:], p)
    rider = cm.gather(["wup0"])
    res = ssm_fwd2("ssm_fwd2", u, bre, bim, a2, f_re, f_im, cre, cim, dskip, rider=rider)
    s_re, s_im, ypre, yg = res[:4]
    cm.landed(["wup0"], res[4:], p)
    z = mm_nn_colshard("ssm_glu", yg, p["wglu4"], F32)
    mix = _unperm(glu_fwd("glu", z))

    x1, h1, gu0, a0, f0 = _ffn_fwd("0", x, mix, p["norm_ffn"][0], p, 0, cm, ["kv", "wq", "wo"])
    x2, hk, h2 = add_rmsnorm("norm_kv_mix1", x1, f0, [p["norm_kv"], p["norm_mix"][1]])
    kvw = p["kvw4"][:, None]
    kv = mm_up("kv_proj", hk, kvw, 0)
    qf = mm_nn("q_proj", h2, p["wq"], BF16)
    rider = cm.gather(["wup1", "wdown1"])
    res = attn_fwd("attn", qf, kv, rider=rider)
    ob, tot = res[0], res[1]
    cm.landed(["wup1", "wdown1"], res[2:], p)
    ao = mm_nn("o_proj", ob, p["wo"], F32)
    x3, h3, gu1, a1, f1 = _ffn_fwd("1", x2, ao, p["norm_ffn"][1], p, 1, cm, None)
    loss, dx4, dg_final, dx4b = final_loss("final_loss", x3, f1, tgt, p["norm_final"])
    gr["norm_final"] = dg_final.reshape(d)

    dx3, dx3b, dg_ffn1, dcw1, dcb1 = _ffn_bwd("1", dx4, dx4b, x3, h3, gu1, a1, p["norm_ffn"][1], p, 1, s, cm, None)
    do2 = mm_nt("o_proj_dx", dx3b, p["wo"], BF16)
    dwo = mm_tn("o_proj_dw", ob, dx3b)
    rider = cm.chips(["wup1", "wdown1"])
    res = attn_bwd("attn_bwd", qf, kv, tot, do2, rider=rider)
    dqf, dkv = res[0], res[1]
    cm.reduced(["wup1", "wdown1"], res[2:])
    dh2 = mm_nt("q_proj_dx", dqf, p["wq"], F32)
    dwq = mm_tn("q_proj_dw", h2, dqf)
    dhk = mm_up_nt("kv_proj_dx", dkv, kvw, 0)
    dwkv = mm_up_tn("kv_proj_dw", hk, dkv, s)
    att = ["wo", "wq", "kv"]
    res = norm_bwd("norm_kv_mix1_bwd", x2, dx3, [(p["norm_mix"][1], dh2), (p["norm_kv"], dhk)], bf16_copy=True,
                   rider=cm.swap(att, [dwo, dwq, dwkv]))
    dx2, dg_mix1, dg_kv, dx2b = res[:4]
    cm.swapped(att, res[4:])
    gr["norm_kv"] = dg_kv.reshape(d)

    dx1, _, dg_ffn0, dcw0, dcb0 = _ffn_bwd("0", dx2, dx2b, x1, h1, gu0, a0, p["norm_ffn"][0], p, 0, s, cm, att)
    dx1p = _perm(dx1)
    dz = glu_bwd("glu_bwd", z, dx1p)
    dyg = mm_nt_colshard("ssm_glu_dx", dz, p["wglu4"], F32)
    dwglu = mm_tn_colshard("ssm_glu_dw", yg, dz, s)
    dy = gelu_bwd("gelu_bwd", ypre, dyg)
    res = ssm_pass1("ssm_bwd1", dy, ctre, ctim, a2c, reverse=True, rider=cm.swap(["wglu"], [dwglu]))
    i_re, i_im = res[0], res[1]
    cm.swapped(["wglu"], res[2:])
    rider = cm.chips(["wup0", "wdown0", "wglu"])
    res = ssm_bwd2("ssm_bwd2", dy, u, s_re, s_im, ctre, ctim, a2c, i_re, i_im, f_re, f_im, bre, bim, dskip, rider=rider)
    du, dbre, dbim, dcre, dcim, da_re, da_im, dd = res[:8]
    cm.reduced(["wup0", "wdown0", "wglu"], res[8:])
    dh0p = mm_nt("ssm_in_dx", du, p["win"], F32)
    dwin = mm_tn("ssm_in_dw", h0p, du)
    dxp, dg_mix0 = norm_bwd("norm_mix0_bwd", xp, dx1p, [(p["norm_mix"][0], dh0p)])
    dx = _unperm(dxp)

    g_are, g_aim, g_ldt, g_bre, g_bim = ssm_disc_bwd(
        "ssm_disc_bwd", a_re, a_im, ldt, bk_re, bk_im,
        jnp.sum(da_re, axis=0, keepdims=True), jnp.sum(da_im, axis=0, keepdims=True),
        _b_unblocks(dbre, d), _b_unblocks(dbim, d))
    g = d // SSM_GROUP
    gr["a_re"] = g_are.reshape(g, STATE)
    gr["a_im"] = g_aim.reshape(g, STATE)
    gr["log_dt"] = jnp.sum(g_ldt.reshape(g, STATE), axis=1)
    gr["b_re"] = g_bre.reshape(SSM_GROUP, g, STATE).transpose(1, 2, 0)
    gr["b_im"] = g_bim.reshape(SSM_GROUP, g, STATE).transpose(1, 2, 0)
    gr["c_re"] = _c_unblocks(dcre, d)
    gr["c_im"] = _c_unblocks(dcim, d)
    gr["d"] = dd.reshape(g, SSM_GROUP)
    gr["norm_mix"] = jnp.concatenate([dg_mix0, dg_mix1], axis=0)
    gr["norm_ffn"] = jnp.concatenate([dg_ffn0, dg_ffn1], axis=0)
    gr["conv_w"] = jnp.stack([dcw0, dcw1])
    gr["conv_b"] = jnp.concatenate([dcb0, dcb1], axis=0)
    gr["win"] = dwin
    gr.update(getattr(cm, "local", {}))
    return loss, dx, gr


MESH = pl.DeviceIdType.MESH
N_CHIPS = 4
N_DEV = 8
ANY = pl.BlockSpec(memory_space=pl.ANY)


def _pos():
    x, y, c = lax.axis_index("x"), lax.axis_index("y"), lax.axis_index("c")
    return x, y, c, 2 * x + y


def _other_chips(x, y):
    return [(1 - x, y), (x, 1 - y), (1 - x, 1 - y)]


def _remote(src, dst, send_sem, recv_sem, dev):
    return pltpu.make_async_remote_copy(src_ref=src, dst_ref=dst, send_sem=send_sem, recv_sem=recv_sem,
                                        device_id=dev, device_id_type=MESH)


def cast_into_slot(name, a, chip, dtype):
    r, cdim = a.shape
    tr = 16 * _blk(r // 16, 32) if r % 16 == 0 else r

    def body(m_ref, a_ref, o_ref):
        o_ref[...] = a_ref[...].astype(o_ref.dtype)

    gs = pltpu.PrefetchScalarGridSpec(
        num_scalar_prefetch=1, grid=(r // tr,),
        in_specs=[pl.BlockSpec((tr, cdim), lambda i, m_ref: (i, 0))],
        out_specs=pl.BlockSpec((None, tr, cdim), lambda i, m_ref: (m_ref[0], i, 0)))
    return pl.pallas_call(body, name=name, grid_spec=gs, out_shape=jax.ShapeDtypeStruct((N_CHIPS, r, cdim), dtype),
                          compiler_params=_cparams(("parallel",)))(chip, a)


def gather_weights(name, bufs):
    n = len(bufs)

    def body(*refs):
        _gather_start(refs[n:2 * n], refs[2 * n:])
        _gather_finish(refs[n:2 * n], refs[2 * n:])

    sem = pltpu.SemaphoreType.DMA
    return pl.pallas_call(
        body, name=name, in_specs=[ANY] * n, out_specs=[ANY] * n,
        out_shape=[jax.ShapeDtypeStruct(b.shape, b.dtype) for b in bufs],
        input_output_aliases={w: w for w in range(n)},
        scratch_shapes=[sem((n, 3)), sem((n, 3)), sem((n, 3)), sem((n, 3))],
        compiler_params=pltpu.CompilerParams(has_side_effects=True),
    )(*bufs)


def _half(ref, chip, core):
    hr = ref.shape[1] // 2
    return ref.at[chip, pl.ds(core * hr, hr), :]


def _gather_start(bufs, sems):
    send_a, recv_a = sems[0], sems[1]
    x, y, c, m = _pos()
    for w, buf in enumerate(bufs):
        for j, (px, py) in enumerate(_other_chips(x, y)):
            blk = _half(buf, m, c)
            _remote(blk, blk, send_a.at[w, j], recv_a.at[w, j], (px, py, c)).start()


def _gather_finish(bufs, sems):
    send_a, recv_a, send_b, recv_b = sems
    x, y, c, m = _pos()
    chips = _other_chips(x, y)
    sib = (x, y, 1 - c)
    for j, (px, py) in enumerate(chips):
        for w, buf in enumerate(bufs):
            blk = _half(buf, 2 * px + py, c)
            _remote(blk, blk, send_a.at[w, j], recv_a.at[w, j], (px, py, c)).wait_recv()
            _remote(blk, blk, send_b.at[w, j], recv_b.at[w, j], sib).start()
    for j, (px, py) in enumerate(chips):
        for w, buf in enumerate(bufs):
            blk = _half(buf, 2 * px + py, 1 - c)
            _remote(blk, blk, send_b.at[w, j], recv_b.at[w, j], sib).wait_recv()
    for j, (px, py) in enumerate(chips):
        for w, buf in enumerate(bufs):
            mine, landed = _half(buf, m, c), _half(buf, 2 * px + py, c)
            _remote(mine, mine, send_a.at[w, j], recv_a.at[w, j], (px, py, c)).wait_send()
            _remote(landed, landed, send_b.at[w, j], recv_b.at[w, j], sib).wait_send()


def gather_rider(bufs):
    n = len(bufs)
    return Rider(ins=list(bufs), outs=[jax.ShapeDtypeStruct(b.shape, b.dtype) for b in bufs],
                 alias={w: w for w in range(n)}, sems=[(n, 3)] * 4,
                 start=lambda i, o, s: _gather_start(o, s), finish=lambda i, o, s: _gather_finish(o, s))


def exchange_halves(name, arrs):
    n = len(arrs)

    def body(*refs):
        _exchange(refs[:n], refs[n:2 * n], refs[2 * n:], "start")
        _exchange(refs[:n], refs[n:2 * n], refs[2 * n:], "wait")

    sem = pltpu.SemaphoreType.DMA
    return pl.pallas_call(
        body, name=name, in_specs=[ANY] * n, out_specs=[ANY] * n, out_shape=_exchange_shapes(arrs),
        scratch_shapes=[sem((n,)), sem((n,))],
        compiler_params=pltpu.CompilerParams(has_side_effects=True),
    )(*arrs)


def _exchange_shapes(arrs):
    return [jax.ShapeDtypeStruct((a.shape[0], a.shape[1] // 2, a.shape[2]), a.dtype) for a in arrs]


def _exchange(ins, outs, sems, what):
    send, recv = sems
    x, y, c, _ = _pos()
    for w in range(len(ins)):
        hr = ins[w].shape[1] // 2
        cp = _remote(ins[w].at[:, pl.ds((1 - c) * hr, hr), :], outs[w], send.at[w], recv.at[w], (x, y, 1 - c))
        if what == "start":
            cp.start()
        else:
            cp.wait()


def exchange_rider(arrs):
    n = len(arrs)
    return Rider(ins=list(arrs), outs=_exchange_shapes(arrs), alias={}, sems=[(n,)] * 2,
                 start=lambda i, o, s: _exchange(i, o, s, "start"), finish=lambda i, o, s: _exchange(i, o, s, "wait"))


def scatter_to_chips(name, arrs):
    n = len(arrs)

    def body(*refs):
        _scatter(refs[:n], refs[n:2 * n], refs[2 * n:], "start")
        _scatter(refs[:n], refs[n:2 * n], refs[2 * n:], "wait")

    sem = pltpu.SemaphoreType.DMA
    return pl.pallas_call(
        body, name=name, in_specs=[ANY] * n, out_specs=[ANY] * n,
        out_shape=[jax.ShapeDtypeStruct((3,) + a.shape[1:], a.dtype) for a in arrs],
        scratch_shapes=[sem((n, 3)), sem((n, 3))],
        compiler_params=pltpu.CompilerParams(has_side_effects=True),
    )(*arrs)


def _scatter(ins, outs, sems, what):
    send, recv = sems
    x, y, c, _ = _pos()
    for w in range(len(ins)):
        for j, (px, py) in enumerate(_other_chips(x, y)):
            cp = _remote(ins[w].at[2 * px + py], outs[w].at[j], send.at[w, j], recv.at[w, j], (px, py, c))
            if what == "start":
                cp.start()
            else:
                cp.wait()


def scatter_rider(arrs):
    n = len(arrs)
    return Rider(ins=list(arrs), outs=[jax.ShapeDtypeStruct((3,) + a.shape[1:], a.dtype) for a in arrs],
                 alias={}, sems=[(n, 3)] * 2,
                 start=lambda i, o, s: _scatter(i, o, s, "start"), finish=lambda i, o, s: _scatter(i, o, s, "wait"))


def share_halves(name, fulls):
    n = len(fulls)

    def body(*refs):
        outs = refs[n:2 * n]
        send, recv = refs[2 * n:]
        x, y, c, _ = _pos()
        cps = []
        for w in range(n):
            hr = outs[w].shape[0] // 2
            blk = outs[w].at[pl.ds(c * hr, hr), :]
            cp = _remote(blk, blk, send.at[w], recv.at[w], (x, y, 1 - c))
            cp.start()
            cps.append(cp)
        for w, cp in enumerate(cps):
            hr = outs[w].shape[0] // 2
            cp.wait_send()
            blk = outs[w].at[pl.ds((1 - c) * hr, hr), :]
            _remote(blk, blk, send.at[w], recv.at[w], (x, y, 1 - c)).wait_recv()

    sem = pltpu.SemaphoreType.DMA
    return pl.pallas_call(
        body, name=name, in_specs=[ANY] * n, out_specs=[ANY] * n,
        out_shape=[jax.ShapeDtypeStruct(a.shape, a.dtype) for a in fulls],
        input_output_aliases={w: w for w in range(n)},
        scratch_shapes=[sem((n,)), sem((n,))],
        compiler_params=pltpu.CompilerParams(has_side_effects=True),
    )(*fulls)


def add_own_half(name, full, got, core, out_dtype):
    n, r, cdim = full.shape
    hr = r // 2
    tr = 8 * _blk(hr // 8, 32) if out_dtype == F32 else 16 * _blk(hr // 16, 16)
    nbh = hr // tr

    def body(c_ref, f_ref, g_ref, o_ref):
        o_ref[...] = (f_ref[...] + g_ref[...]).astype(o_ref.dtype)

    gs = pltpu.PrefetchScalarGridSpec(
        num_scalar_prefetch=1, grid=(n, nbh),
        in_specs=[pl.BlockSpec((None, tr, cdim), lambda s, i, c_ref: (s, c_ref[0] * nbh + i, 0)),
                  pl.BlockSpec((None, tr, cdim), lambda s, i, c_ref: (s, i, 0))],
        out_specs=pl.BlockSpec((None, tr, cdim), lambda s, i, c_ref: (s, i, 0)))
    return pl.pallas_call(body, name=name, grid_spec=gs, out_shape=jax.ShapeDtypeStruct((n, hr, cdim), out_dtype),
                          compiler_params=_cparams(("parallel", "parallel")))(core, full, got)


def sum_into_half(name, part, slots, chip_core, chip_order):
    _, hr, cdim = part.shape
    unit = 8 if part.dtype == F32 else 16
    tr = unit * _blk(hr // unit, 256 // unit)
    nbh = hr // tr

    def body(mc_ref, p_ref, s_ref, o_ref):
        terms = [p_ref[...].astype(F32)] + [s_ref[k].astype(F32) for k in range(3)]
        if chip_order:
            m = mc_ref[0]
            own, fx, fy, fxy = terms
            terms = [jnp.where((k ^ m) == 0, own, jnp.where((k ^ m) == 2, fx, jnp.where((k ^ m) == 1, fy, fxy)))
                     for k in range(N_CHIPS)]
        o_ref[...] = ((terms[0] + terms[1]) + terms[2]) + terms[3]

    gs = pltpu.PrefetchScalarGridSpec(
        num_scalar_prefetch=1, grid=(nbh,),
        in_specs=[pl.BlockSpec((None, tr, cdim), lambda i, mc: (mc[0], i, 0)),
                  pl.BlockSpec((3, tr, cdim), lambda i, mc: (0, i, 0))],
        out_specs=pl.BlockSpec((tr, cdim), lambda i, mc: (mc[1] * nbh + i, 0)))
    return pl.pallas_call(body, name=name, grid_spec=gs, out_shape=jax.ShapeDtypeStruct((2 * hr, cdim), F32),
                          compiler_params=_cparams(("parallel",)))(chip_core, part, slots)


class StepComm:
    def __init__(self, bufs, core, chip_core):
        self.bufs, self.core, self.chip_core = bufs, core, chip_core
        self.whole, self.part, self.fulls = {}, {}, {}

    def gather(self, names):
        return gather_rider([self.bufs[n] for n in names])

    def landed(self, names, bufs, p):
        for n, b in zip(names, bufs):
            if n.startswith("wdown"):
                p["wdown"][int(n[-1])] = b.reshape(-1, b.shape[-1])
            elif n.startswith("wup"):
                p["wup"][int(n[-1])] = b[:, None]
            elif n == "kv":
                p["kvw4"] = b
            elif n == "wglu":
                p["wglu4"] = b
            else:
                p[n] = b.reshape(-1, b.shape[-1])

    def swap(self, names, arrays):
        big = [a if a.ndim == 3 else a.reshape(N_CHIPS, a.shape[0] // N_CHIPS, a.shape[1]) for a in arrays]
        self.whole.update(zip(names, big))
        return exchange_rider(big)

    def swapped(self, names, got, payloads=None):
        payloads = payloads or [BF16] * len(names)
        for n, r, dt in zip(names, got, payloads):
            self.part[n] = add_own_half("rs_add_" + n, self.whole.pop(n), r, self.core, dt)

    def chips(self, names):
        return scatter_rider([self.part[n] for n in names])

    def reduce(self, names, arrays, payloads=None):
        rider = self.swap(names, arrays)
        self.swapped(names, exchange_halves("rs_siblings_" + names[0], rider.ins), payloads)
        return self.chips(names)

    def reduced(self, names, slots, chip_order=()):
        for n, s in zip(names, slots):
            self.fulls[n] = sum_into_half("rs_sum_" + n, self.part[n], s, self.chip_core, n in chip_order)

    def finish(self):
        names = list(self.fulls)
        return dict(zip(names, share_halves("rs_share", [self.fulls[n] for n in names])))


WEIGHTS = ('norm_mix', 'norm_ffn', 'norm_kv', 'norm_final', 'ssm_w_in', 'ssm_a_re', 'ssm_a_im', 'ssm_log_dt',
           'ssm_b_re', 'ssm_b_im', 'ssm_c_re', 'ssm_c_im', 'ssm_d', 'ssm_w_glu', 'kv_w', 'attn_w_q', 'attn_w_o',
           'ffn_w_up', 'ffn_conv_w', 'ffn_conv_b', 'ffn_w_down')
SMALL = ('norm_mix', 'norm_ffn', 'norm_kv', 'norm_final', 'ssm_a_re', 'ssm_a_im', 'ssm_log_dt', 'ssm_b_re', 'ssm_b_im',
         'ssm_c_re', 'ssm_c_im', 'ssm_d', 'ffn_conv_w', 'ffn_conv_b')


def _pad_rows(flat, unit):
    n = flat.shape[0]
    total = -(-n // unit) * unit
    return jnp.pad(flat, (0, total - n)).reshape(total // LANES, LANES)


def kernel(x, norm_mix, norm_ffn, norm_kv, norm_final, ssm_w_in, ssm_a_re, ssm_a_im, ssm_log_dt, ssm_b_re, ssm_b_im, ssm_c_re, ssm_c_im, ssm_d, ssm_w_glu, kv_w, attn_w_q, attn_w_o, ffn_w_up, ffn_conv_w, ffn_conv_b, ffn_w_down, loss_target, m_norm_mix, m_norm_ffn, m_norm_kv, m_norm_final, m_ssm_w_in, m_ssm_a_re, m_ssm_a_im, m_ssm_log_dt, m_ssm_b_re, m_ssm_b_im, m_ssm_c_re, m_ssm_c_im, m_ssm_d, m_ssm_w_glu, m_kv_w, m_attn_w_q, m_attn_w_o, m_ffn_w_up, m_ffn_conv_w, m_ffn_conv_b, m_ffn_w_down, v_norm_mix, v_norm_ffn, v_norm_kv, v_norm_final, v_ssm_w_in, v_ssm_a_re, v_ssm_a_im, v_ssm_log_dt, v_ssm_b_re, v_ssm_b_im, v_ssm_c_re, v_ssm_c_im, v_ssm_d, v_ssm_w_glu, v_kv_w, v_attn_w_q, v_attn_w_o, v_ffn_w_up, v_ffn_conv_w, v_ffn_conv_b, v_ffn_w_down):
    a = dict(locals())
    l, d = x.shape[1], x.shape[2]
    f = ffn_conv_b.shape[1]
    fs = f // N_CHIPS
    m = 2 * lax.axis_index("x") + lax.axis_index("y")
    core = lax.axis_index("c").astype(jnp.int32).reshape(1)
    chip = m.astype(jnp.int32).reshape(1)
    chip_core = jnp.concatenate([chip, core])

    shards = {"win": ssm_w_in[0], "wglu": ssm_w_glu[0], "kv": kv_w, "wq": attn_w_q[0], "wo": attn_w_o[0],
              "wup0": ffn_w_up[0], "wup1": ffn_w_up[1], "wdown0": ffn_w_down[0], "wdown1": ffn_w_down[1],
              "convw": _pad_rows(ffn_conv_w.reshape(-1), 16 * LANES)}
    bufs = {k: cast_into_slot(f"cast_{k}", s, chip, F32 if k == "convw" else BF16) for k, s in shards.items()}
    cm = StepComm(bufs, core, chip_core)
    g_in, g_cw = gather_weights("gather_first", [bufs["win"], bufs["convw"]])
    conv_w = g_cw.reshape(N_CHIPS, -1)[:, :2 * 3 * fs].reshape(N_CHIPS, 2, 3, fs).transpose(1, 2, 0, 3).reshape(2, 3, f)
    p = dict(
        norm_mix=norm_mix, norm_ffn=norm_ffn, norm_kv=norm_kv, norm_final=norm_final,
        a_re=ssm_a_re[0], a_im=ssm_a_im[0], log_dt=ssm_log_dt[0], b_re=ssm_b_re[0], b_im=ssm_b_im[0],
        c_re=ssm_c_re[0], c_im=ssm_c_im[0], d=ssm_d[0],
        win=g_in.reshape(-1, g_in.shape[-1]), wup=[None, None], wdown=[None, None],
        conv_w=conv_w, conv_b=ffn_conv_b)

    loss_slab, dx, gr = _local_step(x[0], loss_target[0], p, cm)

    small = {"norm_mix": gr["norm_mix"], "norm_ffn": gr["norm_ffn"], "norm_kv": gr["norm_kv"], "norm_final": gr["norm_final"],
             "ssm_a_re": gr["a_re"], "ssm_a_im": gr["a_im"], "ssm_log_dt": gr["log_dt"], "ssm_b_re": gr["b_re"],
             "ssm_b_im": gr["b_im"], "ssm_c_re": gr["c_re"], "ssm_c_im": gr["c_im"], "ssm_d": gr["d"],
             "ffn_conv_w": gr["conv_w"], "ffn_conv_b": gr["conv_b"]}
    packed = _pad_rows(jnp.concatenate([small[k].reshape(-1) for k in SMALL] + [loss_slab[0, 0:1]]), 16 * LANES)
    last = cm.reduce(["win", "small"], [gr["win"], jnp.broadcast_to(packed, (N_CHIPS,) + packed.shape)], [BF16, F32])
    cm.reduced(["win", "small"], scatter_to_chips("rs_chips_last", last.ins), chip_order=("small",))
    r = cm.finish()
    grads = {"ssm_w_in": r["win"][None], "ssm_w_glu": r["wglu"][None], "kv_w": r["kv"], "attn_w_q": r["wq"][None],
             "attn_w_o": r["wo"][None], "ffn_w_up": jnp.stack([r["wup0"], r["wup1"]]),
             "ffn_w_down": jnp.stack([r["wdown0"], r["wdown1"]])}
    total = r["small"].reshape(-1)
    off = 0
    for k in SMALL:
        n = small[k].size
        full = total[off:off + n].reshape(small[k].shape)
        off += n
        if k == "ffn_conv_w":
            full = lax.dynamic_slice_in_dim(full, m * fs, fs, axis=2)
        grads[k] = full.reshape(a[k].shape)
    loss = total[off]

    outs = {}
    for k in WEIGHTS:
        outs[k] = adamw(f"adamw_{k}", a[k], grads[k], a["m_" + k], a["v_" + k])
    return (loss, dx[None], *[grads[k] for k in WEIGHTS], *[outs[k][0] for k in WEIGHTS],
            *[outs[k][1] for k in WEIGHTS], *[outs[k][2] for k in WEIGHTS])
```

```python
import functools
import math

import jax
import jax.numpy as jnp
from jax import lax
from jax.experimental import pallas as pl
from jax.experimental.pallas import tpu as pltpu

F32 = jnp.float32
BF16 = jnp.bfloat16

EPS = 1e-6
SSM_GROUP = 16
STATE = 64
HEAD_DIM = 64
N_STREAMS = 8
MXU_DIM = 256
VMEM_LIMIT = 56 * 1024 * 1024

ADAM_LR = 0.001
ADAM_B1 = 0.9
ADAM_B2 = 0.999
ADAM_EPS = 1e-08
ADAM_WD = 0.01
ADAM_STEP = 10


def _cparams(sem=None):
    return pltpu.CompilerParams(dimension_semantics=sem, vmem_limit_bytes=VMEM_LIMIT)


class Rider:
    def __init__(self, ins, outs, alias, sems, start, finish):
        self.ins, self.outs, self.alias, self.sems, self.start, self.finish = ins, outs, alias, sems, start, finish


def join_riders(a, b):
    if a is None or b is None:
        return a if b is None else b
    ni, no, ns = len(a.ins), len(a.outs), len(a.sems)
    alias = dict(a.alias)
    alias.update({ni + i: no + o for i, o in b.alias.items()})

    def start(i, o, s):
        a.start(i[:ni], o[:no], s[:ns])
        b.start(i[ni:], o[no:], s[ns:])

    def finish(i, o, s):
        a.finish(i[:ni], o[:no], s[:ns])
        b.finish(i[ni:], o[no:], s[ns:])

    return Rider(a.ins + b.ins, a.outs + b.outs, alias, a.sems + b.sems, start, finish)


def _call(name, body, *, grid, in_specs, out_specs, out_shape, args, scratch=(), sem=None, rider=None):
    if rider is None:
        return pl.pallas_call(body, name=name, grid=grid, in_specs=in_specs, out_specs=out_specs, out_shape=out_shape,
                              scratch_shapes=list(scratch), compiler_params=_cparams(sem))(*args)
    nin, nout, nscr = len(in_specs), len(out_specs), len(scratch)
    nri, nro = len(rider.ins), len(rider.outs)
    steps = math.prod(grid)

    def hosted(*refs):
        ins, refs = refs[:nin], refs[nin:]
        r_in, refs = refs[:nri], refs[nri:]
        outs, refs = refs[:nout], refs[nout:]
        r_out, refs = refs[:nro], refs[nro:]
        scr, sems = refs[:nscr], refs[nscr:]
        lin = 0
        for ax, size in enumerate(grid):
            lin = lin * size + pl.program_id(ax)

        @pl.when(lin == 0)
        def _():
            rider.start(r_in, r_out, sems)

        body(*ins, *outs, *scr)

        @pl.when(lin == steps - 1)
        def _():
            rider.finish(r_in, r_out, sems)

    any_spec = pl.BlockSpec(memory_space=pl.ANY)
    dma = pltpu.SemaphoreType.DMA
    return pl.pallas_call(
        hosted, name=name, grid=grid, in_specs=list(in_specs) + [any_spec] * nri,
        out_specs=list(out_specs) + [any_spec] * nro, out_shape=list(out_shape) + list(rider.outs),
        input_output_aliases={nin + i: nout + o for i, o in rider.alias.items()},
        scratch_shapes=list(scratch) + [dma(s) for s in rider.sems],
        compiler_params=pltpu.CompilerParams(dimension_semantics=("arbitrary",) * len(grid),
                                             vmem_limit_bytes=VMEM_LIMIT, has_side_effects=True),
    )(*args, *rider.ins)


def _blk(n, want):
    b = min(n, want)
    while n % b:
        b -= 1
    return b


NN = (((1,), (0,)), ((), ()))
NT = (((1,), (1,)), ((), ()))
TN = (((0,), (0,)), ((), ()))


def _matmul(name, a, b, *, a_spec, b_spec, o_spec, out_shape, grid, dn, nk, out_dtype, rider=None):
    nax = len(grid)

    def body(a_ref, b_ref, o_ref, *scr):
        part = lax.dot_general(a_ref[...], b_ref[...], dn, preferred_element_type=F32)
        if nk == 1:
            o_ref[...] = part.astype(o_ref.dtype)
            return
        acc = scr[0] if scr else o_ref
        k = pl.program_id(nax - 1)

        @pl.when(k == 0)
        def _():
            acc[...] = part

        @pl.when(k > 0)
        def _():
            acc[...] += part

        if scr:
            @pl.when(k == nk - 1)
            def _():
                o_ref[...] = acc[...].astype(o_ref.dtype)

    scratch = []
    if nk > 1 and out_dtype != F32:
        blk = tuple(d for d in o_spec.block_shape if d is not None)
        scratch = [pltpu.VMEM(blk, F32)]
    sem = ("parallel",) * (nax - 1) + ("arbitrary",)
    res = _call(name, body, grid=grid, in_specs=[a_spec, b_spec], out_specs=[o_spec],
                out_shape=[jax.ShapeDtypeStruct(out_shape, out_dtype)], scratch=scratch, sem=sem, args=(a, b), rider=rider)
    return res[0] if rider is None else (res[0], res[1:])


MM_TM = 1024
MM_TK = 2048


def mm_nn(name, a, w, out_dtype, tm=MM_TM):
    m, k = a.shape
    n = w.shape[1]
    tm = _blk(m, tm)
    return _matmul(name, a, w, a_spec=pl.BlockSpec((tm, k), lambda i, kk: (i, 0)),
                   b_spec=pl.BlockSpec((k, n), lambda i, kk: (0, 0)),
                   o_spec=pl.BlockSpec((tm, n), lambda i, kk: (i, 0)),
                   out_shape=(m, n), grid=(m // tm, 1), dn=NN, nk=1, out_dtype=out_dtype)


def mm_nt(name, a, w, out_dtype, tm=MM_TM, tn=None):
    m, n = a.shape
    k = w.shape[0]
    tm = _blk(m, tm)
    tn = k if tn is None else tn
    return _matmul(name, a, w, a_spec=pl.BlockSpec((tm, n), lambda i, j, kk: (i, 0)),
                   b_spec=pl.BlockSpec((tn, n), lambda i, j, kk: (j, 0)),
                   o_spec=pl.BlockSpec((tm, tn), lambda i, j, kk: (i, j)),
                   out_shape=(m, k), grid=(m // tm, k // tn, 1), dn=NT, nk=1, out_dtype=out_dtype)


def mm_tn(name, a, b, tmo=None, tk=MM_TK):
    l, m = a.shape
    n = b.shape[1]
    tk = _blk(l, tk)
    tmo = m if tmo is None else tmo
    nk = l // tk
    return _matmul(name, a, b, a_spec=pl.BlockSpec((tk, tmo), lambda i, kk: (kk, i)),
                   b_spec=pl.BlockSpec((tk, n), lambda i, kk: (kk, 0)),
                   o_spec=pl.BlockSpec((tmo, n), lambda i, kk: (i, 0)),
                   out_shape=(m, n), grid=(m // tmo, nk), dn=TN, nk=nk, out_dtype=F32)


def mm_nn_colshard(name, a, w4, out_dtype, tm=MM_TM):
    m, k = a.shape
    s, _, ns = w4.shape
    tm = _blk(m, tm)
    return _matmul(name, a, w4, a_spec=pl.BlockSpec((tm, k), lambda i, j, kk: (i, 0)),
                   b_spec=pl.BlockSpec((None, k, ns), lambda i, j, kk: (j, 0, 0)),
                   o_spec=pl.BlockSpec((tm, ns), lambda i, j, kk: (i, j)),
                   out_shape=(m, s * ns), grid=(m // tm, s, 1), dn=NN, nk=1, out_dtype=out_dtype)


def mm_nt_colshard(name, a, w4, out_dtype, tm=MM_TM):
    m, _ = a.shape
    s, k, ns = w4.shape
    tm = _blk(m, tm)
    return _matmul(name, a, w4, a_spec=pl.BlockSpec((tm, ns), lambda i, kk: (i, kk)),
                   b_spec=pl.BlockSpec((None, k, ns), lambda i, kk: (kk, 0, 0)),
                   o_spec=pl.BlockSpec((tm, k), lambda i, kk: (i, 0)),
                   out_shape=(m, k), grid=(m // tm, s), dn=NT, nk=s, out_dtype=out_dtype)


def mm_tn_colshard(name, a, b, s, tk=MM_TK):
    l, k = a.shape
    ns = b.shape[1] // s
    tk = _blk(l, tk)
    nk = l // tk
    return _matmul(name, a, b, a_spec=pl.BlockSpec((tk, k), lambda j, kk: (kk, 0)),
                   b_spec=pl.BlockSpec((tk, ns), lambda j, kk: (kk, j)),
                   o_spec=pl.BlockSpec((None, k, ns), lambda j, kk: (j, 0, 0)),
                   out_shape=(s, k, ns), grid=(s, nk), dn=TN, nk=nk, out_dtype=F32)


def mm_up(name, h, wup4, layer, tm=MM_TM, rider=None):
    m, k = h.shape
    s, _, _, ns = wup4.shape
    half = s // 2
    tm = _blk(m, tm)
    return _matmul(name, h, wup4, a_spec=pl.BlockSpec((tm, k), lambda i, j, kk: (i, 0)),
                   b_spec=pl.BlockSpec((None, None, k, ns), lambda i, j, kk: (j, layer, 0, 0)),
                   o_spec=pl.BlockSpec((None, tm, ns), lambda i, j, kk: (j // half, i, j % half)),
                   out_shape=(2, m, half * ns), grid=(m // tm, s, 1), dn=NN, nk=1, out_dtype=BF16, rider=rider)


def mm_up_nt(name, dgu, wup4, layer, tm=MM_TM, rider=None):
    _, m, _ = dgu.shape
    s, _, k, ns = wup4.shape
    half = s // 2
    tm = _blk(m, tm)
    return _matmul(name, dgu, wup4,
                   a_spec=pl.BlockSpec((None, tm, ns), lambda i, kk: (kk // half, i, kk % half)),
                   b_spec=pl.BlockSpec((None, None, k, ns), lambda i, kk: (kk, layer, 0, 0)),
                   o_spec=pl.BlockSpec((tm, k), lambda i, kk: (i, 0)),
                   out_shape=(m, k), grid=(m // tm, s), dn=NT, nk=s, out_dtype=F32, rider=rider)


def mm_up_tn(name, h, dgu, s, tk=MM_TK):
    l, k = h.shape
    half = s // 2
    ns = dgu.shape[2] // half
    tk = _blk(l, tk)
    nk = l // tk
    return _matmul(name, h, dgu, a_spec=pl.BlockSpec((tk, k), lambda j, kk: (kk, 0)),
                   b_spec=pl.BlockSpec((None, tk, ns), lambda j, kk: (j // half, kk, j % half)),
                   o_spec=pl.BlockSpec((None, k, ns), lambda j, kk: (j, 0, 0)),
                   out_shape=(s, k, ns), grid=(s, nk), dn=TN, nk=nk, out_dtype=F32)


def add_rmsnorm(name, x, r, gains, tm=512):
    l, d = x.shape
    tm = _blk(l, tm)
    ng = len(gains)
    has_r = r is not None

    def body(*refs):
        x_ref = refs[0]
        pos = 1
        xs = x_ref[...]
        if has_r:
            xs = xs + refs[pos][...]
            pos += 1
        g_refs = refs[pos:pos + ng]
        outs = refs[pos + ng:]
        o = 0
        if has_r:
            outs[0][...] = xs
            o = 1
        xh = xs * lax.rsqrt(jnp.mean(xs * xs, axis=-1, keepdims=True) + EPS)
        for gi in range(ng):
            outs[o + gi][...] = (xh * g_refs[gi][...]).astype(BF16)

    row = pl.BlockSpec((tm, d), lambda i: (i, 0))
    gsp = pl.BlockSpec((1, d), lambda i: (0, 0))
    ins = [x] + ([r] if has_r else []) + [g.reshape(1, d) for g in gains]
    in_specs = [row] * (1 + has_r) + [gsp] * ng
    out_shape = ([jax.ShapeDtypeStruct((l, d), F32)] if has_r else []) + [jax.ShapeDtypeStruct((l, d), BF16)] * ng
    return pl.pallas_call(body, name=name, grid=(l // tm,), in_specs=in_specs,
                          out_specs=[row] * len(out_shape), out_shape=out_shape,
                          compiler_params=_cparams(("parallel",)))(*ins)


def norm_bwd(name, x, dres, pairs, tm=512, bf16_copy=False, rider=None):
    l, d = x.shape
    tm = _blk(l, tm)
    npair = len(pairs)
    has_r = dres is not None

    def body(*refs):
        x_ref = refs[0]
        pos = 1
        xs = x_ref[...]
        dx = jnp.zeros_like(xs)
        if has_r:
            dx = refs[pos][...]
            pos += 1
        ins = refs[pos:pos + 2 * npair]
        outs = refs[pos + 2 * npair:]
        rs = lax.rsqrt(jnp.mean(xs * xs, axis=-1, keepdims=True) + EPS)
        xh = xs * rs
        first = pl.program_id(0) == 0
        for pi in range(npair):
            g = ins[2 * pi][...]
            dh = ins[2 * pi + 1][...].astype(F32)
            dgp = jnp.sum(dh * xh, axis=0, keepdims=True)
            dg_ref = outs[1 + pi]

            @pl.when(first)
            def _():
                dg_ref[...] = dgp

            @pl.when(jnp.logical_not(first))
            def _():
                dg_ref[...] += dgp

            dxh = dh * g
            dx = dx + rs * (dxh - xh * jnp.mean(dxh * xh, axis=-1, keepdims=True))
        outs[0][...] = dx
        if bf16_copy:
            outs[1 + npair][...] = dx.astype(BF16)

    row = pl.BlockSpec((tm, d), lambda i: (i, 0))
    gsp = pl.BlockSpec((1, d), lambda i: (0, 0))
    ins = [x] + ([dres] if has_r else [])
    in_specs = [row] * (1 + has_r)
    for g, dh in pairs:
        ins += [g.reshape(1, d), dh]
        in_specs += [gsp, row]
    out_shape = [jax.ShapeDtypeStruct((l, d), F32)] + [jax.ShapeDtypeStruct((1, d), F32)] * npair
    out_specs = [row] + [gsp] * npair
    if bf16_copy:
        out_shape.append(jax.ShapeDtypeStruct((l, d), BF16))
        out_specs.append(row)
    return _call(name, body, grid=(l // tm,), in_specs=in_specs, out_specs=out_specs, out_shape=out_shape,
                 sem=("arbitrary",), args=ins, rider=rider)


def final_loss(name, x, r, tgt, g, tm=512):
    l, d = x.shape
    tm = _blk(l, tm)

    def body(x_ref, r_ref, t_ref, g_ref, loss_ref, dx_ref, dg_ref, dxb_ref):
        xs = x_ref[...] + r_ref[...]
        gg = g_ref[...]
        rs = lax.rsqrt(jnp.mean(xs * xs, axis=-1, keepdims=True) + EPS)
        xh = xs * rs
        e = xh * gg - t_ref[...]
        part = 0.5 * jnp.sum(jnp.mean(e * e, axis=-1, keepdims=True), axis=0, keepdims=True)
        dy = e * (1.0 / d)
        dgp = jnp.sum(dy * xh, axis=0, keepdims=True)
        dxh = dy * gg
        dx = rs * (dxh - xh * jnp.mean(dxh * xh, axis=-1, keepdims=True))
        dx_ref[...] = dx
        dxb_ref[...] = dx.astype(BF16)
        first = pl.program_id(0) == 0

        @pl.when(first)
        def _():
            loss_ref[...] = jnp.broadcast_to(part, loss_ref.shape)
            dg_ref[...] = dgp

        @pl.when(jnp.logical_not(first))
        def _():
            loss_ref[...] += jnp.broadcast_to(part, loss_ref.shape)
            dg_ref[...] += dgp

    row = pl.BlockSpec((tm, d), lambda i: (i, 0))
    gsp = pl.BlockSpec((1, d), lambda i: (0, 0))
    lsp = pl.BlockSpec((8, 128), lambda i: (0, 0))
    return pl.pallas_call(
        body, name=name, grid=(l // tm,), in_specs=[row, row, row, gsp], out_specs=[lsp, row, gsp, row],
        out_shape=[jax.ShapeDtypeStruct((8, 128), F32), jax.ShapeDtypeStruct((l, d), F32),
                   jax.ShapeDtypeStruct((1, d), F32), jax.ShapeDtypeStruct((l, d), BF16)],
        compiler_params=_cparams(("arbitrary",)))(x, r, tgt, g.reshape(1, d))


def _sigmoid(x):
    return 0.5 * jnp.tanh(0.5 * x) + 0.5


def glu_fwd(name, z, tm=512):
    l, d2 = z.shape
    d = d2 // 2
    tm = _blk(l, tm)

    def body(z_ref, o_ref):
        o_ref[...] = z_ref[:, :d] * _sigmoid(z_ref[:, d:])

    return pl.pallas_call(body, name=name, grid=(l // tm,),
                          in_specs=[pl.BlockSpec((tm, d2), lambda i: (i, 0))],
                          out_specs=pl.BlockSpec((tm, d), lambda i: (i, 0)),
                          out_shape=jax.ShapeDtypeStruct((l, d), F32),
                          compiler_params=_cparams(("parallel",)))(z)


def glu_bwd(name, z, dm, tm=512):
    l, d2 = z.shape
    d = d2 // 2
    tm = _blk(l, tm)

    def body(z_ref, dm_ref, o_ref):
        sg = _sigmoid(z_ref[:, d:])
        g = dm_ref[...]
        o_ref[:, :d] = (g * sg).astype(BF16)
        o_ref[:, d:] = (g * z_ref[:, :d] * sg * (1.0 - sg)).astype(BF16)

    return pl.pallas_call(body, name=name, grid=(l // tm,),
                          in_specs=[pl.BlockSpec((tm, d2), lambda i: (i, 0)), pl.BlockSpec((tm, d), lambda i: (i, 0))],
                          out_specs=pl.BlockSpec((tm, d2), lambda i: (i, 0)),
                          out_shape=jax.ShapeDtypeStruct((l, d2), BF16),
                          compiler_params=_cparams(("parallel",)))(z, dm)


_GELU_C = math.sqrt(2.0 / math.pi)


def _gelu(y):
    return 0.5 * y * (1.0 + jnp.tanh(_GELU_C * (y + 0.044715 * y * y * y)))


def _gelu_grad(y):
    t = jnp.tanh(_GELU_C * (y + 0.044715 * y * y * y))
    return 0.5 * (1.0 + t) + 0.5 * y * (1.0 - t * t) * _GELU_C * (1.0 + 3.0 * 0.044715 * y * y)


def gelu_bwd(name, ypre, dyg, tm=512):
    l, d = ypre.shape
    tm = _blk(l, tm)

    def body(y_ref, d_ref, o_ref):
        o_ref[...] = (d_ref[...] * _gelu_grad(y_ref[...])).astype(BF16)

    row = pl.BlockSpec((tm, d), lambda i: (i, 0))
    return pl.pallas_call(body, name=name, grid=(l // tm,), in_specs=[row, row], out_specs=row,
                          out_shape=jax.ShapeDtypeStruct((l, d), BF16),
                          compiler_params=_cparams(("parallel",)))(ypre, dyg)


FFN_ROWS = 128
HALO = 16


def _taps_back(tail, g):
    ext = jnp.concatenate([tail, g], axis=0)
    return pltpu.roll(ext, 1, axis=0)[HALO:], pltpu.roll(ext, 2, axis=0)[HALO:]


def ffn_mid_fwd(name, gu, cw, cb, tc=128):
    _, l, f = gu.shape
    tc = _blk(f, tc)

    def body(gu_ref, w_ref, b_ref, a_ref):
        g = gu_ref[0].astype(F32)
        u = gu_ref[1].astype(F32)
        g1, g2 = _taps_back(jnp.zeros((HALO, tc), F32), g)
        gc = w_ref[0:1, :] * g2 + w_ref[1:2, :] * g1 + w_ref[2:3, :] * g + b_ref[...]
        a_ref[...] = (gc * _sigmoid(gc) * u).astype(BF16)

    return pl.pallas_call(
        body, name=name, grid=(f // tc,),
        in_specs=[pl.BlockSpec((2, l, tc), lambda c: (0, 0, c)), pl.BlockSpec((3, tc), lambda c: (0, c)),
                  pl.BlockSpec((1, tc), lambda c: (0, c))],
        out_specs=pl.BlockSpec((l, tc), lambda c: (0, c)),
        out_shape=jax.ShapeDtypeStruct((l, f), BF16),
        compiler_params=_cparams(("parallel",)))(gu, cw, cb)


def ffn_mid_bwd(name, gu, da, cw, cb, tc=128):
    _, l, f = gu.shape
    tc = _blk(f, tc)
    rc = _blk(l, FFN_ROWS)
    nchunk = l // rc

    def body(gu_ref, da_ref, w_ref, b_ref, dgu_ref, dw_ref, db_ref):
        w0, w1, w2, b = w_ref[0:1, :], w_ref[1:2, :], w_ref[2:3, :], b_ref[...]

        def fold(x):
            return jnp.sum(x.reshape(rc // 8, 8, tc), axis=0)

        def chunk(k, carry):
            head, s0, s1, s2, sb = carry
            ci = nchunk - 1 - k
            r0 = pl.multiple_of(ci * rc, rc)
            t0 = pl.multiple_of(jnp.maximum(r0 - HALO, 0), HALO)
            tail = jnp.where(ci > 0, gu_ref[0, pl.ds(t0, HALO), :].astype(F32), 0.0)
            g = gu_ref[0, pl.ds(r0, rc), :].astype(F32)
            u = gu_ref[1, pl.ds(r0, rc), :].astype(F32)
            da = da_ref[pl.ds(r0, rc), :].astype(F32)
            g1, g2 = _taps_back(tail, g)
            gc = w0 * g2 + w1 * g1 + w2 * g + b
            sg = _sigmoid(gc)
            dgu_ref[1, pl.ds(r0, rc), :] = (da * gc * sg).astype(BF16)
            dgc = da * u * (sg * (1.0 + gc * (1.0 - sg)))
            ext = jnp.concatenate([dgc, head], axis=0)
            up1 = pltpu.roll(ext, rc + HALO - 1, axis=0)[:rc]
            up2 = pltpu.roll(ext, rc + HALO - 2, axis=0)[:rc]
            dgu_ref[0, pl.ds(r0, rc), :] = (w2 * dgc + w1 * up1 + w0 * up2).astype(BF16)
            return dgc[:HALO], s0 + fold(dgc * g2), s1 + fold(dgc * g1), s2 + fold(dgc * g), sb + fold(dgc)

        z8 = jnp.zeros((8, tc), F32)
        _, s0, s1, s2, sb = lax.fori_loop(0, nchunk, chunk, (jnp.zeros((HALO, tc), F32), z8, z8, z8, z8))
        dw_ref[0:1, :] = jnp.sum(s0, axis=0, keepdims=True)
        dw_ref[1:2, :] = jnp.sum(s1, axis=0, keepdims=True)
        dw_ref[2:3, :] = jnp.sum(s2, axis=0, keepdims=True)
        db_ref[...] = jnp.sum(sb, axis=0, keepdims=True)

    return pl.pallas_call(
        body, name=name, grid=(f // tc,),
        in_specs=[pl.BlockSpec((2, l, tc), lambda c: (0, 0, c)), pl.BlockSpec((l, tc), lambda c: (0, c)),
                  pl.BlockSpec((3, tc), lambda c: (0, c)), pl.BlockSpec((1, tc), lambda c: (0, c))],
        out_specs=[pl.BlockSpec((2, l, tc), lambda c: (0, 0, c)), pl.BlockSpec((3, tc), lambda c: (0, c)),
                   pl.BlockSpec((1, tc), lambda c: (0, c))],
        out_shape=[jax.ShapeDtypeStruct((2, l, f), BF16), jax.ShapeDtypeStruct((3, f), F32),
                   jax.ShapeDtypeStruct((1, f), F32)],
        compiler_params=_cparams(("parallel",)))(gu, da, cw, cb)


SCAN_LANES = 512


def ssm_discretize(name, a_re, a_im, ldt, b_re, b_im):
    dt = jnp.exp(ldt)
    mag = jnp.exp(a_re * dt)
    ab_re = mag * jnp.cos(a_im * dt)
    ab_im = mag * jnp.sin(a_im * dt)
    den = a_re * a_re + a_im * a_im
    f_re = ((ab_re - 1.0) * a_re + ab_im * a_im) / den
    f_im = (ab_im * a_re - (ab_re - 1.0) * a_im) / den
    return ab_re, ab_im, f_re * b_re - f_im * b_im, f_re * b_im + f_im * b_re


def ssm_disc_fwd(name, a_re, a_im, ldt, b_re, b_im):
    def body(ar, ai, ld, br, bi, o_ar, o_ai, o_br, o_bi):
        r = ssm_discretize(None, ar[...], ai[...], ld[...], br[...], bi[...])
        o_ar[...], o_ai[...], o_br[...], o_bi[...] = r

    sd = jax.ShapeDtypeStruct
    return pl.pallas_call(body, name=name,
                          out_shape=[sd(a_re.shape, F32), sd(a_re.shape, F32), sd(b_re.shape, F32), sd(b_re.shape, F32)],
                          compiler_params=_cparams())(a_re, a_im, ldt, b_re, b_im)


def ssm_disc_bwd(name, a_re, a_im, ldt, b_re, b_im, d_ar, d_ai, d_br, d_bi):
    def body(ar, ai, ld, br, bi, g_ar, g_ai, g_br, g_bi, o_ar, o_ai, o_ld, o_br, o_bi):
        fn = functools.partial(ssm_discretize, None)
        _, vjp = jax.vjp(fn, ar[...], ai[...], ld[...], br[...], bi[...])
        r = vjp((g_ar[...], g_ai[...], g_br[...], g_bi[...]))
        o_ar[...], o_ai[...], o_ld[...], o_br[...], o_bi[...] = r

    sd = jax.ShapeDtypeStruct
    return pl.pallas_call(body, name=name,
                          out_shape=[sd(a_re.shape, F32)] * 3 + [sd(b_re.shape, F32)] * 2,
                          compiler_params=_cparams())(a_re, a_im, ldt, b_re, b_im, d_ar, d_ai, d_br, d_bi)


def _drive(x_ref, wre_ref, wim_ref, dre_ref, dim_ref):
    xb = x_ref[...]
    dre_ref[...] = jnp.dot(xb, wre_ref[...], preferred_element_type=F32)
    dim_ref[...] = jnp.dot(xb, wim_ref[...], preferred_element_type=F32)


def _ssm_dims(l, wre, tj):
    nb, kb, nsb = wre.shape
    jn = l // N_STREAMS
    tj = _blk(jn, tj)
    return nb, kb, nsb, jn, tj, tj * N_STREAMS, jn // tj


def _scan(dre_ref, dim_ref, st_re, st_im, a_ref, tj, reverse, write):
    ns = dre_ref.shape[1]
    cw = min(SCAN_LANES, ns)
    for cb in range(ns // cw):
        sl = slice(cb * cw, (cb + 1) * cw)
        ar = jnp.broadcast_to(a_ref[0:1, sl], (N_STREAMS, cw))
        ai = jnp.broadcast_to(a_ref[1:2, sl], (N_STREAMS, cw))

        def step(jj, carry, sl=sl, ar=ar, ai=ai):
            sr, si = carry
            j = (tj - 1 - jj) if reverse else jj
            off = pl.multiple_of(j * N_STREAMS, N_STREAMS)
            nr = ar * sr - ai * si + dre_ref[pl.ds(off, N_STREAMS), sl]
            ni = ar * si + ai * sr + dim_ref[pl.ds(off, N_STREAMS), sl]
            if write:
                dre_ref[pl.ds(off, N_STREAMS), sl] = nr
                dim_ref[pl.ds(off, N_STREAMS), sl] = ni
            return nr, ni

        sr, si = lax.fori_loop(0, tj, step, (st_re[:, sl], st_im[:, sl]))
        st_re[:, sl] = sr
        st_im[:, sl] = si


def ssm_pass1(name, x, wre, wim, a2, *, reverse, tj=64, rider=None):
    l, d = x.shape
    nb, kb, nsb, jn, tj, r, nblk = _ssm_dims(l, wre, tj)
    ns = nb * nsb

    def body(x_ref, wre_ref, wim_ref, a_ref, cre_ref, cim_ref, dre, dim, st_re, st_im):
        i = pl.program_id(1)

        @pl.when(i == 0)
        def _():
            st_re[...] = jnp.zeros_like(st_re)
            st_im[...] = jnp.zeros_like(st_im)

        _drive(x_ref, wre_ref, wim_ref, dre, dim)
        _scan(dre, dim, st_re, st_im, a_ref, tj, reverse, False)

        @pl.when(i == nblk - 1)
        def _():
            pr, pi = a_ref[0:1, :], a_ref[1:2, :]
            rr, ri = jnp.ones_like(pr), jnp.zeros_like(pr)
            e = jn
            while e:
                if e & 1:
                    rr, ri = rr * pr - ri * pi, rr * pi + ri * pr
                pr, pi = pr * pr - pi * pi, 2.0 * pr * pi
                e >>= 1
            order = range(N_STREAMS - 1, -1, -1) if reverse else range(N_STREAMS)
            cr = jnp.zeros_like(rr)
            ci = jnp.zeros_like(rr)
            for s in order:
                cre_ref[s:s + 1, :] = cr
                cim_ref[s:s + 1, :] = ci
                fr, fi = st_re[s:s + 1, :], st_im[s:s + 1, :]
                cr, ci = fr + rr * cr - ri * ci, fi + rr * ci + ri * cr

    blk = (lambda b, i: (nblk - 1 - i, b)) if reverse else (lambda b, i: (i, b))
    w3 = pl.BlockSpec((None, kb, nsb), lambda b, i: (b, 0, 0))
    st = pl.BlockSpec((N_STREAMS, nsb), lambda b, i: (0, b))
    return _call(
        name, body, grid=(nb, nblk),
        in_specs=[pl.BlockSpec((r, kb), blk), w3, w3, pl.BlockSpec((2, nsb), lambda b, i: (0, b))],
        out_specs=[st, st], out_shape=[jax.ShapeDtypeStruct((N_STREAMS, ns), F32)] * 2,
        scratch=[pltpu.VMEM((r, nsb), F32), pltpu.VMEM((r, nsb), F32),
                 pltpu.VMEM((N_STREAMS, nsb), F32), pltpu.VMEM((N_STREAMS, nsb), F32)],
        sem=("parallel", "arbitrary"), args=(x, wre, wim, a2), rider=rider)


def ssm_fwd2(name, u, bre, bim, a2, init_re, init_im, cre, cim, dskip, *, tj=64, rider=None):
    l, d = u.shape
    nb, kb, nsb, jn, tj, r, nblk = _ssm_dims(l, bre, tj)
    ns = nb * nsb

    def body(u_ref, bre_ref, bim_ref, a_ref, ire_ref, iim_ref, cre_ref, cim_ref, d_ref,
             sre_ref, sim_ref, y_ref, yg_ref, dre, dim, st_re, st_im):
        @pl.when(pl.program_id(1) == 0)
        def _():
            st_re[...] = ire_ref[...]
            st_im[...] = iim_ref[...]

        _drive(u_ref, bre_ref, bim_ref, dre, dim)
        _scan(dre, dim, st_re, st_im, a_ref, tj, False, True)
        srb, sib = dre[...].astype(BF16), dim[...].astype(BF16)
        sre_ref[...] = srb
        sim_ref[...] = sib
        y = (jnp.dot(srb, cre_ref[...], preferred_element_type=F32)
             - jnp.dot(sib, cim_ref[...], preferred_element_type=F32)
             + d_ref[...] * u_ref[...].astype(F32))
        y_ref[...] = y
        yg_ref[...] = _gelu(y).astype(BF16)

    rows = lambda w: pl.BlockSpec((r, w), lambda b, i: (i, b))
    w3 = pl.BlockSpec((None, kb, nsb), lambda b, i: (b, 0, 0))
    c3 = pl.BlockSpec((None, nsb, kb), lambda b, i: (b, 0, 0))
    st = pl.BlockSpec((N_STREAMS, nsb), lambda b, i: (0, b))
    return _call(
        name, body, grid=(nb, nblk),
        in_specs=[rows(kb), w3, w3, pl.BlockSpec((2, nsb), lambda b, i: (0, b)), st, st, c3, c3,
                  pl.BlockSpec((1, kb), lambda b, i: (0, b))],
        out_specs=[rows(nsb), rows(nsb), rows(kb), rows(kb)],
        out_shape=[jax.ShapeDtypeStruct((l, ns), BF16), jax.ShapeDtypeStruct((l, ns), BF16),
                   jax.ShapeDtypeStruct((l, d), F32), jax.ShapeDtypeStruct((l, d), BF16)],
        scratch=[pltpu.VMEM((r, nsb), F32), pltpu.VMEM((r, nsb), F32),
                 pltpu.VMEM((N_STREAMS, nsb), F32), pltpu.VMEM((N_STREAMS, nsb), F32)],
        sem=("parallel", "arbitrary"), args=(u, bre, bim, a2, init_re, init_im, cre, cim, dskip), rider=rider)


def ssm_bwd2(name, dy, u, sre, sim, ctre, ctim, a2c, init_re, init_im, fre, fim, bre, bim, dskip, *, tj=64, rider=None):
    l, d = u.shape
    nb, kb, nsb, jn, tj, r, nblk = _ssm_dims(l, bre, tj)
    ns = nb * nsb

    def body(dy_ref, u_ref, sre_ref, sim_ref, pre_ref, pim_ref, ctre_ref, ctim_ref, a_ref, ire_ref, iim_ref,
             fre_ref, fim_ref, bre_ref, bim_ref, d_ref,
             du_ref, dbre_ref, dbim_ref, dcre_ref, dcim_ref, dare_ref, daim_ref, dd_ref,
             lre, lim, sf_re, sf_im, st_re, st_im):
        i = pl.program_id(1)
        first = i == 0
        sf_re[...] = sre_ref[...].astype(F32)
        sf_im[...] = sim_ref[...].astype(F32)

        @pl.when(first)
        def _():
            st_re[...] = ire_ref[...]
            st_im[...] = iim_ref[...]
            dbre_ref[...] = jnp.zeros_like(dbre_ref)
            dbim_ref[...] = jnp.zeros_like(dbim_ref)
            dcre_ref[...] = jnp.zeros_like(dcre_ref)
            dcim_ref[...] = jnp.zeros_like(dcim_ref)
            dare_ref[...] = jnp.zeros_like(dare_ref)
            daim_ref[...] = jnp.zeros_like(daim_ref)
            dd_ref[...] = jnp.zeros_like(dd_ref)

        _drive(dy_ref, ctre_ref, ctim_ref, lre, lim)
        _scan(lre, lim, st_re, st_im, a_ref, tj, True, True)

        is_t0 = i == nblk - 1
        cw = min(SCAN_LANES, nsb)
        for cb in range(nsb // cw):
            sl = slice(cb * cw, (cb + 1) * cw)
            p_r = jnp.where(is_t0, fre_ref[:, sl], pre_ref[:, sl].astype(F32)[N_STREAMS:])
            p_i = jnp.where(is_t0, fim_ref[:, sl], pim_ref[:, sl].astype(F32)[N_STREAMS:])
            l_r, l_i = lre[0:N_STREAMS, sl], lim[0:N_STREAMS, sl]
            acc = (l_r * p_r + l_i * p_i, l_i * p_r - l_r * p_i)

            def step(jj, carry, sl=sl):
                a_r, a_i = carry
                off = pl.multiple_of(jj * N_STREAMS, N_STREAMS)
                prev = pl.multiple_of((jj - 1) * N_STREAMS, N_STREAMS)
                l_r, l_i = lre[pl.ds(off, N_STREAMS), sl], lim[pl.ds(off, N_STREAMS), sl]
                p_r, p_i = sf_re[pl.ds(prev, N_STREAMS), sl], sf_im[pl.ds(prev, N_STREAMS), sl]
                return a_r + l_r * p_r + l_i * p_i, a_i + l_i * p_r - l_r * p_i

            a_r, a_i = lax.fori_loop(1, tj, step, acc)
            dare_ref[:, sl] += a_r
            daim_ref[:, sl] += a_i

        dyf = dy_ref[...].astype(F32)
        uf = u_ref[...].astype(F32)
        dd_ref[...] += jnp.sum(dyf * uf, axis=0, keepdims=True)
        lrb = lre[...].astype(BF16)
        lib = lim[...].astype(BF16)
        ub = u_ref[...]
        dyb = dy_ref[...]
        dbre_ref[...] += lax.dot_general(ub, lrb, TN, preferred_element_type=F32)
        dbim_ref[...] += lax.dot_general(ub, lib, TN, preferred_element_type=F32)
        dcre_ref[...] += lax.dot_general(sre_ref[...].astype(BF16), dyb, TN, preferred_element_type=F32)
        dcim_ref[...] -= lax.dot_general(sim_ref[...].astype(BF16), dyb, TN, preferred_element_type=F32)
        du = (lax.dot_general(lrb, bre_ref[...], NT, preferred_element_type=F32)
              + lax.dot_general(lib, bim_ref[...], NT, preferred_element_type=F32)
              + d_ref[...] * dyf)
        du_ref[...] = du.astype(BF16)

    rev = lambda w: pl.BlockSpec((r, w), lambda b, i: (nblk - 1 - i, b))
    assert tj % 2 == 0 or nblk == 1
    prev_tile = pl.BlockSpec((2 * N_STREAMS, nsb), lambda b, i: (jnp.maximum((nblk - 1 - i) * tj, 2) // 2 - 1, b))
    st = pl.BlockSpec((N_STREAMS, nsb), lambda b, i: (0, b))
    w3 = pl.BlockSpec((None, kb, nsb), lambda b, i: (b, 0, 0))
    c3 = pl.BlockSpec((None, nsb, kb), lambda b, i: (b, 0, 0))
    dsp = pl.BlockSpec((1, kb), lambda b, i: (0, b))
    return _call(
        name, body, grid=(nb, nblk),
        in_specs=[rev(kb), rev(kb), rev(nsb), rev(nsb), prev_tile, prev_tile, w3, w3,
                  pl.BlockSpec((2, nsb), lambda b, i: (0, b)), st, st, st, st, w3, w3, dsp],
        out_specs=[rev(kb), w3, w3, c3, c3, st, st, dsp],
        out_shape=[jax.ShapeDtypeStruct((l, d), BF16), jax.ShapeDtypeStruct((nb, kb, nsb), F32),
                   jax.ShapeDtypeStruct((nb, kb, nsb), F32), jax.ShapeDtypeStruct((nb, nsb, kb), F32),
                   jax.ShapeDtypeStruct((nb, nsb, kb), F32), jax.ShapeDtypeStruct((N_STREAMS, ns), F32),
                   jax.ShapeDtypeStruct((N_STREAMS, ns), F32), jax.ShapeDtypeStruct((1, d), F32)],
        scratch=[pltpu.VMEM((r, nsb), F32), pltpu.VMEM((r, nsb), F32), pltpu.VMEM((r, nsb), F32), pltpu.VMEM((r, nsb), F32),
                 pltpu.VMEM((N_STREAMS, nsb), F32), pltpu.VMEM((N_STREAMS, nsb), F32)],
        sem=("parallel", "arbitrary"),
        args=(dy, u, sre, sim, sre, sim, ctre, ctim, a2c, init_re, init_im, fre, fim, bre, bim, dskip), rider=rider)


ATT_TILE = 256
ATT_HEADS = 4
EXP_ZERO_BELOW = -104.0
LANES = 128


def _split_dot(x, tri):
    hi = x.astype(BF16)
    lo = (x - hi.astype(F32)).astype(BF16)
    return jnp.dot(hi, tri, preferred_element_type=F32) + jnp.dot(lo, tri, preferred_element_type=F32)


def _sb_logs(z, causal):
    sp = jnp.maximum(z, 0.0) + jnp.log(1.0 + jnp.exp(-jnp.abs(z)))
    l1m = -sp
    if causal is not None:
        l1m = jnp.where(causal, l1m, 0.0)
    return z - sp, l1m


def attn_fwd(name, q, kv, t=ATT_TILE, rider=None):
    l, dm = q.shape
    dh = HEAD_DIM
    h = dm // dh
    t = _blk(l, t)
    hb = _blk(h, ATT_HEADS)
    wb = hb * dh
    scale = dh ** -0.5

    def body(q_ref, k_ref, v_ref, o_ref, tot_ref):
        i = pl.program_id(1)
        hd = lambda g: slice(g * dh, (g + 1) * dh)
        qs = [(q_ref[:, hd(g)].astype(F32) * scale).astype(BF16) for g in range(hb)]
        row = lax.broadcasted_iota(jnp.int32, (t, t), 0)
        col = lax.broadcasted_iota(jnp.int32, (t, t), 1)
        tri_gt = jnp.where(row > col, 1.0, 0.0).astype(BF16)
        causal = col < row

        def tile(jb, carry, mask):
            off = pl.multiple_of(jb * t, t)
            heads = range(hb)
            z = [lax.dot_general(qs[g], k_ref[pl.ds(off, t), hd(g)], NT, preferred_element_type=F32) for g in heads]
            logs = [_sb_logs(z[g], mask) for g in heads]
            rem = [_split_dot(logs[g][1], tri_gt) for g in heads]
            w = [jnp.exp(logs[g][0] + rem[g] + carry[g][1]) for g in heads]
            if mask is not None:
                w = [jnp.where(mask, w[g], 0.0) for g in heads]
            pv = [jnp.dot(w[g].astype(BF16), v_ref[pl.ds(off, t), hd(g)], preferred_element_type=F32) for g in heads]
            return tuple((carry[g][0] + pv[g], carry[g][1] + rem[g][:, 0:1] + logs[g][1][:, 0:1]) for g in heads)

        def live(carry):
            top = carry[0][1]
            for g in range(1, hb):
                top = jnp.maximum(top, carry[g][1])
            return (jnp.max(top) > EXP_ZERO_BELOW).astype(jnp.int32)

        def more(c):
            it, alive, _ = c
            return jnp.logical_and(it < i, alive > 0)

        def step(c):
            it, _, carry = c
            carry = tile(i - 1 - it, carry, None)
            return it + 1, live(carry), carry

        carry = tile(i, ((jnp.zeros((t, dh), F32), jnp.zeros((t, 1), F32)),) * hb, causal)
        done, _, carry = lax.while_loop(more, step, (jnp.int32(0), live(carry), carry))
        tot_ref[...] = jnp.zeros_like(tot_ref)
        for g, (acc, run) in enumerate(carry):
            o_ref[:, hd(g)] = acc.astype(BF16)
            tot_ref[:, g:g + 1] = run
        tot_ref[:, hb:hb + 1] = jnp.full((t, 1), done, jnp.int32).astype(F32)

    qsp = pl.BlockSpec((t, wb), lambda hh, i: (i, hh))
    ksp = pl.BlockSpec((None, l, wb), lambda hh, i: (0, 0, hh))
    vsp = pl.BlockSpec((None, l, wb), lambda hh, i: (1, 0, hh))
    tsp = pl.BlockSpec((None, t, LANES), lambda hh, i: (hh, i, 0))
    return _call(name, body, grid=(h // hb, l // t), in_specs=[qsp, ksp, vsp], out_specs=[qsp, tsp],
                 out_shape=[jax.ShapeDtypeStruct((l, dm), BF16), jax.ShapeDtypeStruct((h // hb, l, LANES), F32)],
                 sem=("parallel", "parallel"), args=(q, kv, kv), rider=rider)


def attn_bwd(name, q, kv, tot, do, t=ATT_TILE, rider=None):
    l, dm = q.shape
    dh = HEAD_DIM
    h = dm // dh
    t = _blk(l, t)
    hb = _blk(h, ATT_HEADS)
    wb = hb * dh
    nq = l // t
    scale = dh ** -0.5

    def body(q_ref, k_ref, v_ref, tot_ref, do_ref, dq_ref, dkv_ref, dk_ref, dv_ref):
        i = pl.program_id(1)
        hd = lambda g: slice(g * dh, (g + 1) * dh)

        @pl.when(i == 0)
        def _():
            dk_ref[...] = jnp.zeros_like(dk_ref)
            dv_ref[...] = jnp.zeros_like(dv_ref)

        qs = [(q_ref[:, hd(g)].astype(F32) * scale).astype(BF16) for g in range(hb)]
        dob = [do_ref[:, hd(g)] for g in range(hb)]
        total = [tot_ref[:, g:g + 1] for g in range(hb)]
        row = lax.broadcasted_iota(jnp.int32, (t, t), 0)
        col = lax.broadcasted_iota(jnp.int32, (t, t), 1)
        tri_le = jnp.where(row <= col, 1.0, 0.0).astype(BF16)
        tri_lt = jnp.where(row < col, 1.0, 0.0).astype(BF16)
        causal = col < row

        def tile(jb, carry, mask):
            off = pl.multiple_of(jb * t, t)
            heads = range(hb)
            kj = [k_ref[pl.ds(off, t), hd(g)] for g in heads]
            z = [lax.dot_general(qs[g], kj[g], NT, preferred_element_type=F32) for g in heads]
            dp = [lax.dot_general(dob[g], v_ref[pl.ds(off, t), hd(g)], NT, preferred_element_type=F32) for g in heads]
            logs = [_sb_logs(z[g], mask) for g in heads]
            lpre = [_split_dot(logs[g][1], tri_le) for g in heads]
            w = [jnp.exp(logs[g][0] + (total[g] - carry[g][1] - lpre[g])) for g in heads]
            if mask is not None:
                w = [jnp.where(mask, w[g], 0.0) for g in heads]
            p = [w[g] * dp[g] for g in heads]
            for g in heads:
                dv_ref[pl.ds(off, t), hd(g)] += lax.dot_general(w[g].astype(BF16), dob[g], TN, preferred_element_type=F32)
            qpre = [carry[g][2] + jnp.dot(p[g].astype(BF16), tri_lt, preferred_element_type=F32) for g in heads]
            dz = [p[g] - jnp.exp(logs[g][0]) * (p[g] + qpre[g]) for g in heads]
            if mask is not None:
                dz = [jnp.where(mask, dz[g], 0.0) for g in heads]
            dzb = [dz[g].astype(BF16) for g in heads]
            for g in heads:
                dk_ref[pl.ds(off, t), hd(g)] += lax.dot_general(dzb[g], qs[g], TN, preferred_element_type=F32)
            dq = [carry[g][0] + jnp.dot(dzb[g], kj[g], preferred_element_type=F32) for g in heads]
            return tuple((dq[g], carry[g][1] + lpre[g][:, t - 1:t], qpre[g][:, t - 1:t] + p[g][:, t - 1:t]) for g in heads)

        zero = jnp.zeros((t, 1), F32)
        done = jnp.max(tot_ref[:, hb:hb + 1]).astype(jnp.int32)
        carry = lax.fori_loop(i - done, i, lambda jb, c: tile(jb, c, None), ((jnp.zeros((t, dh), F32), zero, zero),) * hb)
        carry = tile(i, carry, causal)
        for g in range(hb):
            dq_ref[:, hd(g)] = (carry[g][0] * scale).astype(BF16)

        @pl.when(i == nq - 1)
        def _():
            dkv_ref[0] = dk_ref[...].astype(BF16)
            dkv_ref[1] = dv_ref[...].astype(BF16)

    qsp = pl.BlockSpec((t, wb), lambda hh, i: (i, hh))
    ksp = pl.BlockSpec((None, l, wb), lambda hh, i: (0, 0, hh))
    vsp = pl.BlockSpec((None, l, wb), lambda hh, i: (1, 0, hh))
    tsp = pl.BlockSpec((None, t, LANES), lambda hh, i: (hh, i, 0))
    return _call(name, body, grid=(h // hb, nq), in_specs=[qsp, ksp, vsp, tsp, qsp],
                 out_specs=[qsp, pl.BlockSpec((2, l, wb), lambda hh, i: (0, 0, hh))],
                 out_shape=[jax.ShapeDtypeStruct((l, dm), BF16), jax.ShapeDtypeStruct((2, l, dm), BF16)],
                 scratch=[pltpu.VMEM((l, wb), F32), pltpu.VMEM((l, wb), F32)],
                 sem=("parallel", "arbitrary"), args=(q, kv, kv, tot, do), rider=rider)


def adamw(name, w, g, m, v):
    shape = w.shape
    cols = shape[-1]
    rows = w.size // cols
    tr = rows
    if rows % 8 == 0:
        tr = 8 * _blk(rows // 8, 64)
    c1 = 1.0 - ADAM_B1 ** ADAM_STEP
    c2 = 1.0 - ADAM_B2 ** ADAM_STEP

    def body(w_ref, g_ref, m_ref, v_ref, d_ref, nm_ref, nv_ref):
        gg = g_ref[...]
        mm = ADAM_B1 * m_ref[...] + (1.0 - ADAM_B1) * gg
        vv = ADAM_B2 * v_ref[...] + (1.0 - ADAM_B2) * (gg * gg)
        nm_ref[...] = mm
        nv_ref[...] = vv
        d_ref[...] = -ADAM_LR * ((mm / c1) / (jnp.sqrt(vv / c2) + ADAM_EPS) + ADAM_WD * w_ref[...])

    sp = pl.BlockSpec((tr, cols), lambda i: (i, 0))
    sd = jax.ShapeDtypeStruct((rows, cols), F32)
    outs = pl.pallas_call(body, name=name, grid=(rows // tr,), in_specs=[sp] * 4, out_specs=[sp] * 3,
                          out_shape=[sd] * 3, compiler_params=_cparams(("parallel",)))(
        w.reshape(rows, cols), g.reshape(rows, cols), m.reshape(rows, cols), v.reshape(rows, cols))
    return tuple(o.reshape(shape) for o in outs)


def _perm(a):
    l, d = a.shape
    return a.reshape(N_STREAMS, l // N_STREAMS, d).transpose(1, 0, 2).reshape(l, d)


def _unperm(a):
    l, d = a.shape
    return a.reshape(l // N_STREAMS, N_STREAMS, d).transpose(1, 0, 2).reshape(l, d)


def _ssm_layouts(d):
    g = d // SSM_GROUP
    gb = MXU_DIM // SSM_GROUP if d >= MXU_DIM else g
    return g, gb, g // gb


def _b_blocks(bb, d):
    g, gb, nb = _ssm_layouts(d)
    b4 = bb.reshape(SSM_GROUP, nb, gb, STATE)
    eye = jnp.eye(gb, dtype=bb.dtype)
    return jnp.einsum('hbqp,gq->bghqp', b4, eye).reshape(nb, gb * SSM_GROUP, gb * STATE)


def _b_unblocks(db, d):
    g, gb, nb = _ssm_layouts(d)
    eye = jnp.eye(gb, dtype=db.dtype)
    return jnp.einsum('bghqp,gq->hbqp', db.reshape(nb, gb, SSM_GROUP, gb, STATE), eye).reshape(SSM_GROUP, g * STATE)


def _c_blocks(c, d):
    g, gb, nb = _ssm_layouts(d)
    eye = jnp.eye(gb, dtype=c.dtype)
    return jnp.einsum('bghp,gq->bqpgh', c.reshape(nb, gb, SSM_GROUP, STATE), eye).reshape(nb, gb * STATE, gb * SSM_GROUP)


def _c_unblocks(dc, d):
    g, gb, nb = _ssm_layouts(d)
    eye = jnp.eye(gb, dtype=dc.dtype)
    return jnp.einsum('bqpgh,gq->bghp', dc.reshape(nb, gb, STATE, gb, SSM_GROUP), eye).reshape(g, SSM_GROUP, STATE)


class NoComm:
    def __init__(self):
        self.local = {}

    def gather(self, names):
        return None

    def landed(self, names, bufs, p):
        pass

    def swap(self, names, arrays):
        self.local.update(zip(names, arrays))
        return None

    def swapped(self, names, got):
        pass

    def chips(self, names):
        return None

    def reduced(self, names, slots):
        pass


def _ffn_fwd(tag, x, r, gain, p, layer, cm, ride):
    xs, h = add_rmsnorm(f"norm_ffn{tag}", x, r, [gain])
    rider = cm.gather(ride) if ride else None
    gu = mm_up(f"ffn_up{tag}", h, p["wup"][layer], 0, rider=rider)
    if rider is not None:
        gu, landed = gu
        cm.landed(ride, landed, p)
    a = ffn_mid_fwd(f"ffn_mid{tag}", gu, p["conv_w"][layer], p["conv_b"][layer:layer + 1])
    f = mm_nn(f"ffn_down{tag}", a, p["wdown"][layer], F32)
    return xs, h, gu, a, f


def _ffn_bwd(tag, dxo, dfb, xs, h, gu, a, gain, p, layer, s, cm, ride):
    wdown = p["wdown"][layer]
    f = wdown.shape[0]
    da = mm_nt(f"ffn_down_dx{tag}", dfb, wdown, BF16, tn=f // 2)
    dwdown = mm_tn(f"ffn_down_dw{tag}", a, dfb, tmo=f // 2)
    dgu, dcw, dcb = ffn_mid_bwd(f"ffn_mid_bwd{tag}", gu, da, p["conv_w"][layer], p["conv_b"][layer:layer + 1])
    rider = cm.chips(ride) if ride else None
    dh = mm_up_nt(f"ffn_up_dx{tag}", dgu, p["wup"][layer], 0, rider=rider)
    if rider is not None:
        dh, slots = dh
        cm.reduced(ride, slots)
    dwup = mm_up_tn(f"ffn_up_dw{tag}", h, dgu, s)
    own = [f"wup{layer}", f"wdown{layer}"]
    res = norm_bwd(f"norm_ffn_bwd{tag}", xs, dxo, [(gain, dh)], bf16_copy=True, rider=cm.swap(own, [dwup, dwdown]))
    cm.swapped(own, res[3:])
    return res[0], res[2], res[1], dcw, dcb


def _local_step(x, tgt, p, cm=None):
    cm = NoComm() if cm is None else cm
    l, d = x.shape
    s = N_CHIPS
    gr = {}

    a_re, a_im = p["a_re"].reshape(1, -1), p["a_im"].reshape(1, -1)
    ldt = jnp.repeat(p["log_dt"].reshape(-1), STATE).reshape(1, -1)
    bk_re = p["b_re"].transpose(2, 0, 1).reshape(SSM_GROUP, -1)
    bk_im = p["b_im"].transpose(2, 0, 1).reshape(SSM_GROUP, -1)
    ab_re, ab_im, bb_re, bb_im = ssm_disc_fwd("ssm_disc", a_re, a_im, ldt, bk_re, bk_im)
    a2 = jnp.concatenate([ab_re, ab_im], axis=0)
    a2c = jnp.concatenate([ab_re, -ab_im], axis=0)
    bre, bim = _b_blocks(bb_re, d).astype(BF16), _b_blocks(bb_im, d).astype(BF16)
    cre, cim = _c_blocks(p["c_re"], d).astype(BF16), _c_blocks(p["c_im"], d).astype(BF16)
    ctre, ctim = cre.transpose(0, 2, 1), -cim.transpose(0, 2, 1)
    dskip = p["d"].reshape(1, d)

    xp = _perm(x)
    (h0p,) = add_rmsnorm("norm_mix0", xp, None, [p["norm_mix"][0]])
    u = mm_nn("ssm_in", h0p, p["win"], BF16)
    rider = cm.gather(["wglu", "wdown0"])
    res = ssm_pass1("ssm_fwd1", u, bre, bim, a2, reverse=False, rider=rider)
    f_re, f_im = res[0], res[1]
    cm.landed(["wglu", "wdown0"], res[2:], p)
    rider = cm.gather(["wup0"])
    res = ssm_fwd2("ssm_fwd2", u, bre, bim, a2, f_re, f_im, cre, cim, dskip, rider=rider)
    s_re, s_im, ypre, yg = res[:4]
    cm.landed(["wup0"], res[4:], p)
    z = mm_nn_colshard("ssm_glu", yg, p["wglu4"], F32)
    mix = _unperm(glu_fwd("glu", z))

    x1, h1, gu0, a0, f0 = _ffn_fwd("0", x, mix, p["norm_ffn"][0], p, 0, cm, ["kv", "wq", "wo"])
    x2, hk, h2 = add_rmsnorm("norm_kv_mix1", x1, f0, [p["norm_kv"], p["norm_mix"][1]])
    kvw = p["kvw4"][:, None]
    kv = mm_up("kv_proj", hk, kvw, 0)
    qf = mm_nn("q_proj", h2, p["wq"], BF16)
    rider = cm.gather(["wup1", "wdown1"])
    res = attn_fwd("attn", qf, kv, rider=rider)
    ob, tot = res[0], res[1]
    cm.landed(["wup1", "wdown1"], res[2:], p)
    ao = mm_nn("o_proj", ob, p["wo"], F32)
    x3, h3, gu1, a1, f1 = _ffn_fwd("1", x2, ao, p["norm_ffn"][1], p, 1, cm, None)
    loss, dx4, dg_final, dx4b = final_loss("final_loss", x3, f1, tgt, p["norm_final"])
    gr["norm_final"] = dg_final.reshape(d)

    dx3, dx3b, dg_ffn1, dcw1, dcb1 = _ffn_bwd("1", dx4, dx4b, x3, h3, gu1, a1, p["norm_ffn"][1], p, 1, s, cm, None)
    do2 = mm_nt("o_proj_dx", dx3b, p["wo"], BF16)
    dwo = mm_tn("o_proj_dw", ob, dx3b)
    rider = cm.chips(["wup1", "wdown1"])
    res = attn_bwd("attn_bwd", qf, kv, tot, do2, rider=rider)
    dqf, dkv = res[0], res[1]
    cm.reduced(["wup1", "wdown1"], res[2:])
    dh2 = mm_nt("q_proj_dx", dqf, p["wq"], F32)
    dwq = mm_tn("q_proj_dw", h2, dqf)
    dhk = mm_up_nt("kv_proj_dx", dkv, kvw, 0)
    dwkv = mm_up_tn("kv_proj_dw", hk, dkv, s)
    att = ["wo", "wq", "kv"]
    res = norm_bwd("norm_kv_mix1_bwd", x2, dx3, [(p["norm_mix"][1], dh2), (p["norm_kv"], dhk)], bf16_copy=True,
                   rider=cm.swap(att, [dwo, dwq, dwkv]))
    dx2, dg_mix1, dg_kv, dx2b = res[:4]
    cm.swapped(att, res[4:])
    gr["norm_kv"] = dg_kv.reshape(d)

    dx1, _, dg_ffn0, dcw0, dcb0 = _ffn_bwd("0", dx2, dx2b, x1, h1, gu0, a0, p["norm_ffn"][0], p, 0, s, cm, att)
    dx1p = _perm(dx1)
    dz = glu_bwd("glu_bwd", z, dx1p)
    dyg = mm_nt_colshard("ssm_glu_dx", dz, p["wglu4"], F32)
    dwglu = mm_tn_colshard("ssm_glu_dw", yg, dz, s)
    dy = gelu_bwd("gelu_bwd", ypre, dyg)
    rider = join_riders(cm.swap(["wglu"], [dwglu]), cm.chips(["wdown0"]))
    res = ssm_pass1("ssm_bwd1", dy, ctre, ctim, a2c, reverse=True, rider=rider)
    i_re, i_im = res[0], res[1]
    cm.swapped(["wglu"], res[2:3])
    cm.reduced(["wdown0"], res[3:4])
    rider = cm.chips(["wup0", "wglu"])
    res = ssm_bwd2("ssm_bwd2", dy, u, s_re, s_im, ctre, ctim, a2c, i_re, i_im, f_re, f_im, bre, bim, dskip, rider=rider)
    du, dbre, dbim, dcre, dcim, da_re, da_im, dd = res[:8]
    cm.reduced(["wup0", "wglu"], res[8:])
    dh0p = mm_nt("ssm_in_dx", du, p["win"], F32)
    dwin = mm_tn("ssm_in_dw", h0p, du)
    dxp, dg_mix0 = norm_bwd("norm_mix0_bwd", xp, dx1p, [(p["norm_mix"][0], dh0p)])
    dx = _unperm(dxp)

    g_are, g_aim, g_ldt, g_bre, g_bim = ssm_disc_bwd(
        "ssm_disc_bwd", a_re, a_im, ldt, bk_re, bk_im,
        jnp.sum(da_re, axis=0, keepdims=True), jnp.sum(da_im, axis=0, keepdims=True),
        _b_unblocks(dbre, d), _b_unblocks(dbim, d))
    g = d // SSM_GROUP
    gr["a_re"] = g_are.reshape(g, STATE)
    gr["a_im"] = g_aim.reshape(g, STATE)
    gr["log_dt"] = jnp.sum(g_ldt.reshape(g, STATE), axis=1)
    gr["b_re"] = g_bre.reshape(SSM_GROUP, g, STATE).transpose(1, 2, 0)
    gr["b_im"] = g_bim.reshape(SSM_GROUP, g, STATE).transpose(1, 2, 0)
    gr["c_re"] = _c_unblocks(dcre, d)
    gr["c_im"] = _c_unblocks(dcim, d)
    gr["d"] = dd.reshape(g, SSM_GROUP)
    gr["norm_mix"] = jnp.concatenate([dg_mix0, dg_mix1], axis=0)
    gr["norm_ffn"] = jnp.concatenate([dg_ffn0, dg_ffn1], axis=0)
    gr["conv_w"] = jnp.stack([dcw0, dcw1])
    gr["conv_b"] = jnp.concatenate([dcb0, dcb1], axis=0)
    gr["win"] = dwin
    gr.update(getattr(cm, "local", {}))
    return loss, dx, gr


MESH = pl.DeviceIdType.MESH
N_CHIPS = 4
N_DEV = 8
ANY = pl.BlockSpec(memory_space=pl.ANY)


def _pos():
    x, y, c = lax.axis_index("x"), lax.axis_index("y"), lax.axis_index("c")
    return x, y, c, 2 * x + y


def _other_chips(x, y):
    return [(1 - x, y), (x, 1 - y), (1 - x, 1 - y)]


def _remote(src, dst, send_sem, recv_sem, dev):
    return pltpu.make_async_remote_copy(src_ref=src, dst_ref=dst, send_sem=send_sem, recv_sem=recv_sem,
                                        device_id=dev, device_id_type=MESH)


def cast_into_slot(name, a, chip, dtype):
    r, cdim = a.shape
    tr = 16 * _blk(r // 16, 32) if r % 16 == 0 else r

    def body(m_ref, a_ref, o_ref):
        o_ref[...] = a_ref[...].astype(o_ref.dtype)

    gs = pltpu.PrefetchScalarGridSpec(
        num_scalar_prefetch=1, grid=(r // tr,),
        in_specs=[pl.BlockSpec((tr, cdim), lambda i, m_ref: (i, 0))],
        out_specs=pl.BlockSpec((None, tr, cdim), lambda i, m_ref: (m_ref[0], i, 0)))
    return pl.pallas_call(body, name=name, grid_spec=gs, out_shape=jax.ShapeDtypeStruct((N_CHIPS, r, cdim), dtype),
                          compiler_params=_cparams(("parallel",)))(chip, a)


def gather_weights(name, bufs):
    n = len(bufs)

    def body(*refs):
        _gather_start(refs[n:2 * n], refs[2 * n:])
        _gather_finish(refs[n:2 * n], refs[2 * n:])

    sem = pltpu.SemaphoreType.DMA
    return pl.pallas_call(
        body, name=name, in_specs=[ANY] * n, out_specs=[ANY] * n,
        out_shape=[jax.ShapeDtypeStruct(b.shape, b.dtype) for b in bufs],
        input_output_aliases={w: w for w in range(n)},
        scratch_shapes=[sem((n, 3)), sem((n, 3)), sem((n, 3)), sem((n, 3))],
        compiler_params=pltpu.CompilerParams(has_side_effects=True),
    )(*bufs)


def _half(ref, chip, core):
    hr = ref.shape[1] // 2
    return ref.at[chip, pl.ds(core * hr, hr), :]


def _gather_start(bufs, sems):
    send_a, recv_a = sems[0], sems[1]
    x, y, c, m = _pos()
    for w, buf in enumerate(bufs):
        for j, (px, py) in enumerate(_other_chips(x, y)):
            blk = _half(buf, m, c)
            _remote(blk, blk, send_a.at[w, j], recv_a.at[w, j], (px, py, c)).start()


def _gather_finish(bufs, sems):
    send_a, recv_a, send_b, recv_b = sems
    x, y, c, m = _pos()
    chips = _other_chips(x, y)
    sib = (x, y, 1 - c)
    for j, (px, py) in enumerate(chips):
        for w, buf in enumerate(bufs):
            blk = _half(buf, 2 * px + py, c)
            _remote(blk, blk, send_a.at[w, j], recv_a.at[w, j], (px, py, c)).wait_recv()
            _remote(blk, blk, send_b.at[w, j], recv_b.at[w, j], sib).start()
    for j, (px, py) in enumerate(chips):
        for w, buf in enumerate(bufs):
            blk = _half(buf, 2 * px + py, 1 - c)
            _remote(blk, blk, send_b.at[w, j], recv_b.at[w, j], sib).wait_recv()
    for j, (px, py) in enumerate(chips):
        for w, buf in enumerate(bufs):
            mine, landed = _half(buf, m, c), _half(buf, 2 * px + py, c)
            _remote(mine, mine, send_a.at[w, j], recv_a.at[w, j], (px, py, c)).wait_send()
            _remote(landed, landed, send_b.at[w, j], recv_b.at[w, j], sib).wait_send()


def gather_rider(bufs):
    n = len(bufs)
    return Rider(ins=list(bufs), outs=[jax.ShapeDtypeStruct(b.shape, b.dtype) for b in bufs],
                 alias={w: w for w in range(n)}, sems=[(n, 3)] * 4,
                 start=lambda i, o, s: _gather_start(o, s), finish=lambda i, o, s: _gather_finish(o, s))


def exchange_halves(name, arrs):
    n = len(arrs)

    def body(*refs):
        _exchange(refs[:n], refs[n:2 * n], refs[2 * n:], "start")
        _exchange(refs[:n], refs[n:2 * n], refs[2 * n:], "wait")

    sem = pltpu.SemaphoreType.DMA
    return pl.pallas_call(
        body, name=name, in_specs=[ANY] * n, out_specs=[ANY] * n, out_shape=_exchange_shapes(arrs),
        scratch_shapes=[sem((n,)), sem((n,))],
        compiler_params=pltpu.CompilerParams(has_side_effects=True),
    )(*arrs)


def _exchange_shapes(arrs):
    return [jax.ShapeDtypeStruct((a.shape[0], a.shape[1] // 2, a.shape[2]), a.dtype) for a in arrs]


def _exchange(ins, outs, sems, what):
    send, recv = sems
    x, y, c, _ = _pos()
    for w in range(len(ins)):
        hr = ins[w].shape[1] // 2
        cp = _remote(ins[w].at[:, pl.ds((1 - c) * hr, hr), :], outs[w], send.at[w], recv.at[w], (x, y, 1 - c))
        if what == "start":
            cp.start()
        else:
            cp.wait()


def exchange_rider(arrs):
    n = len(arrs)
    return Rider(ins=list(arrs), outs=_exchange_shapes(arrs), alias={}, sems=[(n,)] * 2,
                 start=lambda i, o, s: _exchange(i, o, s, "start"), finish=lambda i, o, s: _exchange(i, o, s, "wait"))


def scatter_to_chips(name, arrs):
    n = len(arrs)

    def body(*refs):
        _scatter(refs[:n], refs[n:2 * n], refs[2 * n:], "start")
        _scatter(refs[:n], refs[n:2 * n], refs[2 * n:], "wait")

    sem = pltpu.SemaphoreType.DMA
    return pl.pallas_call(
        body, name=name, in_specs=[ANY] * n, out_specs=[ANY] * n,
        out_shape=[jax.ShapeDtypeStruct((3,) + a.shape[1:], a.dtype) for a in arrs],
        scratch_shapes=[sem((n, 3)), sem((n, 3))],
        compiler_params=pltpu.CompilerParams(has_side_effects=True),
    )(*arrs)


def _scatter(ins, outs, sems, what):
    send, recv = sems
    x, y, c, _ = _pos()
    for w in range(len(ins)):
        for j, (px, py) in enumerate(_other_chips(x, y)):
            cp = _remote(ins[w].at[2 * px + py], outs[w].at[j], send.at[w, j], recv.at[w, j], (px, py, c))
            if what == "start":
                cp.start()
            else:
                cp.wait()


def scatter_rider(arrs):
    n = len(arrs)
    return Rider(ins=list(arrs), outs=[jax.ShapeDtypeStruct((3,) + a.shape[1:], a.dtype) for a in arrs],
                 alias={}, sems=[(n, 3)] * 2,
                 start=lambda i, o, s: _scatter(i, o, s, "start"), finish=lambda i, o, s: _scatter(i, o, s, "wait"))


def share_halves(name, fulls):
    n = len(fulls)

    def body(*refs):
        outs = refs[n:2 * n]
        send, recv = refs[2 * n:]
        x, y, c, _ = _pos()
        cps = []
        for w in range(n):
            hr = outs[w].shape[0] // 2
            blk = outs[w].at[pl.ds(c * hr, hr), :]
            cp = _remote(blk, blk, send.at[w], recv.at[w], (x, y, 1 - c))
            cp.start()
            cps.append(cp)
        for w, cp in enumerate(cps):
            hr = outs[w].shape[0] // 2
            cp.wait_send()
            blk = outs[w].at[pl.ds((1 - c) * hr, hr), :]
            _remote(blk, blk, send.at[w], recv.at[w], (x, y, 1 - c)).wait_recv()

    sem = pltpu.SemaphoreType.DMA
    return pl.pallas_call(
        body, name=name, in_specs=[ANY] * n, out_specs=[ANY] * n,
        out_shape=[jax.ShapeDtypeStruct(a.shape, a.dtype) for a in fulls],
        input_output_aliases={w: w for w in range(n)},
        scratch_shapes=[sem((n,)), sem((n,))],
        compiler_params=pltpu.CompilerParams(has_side_effects=True),
    )(*fulls)


def add_own_half(name, full, got, core, out_dtype):
    n, r, cdim = full.shape
    hr = r // 2
    tr = 8 * _blk(hr // 8, 32) if out_dtype == F32 else 16 * _blk(hr // 16, 16)
    nbh = hr // tr

    def body(c_ref, f_ref, g_ref, o_ref):
        o_ref[...] = (f_ref[...] + g_ref[...]).astype(o_ref.dtype)

    gs = pltpu.PrefetchScalarGridSpec(
        num_scalar_prefetch=1, grid=(n, nbh),
        in_specs=[pl.BlockSpec((None, tr, cdim), lambda s, i, c_ref: (s, c_ref[0] * nbh + i, 0)),
                  pl.BlockSpec((None, tr, cdim), lambda s, i, c_ref: (s, i, 0))],
        out_specs=pl.BlockSpec((None, tr, cdim), lambda s, i, c_ref: (s, i, 0)))
    return pl.pallas_call(body, name=name, grid_spec=gs, out_shape=jax.ShapeDtypeStruct((n, hr, cdim), out_dtype),
                          compiler_params=_cparams(("parallel", "parallel")))(core, full, got)


def sum_into_half(name, part, slots, chip_core, chip_order):
    _, hr, cdim = part.shape
    unit = 8 if part.dtype == F32 else 16
    tr = unit * _blk(hr // unit, 256 // unit)
    nbh = hr // tr

    def body(mc_ref, p_ref, s_ref, o_ref):
        terms = [p_ref[...].astype(F32)] + [s_ref[k].astype(F32) for k in range(3)]
        if chip_order:
            m = mc_ref[0]
            own, fx, fy, fxy = terms
            terms = [jnp.where((k ^ m) == 0, own, jnp.where((k ^ m) == 2, fx, jnp.where((k ^ m) == 1, fy, fxy)))
                     for k in range(N_CHIPS)]
        o_ref[...] = ((terms[0] + terms[1]) + terms[2]) + terms[3]

    gs = pltpu.PrefetchScalarGridSpec(
        num_scalar_prefetch=1, grid=(nbh,),
        in_specs=[pl.BlockSpec((None, tr, cdim), lambda i, mc: (mc[0], i, 0)),
                  pl.BlockSpec((3, tr, cdim), lambda i, mc: (0, i, 0))],
        out_specs=pl.BlockSpec((tr, cdim), lambda i, mc: (mc[1] * nbh + i, 0)))
    return pl.pallas_call(body, name=name, grid_spec=gs, out_shape=jax.ShapeDtypeStruct((2 * hr, cdim), F32),
                          compiler_params=_cparams(("parallel",)))(chip_core, part, slots)


class StepComm:
    def __init__(self, bufs, core, chip_core):
        self.bufs, self.core, self.chip_core = bufs, core, chip_core
        self.whole, self.part, self.fulls = {}, {}, {}

    def gather(self, names):
        return gather_rider([self.bufs[n] for n in names])

    def landed(self, names, bufs, p):
        for n, b in zip(names, bufs):
            if n.startswith("wdown"):
                p["wdown"][int(n[-1])] = b.reshape(-1, b.shape[-1])
            elif n.startswith("wup"):
                p["wup"][int(n[-1])] = b[:, None]
            elif n == "kv":
                p["kvw4"] = b
            elif n == "wglu":
                p["wglu4"] = b
            else:
                p[n] = b.reshape(-1, b.shape[-1])

    def swap(self, names, arrays):
        big = [a if a.ndim == 3 else a.reshape(N_CHIPS, a.shape[0] // N_CHIPS, a.shape[1]) for a in arrays]
        self.whole.update(zip(names, big))
        return exchange_rider(big)

    def swapped(self, names, got, payloads=None):
        payloads = payloads or [BF16] * len(names)
        for n, r, dt in zip(names, got, payloads):
            self.part[n] = add_own_half("rs_add_" + n, self.whole.pop(n), r, self.core, dt)

    def chips(self, names):
        return scatter_rider([self.part[n] for n in names])

    def reduce(self, names, arrays, payloads=None):
        rider = self.swap(names, arrays)
        self.swapped(names, exchange_halves("rs_siblings_" + names[0], rider.ins), payloads)
        return self.chips(names)

    def reduced(self, names, slots, chip_order=()):
        for n, s in zip(names, slots):
            self.fulls[n] = sum_into_half("rs_sum_" + n, self.part[n], s, self.chip_core, n in chip_order)

    def finish(self):
        names = list(self.fulls)
        return dict(zip(names, share_halves("rs_share", [self.fulls[n] for n in names])))


WEIGHTS = ('norm_mix', 'norm_ffn', 'norm_kv', 'norm_final', 'ssm_w_in', 'ssm_a_re', 'ssm_a_im', 'ssm_log_dt',
           'ssm_b_re', 'ssm_b_im', 'ssm_c_re', 'ssm_c_im', 'ssm_d', 'ssm_w_glu', 'kv_w', 'attn_w_q', 'attn_w_o',
           'ffn_w_up', 'ffn_conv_w', 'ffn_conv_b', 'ffn_w_down')
SMALL = ('norm_mix', 'norm_ffn', 'norm_kv', 'norm_final', 'ssm_a_re', 'ssm_a_im', 'ssm_log_dt', 'ssm_b_re', 'ssm_b_im',
         'ssm_c_re', 'ssm_c_im', 'ssm_d', 'ffn_conv_w', 'ffn_conv_b')


def _pad_rows(flat, unit):
    n = flat.shape[0]
    total = -(-n // unit) * unit
    return jnp.pad(flat, (0, total - n)).reshape(total // LANES, LANES)


def kernel(x, norm_mix, norm_ffn, norm_kv, norm_final, ssm_w_in, ssm_a_re, ssm_a_im, ssm_log_dt, ssm_b_re, ssm_b_im, ssm_c_re, ssm_c_im, ssm_d, ssm_w_glu, kv_w, attn_w_q, attn_w_o, ffn_w_up, ffn_conv_w, ffn_conv_b, ffn_w_down, loss_target, m_norm_mix, m_norm_ffn, m_norm_kv, m_norm_final, m_ssm_w_in, m_ssm_a_re, m_ssm_a_im, m_ssm_log_dt, m_ssm_b_re, m_ssm_b_im, m_ssm_c_re, m_ssm_c_im, m_ssm_d, m_ssm_w_glu, m_kv_w, m_attn_w_q, m_attn_w_o, m_ffn_w_up, m_ffn_conv_w, m_ffn_conv_b, m_ffn_w_down, v_norm_mix, v_norm_ffn, v_norm_kv, v_norm_final, v_ssm_w_in, v_ssm_a_re, v_ssm_a_im, v_ssm_log_dt, v_ssm_b_re, v_ssm_b_im, v_ssm_c_re, v_ssm_c_im, v_ssm_d, v_ssm_w_glu, v_kv_w, v_attn_w_q, v_attn_w_o, v_ffn_w_up, v_ffn_conv_w, v_ffn_conv_b, v_ffn_w_down):
    a = dict(locals())
    l, d = x.shape[1], x.shape[2]
    f = ffn_conv_b.shape[1]
    fs = f // N_CHIPS
    m = 2 * lax.axis_index("x") + lax.axis_index("y")
    core = lax.axis_index("c").astype(jnp.int32).reshape(1)
    chip = m.astype(jnp.int32).reshape(1)
    chip_core = jnp.concatenate([chip, core])

    shards = {"win": ssm_w_in[0], "wglu": ssm_w_glu[0], "kv": kv_w, "wq": attn_w_q[0], "wo": attn_w_o[0],
              "wup0": ffn_w_up[0], "wup1": ffn_w_up[1], "wdown0": ffn_w_down[0], "wdown1": ffn_w_down[1],
              "convw": _pad_rows(ffn_conv_w.reshape(-1), 16 * LANES)}
    bufs = {k: cast_into_slot(f"cast_{k}", s, chip, F32 if k == "convw" else BF16) for k, s in shards.items()}
    cm = StepComm(bufs, core, chip_core)
    g_in, g_cw = gather_weights("gather_first", [bufs["win"], bufs["convw"]])
    conv_w = g_cw.reshape(N_CHIPS, -1)[:, :2 * 3 * fs].reshape(N_CHIPS, 2, 3, fs).transpose(1, 2, 0, 3).reshape(2, 3, f)
    p = dict(
        norm_mix=norm_mix, norm_ffn=norm_ffn, norm_kv=norm_kv, norm_final=norm_final,
        a_re=ssm_a_re[0], a_im=ssm_a_im[0], log_dt=ssm_log_dt[0], b_re=ssm_b_re[0], b_im=ssm_b_im[0],
        c_re=ssm_c_re[0], c_im=ssm_c_im[0], d=ssm_d[0],
        win=g_in.reshape(-1, g_in.shape[-1]), wup=[None, None], wdown=[None, None],
        conv_w=conv_w, conv_b=ffn_conv_b)

    loss_slab, dx, gr = _local_step(x[0], loss_target[0], p, cm)

    small = {"norm_mix": gr["norm_mix"], "norm_ffn": gr["norm_ffn"], "norm_kv": gr["norm_kv"], "norm_final": gr["norm_final"],
             "ssm_a_re": gr["a_re"], "ssm_a_im": gr["a_im"], "ssm_log_dt": gr["log_dt"], "ssm_b_re": gr["b_re"],
             "ssm_b_im": gr["b_im"], "ssm_c_re": gr["c_re"], "ssm_c_im": gr["c_im"], "ssm_d": gr["d"],
             "ffn_conv_w": gr["conv_w"], "ffn_conv_b": gr["conv_b"]}
    packed = _pad_rows(jnp.concatenate([small[k].reshape(-1) for k in SMALL] + [loss_slab[0, 0:1]]), 16 * LANES)
    last = cm.reduce(["win", "small"], [gr["win"], jnp.broadcast_to(packed, (N_CHIPS,) + packed.shape)], [BF16, F32])
    cm.reduced(["win", "small"], scatter_to_chips("rs_chips_last", last.ins), chip_order=("small",))
    r = cm.finish()
    grads = {"ssm_w_in": r["win"][None], "ssm_w_glu": r["wglu"][None], "kv_w": r["kv"], "attn_w_q": r["wq"][None],
             "attn_w_o": r["wo"][None], "ffn_w_up": jnp.stack([r["wup0"], r["wup1"]]),
             "ffn_w_down": jnp.stack([r["wdown0"], r["wdown1"]])}
    total = r["small"].reshape(-1)
    off = 0
    for k in SMALL:
        n = small[k].size
        full = total[off:off + n].reshape(small[k].shape)
        off += n
        if k == "ffn_conv_w":
            full = lax.dynamic_slice_in_dim(full, m * fs, fs, axis=2)
        grads[k] = full.reshape(a[k].shape)
    loss = total[off]

    outs = {}
    for k in WEIGHTS:
        outs[k] = adamw(f"adamw_{k}", a[k], grads[k], a["m_" + k], a["v_" + k])
    return (loss, dx[None], *[grads[k] for k in WEIGHTS], *[outs[k][0] for k in WEIGHTS],
            *[outs[k][1] for k in WEIGHTS], *[outs[k][2] for k in WEIGHTS])
```
